```python
import math
import jax
import jax.numpy as jnp
from jax import lax
import numpy as np

D_MODEL = 1024
BATCH = 8
SEQ = 8192
DEPTH = 4

EPS = 1e-6
PLE_DIM = 256
N_BRANCHES = 4

CONV_CH = 512
CONV_K = 31

SSM_HEADS = 8
SSM_HEAD_DIM = 64
SSM_INNER = SSM_HEADS * SSM_HEAD_DIM
SSM_GROUPS = 2
SSM_STATE = 64
SSM_CONV = 5
SSM_CONV_CH = SSM_INNER + 2 * SSM_GROUPS * SSM_STATE
SSM_CHUNK = 128

ATT_PATTERNS = ((128, 1), (512, 4), (2048, 16))
ATT_GROUPS = len(ATT_PATTERNS)
ATT_HEADS = 8
ATT_HEAD_DIM = 64
ATT_WIDTH = ATT_HEADS * ATT_HEAD_DIM
REL_BUCKETS = 32
REL_MAX_DIST = 1024

FNET_GROUPS = 4
FNET_GROUP_DIM = 128
FNET_WIDTH = FNET_GROUPS * FNET_GROUP_DIM

FFN_DIM = 2816
N_EXPERTS = 8
TOP_K = 2
EXPERT_DIM = 3584

ATT_IN_COLS = 3 * ATT_GROUPS * ATT_WIDTH
CONV_IN_COLS = 2 * CONV_CH
SSM_IN_COLS = SSM_INNER + SSM_CONV_CH + 2 * SSM_HEADS
FNET_IN_COLS = FNET_WIDTH
OFF_CONV = ATT_IN_COLS
OFF_SSM = OFF_CONV + CONV_IN_COLS
OFF_FNET = OFF_SSM + SSM_IN_COLS
IN_COLS = OFF_FNET + FNET_IN_COLS

kernel_name = 'hybrid_parallel_mixer_encoder'


def rmsnorm(x, g):
    xf = x.astype(jnp.float32)
    y = xf * lax.rsqrt(jnp.mean(xf * xf, axis=-1, keepdims=True) + EPS)
    return (y * g.astype(jnp.float32)).astype(x.dtype)


def layernorm(x, g, b):
    xf = x.astype(jnp.float32)
    mu = jnp.mean(xf, axis=-1, keepdims=True)
    var = jnp.mean(jnp.square(xf - mu), axis=-1, keepdims=True)
    y = (xf - mu) * lax.rsqrt(var + EPS)
    return (y * g.astype(jnp.float32) + b.astype(jnp.float32)).astype(x.dtype)


def depthwise_conv(x, w, b):
    y = lax.conv_general_dilated(
        x, w[:, None, :].astype(x.dtype), window_strides=(1,), padding='SAME',
        dimension_numbers=('NWC', 'WIO', 'NWC'), feature_group_count=x.shape[-1])
    return y + b.astype(x.dtype)


def swiglu(x, w1, w3, w2):
    return (jax.nn.silu(x @ w1) * (x @ w3)) @ w2


def t5_bucket(rel):
    half = REL_BUCKETS // 2
    max_exact = half // 2
    n = np.abs(rel)
    large = max_exact + (np.log(np.maximum(n, 1) / max_exact) / math.log(REL_MAX_DIST / max_exact)
                         * (half - max_exact)).astype(np.int32)
    large = np.minimum(large, half - 1)
    return np.where(rel > 0, half, 0) + np.where(n < max_exact, n, large)


def dilated_window_attention(q, k, v, bias, dilation, half):
    bsz, S, H, Dh = q.shape
    L = S // dilation
    W = half
    nb = -(-L // W)
    Lp = nb * W

    def to_sub(t):
        t = t.reshape(bsz, L, dilation, H, Dh).transpose(0, 2, 1, 3, 4)
        return jnp.pad(t, ((0, 0), (0, 0), (0, Lp - L), (0, 0), (0, 0)))

    def band(t):
        t = jnp.pad(to_sub(t), ((0, 0), (0, 0), (W, W), (0, 0), (0, 0)))
        t = t.reshape(bsz, dilation, nb + 2, W, H, Dh)
        return jnp.concatenate([t[:, :, :-2], t[:, :, 1:-1], t[:, :, 2:]], axis=3)

    qb = to_sub(q).reshape(bsz, dilation, nb, W, H, Dh)
    kb, vb = band(k), band(v)

    qi = np.arange(W)[:, None]
    kj = np.arange(3 * W)[None, :] - W
    rel = kj - qi
    key_pos = np.arange(nb)[:, None, None] * W + kj[None]
    valid = (np.abs(rel) <= W)[None] & (key_pos >= 0) & (key_pos < L)

    s = jnp.einsum('brnqhd,brnkhd->brnhqk', qb, kb, preferred_element_type=jnp.float32)
    s = s * (Dh ** -0.5) + bias.astype(jnp.float32)[None, None, None]
    s = jnp.where(valid[None, None, :, None], s, -1e30)
    m = jnp.max(s, axis=-1, keepdims=True)
    e = jnp.exp(s - m)
    den = jnp.sum(e, axis=-1)
    o = jnp.einsum('brnhqk,brnkhd->brnqhd', e, vb.astype(jnp.float32))
    o = o / den.transpose(0, 1, 2, 4, 3)[..., None]
    lse = (m[..., 0] + jnp.log(den)).transpose(0, 1, 2, 4, 3)

    def from_sub(t):
        t = t.reshape((bsz, dilation, Lp) + t.shape[4:])[:, :, :L]
        t = jnp.moveaxis(t, 1, 2)
        return t.reshape((bsz, S) + t.shape[3:])

    return from_sub(o), from_sub(lse)


def attention_branch(u, rel_bias, w_o):
    bsz, S, _ = u.shape
    qkv = u.reshape(bsz, S, 3, ATT_GROUPS, ATT_HEADS, ATT_HEAD_DIM)
    outs, lses = [], []
    for g, (window, dil) in enumerate(ATT_PATTERNS):
        half = window // (2 * dil)
        rel = np.arange(3 * half)[None, :] - half - np.arange(half)[:, None]
        bucket = t5_bucket(dil * rel)
        bias = rel_bias[bucket][:, :, g * ATT_HEADS:(g + 1) * ATT_HEADS].transpose(2, 0, 1)
        o, lse = dilated_window_attention(qkv[:, :, 0, g], qkv[:, :, 1, g], qkv[:, :, 2, g],
                                          bias, dil, half)
        outs.append(o)
        lses.append(lse)
    wts = jax.nn.softmax(jnp.stack(lses, axis=2), axis=2)
    o = jnp.einsum('bsghd,bsgh->bshd', jnp.stack(outs, axis=2), wts)
    return o.reshape(bsz, S, ATT_WIDTH).astype(u.dtype) @ w_o


def conformer_branch(u, dw, dw_b, ln_g, ln_b, w_o):
    a, gate = jnp.split(u, 2, axis=-1)
    hid = a * jax.nn.sigmoid(gate)
    hid = depthwise_conv(hid, dw, dw_b)
    hid = jax.nn.silu(layernorm(hid, ln_g, ln_b))
    return hid @ w_o


def ssd_scan(x, dt, a_log, b_in, c_in):
    bsz, L = x.shape[:2]
    nc = L // SSM_CHUNK
    hg = SSM_HEADS // SSM_GROUPS
    A = -jnp.exp(a_log.astype(jnp.float32)).reshape(SSM_GROUPS, hg)
    dt = dt.reshape(bsz, nc, SSM_CHUNK, SSM_GROUPS, hg)
    xd = x.astype(jnp.float32).reshape(bsz, nc, SSM_CHUNK, SSM_GROUPS, hg, SSM_HEAD_DIM) * dt[..., None]
    bb = b_in.astype(jnp.float32).reshape(bsz, nc, SSM_CHUNK, SSM_GROUPS, SSM_STATE)
    cc = c_in.astype(jnp.float32).reshape(bsz, nc, SSM_CHUNK, SSM_GROUPS, SSM_STATE)
    a_cs = jnp.cumsum(dt * A, axis=2)

    seg = a_cs[:, :, :, None] - a_cs[:, :, None, :]
    lower = np.tril(np.ones((SSM_CHUNK, SSM_CHUNK), dtype=bool))[:, :, None, None]
    decay = jnp.exp(jnp.where(lower, seg, -jnp.inf))
    scores = jnp.einsum('bclgn,bcsgn->bclsg', cc, bb)
    y_diag = jnp.einsum('bclsg,bclsgh,bcsghp->bclghp', scores, decay, xd)

    decay_st = jnp.exp(a_cs[:, :, -1:] - a_cs)
    states = jnp.einsum('bclgn,bclgh,bclghp->bcghpn', bb, decay_st, xd)
    chunk_decay = jnp.exp(a_cs[:, :, -1])

    def step(carry, inp):
        st, dec = inp
        return carry * dec[..., None, None] + st, carry

    init = jnp.zeros((bsz, SSM_GROUPS, hg, SSM_HEAD_DIM, SSM_STATE), jnp.float32)
    _, prev = lax.scan(step, init, (jnp.moveaxis(states, 1, 0), jnp.moveaxis(chunk_decay, 1, 0)))
    prev = jnp.moveaxis(prev, 0, 1)
    y_off = jnp.einsum('bclgn,bcghpn,bclgh->bclghp', cc, prev, jnp.exp(a_cs))
    return (y_diag + y_off).reshape(bsz, L, SSM_HEADS, SSM_HEAD_DIM)


def ssd_branch(u, conv_w, conv_b, a_log, dt_bias, d_skip, norm_g, w_o):
    bsz, L, _ = u.shape
    z = u[..., :SSM_INNER]
    xbc = u[..., SSM_INNER:SSM_INNER + SSM_CONV_CH]
    dt_raw = u[..., SSM_INNER + SSM_CONV_CH:]
    xbc = jax.nn.silu(depthwise_conv(xbc, conv_w, conv_b))
    xs = xbc[..., :SSM_INNER].reshape(bsz, L, SSM_HEADS, SSM_HEAD_DIM)
    bs = xbc[..., SSM_INNER:SSM_INNER + SSM_GROUPS * SSM_STATE].reshape(bsz, L, SSM_GROUPS, SSM_STATE)
    cs = xbc[..., SSM_INNER + SSM_GROUPS * SSM_STATE:].reshape(bsz, L, SSM_GROUPS, SSM_STATE)
    dt = jax.nn.softplus(dt_raw.astype(jnp.float32).reshape(bsz, L, 2, SSM_HEADS)
                         + dt_bias.astype(jnp.float32))
    flip = lambda t: jnp.flip(t, axis=1)
    y_fwd = ssd_scan(xs, dt[:, :, 0], a_log[0], bs, cs)
    y_bwd = flip(ssd_scan(flip(xs), flip(dt[:, :, 1]), a_log[1], flip(bs), flip(cs)))
    y = y_fwd + y_bwd + d_skip.astype(jnp.float32)[:, None] * xs.astype(jnp.float32)
    y = y.reshape(bsz, L, SSM_INNER) * jax.nn.silu(z.astype(jnp.float32))
    y = rmsnorm(y, norm_g)
    return y.astype(u.dtype) @ w_o


def fourier_branch(u, w_o):
    bsz, S, _ = u.shape
    t = u.astype(jnp.float32).reshape(bsz, S, FNET_GROUPS, FNET_GROUP_DIM)
    f = jnp.fft.fft2(t, axes=(1, 3), norm='ortho').real
    return f.reshape(bsz, S, FNET_WIDTH).astype(u.dtype) @ w_o


def moe_swiglu(x, router, w1, w3, w2):
    logits = jnp.einsum('bsd,de->bse', x, router, preferred_element_type=jnp.float32)
    top_val, top_idx = lax.top_k(logits, TOP_K)
    top_w = jax.nn.softmax(top_val, axis=-1)
    combine = jnp.einsum('bske,bsk->bse', jax.nn.one_hot(top_idx, N_EXPERTS, dtype=jnp.float32),
                         top_w).astype(x.dtype)
    out = jnp.zeros_like(x)
    for e in range(N_EXPERTS):
        out = out + combine[..., e:e + 1] * swiglu(x, w1[e], w3[e], w2[e])
    return out


def setup_inputs(seed: int = 0) -> dict:
    key = jax.random.key(seed)
    ks = iter(jax.random.split(key, 48))
    f32 = jnp.float32

    def nrm(shape, scale):
        return jax.random.normal(next(ks), shape, f32) * scale

    def gain(shape):
        return 1.0 + 0.02 * jax.random.normal(next(ks), shape, f32)

    n_dense = (DEPTH + 1) // 2
    n_moe = DEPTH // 2
    dt0 = jnp.exp(jax.random.uniform(next(ks), (DEPTH, 2, SSM_HEADS), f32,
                                     math.log(1e-3), math.log(1e-1)))
    return {
        'x': nrm((BATCH, SEQ, D_MODEL), 1.0),
        'p': nrm((DEPTH, BATCH, SEQ, PLE_DIM), 1.0),
        'rel_bias': nrm((REL_BUCKETS, ATT_GROUPS * ATT_HEADS), 0.2),
        'norm_mix': gain((DEPTH, D_MODEL)),
        'w_in': nrm((DEPTH, D_MODEL, IN_COLS), D_MODEL ** -0.5),
        'conv_dw': nrm((DEPTH, CONV_K, CONV_CH), CONV_K ** -0.5),
        'conv_dw_b': nrm((DEPTH, CONV_CH), 0.02),
        'conv_ln_g': gain((DEPTH, CONV_CH)),
        'conv_ln_b': nrm((DEPTH, CONV_CH), 0.02),
        'conv_out': nrm((DEPTH, CONV_CH, D_MODEL), CONV_CH ** -0.5),
        'ssm_conv_w': nrm((DEPTH, SSM_CONV, SSM_CONV_CH), SSM_CONV ** -0.5),
        'ssm_conv_b': nrm((DEPTH, SSM_CONV_CH), 0.02),
        'ssm_a_log': jnp.log(jax.random.uniform(next(ks), (DEPTH, 2, SSM_HEADS), f32, 1.0, 16.0)),
        'ssm_dt_bias': dt0 + jnp.log(-jnp.expm1(-dt0)),
        'ssm_d': gain((DEPTH, SSM_HEADS)),
        'ssm_norm': gain((DEPTH, SSM_INNER)),
        'ssm_out': nrm((DEPTH, SSM_INNER, D_MODEL), SSM_INNER ** -0.5),
        'attn_out': nrm((DEPTH, ATT_WIDTH, D_MODEL), ATT_WIDTH ** -0.5),
        'fnet_out': nrm((DEPTH, FNET_WIDTH, D_MODEL), FNET_WIDTH ** -0.5),
        'w_gate': nrm((DEPTH, N_BRANCHES, D_MODEL, D_MODEL), D_MODEL ** -0.5),
        'b_gate': nrm((DEPTH, N_BRANCHES, D_MODEL), 0.02),
        'w_out': nrm((DEPTH, D_MODEL, D_MODEL), D_MODEL ** -0.5),
        'norm_ffn': gain((DEPTH, D_MODEL)),
        'ffn_w1': nrm((n_dense, D_MODEL, FFN_DIM), D_MODEL ** -0.5),
        'ffn_w3': nrm((n_dense, D_MODEL, FFN_DIM), D_MODEL ** -0.5),
        'ffn_w2': nrm((n_dense, FFN_DIM, D_MODEL), FFN_DIM ** -0.5),
        'moe_router': nrm((n_moe, D_MODEL, N_EXPERTS), D_MODEL ** -0.5),
        'moe_w1': nrm((n_moe, N_EXPERTS, D_MODEL, EXPERT_DIM), D_MODEL ** -0.5),
        'moe_w3': nrm((n_moe, N_EXPERTS, D_MODEL, EXPERT_DIM), D_MODEL ** -0.5),
        'moe_w2': nrm((n_moe, N_EXPERTS, EXPERT_DIM, D_MODEL), EXPERT_DIM ** -0.5),
        'ple_gate': nrm((DEPTH, D_MODEL, D_MODEL), D_MODEL ** -0.5),
        'ple_proj': nrm((DEPTH, PLE_DIM, D_MODEL), PLE_DIM ** -0.5),
        'final_norm': gain((D_MODEL,)),
    }


def reference(x, p, rel_bias, norm_mix, w_in, conv_dw, conv_dw_b, conv_ln_g, conv_ln_b, conv_out,
              ssm_conv_w, ssm_conv_b, ssm_a_log, ssm_dt_bias, ssm_d, ssm_norm, ssm_out,
              attn_out, fnet_out, w_gate, b_gate, w_out, norm_ffn, ffn_w1, ffn_w3, ffn_w2,
              moe_router, moe_w1, moe_w3, moe_w2, ple_gate, ple_proj, final_norm):
    h = x
    for l in range(DEPTH):
        xn = rmsnorm(h, norm_mix[l])
        u = xn @ w_in[l]
        branches = (
            attention_branch(u[..., :OFF_CONV], rel_bias, attn_out[l]),
            conformer_branch(u[..., OFF_CONV:OFF_SSM], conv_dw[l], conv_dw_b[l],
                             conv_ln_g[l], conv_ln_b[l], conv_out[l]),
            ssd_branch(u[..., OFF_SSM:OFF_FNET], ssm_conv_w[l], ssm_conv_b[l], ssm_a_log[l],
                       ssm_dt_bias[l], ssm_d[l], ssm_norm[l], ssm_out[l]),
            fourier_branch(u[..., OFF_FNET:], fnet_out[l]),
        )
        mixed = jnp.zeros_like(h)
        for i, br in enumerate(branches):
            mixed = mixed + jax.nn.sigmoid(xn @ w_gate[l, i] + b_gate[l, i]) * br
        h = h + mixed @ w_out[l]

        xn = rmsnorm(h, norm_ffn[l])
        if l % 2 == 0:
            h = h + swiglu(xn, ffn_w1[l // 2], ffn_w3[l // 2], ffn_w2[l // 2])
        else:
            h = h + moe_swiglu(xn, moe_router[l // 2], moe_w1[l // 2], moe_w3[l // 2], moe_w2[l // 2])

        h = h + jax.nn.sigmoid(h @ ple_gate[l]) * (p[l] @ ple_proj[l])
    return rmsnorm(h, final_norm)
```

```python
import functools
import math

import numpy as np
import jax
import jax.numpy as jnp
from jax import lax
from jax.experimental import pallas as pl
from jax.experimental.pallas import tpu as pltpu

F32 = jnp.float32
BF16 = jnp.bfloat16

EPS = 1e-6
N_BRANCHES = 4
CONV_CH = 512
CONV_K = 31
SSM_HEADS = 8
SSM_HEAD_DIM = 64
SSM_INNER = SSM_HEADS * SSM_HEAD_DIM
SSM_GROUPS = 2
SSM_STATE = 64
SSM_CONV = 5
SSM_CONV_CH = SSM_INNER + 2 * SSM_GROUPS * SSM_STATE
SSM_CHUNK = 128
ATT_PATTERNS = ((128, 1), (512, 4), (2048, 16))
ATT_GROUPS = len(ATT_PATTERNS)
ATT_HEADS = 8
ATT_HEAD_DIM = 64
ATT_WIDTH = ATT_HEADS * ATT_HEAD_DIM
REL_BUCKETS = 32
REL_MAX_DIST = 1024
FNET_GROUPS = 4
FNET_GROUP_DIM = 128
FNET_WIDTH = FNET_GROUPS * FNET_GROUP_DIM
N_EXPERTS = 8
TOP_K = 2

ATT_IN_COLS = 3 * ATT_GROUPS * ATT_WIDTH
CONV_IN_COLS = 2 * CONV_CH
SSM_IN_COLS = SSM_INNER + SSM_CONV_CH + 2 * SSM_HEADS
OFF_CONV = ATT_IN_COLS
OFF_SSM = OFF_CONV + CONV_IN_COLS
OFF_FNET = OFF_SSM + SSM_IN_COLS

LANES = 128
DT_PAD = LANES
VMEM_LIMIT = 56 * 1024 * 1024


def _cparams(sem):
    return pltpu.CompilerParams(dimension_semantics=sem, vmem_limit_bytes=VMEM_LIMIT)


def _const_spec(shape):
    nd = len(shape)
    return pl.BlockSpec(shape, lambda *_: (0,) * nd, pipeline_mode=pl.Buffered(1))


def _rms(x, g):
    return x * lax.rsqrt(jnp.mean(x * x, axis=-1, keepdims=True) + EPS) * g


_SEC_QKV = (0, ATT_IN_COLS)
_SEC_CONV = (_SEC_QKV[0] + _SEC_QKV[1], CONV_IN_COLS)
_SEC_Z = (_SEC_CONV[0] + _SEC_CONV[1], SSM_INNER)
_SEC_XBC = (_SEC_Z[0] + _SEC_Z[1], SSM_CONV_CH)
_SEC_FNET = (_SEC_XBC[0] + _SEC_XBC[1], FNET_WIDTH)
_SEC_DT = (_SEC_FNET[0] + _SEC_FNET[1], DT_PAD)
_IN_COLS_PAD = _SEC_DT[0] + _SEC_DT[1]
_MM_CHUNK = 512


def _reorder_w_in(w):
    d = w.shape[0]
    ssm = w[:, OFF_SSM:OFF_FNET]
    dt = ssm[:, SSM_INNER + SSM_CONV_CH:]
    parts = [w[:, :OFF_CONV], w[:, OFF_CONV:OFF_SSM], ssm[:, :SSM_INNER],
             ssm[:, SSM_INNER:SSM_INNER + SSM_CONV_CH], w[:, OFF_FNET:],
             dt, jnp.zeros((d, DT_PAD - dt.shape[1]), w.dtype)]
    return jnp.concatenate(parts, axis=1).astype(BF16)


def _inproj_kernel(h_ref, g_ref, w_ref, qkv_ref, conv_ref, z_ref, xbc_ref, fn_ref, dt_ref):
    xn = _rms(h_ref[...], g_ref[...]).astype(BF16)

    def section(sec, store):
        start, width = sec
        for c in range(0, width, _MM_CHUNK):
            cw = min(_MM_CHUNK, width - c)
            store(c, cw, jnp.dot(xn, w_ref[:, start + c:start + c + cw],
                                 preferred_element_type=F32))

    def to(ref):
        def store(c, cw, val):
            ref[:, c:c + cw] = val.astype(ref.dtype)
        return store

    def to_fnet(c, cw, val):
        for g in range(cw // FNET_GROUP_DIM):
            fn_ref[0, c // FNET_GROUP_DIM + g] = val[:, g * FNET_GROUP_DIM:(g + 1) * FNET_GROUP_DIM]

    section(_SEC_QKV, to(qkv_ref))
    section(_SEC_CONV, to(conv_ref))
    section(_SEC_Z, to(z_ref))
    section(_SEC_XBC, to(xbc_ref))
    section(_SEC_FNET, to_fnet)
    section(_SEC_DT, to(dt_ref))


def _inproj(h, g, w, bsz, seq):
    t, d = h.shape
    tm = 256
    spt = seq // tm
    row = lambda n: pl.BlockSpec((tm, n), lambda i: (i, 0))
    return pl.pallas_call(
        _inproj_kernel,
        grid=(t // tm,),
        in_specs=[row(d), _const_spec((1, d)), _const_spec(w.shape)],
        out_specs=[row(ATT_IN_COLS), row(CONV_IN_COLS), row(SSM_INNER), row(SSM_CONV_CH),
                   pl.BlockSpec((1, FNET_GROUPS, tm, FNET_GROUP_DIM),
                                lambda i: (i // spt, 0, i % spt, 0)),
                   row(DT_PAD)],
        out_shape=[jax.ShapeDtypeStruct((t, ATT_IN_COLS), BF16),
                   jax.ShapeDtypeStruct((t, CONV_IN_COLS), F32),
                   jax.ShapeDtypeStruct((t, SSM_INNER), F32),
                   jax.ShapeDtypeStruct((t, SSM_CONV_CH), F32),
                   jax.ShapeDtypeStruct((bsz, FNET_GROUPS, seq, FNET_GROUP_DIM), F32),
                   jax.ShapeDtypeStruct((t, DT_PAD), F32)],
        compiler_params=_cparams(("parallel",)),
        name="inproj",
    )(h, g, w)


def _mix_kernel(h_ref, g_ref, b0_ref, b1_ref, b2_ref, b3_ref, wbr_ref, wg_ref, cg_ref, wo_ref, o_ref):
    h = h_ref[...]
    xn = _rms(h, g_ref[...]).astype(BF16)
    acc = None
    for b, br_ref in enumerate((b0_ref, b1_ref, b2_ref, b3_ref)):
        gate = jax.nn.sigmoid(jnp.dot(xn, wg_ref[b], preferred_element_type=F32) + cg_ref[b])
        br = jnp.dot(br_ref[...], wbr_ref[b], preferred_element_type=F32)
        acc = gate * br if acc is None else acc + gate * br
    o_ref[...] = h + jnp.dot(acc.astype(BF16), wo_ref[...], preferred_element_type=F32)


def _mix(h, g, branches, wbr, wg, cg, wo):
    t, d = h.shape
    tm = 512
    row = lambda n: pl.BlockSpec((tm, n), lambda i: (i, 0))
    return pl.pallas_call(
        _mix_kernel,
        grid=(t // tm,),
        in_specs=[row(d), _const_spec((1, d))] + [row(b.shape[1]) for b in branches]
                 + [_const_spec(wbr.shape), _const_spec(wg.shape), _const_spec(cg.shape),
                    _const_spec(wo.shape)],
        out_specs=row(d),
        out_shape=jax.ShapeDtypeStruct((t, d), F32),
        compiler_params=_cparams(("parallel",)),
        name="mix",
    )(h, g, *branches, wbr, wg, cg, wo)


def _ple(h2, p_ref, wpg_ref, wpp_ref):
    gate = jax.nn.sigmoid(jnp.dot(h2.astype(BF16), wpg_ref[...], preferred_element_type=F32))
    pe = jnp.dot(p_ref[...].astype(BF16), wpp_ref[...], preferred_element_type=F32)
    return h2 + gate * pe


def _swiglu_partial(xn, w1, w3, w2, scale=None):
    a = jnp.dot(xn, w1, preferred_element_type=F32)
    b = jnp.dot(xn, w3, preferred_element_type=F32)
    hid = a * jax.nn.sigmoid(a) * b
    if scale is not None:
        hid = hid * scale
    return jnp.dot(hid.astype(BF16), w2, preferred_element_type=F32)


def _ffn_kernel(h_ref, g_ref, w1_ref, w3_ref, w2_ref, p_ref, wpg_ref, wpp_ref, o_ref, xn_ref):
    j = pl.program_id(1)

    @pl.when(j == 0)
    def _():
        h = h_ref[...]
        xn_ref[...] = _rms(h, g_ref[...]).astype(BF16)
        o_ref[...] = h

    o_ref[...] += _swiglu_partial(xn_ref[...], w1_ref[...], w3_ref[...], w2_ref[...])

    @pl.when(j == pl.num_programs(1) - 1)
    def _():
        o_ref[...] = _ple(o_ref[...], p_ref, wpg_ref, wpp_ref)


def _ffn(h, g, w1, w3, w2, p, wpg, wpp):
    t, d = h.shape
    f = w1.shape[1]
    tm, tf = 512, 1408
    row = lambda n: pl.BlockSpec((tm, n), lambda i, j: (i, 0))
    return pl.pallas_call(
        _ffn_kernel,
        grid=(t // tm, f // tf),
        in_specs=[row(d), _const_spec((1, d)),
                  pl.BlockSpec((d, tf), lambda i, j: (0, j)),
                  pl.BlockSpec((d, tf), lambda i, j: (0, j)),
                  pl.BlockSpec((tf, d), lambda i, j: (j, 0)),
                  row(p.shape[1]), _const_spec(wpg.shape), _const_spec(wpp.shape)],
        out_specs=row(d),
        out_shape=jax.ShapeDtypeStruct((t, d), F32),
        scratch_shapes=[pltpu.VMEM((tm, d), BF16)],
        compiler_params=_cparams(("parallel", "arbitrary")),
        name="ffn",
    )(h, g, w1, w3, w2, p, wpg, wpp)


def _route(logits):
    ne = logits.shape[1]
    lane = lax.broadcasted_iota(jnp.int32, logits.shape, 1)
    m1 = jnp.max(logits, axis=-1, keepdims=True)
    i1 = jnp.min(jnp.where(logits == m1, lane, ne), axis=-1, keepdims=True)
    rest = jnp.where(lane == i1, -jnp.inf, logits)
    m2 = jnp.max(rest, axis=-1, keepdims=True)
    i2 = jnp.min(jnp.where(rest == m2, lane, ne), axis=-1, keepdims=True)
    e2 = jnp.exp(m2 - m1)
    den = 1.0 + e2
    return jnp.where(lane == i1, 1.0 / den, 0.0) + jnp.where(lane == i2, e2 / den, 0.0)


def _moe_dense_kernel(h_ref, g_ref, r_ref, w1_ref, w3_ref, w2_ref, p_ref, wpg_ref, wpp_ref,
                      o_ref, xn_ref, comb_ref):
    e = pl.program_id(1)
    j = pl.program_id(2)

    @pl.when((e == 0) & (j == 0))
    def _():
        h = h_ref[...]
        xn = _rms(h, g_ref[...])
        xn_ref[...] = xn.astype(BF16)
        logits = jnp.dot(xn, r_ref[...], preferred_element_type=F32,
                         precision=lax.Precision.HIGHEST)
        comb_ref[...] = _route(logits)
        o_ref[...] = h

    comb = comb_ref[...]
    lane = lax.broadcasted_iota(jnp.int32, comb.shape, 1)
    scale = jnp.sum(jnp.where(lane == e, comb, 0.0), axis=-1, keepdims=True)
    o_ref[...] += _swiglu_partial(xn_ref[...], w1_ref[0], w3_ref[0], w2_ref[0], scale)

    @pl.when((e == pl.num_programs(1) - 1) & (j == pl.num_programs(2) - 1))
    def _():
        o_ref[...] = _ple(o_ref[...], p_ref, wpg_ref, wpp_ref)


def _moe_dense(h, g, router, w1, w3, w2, p, wpg, wpp):
    t, d = h.shape
    ne, _, f = w1.shape
    tm, tf = 512, 896
    row = lambda n: pl.BlockSpec((tm, n), lambda i, e, j: (i, 0))
    return pl.pallas_call(
        _moe_dense_kernel,
        grid=(t // tm, ne, f // tf),
        in_specs=[row(d), _const_spec((1, d)), _const_spec(router.shape),
                  pl.BlockSpec((1, d, tf), lambda i, e, j: (e, 0, j)),
                  pl.BlockSpec((1, d, tf), lambda i, e, j: (e, 0, j)),
                  pl.BlockSpec((1, tf, d), lambda i, e, j: (e, j, 0)),
                  row(p.shape[1]), _const_spec(wpg.shape), _const_spec(wpp.shape)],
        out_specs=row(d),
        out_shape=jax.ShapeDtypeStruct((t, d), F32),
        scratch_shapes=[pltpu.VMEM((tm, d), BF16), pltpu.VMEM((tm, ne), F32)],
        compiler_params=_cparams(("parallel", "arbitrary", "arbitrary")),
        name="moe",
    )(h, g, router, w1, w3, w2, p, wpg, wpp)


def _final_norm_kernel(h_ref, g_ref, o_ref):
    o_ref[...] = _rms(h_ref[...], g_ref[...])


def _final_norm(h, g):
    t, d = h.shape
    tm = 1024
    row = pl.BlockSpec((tm, d), lambda i: (i, 0))
    return pl.pallas_call(
        _final_norm_kernel, grid=(t // tm,),
        in_specs=[row, _const_spec((1, d))], out_specs=row,
        out_shape=jax.ShapeDtypeStruct((t, d), F32),
        compiler_params=_cparams(("parallel",)), name="final_norm",
    )(h, g)


def _depthwise_conv(x, w, b):
    y = lax.conv_general_dilated(
        x, w[:, None, :].astype(x.dtype), window_strides=(1,), padding='SAME',
        dimension_numbers=('NWC', 'WIO', 'NWC'), feature_group_count=x.shape[-1])
    return y + b.astype(x.dtype)


def _layernorm(x, g, b):
    mu = jnp.mean(x, axis=-1, keepdims=True)
    var = jnp.mean(jnp.square(x - mu), axis=-1, keepdims=True)
    return (x - mu) * lax.rsqrt(var + EPS) * g + b


def _t5_bucket(rel):
    half = REL_BUCKETS // 2
    max_exact = half // 2
    n = np.abs(rel)
    large = max_exact + (np.log(np.maximum(n, 1) / max_exact) / math.log(REL_MAX_DIST / max_exact)
                         * (half - max_exact)).astype(np.int32)
    large = np.minimum(large, half - 1)
    return np.where(rel > 0, half, 0) + np.where(n < max_exact, n, large)


def _dilated_window_attention(q, k, v, bias, dilation, half):
    bsz, S, H, Dh = q.shape
    L = S // dilation
    W = half
    nb = -(-L // W)
    Lp = nb * W

    def to_sub(t):
        t = t.reshape(bsz, L, dilation, H, Dh).transpose(0, 2, 1, 3, 4)
        return jnp.pad(t, ((0, 0), (0, 0), (0, Lp - L), (0, 0), (0, 0)))

    def band(t):
        t = jnp.pad(to_sub(t), ((0, 0), (0, 0), (W, W), (0, 0), (0, 0)))
        t = t.reshape(bsz, dilation, nb + 2, W, H, Dh)
        return jnp.concatenate([t[:, :, :-2], t[:, :, 1:-1], t[:, :, 2:]], axis=3)

    qb = to_sub(q).reshape(bsz, dilation, nb, W, H, Dh)
    kb, vb = band(k), band(v)
    qi = np.arange(W)[:, None]
    kj = np.arange(3 * W)[None, :] - W
    rel = kj - qi
    key_pos = np.arange(nb)[:, None, None] * W + kj[None]
    valid = (np.abs(rel) <= W)[None] & (key_pos >= 0) & (key_pos < L)
    s = jnp.einsum('brnqhd,brnkhd->brnhqk', qb, kb, preferred_element_type=jnp.float32)
    s = s * (Dh ** -0.5) + bias.astype(jnp.float32)[None, None, None]
    s = jnp.where(valid[None, None, :, None], s, -1e30)
    m = jnp.max(s, axis=-1, keepdims=True)
    e = jnp.exp(s - m)
    den = jnp.sum(e, axis=-1)
    o = jnp.einsum('brnhqk,brnkhd->brnqhd', e, vb.astype(jnp.float32))
    o = o / den.transpose(0, 1, 2, 4, 3)[..., None]
    lse = (m[..., 0] + jnp.log(den)).transpose(0, 1, 2, 4, 3)

    def from_sub(t):
        t = t.reshape((bsz, dilation, Lp) + t.shape[4:])[:, :, :L]
        t = jnp.moveaxis(t, 1, 2)
        return t.reshape((bsz, S) + t.shape[3:])

    return from_sub(o), from_sub(lse)


def _attention_mixer(qkv, rel_bias):
    bsz, S, _ = qkv.shape
    qkv = qkv.reshape(bsz, S, 3, ATT_GROUPS, ATT_HEADS, ATT_HEAD_DIM)
    outs, lses = [], []
    for g, (window, dil) in enumerate(ATT_PATTERNS):
        half = window // (2 * dil)
        rel = np.arange(3 * half)[None, :] - half - np.arange(half)[:, None]
        bucket = _t5_bucket(dil * rel)
        bias = rel_bias[bucket][:, :, g * ATT_HEADS:(g + 1) * ATT_HEADS].transpose(2, 0, 1)
        o, lse = _dilated_window_attention(qkv[:, :, 0, g], qkv[:, :, 1, g], qkv[:, :, 2, g],
                                           bias, dil, half)
        outs.append(o)
        lses.append(lse)
    wts = jax.nn.softmax(jnp.stack(lses, axis=2), axis=2)
    o = jnp.einsum('bsghd,bsgh->bshd', jnp.stack(outs, axis=2), wts)
    return o.reshape(bsz, S, ATT_WIDTH)


def _conformer_mixer(u, dw, dw_b, ln_g, ln_b):
    a, gate = jnp.split(u, 2, axis=-1)
    hid = a * jax.nn.sigmoid(gate)
    hid = _depthwise_conv(hid, dw, dw_b)
    return jax.nn.silu(_layernorm(hid, ln_g, ln_b))


def _ssd_scan(x, dt, a_log, b_in, c_in):
    bsz, L = x.shape[:2]
    nc = L // SSM_CHUNK
    hg = SSM_HEADS // SSM_GROUPS
    A = -jnp.exp(a_log.astype(jnp.float32)).reshape(SSM_GROUPS, hg)
    dt = dt.reshape(bsz, nc, SSM_CHUNK, SSM_GROUPS, hg)
    xd = x.reshape(bsz, nc, SSM_CHUNK, SSM_GROUPS, hg, SSM_HEAD_DIM) * dt[..., None]
    bb = b_in.reshape(bsz, nc, SSM_CHUNK, SSM_GROUPS, SSM_STATE)
    cc = c_in.reshape(bsz, nc, SSM_CHUNK, SSM_GROUPS, SSM_STATE)
    a_cs = jnp.cumsum(dt * A, axis=2)
    seg = a_cs[:, :, :, None] - a_cs[:, :, None, :]
    lower = np.tril(np.ones((SSM_CHUNK, SSM_CHUNK), dtype=bool))[:, :, None, None]
    decay = jnp.exp(jnp.where(lower, seg, -jnp.inf))
    scores = jnp.einsum('bclgn,bcsgn->bclsg', cc, bb)
    y_diag = jnp.einsum('bclsg,bclsgh,bcsghp->bclghp', scores, decay, xd)
    decay_st = jnp.exp(a_cs[:, :, -1:] - a_cs)
    states = jnp.einsum('bclgn,bclgh,bclghp->bcghpn', bb, decay_st, xd)
    chunk_decay = jnp.exp(a_cs[:, :, -1])

    def step(carry, inp):
        st, dec = inp
        return carry * dec[..., None, None] + st, carry

    init = jnp.zeros((bsz, SSM_GROUPS, hg, SSM_HEAD_DIM, SSM_STATE), jnp.float32)
    _, prev = lax.scan(step, init, (jnp.moveaxis(states, 1, 0), jnp.moveaxis(chunk_decay, 1, 0)))
    prev = jnp.moveaxis(prev, 0, 1)
    y_off = jnp.einsum('bclgn,bcghpn,bclgh->bclghp', cc, prev, jnp.exp(a_cs))
    return (y_diag + y_off).reshape(bsz, L, SSM_HEADS, SSM_HEAD_DIM)


def _ssd_mixer(z, xbc, dt_raw, conv_w, conv_b, a_log, dt_bias, d_skip, norm_g):
    bsz, L, _ = z.shape
    xbc = jax.nn.silu(_depthwise_conv(xbc, conv_w, conv_b))
    xs = xbc[..., :SSM_INNER].reshape(bsz, L, SSM_HEADS, SSM_HEAD_DIM)
    bs = xbc[..., SSM_INNER:SSM_INNER + SSM_GROUPS * SSM_STATE].reshape(bsz, L, SSM_GROUPS, SSM_STATE)
    cs = xbc[..., SSM_INNER + SSM_GROUPS * SSM_STATE:].reshape(bsz, L, SSM_GROUPS, SSM_STATE)
    dt = jax.nn.softplus(dt_raw.reshape(bsz, L, 2, SSM_HEADS) + dt_bias)
    flip = lambda t: jnp.flip(t, axis=1)
    y_fwd = _ssd_scan(xs, dt[:, :, 0], a_log[0], bs, cs)
    y_bwd = flip(_ssd_scan(flip(xs), flip(dt[:, :, 1]), a_log[1], flip(bs), flip(cs)))
    y = y_fwd + y_bwd + d_skip[:, None] * xs
    y = y.reshape(bsz, L, SSM_INNER) * jax.nn.silu(z)
    return _rms(y, norm_g)


def _fourier_mixer(fn):
    bsz, g, S, c = fn.shape
    f = jnp.fft.fft2(fn, axes=(2, 3), norm='ortho').real
    return f.transpose(0, 2, 1, 3).reshape(bsz, S, g * c)


def kernel(x, p, rel_bias, norm_mix, w_in, conv_dw, conv_dw_b, conv_ln_g, conv_ln_b, conv_out,
           ssm_conv_w, ssm_conv_b, ssm_a_log, ssm_dt_bias, ssm_d, ssm_norm, ssm_out,
           attn_out, fnet_out, w_gate, b_gate, w_out, norm_ffn, ffn_w1, ffn_w3, ffn_w2,
           moe_router, moe_w1, moe_w3, moe_w2, ple_gate, ple_proj, final_norm):
    bsz, seq, d = x.shape
    depth = w_in.shape[0]
    t = bsz * seq
    h = x.reshape(t, d)
    for l in range(depth):
        qkv, conv_u, z, xbc, fn, dt = _inproj(h, norm_mix[l][None], _reorder_w_in(w_in[l]), bsz, seq)
        att = _attention_mixer(qkv.astype(F32).reshape(bsz, seq, -1), rel_bias)
        cnf = _conformer_mixer(conv_u.reshape(bsz, seq, -1), conv_dw[l], conv_dw_b[l],
                               conv_ln_g[l], conv_ln_b[l])
        ssd = _ssd_mixer(z.reshape(bsz, seq, -1), xbc.reshape(bsz, seq, -1),
                         dt[:, :2 * SSM_HEADS].reshape(bsz, seq, -1), ssm_conv_w[l], ssm_conv_b[l],
                         ssm_a_log[l], ssm_dt_bias[l], ssm_d[l], ssm_norm[l])
        fnt = _fourier_mixer(fn)
        branches = [b.reshape(t, -1).astype(BF16) for b in (att, cnf, ssd, fnt)]
        wbr = jnp.stack([attn_out[l], conv_out[l], ssm_out[l], fnet_out[l]]).astype(BF16)
        h = _mix(h, norm_mix[l][None], branches, wbr, w_gate[l].astype(BF16), b_gate[l][:, None, :],
                 w_out[l].astype(BF16))
        pl_in = p[l].reshape(t, -1)
        wpg, wpp = ple_gate[l].astype(BF16), ple_proj[l].astype(BF16)
        if l % 2 == 0:
            i = l // 2
            h = _ffn(h, norm_ffn[l][None], ffn_w1[i].astype(BF16), ffn_w3[i].astype(BF16),
                     ffn_w2[i].astype(BF16), pl_in, wpg, wpp)
        else:
            i = l // 2
            h = _moe_dense(h, norm_ffn[l][None], moe_router[i], moe_w1[i].astype(BF16),
                           moe_w3[i].astype(BF16), moe_w2[i].astype(BF16), pl_in, wpg, wpp)
    return _final_norm(h, final_norm[None]).reshape(bsz, seq, d)
```

```python
import functools
import math

import numpy as np
import jax
import jax.numpy as jnp
from jax import lax
from jax.experimental import pallas as pl
from jax.experimental.pallas import tpu as pltpu

F32 = jnp.float32
BF16 = jnp.bfloat16

EPS = 1e-6
N_BRANCHES = 4
CONV_CH = 512
CONV_K = 31
SSM_HEADS = 8
SSM_HEAD_DIM = 64
SSM_INNER = SSM_HEADS * SSM_HEAD_DIM
SSM_GROUPS = 2
SSM_STATE = 64
SSM_CONV = 5
SSM_CONV_CH = SSM_INNER + 2 * SSM_GROUPS * SSM_STATE
SSM_CHUNK = 128
ATT_PATTERNS = ((128, 1), (512, 4), (2048, 16))
ATT_GROUPS = len(ATT_PATTERNS)
ATT_HEADS = 8
ATT_HEAD_DIM = 64
ATT_WIDTH = ATT_HEADS * ATT_HEAD_DIM
REL_BUCKETS = 32
REL_MAX_DIST = 1024
FNET_GROUPS = 4
FNET_GROUP_DIM = 128
FNET_WIDTH = FNET_GROUPS * FNET_GROUP_DIM
N_EXPERTS = 8
TOP_K = 2

ATT_IN_COLS = 3 * ATT_GROUPS * ATT_WIDTH
CONV_IN_COLS = 2 * CONV_CH
SSM_IN_COLS = SSM_INNER + SSM_CONV_CH + 2 * SSM_HEADS
OFF_CONV = ATT_IN_COLS
OFF_SSM = OFF_CONV + CONV_IN_COLS
OFF_FNET = OFF_SSM + SSM_IN_COLS

LANES = 128
DT_PAD = LANES
VMEM_LIMIT = 56 * 1024 * 1024


def _cparams(sem):
    return pltpu.CompilerParams(dimension_semantics=sem, vmem_limit_bytes=VMEM_LIMIT)


def _const_spec(shape):
    nd = len(shape)
    return pl.BlockSpec(shape, lambda *_: (0,) * nd, pipeline_mode=pl.Buffered(1))


def _rms(x, g):
    return x * lax.rsqrt(jnp.mean(x * x, axis=-1, keepdims=True) + EPS) * g


_SEC_QKV = (0, ATT_IN_COLS)
_SEC_CONV = (_SEC_QKV[0] + _SEC_QKV[1], CONV_IN_COLS)
_SEC_Z = (_SEC_CONV[0] + _SEC_CONV[1], SSM_INNER)
_SEC_XBC = (_SEC_Z[0] + _SEC_Z[1], SSM_CONV_CH)
_SEC_FNET = (_SEC_XBC[0] + _SEC_XBC[1], FNET_WIDTH)
_SEC_DT = (_SEC_FNET[0] + _SEC_FNET[1], DT_PAD)
_IN_COLS_PAD = _SEC_DT[0] + _SEC_DT[1]
_MM_CHUNK = 512


def _reorder_w_in(w):
    d = w.shape[0]
    ssm = w[:, OFF_SSM:OFF_FNET]
    dt = ssm[:, SSM_INNER + SSM_CONV_CH:]
    parts = [w[:, :OFF_CONV], w[:, OFF_CONV:OFF_SSM], ssm[:, :SSM_INNER],
             ssm[:, SSM_INNER:SSM_INNER + SSM_CONV_CH], w[:, OFF_FNET:],
             dt, jnp.zeros((d, DT_PAD - dt.shape[1]), w.dtype)]
    return jnp.concatenate(parts, axis=1).astype(BF16)


def _inproj_kernel(h_ref, g_ref, w_ref, qkv_ref, conv_ref, z_ref, xbc_ref, fn_ref, dt_ref):
    xn = _rms(h_ref[...], g_ref[...]).astype(BF16)

    def section(sec, store):
        start, width = sec
        for c in range(0, width, _MM_CHUNK):
            cw = min(_MM_CHUNK, width - c)
            store(c, cw, jnp.dot(xn, w_ref[:, start + c:start + c + cw],
                                 preferred_element_type=F32))

    def to(ref):
        def store(c, cw, val):
            ref[:, c:c + cw] = val.astype(ref.dtype)
        return store

    def to_fnet(c, cw, val):
        for g in range(cw // FNET_GROUP_DIM):
            fn_ref[0, c // FNET_GROUP_DIM + g] = val[:, g * FNET_GROUP_DIM:(g + 1) * FNET_GROUP_DIM]

    section(_SEC_QKV, to(qkv_ref))
    section(_SEC_CONV, to(conv_ref))
    section(_SEC_Z, to(z_ref))
    section(_SEC_XBC, to(xbc_ref))
    section(_SEC_FNET, to_fnet)
    section(_SEC_DT, to(dt_ref))


def _inproj(h, g, w, bsz, seq):
    t, d = h.shape
    tm = 256
    spt = seq // tm
    row = lambda n: pl.BlockSpec((tm, n), lambda i: (i, 0))
    return pl.pallas_call(
        _inproj_kernel,
        grid=(t // tm,),
        in_specs=[row(d), _const_spec((1, d)), _const_spec(w.shape)],
        out_specs=[row(ATT_IN_COLS), row(CONV_IN_COLS), row(SSM_INNER), row(SSM_CONV_CH),
                   pl.BlockSpec((1, FNET_GROUPS, tm, FNET_GROUP_DIM),
                                lambda i: (i // spt, 0, i % spt, 0)),
                   row(DT_PAD)],
        out_shape=[jax.ShapeDtypeStruct((t, ATT_IN_COLS), BF16),
                   jax.ShapeDtypeStruct((t, CONV_IN_COLS), F32),
                   jax.ShapeDtypeStruct((t, SSM_INNER), F32),
                   jax.ShapeDtypeStruct((t, SSM_CONV_CH), F32),
                   jax.ShapeDtypeStruct((bsz, FNET_GROUPS, seq, FNET_GROUP_DIM), F32),
                   jax.ShapeDtypeStruct((t, DT_PAD), F32)],
        compiler_params=_cparams(("parallel",)),
        name="inproj",
    )(h, g, w)


def _split_dot(v, m):
    hi = v.astype(BF16)
    lo = (v - hi.astype(F32)).astype(BF16)
    return (jnp.dot(hi, m, preferred_element_type=F32) + jnp.dot(lo, m, preferred_element_type=F32))


def _mix_kernel(h_ref, g_ref, o0_ref, o1_ref, o2_ref, l0_ref, l1_ref, l2_ref, b1_ref, b2_ref,
                b3_ref, hx_ref, wbr_ref, wg_ref, cg_ref, wo_ref, o_ref):
    h = h_ref[...]
    xn = _rms(h, g_ref[...]).astype(BF16)
    lses = [l0_ref[...], l1_ref[...], l2_ref[...]]
    top = jnp.maximum(jnp.maximum(lses[0], lses[1]), lses[2])
    es = [jnp.exp(l - top) for l in lses]
    inv = 1.0 / (es[0] + es[1] + es[2])
    att = None
    for e, og_ref in zip(es, (o0_ref, o1_ref, o2_ref)):
        term = og_ref[...].astype(F32) * _split_dot(e * inv, hx_ref[...])
        att = term if att is None else att + term
    acc = None
    for b, hid in enumerate((att.astype(BF16), b1_ref[...], b2_ref[...], b3_ref[...])):
        gate = jax.nn.sigmoid(jnp.dot(xn, wg_ref[b], preferred_element_type=F32) + cg_ref[b])
        br = jnp.dot(hid, wbr_ref[b], preferred_element_type=F32)
        acc = gate * br if acc is None else acc + gate * br
    o_ref[...] = h + jnp.dot(acc.astype(BF16), wo_ref[...], preferred_element_type=F32)


def _mix(h, g, att, others, wbr, wg, cg, wo):
    t, d = h.shape
    tm = 512
    row = lambda n: pl.BlockSpec((tm, n), lambda i: (i, 0))
    head_expand = np.zeros((LANES, ATT_WIDTH), np.float32)
    for hd in range(ATT_HEADS):
        head_expand[hd, hd * ATT_HEAD_DIM:(hd + 1) * ATT_HEAD_DIM] = 1.0
    head_expand = jnp.asarray(head_expand, BF16)
    outs = [o for o, _ in att]
    lses = [l for _, l in att]
    return pl.pallas_call(
        _mix_kernel,
        grid=(t // tm,),
        in_specs=[row(d), _const_spec((1, d))] + [row(a.shape[1]) for a in outs + lses + list(others)]
                 + [_const_spec(head_expand.shape), _const_spec(wbr.shape), _const_spec(wg.shape),
                    _const_spec(cg.shape), _const_spec(wo.shape)],
        out_specs=row(d),
        out_shape=jax.ShapeDtypeStruct((t, d), F32),
        compiler_params=_cparams(("parallel",)),
        name="mix",
    )(h, g, *outs, *lses, *others, head_expand, wbr, wg, cg, wo)


def _ple(h2, p_ref, wpg_ref, wpp_ref):
    gate = jax.nn.sigmoid(jnp.dot(h2.astype(BF16), wpg_ref[...], preferred_element_type=F32))
    pe = jnp.dot(p_ref[...].astype(BF16), wpp_ref[...], preferred_element_type=F32)
    return h2 + gate * pe


def _swiglu_partial(xn, w1, w3, w2, scale=None):
    a = jnp.dot(xn, w1, preferred_element_type=F32)
    b = jnp.dot(xn, w3, preferred_element_type=F32)
    hid = a * jax.nn.sigmoid(a) * b
    if scale is not None:
        hid = hid * scale
    return jnp.dot(hid.astype(BF16), w2, preferred_element_type=F32)


def _ffn_kernel(h_ref, g_ref, w1_ref, w3_ref, w2_ref, p_ref, wpg_ref, wpp_ref, o_ref, xn_ref):
    j = pl.program_id(1)

    @pl.when(j == 0)
    def _():
        h = h_ref[...]
        xn_ref[...] = _rms(h, g_ref[...]).astype(BF16)
        o_ref[...] = h

    o_ref[...] += _swiglu_partial(xn_ref[...], w1_ref[...], w3_ref[...], w2_ref[...])

    @pl.when(j == pl.num_programs(1) - 1)
    def _():
        o_ref[...] = _ple(o_ref[...], p_ref, wpg_ref, wpp_ref)


def _ffn(h, g, w1, w3, w2, p, wpg, wpp):
    t, d = h.shape
    f = w1.shape[1]
    tm, tf = 512, 1408
    row = lambda n: pl.BlockSpec((tm, n), lambda i, j: (i, 0))
    return pl.pallas_call(
        _ffn_kernel,
        grid=(t // tm, f // tf),
        in_specs=[row(d), _const_spec((1, d)),
                  pl.BlockSpec((d, tf), lambda i, j: (0, j)),
                  pl.BlockSpec((d, tf), lambda i, j: (0, j)),
                  pl.BlockSpec((tf, d), lambda i, j: (j, 0)),
                  row(p.shape[1]), _const_spec(wpg.shape), _const_spec(wpp.shape)],
        out_specs=row(d),
        out_shape=jax.ShapeDtypeStruct((t, d), F32),
        scratch_shapes=[pltpu.VMEM((tm, d), BF16)],
        compiler_params=_cparams(("parallel", "arbitrary")),
        name="ffn",
    )(h, g, w1, w3, w2, p, wpg, wpp)


MOE_ROWS = 128


def _moe_route(logits):
    ne, tm = logits.shape
    eidx = lax.broadcasted_iota(jnp.int32, logits.shape, 0)
    m1 = jnp.max(logits, axis=0, keepdims=True)
    i1 = jnp.min(jnp.where(logits == m1, eidx, ne), axis=0, keepdims=True)
    rest = jnp.where(eidx == i1, -jnp.inf, logits)
    m2 = jnp.max(rest, axis=0, keepdims=True)
    i2 = jnp.min(jnp.where(rest == m2, eidx, ne), axis=0, keepdims=True)
    e2 = jnp.exp(m2 - m1)
    den = 1.0 + e2
    combine = jnp.where(eidx == i1, 1.0 / den, 0.0) + jnp.where(eidx == i2, e2 / den, 0.0)
    routed = jnp.where((eidx == i1) | (eidx == i2), 1.0, 0.0)
    r = lax.broadcasted_iota(jnp.int32, (LANES, LANES), 0)
    c = lax.broadcasted_iota(jnp.int32, (LANES, LANES), 1)
    before = jnp.where(r < c, 1.0, 0.0).astype(BF16)
    counts = jnp.zeros((ne, 1), F32)
    slots = []
    for k in range(tm // LANES):
        blk = routed[:, k * LANES:(k + 1) * LANES]
        slots.append(jnp.dot(blk.astype(BF16), before, preferred_element_type=F32) + counts)
        counts = counts + jnp.sum(blk, axis=1, keepdims=True)
    slot = jnp.where(routed > 0.0, jnp.concatenate(slots, axis=1), -1.0).astype(jnp.int32)
    return combine, slot, counts


def _moe_kernel(h_ref, g_ref, rt_ref, w1_ref, w3_ref, w2_ref, p_ref, wpg_ref, wpp_ref, o_ref,
                xn_ref, comb_ref, slot_ref, cnt_ref, xe_ref, ye_ref):
    e = pl.program_id(1)
    j = pl.program_id(2)
    ne = pl.num_programs(1)
    tm = xn_ref.shape[0]

    @pl.when((e == 0) & (j == 0))
    def _():
        h = h_ref[...]
        xn = _rms(h, g_ref[...])
        xn_ref[...] = xn.astype(BF16)
        logits = lax.dot_general(rt_ref[...], xn, (((1,), (1,)), ((), ())),
                                 preferred_element_type=F32, precision=HI)
        combine, slot, counts = _moe_route(logits)
        comb_ref[...] = combine
        slot_ref[...] = slot
        for k in range(comb_ref.shape[0]):
            cnt_ref[k] = jnp.sum(counts[k:k + 1, :]).astype(jnp.int32)
        o_ref[...] = h

    n_blocks = (cnt_ref[e] + MOE_ROWS - 1) // MOE_ROWS
    sub = lax.broadcasted_iota(jnp.int32, (MOE_ROWS, tm), 0)

    def one_hot(blk):
        return slot_ref[pl.ds(e, 1), :] == sub + blk * MOE_ROWS

    @pl.when(j == 0)
    def _():
        def gather(blk, carry):
            r0 = pl.multiple_of(blk * MOE_ROWS, MOE_ROWS)
            sel = jnp.where(one_hot(blk), 1.0, 0.0).astype(BF16)
            xe_ref[pl.ds(r0, MOE_ROWS), :] = jnp.dot(
                sel, xn_ref[...], preferred_element_type=F32).astype(BF16)
            ye_ref[pl.ds(r0, MOE_ROWS), :] = jnp.zeros((MOE_ROWS, ye_ref.shape[1]), F32)
            return carry
        lax.fori_loop(0, n_blocks, gather, 0)

    def expert(blk, carry):
        r0 = pl.multiple_of(blk * MOE_ROWS, MOE_ROWS)
        ye_ref[pl.ds(r0, MOE_ROWS), :] += _swiglu_partial(
            xe_ref[pl.ds(r0, MOE_ROWS), :], w1_ref[0], w3_ref[0], w2_ref[0])
        return carry
    lax.fori_loop(0, n_blocks, expert, 0)

    @pl.when(j == pl.num_programs(2) - 1)
    def _():
        def scatter(blk, carry):
            r0 = pl.multiple_of(blk * MOE_ROWS, MOE_ROWS)
            hot = one_hot(blk)
            weight = jnp.sum(jnp.where(hot, comb_ref[pl.ds(e, 1), :], 0.0), axis=1, keepdims=True)
            yw = (ye_ref[pl.ds(r0, MOE_ROWS), :] * weight).astype(BF16)
            o_ref[...] += lax.dot_general(jnp.where(hot, 1.0, 0.0).astype(BF16), yw,
                                          (((0,), (0,)), ((), ())), preferred_element_type=F32)
            return carry
        lax.fori_loop(0, n_blocks, scatter, 0)

    @pl.when((e == ne - 1) & (j == pl.num_programs(2) - 1))
    def _():
        o_ref[...] = _ple(o_ref[...], p_ref, wpg_ref, wpp_ref)


def _moe(h, g, router, w1, w3, w2, p, wpg, wpp):
    t, d = h.shape
    ne, _, f = w1.shape
    tm, tf = 1024, 896
    row = lambda n: pl.BlockSpec((tm, n), lambda i, e, j: (i, 0))
    return pl.pallas_call(
        _moe_kernel,
        grid=(t // tm, ne, f // tf),
        in_specs=[row(d), _const_spec((1, d)), _const_spec((ne, d)),
                  pl.BlockSpec((1, d, tf), lambda i, e, j: (e, 0, j)),
                  pl.BlockSpec((1, d, tf), lambda i, e, j: (e, 0, j)),
                  pl.BlockSpec((1, tf, d), lambda i, e, j: (e, j, 0)),
                  row(p.shape[1]), _const_spec(wpg.shape), _const_spec(wpp.shape)],
        out_specs=row(d),
        out_shape=jax.ShapeDtypeStruct((t, d), F32),
        scratch_shapes=[pltpu.VMEM((tm, d), BF16), pltpu.VMEM((ne, tm), F32),
                        pltpu.VMEM((ne, tm), jnp.int32), pltpu.SMEM((ne,), jnp.int32),
                        pltpu.VMEM((tm, d), BF16), pltpu.VMEM((tm, d), F32)],
        compiler_params=_cparams(("parallel", "arbitrary", "arbitrary")),
        name="moe",
    )(h, g, router.T, w1, w3, w2, p, wpg, wpp)


def _final_norm_kernel(h_ref, g_ref, o_ref):
    o_ref[...] = _rms(h_ref[...], g_ref[...])


def _final_norm(h, g):
    t, d = h.shape
    tm = 1024
    row = pl.BlockSpec((tm, d), lambda i: (i, 0))
    return pl.pallas_call(
        _final_norm_kernel, grid=(t // tm,),
        in_specs=[row, _const_spec((1, d))], out_specs=row,
        out_shape=jax.ShapeDtypeStruct((t, d), F32),
        compiler_params=_cparams(("parallel",)), name="final_norm",
    )(h, g)


ATT_HALF = 64
ATT_QB = 128
ATT_KB = ATT_QB + 2 * ATT_HALF
NEG = -1e30
assert all(w // (2 * d) == ATT_HALF for w, d in ATT_PATTERNS)


def _t5_bucket(rel):
    half = REL_BUCKETS // 2
    max_exact = half // 2
    n = np.abs(rel)
    large = max_exact + (np.log(np.maximum(n, 1) / max_exact) / math.log(REL_MAX_DIST / max_exact)
                         * (half - max_exact)).astype(np.int32)
    large = np.minimum(large, half - 1)
    return np.where(rel > 0, half, 0) + np.where(n < max_exact, n, large)


def _att_bias_tables(rel_bias, g, dil):
    i = np.arange(ATT_QB)[:, None]
    j = np.arange(ATT_KB)[None, :]
    rel = j - ATT_HALF - i
    band = np.abs(rel) <= ATT_HALF
    bias = rel_bias[_t5_bucket(dil * rel)][:, :, g * ATT_HEADS:(g + 1) * ATT_HEADS]
    bias = bias.transpose(2, 0, 1).astype(F32)
    tables = []
    for v in range(4):
        ok = band
        if v & 1:
            ok = ok & (j >= ATT_HALF)
        if v & 2:
            ok = ok & (j < ATT_QB + ATT_HALF)
        tables.append(jnp.where(ok[None], bias, NEG))
    return jnp.stack(tables)


def _attn_kernel(q_ref, k_ref, v_ref, kp_ref, kn_ref, vp_ref, vn_ref, bias_ref, o_ref, lse_ref,
                 kbuf, vbuf, *, tq, n_blocks):
    kbuf[0:ATT_HALF] = kp_ref[0]
    kbuf[ATT_HALF:ATT_HALF + tq] = k_ref[0]
    kbuf[ATT_HALF + tq:] = kn_ref[0]
    vbuf[0:ATT_HALF] = vp_ref[0]
    vbuf[ATT_HALF:ATT_HALF + tq] = v_ref[0]
    vbuf[ATT_HALF + tq:] = vn_ref[0]
    nsb = tq // ATT_QB
    first = pl.program_id(2) * nsb
    lane = lax.broadcasted_iota(jnp.int32, (ATT_QB, LANES), 1)
    low = lane < ATT_HEAD_DIM
    lane_row = lax.broadcasted_iota(jnp.int32, (1, LANES), 1)
    keep = [(lane_row < ATT_HEAD_DIM).astype(BF16), (lane_row >= ATT_HEAD_DIM).astype(BF16)]

    def block(sb, carry):
        r0 = pl.multiple_of(sb * ATT_QB, ATT_QB)
        gsb = first + sb
        variant = (gsb == 0).astype(jnp.int32) + 2 * (gsb == n_blocks - 1).astype(jnp.int32)
        q = q_ref[0, pl.ds(r0, ATT_QB), :] * (ATT_HEAD_DIM ** -0.5)
        lse_all = jnp.zeros((ATT_QB, LANES), F32)
        outs = []
        for pair in range(ATT_HEADS // 2):
            cols = slice(pair * LANES, (pair + 1) * LANES)
            qp = q[:, cols]
            kp = kbuf[pl.ds(r0, ATT_KB), cols]
            vp = vbuf[pl.ds(r0, ATT_KB), cols]
            res = []
            for half in range(2):
                h = 2 * pair + half
                s = lax.dot_general(qp * keep[half], kp, (((1,), (1,)), ((), ())),
                                    preferred_element_type=F32)
                s = s + bias_ref[variant, h]
                m = jnp.max(s, axis=-1, keepdims=True)
                e = jnp.exp(s - m)
                den = jnp.sum(e, axis=-1, keepdims=True)
                pv = jnp.dot(e.astype(BF16), vp, preferred_element_type=F32)
                res.append(pv / den)
                lse_all = jnp.where(lane == h, m + jnp.log(den), lse_all)
            outs.append(jnp.where(low, res[0], res[1]))
        o_ref[0, pl.ds(r0, ATT_QB), :] = jnp.concatenate(outs, axis=1).astype(o_ref.dtype)
        lse_ref[0, pl.ds(r0, ATT_QB), :] = lse_all
        return carry

    lax.fori_loop(0, nsb, block, 0)


def _attention_group(qkv, bias_tables, g, dil, bsz, seq):
    sub_len = seq // dil
    assert sub_len % ATT_QB == 0
    tq = min(512, sub_len)
    hb = tq // ATT_HALF
    n_hb = sub_len // ATT_HALF
    w = ATT_WIDTH
    ncol = qkv.shape[1] // w
    view = qkv.reshape(bsz, sub_len, dil * qkv.shape[1])

    def main(part):
        return pl.BlockSpec((1, tq, w), lambda b, r, n: (b, n, r * ncol + part * ATT_GROUPS + g))

    def halo(part, nxt):
        if nxt:
            row = lambda n: jnp.minimum((n + 1) * hb, n_hb - 1)
        else:
            row = lambda n: jnp.maximum(n * hb - 1, 0)
        return pl.BlockSpec((1, ATT_HALF, w),
                            lambda b, r, n: (b, row(n), r * ncol + part * ATT_GROUPS + g))

    o, lse = pl.pallas_call(
        functools.partial(_attn_kernel, tq=tq, n_blocks=sub_len // ATT_QB),
        grid=(bsz, dil, sub_len // tq),
        in_specs=[main(0), main(1), main(2), halo(1, False), halo(1, True), halo(2, False),
                  halo(2, True), _const_spec(bias_tables.shape)],
        out_specs=[pl.BlockSpec((1, tq, w), lambda b, r, n: (b, n, r)),
                   pl.BlockSpec((1, tq, LANES), lambda b, r, n: (b, n, r))],
        out_shape=[jax.ShapeDtypeStruct((bsz, sub_len, dil * w), BF16),
                   jax.ShapeDtypeStruct((bsz, sub_len, dil * LANES), F32)],
        scratch_shapes=[pltpu.VMEM((tq + 2 * ATT_HALF, w), BF16),
                        pltpu.VMEM((tq + 2 * ATT_HALF, w), BF16)],
        compiler_params=_cparams(("parallel", "parallel", "parallel")),
        name=f"attn{g}",
    )(view, view, view, view, view, view, view, bias_tables)
    return o.reshape(bsz * seq, w), lse.reshape(bsz * seq, LANES)


SUBLANES = 8
CONV_HALO = 16
CONV_ROWS = 64


def _conformer_kernel(u_ref, up_ref, un_ref, dw_ref, dwb_ref, lng_ref, lnb_ref, o_ref,
                      hp_ref, sh_ref, *, ts):
    n = pl.program_id(1)

    def glu(u):
        return u[:, :CONV_CH] * jax.nn.sigmoid(u[:, CONV_CH:])

    hp_ref[0:CONV_HALO] = jnp.where(n > 0, glu(up_ref[0]), 0.0)
    hp_ref[CONV_HALO:CONV_HALO + ts] = glu(u_ref[0])
    hp_ref[CONV_HALO + ts:] = jnp.where(n < pl.num_programs(1) - 1, glu(un_ref[0]), 0.0)
    span = ts + 2 * CONV_HALO - SUBLANES
    for b in range(SUBLANES):
        sh_ref[b] = hp_ref[pl.ds(b, span), :]
    first = CONV_HALO - CONV_K // 2

    def rows(c, carry):
        r0 = pl.multiple_of(c * CONV_ROWS, CONV_ROWS)
        acc = jnp.broadcast_to(dwb_ref[...], (CONV_ROWS, CONV_CH))
        for k in range(CONV_K):
            a, b = divmod(first + k, SUBLANES)
            acc = acc + dw_ref[pl.ds(k, 1), :] * sh_ref[b, pl.ds(r0 + a * SUBLANES, CONV_ROWS), :]
        mu = jnp.mean(acc, axis=-1, keepdims=True)
        cen = acc - mu
        var = jnp.mean(cen * cen, axis=-1, keepdims=True)
        y = cen * lax.rsqrt(var + EPS) * lng_ref[...] + lnb_ref[...]
        o_ref[0, pl.ds(r0, CONV_ROWS), :] = (y * jax.nn.sigmoid(y)).astype(o_ref.dtype)
        return carry

    lax.fori_loop(0, ts // CONV_ROWS, rows, 0)


def _conformer(u, dw, dw_b, ln_g, ln_b, bsz, seq):
    ts = min(512, seq)
    view = u.reshape(bsz, seq, 2 * CONV_CH)
    hb = ts // CONV_HALO
    n_hb = seq // CONV_HALO
    out = pl.pallas_call(
        functools.partial(_conformer_kernel, ts=ts),
        grid=(bsz, seq // ts),
        in_specs=[pl.BlockSpec((1, ts, 2 * CONV_CH), lambda b, n: (b, n, 0)),
                  pl.BlockSpec((1, CONV_HALO, 2 * CONV_CH),
                               lambda b, n: (b, jnp.maximum(n * hb - 1, 0), 0)),
                  pl.BlockSpec((1, CONV_HALO, 2 * CONV_CH),
                               lambda b, n: (b, jnp.minimum((n + 1) * hb, n_hb - 1), 0)),
                  _const_spec(dw.shape), _const_spec((1, CONV_CH)), _const_spec((1, CONV_CH)),
                  _const_spec((1, CONV_CH))],
        out_specs=pl.BlockSpec((1, ts, CONV_CH), lambda b, n: (b, n, 0)),
        out_shape=jax.ShapeDtypeStruct((bsz, seq, CONV_CH), BF16),
        scratch_shapes=[pltpu.VMEM((ts + 2 * CONV_HALO, CONV_CH), F32),
                        pltpu.VMEM((SUBLANES, ts + 2 * CONV_HALO - SUBLANES, CONV_CH), F32)],
        compiler_params=_cparams(("parallel", "parallel")),
        name="conformer",
    )(view, view, view, dw, dw_b[None], ln_g[None], ln_b[None])
    return out.reshape(bsz * seq, CONV_CH)


SSD_HALO = SUBLANES
SSD_ROWS = 64
SSD_BC = SSM_GROUPS * SSM_STATE
HEADS_PER_GROUP = SSM_HEADS // SSM_GROUPS
GROUP_LANES = HEADS_PER_GROUP * SSM_HEAD_DIM
HI = lax.Precision.HIGHEST
assert SSD_BC == LANES and SSM_CHUNK == LANES and 2 * SSM_HEADS <= LANES


def _ssd_pre_kernel(x_ref, xp_ref, xn_ref, dt_ref, cw_ref, cb_ref, dtb_ref, xo_ref, dto_ref,
                    hp_ref, sh_ref, *, ts, phases):
    n = pl.program_id(1)
    hp_ref[0:SSD_HALO] = jnp.where(n > 0, xp_ref[0], 0.0)
    hp_ref[SSD_HALO:SSD_HALO + ts] = x_ref[0]
    hp_ref[SSD_HALO + ts:] = jnp.where(n < pl.num_programs(1) - 1, xn_ref[0], 0.0)
    for i, b in enumerate(phases):
        sh_ref[i] = hp_ref[pl.ds(b, ts + SSD_HALO), :]
    first = SSD_HALO - SSM_CONV // 2

    def rows(c, carry):
        r0 = pl.multiple_of(c * SSD_ROWS, SSD_ROWS)
        acc = jnp.broadcast_to(cb_ref[...], (SSD_ROWS, SSM_CONV_CH))
        for k in range(SSM_CONV):
            a, b = divmod(first + k, SUBLANES)
            acc = acc + cw_ref[pl.ds(k, 1), :] * sh_ref[phases.index(b),
                                                        pl.ds(r0 + a * SUBLANES, SSD_ROWS), :]
        xo_ref[0, pl.ds(r0, SSD_ROWS), :] = acc * jax.nn.sigmoid(acc)
        return carry

    lax.fori_loop(0, ts // SSD_ROWS, rows, 0)
    x = dt_ref[0] + dtb_ref[...]
    softplus = jnp.maximum(x, 0.0) + jnp.log1p(jnp.exp(-jnp.abs(x)))
    lane = lax.broadcasted_iota(jnp.int32, x.shape, 1)
    dto_ref[0] = jnp.where(lane < 2 * SSM_HEADS, softplus, 0.0)


def _ssd_pre(xbc, dt_raw, conv_w, conv_b, dt_bias, bsz, seq):
    ts = min(512, seq)
    xv = xbc.reshape(bsz, seq, SSM_CONV_CH)
    dv = dt_raw.reshape(bsz, seq, DT_PAD)
    hb = ts // SSD_HALO
    n_hb = seq // SSD_HALO
    first = SSD_HALO - SSM_CONV // 2
    phases = tuple(sorted({(first + k) % SUBLANES for k in range(SSM_CONV)}))
    dtb = jnp.zeros((1, DT_PAD), F32).at[0, :2 * SSM_HEADS].set(dt_bias.reshape(-1))
    return pl.pallas_call(
        functools.partial(_ssd_pre_kernel, ts=ts, phases=phases),
        grid=(bsz, seq // ts),
        in_specs=[pl.BlockSpec((1, ts, SSM_CONV_CH), lambda b, n: (b, n, 0)),
                  pl.BlockSpec((1, SSD_HALO, SSM_CONV_CH),
                               lambda b, n: (b, jnp.maximum(n * hb - 1, 0), 0)),
                  pl.BlockSpec((1, SSD_HALO, SSM_CONV_CH),
                               lambda b, n: (b, jnp.minimum((n + 1) * hb, n_hb - 1), 0)),
                  pl.BlockSpec((1, ts, DT_PAD), lambda b, n: (b, n, 0)),
                  _const_spec(conv_w.shape), _const_spec((1, SSM_CONV_CH)),
                  _const_spec((1, DT_PAD))],
        out_specs=[pl.BlockSpec((1, ts, SSM_CONV_CH), lambda b, n: (b, n, 0)),
                   pl.BlockSpec((1, ts, DT_PAD), lambda b, n: (b, n, 0))],
        out_shape=[jax.ShapeDtypeStruct((bsz, seq, SSM_CONV_CH), F32),
                   jax.ShapeDtypeStruct((bsz, seq, DT_PAD), F32)],
        scratch_shapes=[pltpu.VMEM((ts + 2 * SSD_HALO, SSM_CONV_CH), F32),
                        pltpu.VMEM((len(phases), ts + SSD_HALO, SSM_CONV_CH), F32)],
        compiler_params=_cparams(("parallel", "parallel")),
        name="ssd_pre",
    )(xv, xv, xv, dv, conv_w, conv_b[None], dtb)


def _ssd_consts(a_log):
    a_row = jnp.zeros((1, DT_PAD), F32).at[0, :2 * SSM_HEADS].set(-jnp.exp(a_log.reshape(-1)))
    tri = jnp.asarray(np.tril(np.ones((SSM_CHUNK, SSM_CHUNK), np.float32)))
    expand = np.zeros((2, DT_PAD, SSM_INNER), np.float32)
    for d in range(2):
        for h in range(SSM_HEADS):
            expand[d, d * SSM_HEADS + h, h * SSM_HEAD_DIM:(h + 1) * SSM_HEAD_DIM] = 1.0
    return a_row, tri, jnp.asarray(expand, BF16)


def _cumsum_both(dta, tri_ref):
    fwd = jnp.dot(tri_ref[...], dta, preferred_element_type=F32, precision=HI)
    bwd = lax.dot_general(tri_ref[...], dta, (((0,), (0,)), ((), ())),
                          preferred_element_type=F32, precision=HI)
    lane = lax.broadcasted_iota(jnp.int32, dta.shape, 1)
    return jnp.where(lane < SSM_HEADS, fwd, bwd)


def _expand(v, ex_ref, d):
    return _split_dot(v, ex_ref[d])


def _ssd_state_kernel(xf_ref, xb_ref, dtf_ref, dtb_ref, arow_ref, tri_ref, ex_ref,
                      pf_ref, pb_ref, sf_ref, sb_ref):
    @pl.when(pl.program_id(1) == 0)
    def _():
        sf_ref[...] = jnp.zeros_like(sf_ref)
        sb_ref[...] = jnp.zeros_like(sb_ref)

    def one(d, x_ref, dt_ref, st_ref, out_ref):
        out_ref[0, 0] = st_ref[...].astype(out_ref.dtype)
        dtv = dt_ref[0]
        acs = _cumsum_both(dtv * arow_ref[...], tri_ref)
        total = acs[SSM_CHUNK - 1:SSM_CHUNK, :] if d == 0 else acs[0:1, :]
        lane = lax.broadcasted_iota(jnp.int32, acs.shape, 1)
        mine = (lane >= d * SSM_HEADS) & (lane < (d + 1) * SSM_HEADS)
        w = dtv * jnp.exp(jnp.where(mine, total - acs, 0.0))
        xw = (x_ref[0, :, :SSM_INNER] * _expand(w, ex_ref, d)).astype(BF16)
        bmat = x_ref[0, :, SSM_INNER:SSM_INNER + SSD_BC].astype(BF16)
        new = []
        for g in range(SSM_GROUPS):
            new.append(lax.dot_general(
                bmat[:, g * SSM_STATE:(g + 1) * SSM_STATE],
                xw[:, g * GROUP_LANES:(g + 1) * GROUP_LANES],
                (((0,), (0,)), ((), ())), preferred_element_type=F32))
        carry = _expand(jnp.broadcast_to(jnp.exp(total), (SUBLANES, DT_PAD)), ex_ref, d)[0:1]
        st_ref[...] = st_ref[...] * carry + jnp.concatenate(new, axis=1)

    one(0, xf_ref, dtf_ref, sf_ref, pf_ref)
    one(1, xb_ref, dtb_ref, sb_ref, pb_ref)


def _ssd_states(xact, dtv, consts, bsz, seq):
    nc = seq // SSM_CHUNK
    a_row, tri, expand = consts
    fwd = lambda b, c: (b, c, 0)
    bwd = lambda b, c: (b, nc - 1 - c, 0)
    st = jax.ShapeDtypeStruct((bsz, nc, SSM_STATE, SSM_INNER), BF16)
    return pl.pallas_call(
        _ssd_state_kernel,
        grid=(bsz, nc),
        in_specs=[pl.BlockSpec((1, SSM_CHUNK, SSM_CONV_CH), fwd),
                  pl.BlockSpec((1, SSM_CHUNK, SSM_CONV_CH), bwd),
                  pl.BlockSpec((1, SSM_CHUNK, DT_PAD), fwd),
                  pl.BlockSpec((1, SSM_CHUNK, DT_PAD), bwd),
                  _const_spec(a_row.shape), _const_spec(tri.shape), _const_spec(expand.shape)],
        out_specs=[pl.BlockSpec((1, 1, SSM_STATE, SSM_INNER), lambda b, c: (b, c, 0, 0)),
                   pl.BlockSpec((1, 1, SSM_STATE, SSM_INNER), lambda b, c: (b, nc - 1 - c, 0, 0))],
        out_shape=[st, st],
        scratch_shapes=[pltpu.VMEM((SSM_STATE, SSM_INNER), F32),
                        pltpu.VMEM((SSM_STATE, SSM_INNER), F32)],
        compiler_params=_cparams(("parallel", "arbitrary")),
        name="ssd_states",
    )(xact, xact, dtv, dtv, a_row, tri, expand)


def _ssd_out_kernel(x_ref, dt_ref, z_ref, pf_ref, pb_ref, arow_ref, tri_ref, ex_ref, dskip_ref,
                    ng_ref, o_ref):
    xs = x_ref[0, :, :SSM_INNER]
    bmat = x_ref[0, :, SSM_INNER:SSM_INNER + SSD_BC].astype(BF16)
    cmat = x_ref[0, :, SSM_INNER + SSD_BC:].astype(BF16)
    dtv = dt_ref[0]
    acs = _cumsum_both(dtv * arow_ref[...], tri_ref)
    acs_t = acs.T
    eacs = jnp.exp(acs)
    row = lax.broadcasted_iota(jnp.int32, (SSM_CHUNK, SSM_CHUNK), 0)
    col = lax.broadcasted_iota(jnp.int32, (SSM_CHUNK, SSM_CHUNK), 1)
    causal = (col <= row, col >= row)
    low = lax.broadcasted_iota(jnp.int32, (SSM_CHUNK, LANES), 1) < SSM_HEAD_DIM
    scores = [lax.dot_general(cmat[:, g * SSM_STATE:(g + 1) * SSM_STATE],
                              bmat[:, g * SSM_STATE:(g + 1) * SSM_STATE],
                              (((1,), (1,)), ((), ())), preferred_element_type=F32)
              for g in range(SSM_GROUPS)]
    y = dskip_ref[...] * xs
    for d, p_ref in enumerate((pf_ref, pb_ref)):
        xd = (xs * _expand(dtv, ex_ref, d)).astype(BF16)
        diag = []
        for pair in range(SSM_HEADS // 2):
            xp = xd[:, pair * LANES:(pair + 1) * LANES]
            res = []
            for half in range(2):
                h = 2 * pair + half
                i = d * SSM_HEADS + h
                seg = acs[:, i:i + 1] - acs_t[i:i + 1, :]
                decay = jnp.exp(jnp.where(causal[d], seg, -jnp.inf))
                m = (scores[h // HEADS_PER_GROUP] * decay).astype(BF16)
                res.append(jnp.dot(m, xp, preferred_element_type=F32))
            diag.append(jnp.where(low, res[0], res[1]))
        off = [jnp.dot(cmat[:, g * SSM_STATE:(g + 1) * SSM_STATE],
                       p_ref[0, 0, :, g * GROUP_LANES:(g + 1) * GROUP_LANES],
                       preferred_element_type=F32) for g in range(SSM_GROUPS)]
        y = y + jnp.concatenate(diag, axis=1) + jnp.concatenate(off, axis=1) * _expand(eacs, ex_ref, d)
    z = z_ref[0]
    y = y * (z * jax.nn.sigmoid(z))
    o_ref[0] = _rms(y, ng_ref[...]).astype(o_ref.dtype)


def _ssd_out(xact, dtv, z, prev_f, prev_b, consts, d_skip, norm_g, bsz, seq):
    nc = seq // SSM_CHUNK
    a_row, tri, expand = consts
    dsk = jnp.repeat(d_skip, SSM_HEAD_DIM)[None]
    chunk = lambda n: pl.BlockSpec((1, SSM_CHUNK, n), lambda b, c: (b, c, 0))
    state = pl.BlockSpec((1, 1, SSM_STATE, SSM_INNER), lambda b, c: (b, c, 0, 0))
    out = pl.pallas_call(
        _ssd_out_kernel,
        grid=(bsz, nc),
        in_specs=[chunk(SSM_CONV_CH), chunk(DT_PAD), chunk(SSM_INNER), state, state,
                  _const_spec(a_row.shape), _const_spec(tri.shape), _const_spec(expand.shape),
                  _const_spec(dsk.shape), _const_spec((1, SSM_INNER))],
        out_specs=chunk(SSM_INNER),
        out_shape=jax.ShapeDtypeStruct((bsz, seq, SSM_INNER), BF16),
        compiler_params=_cparams(("parallel", "parallel")),
        name="ssd_out",
    )(xact, dtv, z.reshape(bsz, seq, SSM_INNER), prev_f, prev_b, a_row, tri, expand, dsk,
      norm_g[None])
    return out.reshape(bsz * seq, SSM_INNER)


def _ssd(z, xbc, dt_raw, conv_w, conv_b, a_log, dt_bias, d_skip, norm_g, bsz, seq):
    xact, dtv = _ssd_pre(xbc, dt_raw, conv_w, conv_b, dt_bias, bsz, seq)
    consts = _ssd_consts(a_log)
    prev_f, prev_b = _ssd_states(xact, dtv, consts, bsz, seq)
    return _ssd_out(xact, dtv, z, prev_f, prev_b, consts, d_skip, norm_g, bsz, seq)


FNET_COLS = 4096


def _dft_cos_sin(n):
    ang = 2.0 * np.pi * np.outer(np.arange(n), np.arange(n)) / n
    return np.cos(ang), np.sin(ang)


def _fnet_consts(seq):
    c = FNET_GROUP_DIM
    n1 = seq // LANES
    c1, s1 = _dft_cos_sin(n1)
    stage1 = np.concatenate([c1, -s1], axis=0)
    ang = 2.0 * np.pi * np.outer(np.arange(n1), np.arange(LANES)) / seq
    twr = np.repeat(np.cos(ang), c, axis=1)
    twi = np.repeat(-np.sin(ang), c, axis=1)
    cc, sc = _dft_cos_sin(c)
    chan = np.block([[cc, -sc], [sc, cc]])
    c2, s2 = _dft_cos_sin(LANES)
    return (jnp.asarray(stage1, BF16), jnp.asarray(twr, F32), jnp.asarray(twi, F32),
            jnp.asarray(chan, BF16), jnp.asarray(c2, BF16), jnp.asarray(s2, BF16))


def _fnet1_kernel(x_ref, f_ref, twr_ref, twi_ref, o_ref, *, n1):
    c = FNET_GROUP_DIM
    a = jnp.dot(f_ref[...], x_ref[0].astype(BF16), preferred_element_type=F32)
    ar, ai = a[:n1], a[n1:]
    twr, twi = twr_ref[...], twi_ref[...]
    re = (ar * twr - ai * twi).astype(o_ref.dtype)
    im = (ar * twi + ai * twr).astype(o_ref.dtype)
    for j in range(re.shape[1] // c):
        o_ref[0, :, (2 * j) * c:(2 * j + 1) * c] = re[:, j * c:(j + 1) * c]
        o_ref[0, :, (2 * j + 1) * c:(2 * j + 2) * c] = im[:, j * c:(j + 1) * c]


def _fnet2_kernel(a_ref, chan_ref, c2_ref, s2_ref, o_ref, scr_ref, *, n1, scale):
    c = FNET_GROUP_DIM

    def body(k1, carry):
        g = jnp.dot(a_ref[0, k1], chan_ref[...], preferred_element_type=F32).astype(BF16)
        y = (jnp.dot(c2_ref[...], g[:, :c], preferred_element_type=F32)
             + jnp.dot(s2_ref[...], g[:, c:], preferred_element_type=F32))
        scr_ref[pl.ds(k1, LANES, stride=n1), :] = y * scale
        return carry

    lax.fori_loop(0, n1, body, 0)
    o_ref[0] = scr_ref[...].astype(o_ref.dtype)


def _fourier(fn, bsz, seq):
    c = FNET_GROUP_DIM
    assert c == LANES and seq % LANES == 0
    n1 = seq // LANES
    stage1, twr, twi, chan, c2, s2 = _fnet_consts(seq)
    ncols = LANES * c
    nb = min(FNET_COLS, ncols)
    x2 = fn.reshape(bsz * FNET_GROUPS, n1, ncols)
    a = pl.pallas_call(
        functools.partial(_fnet1_kernel, n1=n1),
        grid=(ncols // nb, bsz * FNET_GROUPS),
        in_specs=[pl.BlockSpec((1, n1, nb), lambda j, i: (i, 0, j)),
                  _const_spec(stage1.shape),
                  pl.BlockSpec((n1, nb), lambda j, i: (0, j)),
                  pl.BlockSpec((n1, nb), lambda j, i: (0, j))],
        out_specs=pl.BlockSpec((1, n1, 2 * nb), lambda j, i: (i, 0, j)),
        out_shape=jax.ShapeDtypeStruct((bsz * FNET_GROUPS, n1, 2 * ncols), BF16),
        compiler_params=_cparams(("parallel", "parallel")),
        name="fnet1",
    )(x2, stage1, twr, twi)
    a4 = a.reshape(bsz * FNET_GROUPS, n1, LANES, 2 * c)
    out = pl.pallas_call(
        functools.partial(_fnet2_kernel, n1=n1, scale=1.0 / math.sqrt(seq * c)),
        grid=(bsz, FNET_GROUPS),
        in_specs=[pl.BlockSpec((1, n1, LANES, 2 * c), lambda b, g: (b * FNET_GROUPS + g, 0, 0, 0)),
                  _const_spec(chan.shape), _const_spec(c2.shape), _const_spec(s2.shape)],
        out_specs=pl.BlockSpec((1, seq, c), lambda b, g: (b, 0, g)),
        out_shape=jax.ShapeDtypeStruct((bsz, seq, FNET_WIDTH), BF16),
        scratch_shapes=[pltpu.VMEM((seq, c), F32)],
        compiler_params=_cparams(("parallel", "parallel")),
        name="fnet2",
    )(a4, chan, c2, s2)
    return out.reshape(bsz * seq, FNET_WIDTH)


def kernel(x, p, rel_bias, norm_mix, w_in, conv_dw, conv_dw_b, conv_ln_g, conv_ln_b, conv_out,
           ssm_conv_w, ssm_conv_b, ssm_a_log, ssm_dt_bias, ssm_d, ssm_norm, ssm_out,
           attn_out, fnet_out, w_gate, b_gate, w_out, norm_ffn, ffn_w1, ffn_w3, ffn_w2,
           moe_router, moe_w1, moe_w3, moe_w2, ple_gate, ple_proj, final_norm):
    bsz, seq, d = x.shape
    depth = w_in.shape[0]
    t = bsz * seq
    h = x.reshape(t, d)
    bias_tables = [_att_bias_tables(rel_bias, g, dil) for g, (_, dil) in enumerate(ATT_PATTERNS)]
    for l in range(depth):
        qkv, conv_u, z, xbc, fn, dt = _inproj(h, norm_mix[l][None], _reorder_w_in(w_in[l]), bsz, seq)
        att = [_attention_group(qkv, bias_tables[g], g, dil, bsz, seq)
               for g, (_, dil) in enumerate(ATT_PATTERNS)]
        cnf = _conformer(conv_u, conv_dw[l], conv_dw_b[l], conv_ln_g[l], conv_ln_b[l], bsz, seq)
        ssd = _ssd(z, xbc, dt, ssm_conv_w[l], ssm_conv_b[l], ssm_a_log[l], ssm_dt_bias[l],
                   ssm_d[l], ssm_norm[l], bsz, seq)
        fnt = _fourier(fn, bsz, seq)
        wbr = jnp.stack([attn_out[l], conv_out[l], ssm_out[l], fnet_out[l]]).astype(BF16)
        h = _mix(h, norm_mix[l][None], att, (cnf, ssd, fnt), wbr, w_gate[l].astype(BF16),
                 b_gate[l][:, None, :], w_out[l].astype(BF16))
        pl_in = p[l].reshape(t, -1)
        wpg, wpp = ple_gate[l].astype(BF16), ple_proj[l].astype(BF16)
        i = l // 2
        if l % 2 == 0:
            h = _ffn(h, norm_ffn[l][None], ffn_w1[i].astype(BF16), ffn_w3[i].astype(BF16),
                     ffn_w2[i].astype(BF16), pl_in, wpg, wpp)
        else:
            h = _moe(h, norm_ffn[l][None], moe_router[i], moe_w1[i].astype(BF16),
                     moe_w3[i].astype(BF16), moe_w2[i].astype(BF16), pl_in, wpg, wpp)
    return _final_norm(h, final_norm[None]).reshape(bsz, seq, d)
```

```python
import functools
import math

import numpy as np
import jax
import jax.numpy as jnp
from jax import lax
from jax.experimental import pallas as pl
from jax.experimental.pallas import tpu as pltpu

F32 = jnp.float32
BF16 = jnp.bfloat16

EPS = 1e-6
N_BRANCHES = 4
CONV_CH = 512
CONV_K = 31
SSM_HEADS = 8
SSM_HEAD_DIM = 64
SSM_INNER = SSM_HEADS * SSM_HEAD_DIM
SSM_GROUPS = 2
SSM_STATE = 64
SSM_CONV = 5
SSM_CONV_CH = SSM_INNER + 2 * SSM_GROUPS * SSM_STATE
SSM_CHUNK = 128
ATT_PATTERNS = ((128, 1), (512, 4), (2048, 16))
ATT_GROUPS = len(ATT_PATTERNS)
ATT_HEADS = 8
ATT_HEAD_DIM = 64
ATT_WIDTH = ATT_HEADS * ATT_HEAD_DIM
REL_BUCKETS = 32
REL_MAX_DIST = 1024
FNET_GROUPS = 4
FNET_GROUP_DIM = 128
FNET_WIDTH = FNET_GROUPS * FNET_GROUP_DIM
N_EXPERTS = 8
TOP_K = 2

ATT_IN_COLS = 3 * ATT_GROUPS * ATT_WIDTH
CONV_IN_COLS = 2 * CONV_CH
SSM_IN_COLS = SSM_INNER + SSM_CONV_CH + 2 * SSM_HEADS
OFF_CONV = ATT_IN_COLS
OFF_SSM = OFF_CONV + CONV_IN_COLS
OFF_FNET = OFF_SSM + SSM_IN_COLS

LANES = 128
DT_PAD = LANES
VMEM_LIMIT = 56 * 1024 * 1024


def _cparams(sem):
    return pltpu.CompilerParams(dimension_semantics=sem, vmem_limit_bytes=VMEM_LIMIT)


def _const_spec(shape):
    nd = len(shape)
    return pl.BlockSpec(shape, lambda *_: (0,) * nd, pipeline_mode=pl.Buffered(1))


def _rms(x, g):
    return x * lax.rsqrt(jnp.mean(x * x, axis=-1, keepdims=True) + EPS) * g


_SEC_QKV = (0, ATT_IN_COLS)
_SEC_CONV = (_SEC_QKV[0] + _SEC_QKV[1], CONV_IN_COLS)
_SEC_Z = (_SEC_CONV[0] + _SEC_CONV[1], SSM_INNER)
_SEC_XBC = (_SEC_Z[0] + _SEC_Z[1], SSM_CONV_CH)
_SEC_FNET = (_SEC_XBC[0] + _SEC_XBC[1], FNET_WIDTH)
_SEC_DT = (_SEC_FNET[0] + _SEC_FNET[1], DT_PAD)
_IN_COLS_PAD = _SEC_DT[0] + _SEC_DT[1]
_MM_CHUNK = 512


ATT_TILE = 256
QKV_COLS = 3 * ATT_WIDTH


def _reorder_w_in(w):
    d = w.shape[0]
    qkv = w[:, :OFF_CONV].reshape(d, 3, ATT_GROUPS, ATT_WIDTH).transpose(0, 2, 1, 3)
    ssm = w[:, OFF_SSM:OFF_FNET]
    dt = ssm[:, SSM_INNER + SSM_CONV_CH:]
    parts = [qkv.reshape(d, OFF_CONV), w[:, OFF_CONV:OFF_SSM], ssm[:, :SSM_INNER],
             ssm[:, SSM_INNER:SSM_INNER + SSM_CONV_CH], w[:, OFF_FNET:],
             dt, jnp.zeros((d, DT_PAD - dt.shape[1]), w.dtype)]
    return jnp.concatenate(parts, axis=1).astype(BF16)


def _deinterleave_matrix(dil):
    s = np.arange(ATT_TILE)
    m = np.zeros((ATT_TILE, ATT_TILE), np.float32)
    m[(s % dil) * (ATT_TILE // dil) + s // dil, s] = 1.0
    return m


def _inproj_kernel(h_ref, g_ref, w_ref, perm_ref, q0_ref, q1_ref, q2_ref, conv_ref, z_ref,
                   xbc_ref, fn_ref, dt_ref):
    xn = _rms(h_ref[...], g_ref[...]).astype(BF16)

    def section(x, sec, store):
        start, width = sec
        for c in range(0, width, _MM_CHUNK):
            cw = min(_MM_CHUNK, width - c)
            store(c, cw, jnp.dot(x, w_ref[:, start + c:start + c + cw],
                                 preferred_element_type=F32))

    def to(ref):
        def store(c, cw, val):
            ref[:, c:c + cw] = val.astype(ref.dtype)
        return store

    def to_fnet(c, cw, val):
        for g in range(cw // FNET_GROUP_DIM):
            fn_ref[0, c // FNET_GROUP_DIM + g] = val[:, g * FNET_GROUP_DIM:(g + 1) * FNET_GROUP_DIM]

    for g, q_ref in enumerate((q0_ref, q1_ref, q2_ref)):
        x = xn
        if ATT_PATTERNS[g][1] > 1:
            x = jnp.dot(perm_ref[g], xn, preferred_element_type=F32).astype(BF16)
        section(x, (g * QKV_COLS, QKV_COLS), to(q_ref))
    section(xn, _SEC_CONV, to(conv_ref))
    section(xn, _SEC_Z, to(z_ref))
    section(xn, _SEC_XBC, to(xbc_ref))
    section(xn, _SEC_FNET, to_fnet)
    section(xn, _SEC_DT, to(dt_ref))


def _inproj(h, g, w, bsz, seq):
    t, d = h.shape
    tm = ATT_TILE
    spt = seq // tm
    perm = jnp.asarray(np.stack([_deinterleave_matrix(dil) for _, dil in ATT_PATTERNS]), BF16)
    row = lambda n: pl.BlockSpec((tm, n), lambda i: (i, 0))
    qkv = jax.ShapeDtypeStruct((t, QKV_COLS), BF16)
    return pl.pallas_call(
        _inproj_kernel,
        grid=(t // tm,),
        in_specs=[row(d), _const_spec((1, d)), _const_spec(w.shape), _const_spec(perm.shape)],
        out_specs=[row(QKV_COLS), row(QKV_COLS), row(QKV_COLS), row(CONV_IN_COLS),
                   row(SSM_INNER), row(SSM_CONV_CH),
                   pl.BlockSpec((1, FNET_GROUPS, tm, FNET_GROUP_DIM),
                                lambda i: (i // spt, 0, i % spt, 0)),
                   row(DT_PAD)],
        out_shape=[qkv, qkv, qkv,
                   jax.ShapeDtypeStruct((t, CONV_IN_COLS), F32),
                   jax.ShapeDtypeStruct((t, SSM_INNER), F32),
                   jax.ShapeDtypeStruct((t, SSM_CONV_CH), F32),
                   jax.ShapeDtypeStruct((bsz, FNET_GROUPS, seq, FNET_GROUP_DIM), F32),
                   jax.ShapeDtypeStruct((t, DT_PAD), F32)],
        compiler_params=_cparams(("parallel",)),
        name="inproj",
    )(h, g, w, perm)


def _split_dot(v, m):
    hi = v.astype(BF16)
    lo = (v - hi.astype(F32)).astype(BF16)
    return (jnp.dot(hi, m, preferred_element_type=F32) + jnp.dot(lo, m, preferred_element_type=F32))


def _interleave(pt, v):
    if v.dtype == BF16:
        terms = [v]
    else:
        hi = v.astype(BF16)
        r1 = v - hi.astype(F32)
        mid = r1.astype(BF16)
        terms = [hi, mid, (r1 - mid.astype(F32)).astype(BF16)]
    tiles = []
    for i in range(v.shape[0] // ATT_TILE):
        rows = slice(i * ATT_TILE, (i + 1) * ATT_TILE)
        parts = [jnp.dot(pt, t[rows], preferred_element_type=F32) for t in terms]
        tiles.append(functools.reduce(lambda a, b: a + b, parts))
    return jnp.concatenate(tiles, axis=0)


def _mix_kernel(h_ref, g_ref, o0_ref, o1_ref, o2_ref, l0_ref, l1_ref, l2_ref, b1_ref, b2_ref,
                b3_ref, pt_ref, hx_ref, wbr_ref, wg_ref, cg_ref, wo_ref, o_ref):
    h = h_ref[...]
    xn = _rms(h, g_ref[...]).astype(BF16)
    outs, lses = [], []
    for g, (og_ref, lg_ref) in enumerate(zip((o0_ref, o1_ref, o2_ref), (l0_ref, l1_ref, l2_ref))):
        if ATT_PATTERNS[g][1] > 1:
            outs.append(_interleave(pt_ref[g], og_ref[...]))
            lses.append(_interleave(pt_ref[g], lg_ref[...]))
        else:
            outs.append(og_ref[...].astype(F32))
            lses.append(lg_ref[...])
    top = jnp.maximum(jnp.maximum(lses[0], lses[1]), lses[2])
    es = [jnp.exp(l - top) for l in lses]
    inv = 1.0 / (es[0] + es[1] + es[2])
    att = None
    for e, og in zip(es, outs):
        term = og * _split_dot(e * inv, hx_ref[...])
        att = term if att is None else att + term
    acc = None
    for b, hid in enumerate((att.astype(BF16), b1_ref[...], b2_ref[...], b3_ref[...])):
        gate = jax.nn.sigmoid(jnp.dot(xn, wg_ref[b], preferred_element_type=F32) + cg_ref[b])
        br = jnp.dot(hid, wbr_ref[b], preferred_element_type=F32)
        acc = gate * br if acc is None else acc + gate * br
    o_ref[...] = h + jnp.dot(acc.astype(BF16), wo_ref[...], preferred_element_type=F32)


def _mix(h, g, att, others, wbr, wg, cg, wo):
    t, d = h.shape
    tm = 512
    row = lambda n: pl.BlockSpec((tm, n), lambda i: (i, 0))
    head_expand = np.zeros((LANES, ATT_WIDTH), np.float32)
    for hd in range(ATT_HEADS):
        head_expand[hd, hd * ATT_HEAD_DIM:(hd + 1) * ATT_HEAD_DIM] = 1.0
    head_expand = jnp.asarray(head_expand, BF16)
    unperm = jnp.asarray(np.stack([_deinterleave_matrix(dil).T for _, dil in ATT_PATTERNS]), BF16)
    outs = [o for o, _ in att]
    lses = [l for _, l in att]
    return pl.pallas_call(
        _mix_kernel,
        grid=(t // tm,),
        in_specs=[row(d), _const_spec((1, d))] + [row(a.shape[1]) for a in outs + lses + list(others)]
                 + [_const_spec(unperm.shape), _const_spec(head_expand.shape), _const_spec(wbr.shape),
                    _const_spec(wg.shape), _const_spec(cg.shape), _const_spec(wo.shape)],
        out_specs=row(d),
        out_shape=jax.ShapeDtypeStruct((t, d), F32),
        compiler_params=_cparams(("parallel",)),
        name="mix",
    )(h, g, *outs, *lses, *others, unperm, head_expand, wbr, wg, cg, wo)


def _ple(h2, p_ref, wpg_ref, wpp_ref):
    gate = jax.nn.sigmoid(jnp.dot(h2.astype(BF16), wpg_ref[...], preferred_element_type=F32))
    pe = jnp.dot(p_ref[...].astype(BF16), wpp_ref[...], preferred_element_type=F32)
    return h2 + gate * pe


def _swiglu_partial(xn, w1, w3, w2, scale=None):
    a = jnp.dot(xn, w1, preferred_element_type=F32)
    b = jnp.dot(xn, w3, preferred_element_type=F32)
    hid = a * jax.nn.sigmoid(a) * b
    if scale is not None:
        hid = hid * scale
    return jnp.dot(hid.astype(BF16), w2, preferred_element_type=F32)


def _ffn_kernel(h_ref, g_ref, w1_ref, w3_ref, w2_ref, p_ref, wpg_ref, wpp_ref, o_ref, xn_ref):
    j = pl.program_id(1)

    @pl.when(j == 0)
    def _():
        h = h_ref[...]
        xn_ref[...] = _rms(h, g_ref[...]).astype(BF16)
        o_ref[...] = h

    o_ref[...] += _swiglu_partial(xn_ref[...], w1_ref[...], w3_ref[...], w2_ref[...])

    @pl.when(j == pl.num_programs(1) - 1)
    def _():
        o_ref[...] = _ple(o_ref[...], p_ref, wpg_ref, wpp_ref)


def _ffn(h, g, w1, w3, w2, p, wpg, wpp):
    t, d = h.shape
    f = w1.shape[1]
    tm, tf = 512, 1408
    row = lambda n: pl.BlockSpec((tm, n), lambda i, j: (i, 0))
    return pl.pallas_call(
        _ffn_kernel,
        grid=(t // tm, f // tf),
        in_specs=[row(d), _const_spec((1, d)),
                  pl.BlockSpec((d, tf), lambda i, j: (0, j)),
                  pl.BlockSpec((d, tf), lambda i, j: (0, j)),
                  pl.BlockSpec((tf, d), lambda i, j: (j, 0)),
                  row(p.shape[1]), _const_spec(wpg.shape), _const_spec(wpp.shape)],
        out_specs=row(d),
        out_shape=jax.ShapeDtypeStruct((t, d), F32),
        scratch_shapes=[pltpu.VMEM((tm, d), BF16)],
        compiler_params=_cparams(("parallel", "arbitrary")),
        name="ffn",
    )(h, g, w1, w3, w2, p, wpg, wpp)


MOE_ROWS = 128


def _moe_route(logits):
    ne, tm = logits.shape
    eidx = lax.broadcasted_iota(jnp.int32, logits.shape, 0)
    m1 = jnp.max(logits, axis=0, keepdims=True)
    i1 = jnp.min(jnp.where(logits == m1, eidx, ne), axis=0, keepdims=True)
    rest = jnp.where(eidx == i1, -jnp.inf, logits)
    m2 = jnp.max(rest, axis=0, keepdims=True)
    i2 = jnp.min(jnp.where(rest == m2, eidx, ne), axis=0, keepdims=True)
    e2 = jnp.exp(m2 - m1)
    den = 1.0 + e2
    combine = jnp.where(eidx == i1, 1.0 / den, 0.0) + jnp.where(eidx == i2, e2 / den, 0.0)
    routed = jnp.where((eidx == i1) | (eidx == i2), 1.0, 0.0)
    r = lax.broadcasted_iota(jnp.int32, (LANES, LANES), 0)
    c = lax.broadcasted_iota(jnp.int32, (LANES, LANES), 1)
    before = jnp.where(r < c, 1.0, 0.0).astype(BF16)
    counts = jnp.zeros((ne, 1), F32)
    slots = []
    for k in range(tm // LANES):
        blk = routed[:, k * LANES:(k + 1) * LANES]
        slots.append(jnp.dot(blk.astype(BF16), before, preferred_element_type=F32) + counts)
        counts = counts + jnp.sum(blk, axis=1, keepdims=True)
    slot = jnp.where(routed > 0.0, jnp.concatenate(slots, axis=1), -1.0).astype(jnp.int32)
    return combine, slot, counts


def _moe_kernel(h_ref, g_ref, rt_ref, w1_ref, w3_ref, w2_ref, p_ref, wpg_ref, wpp_ref, o_ref,
                xn_ref, comb_ref, slot_ref, cnt_ref, xe_ref, ye_ref):
    e = pl.program_id(1)
    j = pl.program_id(2)
    ne = pl.num_programs(1)
    tm = xn_ref.shape[0]

    @pl.when((e == 0) & (j == 0))
    def _():
        h = h_ref[...]
        xn = _rms(h, g_ref[...])
        xn_ref[...] = xn.astype(BF16)
        logits = lax.dot_general(rt_ref[...], xn, (((1,), (1,)), ((), ())),
                                 preferred_element_type=F32, precision=HI)
        combine, slot, counts = _moe_route(logits)
        comb_ref[...] = combine
        slot_ref[...] = slot
        for k in range(comb_ref.shape[0]):
            cnt_ref[k] = jnp.sum(counts[k:k + 1, :]).astype(jnp.int32)
        o_ref[...] = h

    n_blocks = (cnt_ref[e] + MOE_ROWS - 1) // MOE_ROWS
    sub = lax.broadcasted_iota(jnp.int32, (MOE_ROWS, tm), 0)

    def one_hot(blk):
        return slot_ref[pl.ds(e, 1), :] == sub + blk * MOE_ROWS

    @pl.when(j == 0)
    def _():
        def gather(blk, carry):
            r0 = pl.multiple_of(blk * MOE_ROWS, MOE_ROWS)
            sel = jnp.where(one_hot(blk), 1.0, 0.0).astype(BF16)
            xe_ref[pl.ds(r0, MOE_ROWS), :] = jnp.dot(
                sel, xn_ref[...], preferred_element_type=F32).astype(BF16)
            ye_ref[pl.ds(r0, MOE_ROWS), :] = jnp.zeros((MOE_ROWS, ye_ref.shape[1]), F32)
            return carry
        lax.fori_loop(0, n_blocks, gather, 0)

    def expert(blk, carry):
        r0 = pl.multiple_of(blk * MOE_ROWS, MOE_ROWS)
        ye_ref[pl.ds(r0, MOE_ROWS), :] += _swiglu_partial(
            xe_ref[pl.ds(r0, MOE_ROWS), :], w1_ref[0], w3_ref[0], w2_ref[0])
        return carry
    lax.fori_loop(0, n_blocks, expert, 0)

    @pl.when(j == pl.num_programs(2) - 1)
    def _():
        def scatter(blk, carry):
            r0 = pl.multiple_of(blk * MOE_ROWS, MOE_ROWS)
            hot = one_hot(blk)
            weight = jnp.sum(jnp.where(hot, comb_ref[pl.ds(e, 1), :], 0.0), axis=1, keepdims=True)
            yw = (ye_ref[pl.ds(r0, MOE_ROWS), :] * weight).astype(BF16)
            o_ref[...] += lax.dot_general(jnp.where(hot, 1.0, 0.0).astype(BF16), yw,
                                          (((0,), (0,)), ((), ())), preferred_element_type=F32)
            return carry
        lax.fori_loop(0, n_blocks, scatter, 0)

    @pl.when((e == ne - 1) & (j == pl.num_programs(2) - 1))
    def _():
        o_ref[...] = _ple(o_ref[...], p_ref, wpg_ref, wpp_ref)


def _moe(h, g, router, w1, w3, w2, p, wpg, wpp):
    t, d = h.shape
    ne, _, f = w1.shape
    tm, tf = 1024, 896
    row = lambda n: pl.BlockSpec((tm, n), lambda i, e, j: (i, 0))
    return pl.pallas_call(
        _moe_kernel,
        grid=(t // tm, ne, f // tf),
        in_specs=[row(d), _const_spec((1, d)), _const_spec((ne, d)),
                  pl.BlockSpec((1, d, tf), lambda i, e, j: (e, 0, j)),
                  pl.BlockSpec((1, d, tf), lambda i, e, j: (e, 0, j)),
                  pl.BlockSpec((1, tf, d), lambda i, e, j: (e, j, 0)),
                  row(p.shape[1]), _const_spec(wpg.shape), _const_spec(wpp.shape)],
        out_specs=row(d),
        out_shape=jax.ShapeDtypeStruct((t, d), F32),
        scratch_shapes=[pltpu.VMEM((tm, d), BF16), pltpu.VMEM((ne, tm), F32),
                        pltpu.VMEM((ne, tm), jnp.int32), pltpu.SMEM((ne,), jnp.int32),
                        pltpu.VMEM((tm, d), BF16), pltpu.VMEM((tm, d), F32)],
        compiler_params=_cparams(("parallel", "arbitrary", "arbitrary")),
        name="moe",
    )(h, g, router.T, w1, w3, w2, p, wpg, wpp)


def _final_norm_kernel(h_ref, g_ref, o_ref):
    o_ref[...] = _rms(h_ref[...], g_ref[...])


def _final_norm(h, g):
    t, d = h.shape
    tm = 1024
    row = pl.BlockSpec((tm, d), lambda i: (i, 0))
    return pl.pallas_call(
        _final_norm_kernel, grid=(t // tm,),
        in_specs=[row, _const_spec((1, d))], out_specs=row,
        out_shape=jax.ShapeDtypeStruct((t, d), F32),
        compiler_params=_cparams(("parallel",)), name="final_norm",
    )(h, g)


ATT_HALF = 64
ATT_QB = 128
ATT_KB = ATT_QB + 2 * ATT_HALF
NEG = -1e30
assert all(w // (2 * d) == ATT_HALF for w, d in ATT_PATTERNS)


def _t5_bucket(rel):
    half = REL_BUCKETS // 2
    max_exact = half // 2
    n = np.abs(rel)
    large = max_exact + (np.log(np.maximum(n, 1) / max_exact) / math.log(REL_MAX_DIST / max_exact)
                         * (half - max_exact)).astype(np.int32)
    large = np.minimum(large, half - 1)
    return np.where(rel > 0, half, 0) + np.where(n < max_exact, n, large)


def _att_bias_tables(rel_bias, g, dil):
    i = np.arange(ATT_QB)[:, None]
    j = np.arange(ATT_KB)[None, :]
    rel = j - ATT_HALF - i
    band = np.abs(rel) <= ATT_HALF
    bias = rel_bias[_t5_bucket(dil * rel)][:, :, g * ATT_HEADS:(g + 1) * ATT_HEADS]
    bias = bias.transpose(2, 0, 1).astype(F32)
    tables = []
    for v in range(4):
        ok = band
        if v & 1:
            ok = ok & (j >= ATT_HALF)
        if v & 2:
            ok = ok & (j < ATT_QB + ATT_HALF)
        tables.append(jnp.where(ok[None], bias, NEG))
    return jnp.stack(tables)


def _attn_kernel(q_ref, k_ref, v_ref, kp_ref, kn_ref, vp_ref, vn_ref, bias_ref, o_ref, lse_ref,
                 qbuf, kbuf, vbuf, obuf, lbuf, *, tq, n_blocks):
    flat = lambda ref: ref[0].reshape(-1, ref.shape[-1])
    qbuf[...] = flat(q_ref)
    kbuf[0:ATT_HALF] = flat(kp_ref)
    kbuf[ATT_HALF:ATT_HALF + tq] = flat(k_ref)
    kbuf[ATT_HALF + tq:] = flat(kn_ref)
    vbuf[0:ATT_HALF] = flat(vp_ref)
    vbuf[ATT_HALF:ATT_HALF + tq] = flat(v_ref)
    vbuf[ATT_HALF + tq:] = flat(vn_ref)
    nsb = tq // ATT_QB
    first = pl.program_id(2) * nsb
    lane = lax.broadcasted_iota(jnp.int32, (ATT_QB, LANES), 1)
    low = lane < ATT_HEAD_DIM
    lane_row = lax.broadcasted_iota(jnp.int32, (1, LANES), 1)
    keep = [(lane_row < ATT_HEAD_DIM).astype(BF16), (lane_row >= ATT_HEAD_DIM).astype(BF16)]

    def block(sb, carry):
        r0 = pl.multiple_of(sb * ATT_QB, ATT_QB)
        gsb = first + sb
        variant = (gsb == 0).astype(jnp.int32) + 2 * (gsb == n_blocks - 1).astype(jnp.int32)
        q = qbuf[pl.ds(r0, ATT_QB), :] * (ATT_HEAD_DIM ** -0.5)
        lse_all = jnp.zeros((ATT_QB, LANES), F32)
        outs = []
        for pair in range(ATT_HEADS // 2):
            cols = slice(pair * LANES, (pair + 1) * LANES)
            qp = q[:, cols]
            kp = kbuf[pl.ds(r0, ATT_KB), cols]
            vp = vbuf[pl.ds(r0, ATT_KB), cols]
            res = []
            for half in range(2):
                h = 2 * pair + half
                s = lax.dot_general(qp * keep[half], kp, (((1,), (1,)), ((), ())),
                                    preferred_element_type=F32)
                s = s + bias_ref[variant, h]
                m = jnp.max(s, axis=-1, keepdims=True)
                e = jnp.exp(s - m)
                den = jnp.sum(e, axis=-1, keepdims=True)
                pv = jnp.dot(e.astype(BF16), vp, preferred_element_type=F32)
                res.append(pv / den)
                lse_all = jnp.where(lane == h, m + jnp.log(den), lse_all)
            outs.append(jnp.where(low, res[0], res[1]))
        obuf[pl.ds(r0, ATT_QB), :] = jnp.concatenate(outs, axis=1).astype(obuf.dtype)
        lbuf[pl.ds(r0, ATT_QB), :] = lse_all
        return carry

    lax.fori_loop(0, nsb, block, 0)
    o_ref[0] = obuf[...].reshape(o_ref.shape[1:])
    lse_ref[0] = lbuf[...].reshape(lse_ref.shape[1:])


def _attention_group(qkv, bias_tables, dil, bsz, seq, name):
    sub_len = seq // dil
    assert sub_len % ATT_QB == 0 and seq % ATT_TILE == 0 and ATT_TILE % dil == 0
    rows = ATT_TILE // dil
    tq = min(512, sub_len)
    w = ATT_WIDTH
    hrows = min(rows, ATT_HALF)
    n_hb = sub_len // ATT_HALF

    def view(a, chunk):
        return a.reshape(bsz, (seq // ATT_TILE) * (rows // chunk), dil, chunk, a.shape[-1])

    def main(part):
        return pl.BlockSpec((1, tq // rows, None, rows, w), lambda b, r, n: (b, n, r, 0, part))

    def halo(part, nxt):
        if nxt:
            blk = lambda n: jnp.minimum((n + 1) * (tq // ATT_HALF), n_hb - 1)
        else:
            blk = lambda n: jnp.maximum(n * (tq // ATT_HALF) - 1, 0)
        return pl.BlockSpec((1, ATT_HALF // hrows, None, hrows, w),
                            lambda b, r, n: (b, blk(n), r, 0, part))

    mv, hv = view(qkv, rows), view(qkv, hrows)
    o, lse = pl.pallas_call(
        functools.partial(_attn_kernel, tq=tq, n_blocks=sub_len // ATT_QB),
        grid=(bsz, dil, sub_len // tq),
        in_specs=[main(0), main(1), main(2), halo(1, False), halo(1, True), halo(2, False),
                  halo(2, True), _const_spec(bias_tables.shape)],
        out_specs=[pl.BlockSpec((1, tq // rows, None, rows, w), lambda b, r, n: (b, n, r, 0, 0)),
                   pl.BlockSpec((1, tq // rows, None, rows, LANES), lambda b, r, n: (b, n, r, 0, 0))],
        out_shape=[jax.ShapeDtypeStruct((bsz, seq // ATT_TILE, dil, rows, w), BF16),
                   jax.ShapeDtypeStruct((bsz, seq // ATT_TILE, dil, rows, LANES), F32)],
        scratch_shapes=[pltpu.VMEM((tq, w), BF16),
                        pltpu.VMEM((tq + 2 * ATT_HALF, w), BF16),
                        pltpu.VMEM((tq + 2 * ATT_HALF, w), BF16),
                        pltpu.VMEM((tq, w), BF16), pltpu.VMEM((tq, LANES), F32)],
        compiler_params=_cparams(("parallel", "parallel", "parallel")),
        name=name,
    )(mv, mv, mv, hv, hv, hv, hv, bias_tables)
    return o.reshape(bsz * seq, w), lse.reshape(bsz * seq, LANES)


SUBLANES = 8
CONV_HALO = 16
CONV_ROWS = 64


def _conformer_kernel(u_ref, up_ref, un_ref, dw_ref, dwb_ref, lng_ref, lnb_ref, o_ref,
                      hp_ref, sh_ref, *, ts):
    n = pl.program_id(1)

    def glu(u):
        return u[:, :CONV_CH] * jax.nn.sigmoid(u[:, CONV_CH:])

    hp_ref[0:CONV_HALO] = jnp.where(n > 0, glu(up_ref[0]), 0.0)
    hp_ref[CONV_HALO:CONV_HALO + ts] = glu(u_ref[0])
    hp_ref[CONV_HALO + ts:] = jnp.where(n < pl.num_programs(1) - 1, glu(un_ref[0]), 0.0)
    span = ts + 2 * CONV_HALO - SUBLANES
    for b in range(SUBLANES):
        sh_ref[b] = hp_ref[pl.ds(b, span), :]
    first = CONV_HALO - CONV_K // 2

    def rows(c, carry):
        r0 = pl.multiple_of(c * CONV_ROWS, CONV_ROWS)
        acc = jnp.broadcast_to(dwb_ref[...], (CONV_ROWS, CONV_CH))
        for k in range(CONV_K):
            a, b = divmod(first + k, SUBLANES)
            acc = acc + dw_ref[pl.ds(k, 1), :] * sh_ref[b, pl.ds(r0 + a * SUBLANES, CONV_ROWS), :]
        mu = jnp.mean(acc, axis=-1, keepdims=True)
        cen = acc - mu
        var = jnp.mean(cen * cen, axis=-1, keepdims=True)
        y = cen * lax.rsqrt(var + EPS) * lng_ref[...] + lnb_ref[...]
        o_ref[0, pl.ds(r0, CONV_ROWS), :] = (y * jax.nn.sigmoid(y)).astype(o_ref.dtype)
        return carry

    lax.fori_loop(0, ts // CONV_ROWS, rows, 0)


def _conformer(u, dw, dw_b, ln_g, ln_b, bsz, seq):
    ts = min(512, seq)
    view = u.reshape(bsz, seq, 2 * CONV_CH)
    hb = ts // CONV_HALO
    n_hb = seq // CONV_HALO
    out = pl.pallas_call(
        functools.partial(_conformer_kernel, ts=ts),
        grid=(bsz, seq // ts),
        in_specs=[pl.BlockSpec((1, ts, 2 * CONV_CH), lambda b, n: (b, n, 0)),
                  pl.BlockSpec((1, CONV_HALO, 2 * CONV_CH),
                               lambda b, n: (b, jnp.maximum(n * hb - 1, 0), 0)),
                  pl.BlockSpec((1, CONV_HALO, 2 * CONV_CH),
                               lambda b, n: (b, jnp.minimum((n + 1) * hb, n_hb - 1), 0)),
                  _const_spec(dw.shape), _const_spec((1, CONV_CH)), _const_spec((1, CONV_CH)),
                  _const_spec((1, CONV_CH))],
        out_specs=pl.BlockSpec((1, ts, CONV_CH), lambda b, n: (b, n, 0)),
        out_shape=jax.ShapeDtypeStruct((bsz, seq, CONV_CH), BF16),
        scratch_shapes=[pltpu.VMEM((ts + 2 * CONV_HALO, CONV_CH), F32),
                        pltpu.VMEM((SUBLANES, ts + 2 * CONV_HALO - SUBLANES, CONV_CH), F32)],
        compiler_params=_cparams(("parallel", "parallel")),
        name="conformer",
    )(view, view, view, dw, dw_b[None], ln_g[None], ln_b[None])
    return out.reshape(bsz * seq, CONV_CH)


SSD_HALO = SUBLANES
SSD_ROWS = 64
SSD_BC = SSM_GROUPS * SSM_STATE
HEADS_PER_GROUP = SSM_HEADS // SSM_GROUPS
GROUP_LANES = HEADS_PER_GROUP * SSM_HEAD_DIM
HI = lax.Precision.HIGHEST
assert SSD_BC == LANES and SSM_CHUNK == LANES and 2 * SSM_HEADS <= LANES


def _ssd_pre_kernel(x_ref, xp_ref, xn_ref, dt_ref, cw_ref, cb_ref, dtb_ref, xo_ref, dto_ref,
                    hp_ref, sh_ref, *, ts, phases):
    n = pl.program_id(1)
    hp_ref[0:SSD_HALO] = jnp.where(n > 0, xp_ref[0], 0.0)
    hp_ref[SSD_HALO:SSD_HALO + ts] = x_ref[0]
    hp_ref[SSD_HALO + ts:] = jnp.where(n < pl.num_programs(1) - 1, xn_ref[0], 0.0)
    for i, b in enumerate(phases):
        sh_ref[i] = hp_ref[pl.ds(b, ts + SSD_HALO), :]
    first = SSD_HALO - SSM_CONV // 2

    def rows(c, carry):
        r0 = pl.multiple_of(c * SSD_ROWS, SSD_ROWS)
        acc = jnp.broadcast_to(cb_ref[...], (SSD_ROWS, SSM_CONV_CH))
        for k in range(SSM_CONV):
            a, b = divmod(first + k, SUBLANES)
            acc = acc + cw_ref[pl.ds(k, 1), :] * sh_ref[phases.index(b),
                                                        pl.ds(r0 + a * SUBLANES, SSD_ROWS), :]
        xo_ref[0, pl.ds(r0, SSD_ROWS), :] = acc * jax.nn.sigmoid(acc)
        return carry

    lax.fori_loop(0, ts // SSD_ROWS, rows, 0)
    x = dt_ref[0] + dtb_ref[...]
    softplus = jnp.maximum(x, 0.0) + jnp.log1p(jnp.exp(-jnp.abs(x)))
    lane = lax.broadcasted_iota(jnp.int32, x.shape, 1)
    dto_ref[0] = jnp.where(lane < 2 * SSM_HEADS, softplus, 0.0)


def _ssd_pre(xbc, dt_raw, conv_w, conv_b, dt_bias, bsz, seq):
    ts = min(512, seq)
    xv = xbc.reshape(bsz, seq, SSM_CONV_CH)
    dv = dt_raw.reshape(bsz, seq, DT_PAD)
    hb = ts // SSD_HALO
    n_hb = seq // SSD_HALO
    first = SSD_HALO - SSM_CONV // 2
    phases = tuple(sorted({(first + k) % SUBLANES for k in range(SSM_CONV)}))
    dtb = jnp.zeros((1, DT_PAD), F32).at[0, :2 * SSM_HEADS].set(dt_bias.reshape(-1))
    return pl.pallas_call(
        functools.partial(_ssd_pre_kernel, ts=ts, phases=phases),
        grid=(bsz, seq // ts),
        in_specs=[pl.BlockSpec((1, ts, SSM_CONV_CH), lambda b, n: (b, n, 0)),
                  pl.BlockSpec((1, SSD_HALO, SSM_CONV_CH),
                               lambda b, n: (b, jnp.maximum(n * hb - 1, 0), 0)),
                  pl.BlockSpec((1, SSD_HALO, SSM_CONV_CH),
                               lambda b, n: (b, jnp.minimum((n + 1) * hb, n_hb - 1), 0)),
                  pl.BlockSpec((1, ts, DT_PAD), lambda b, n: (b, n, 0)),
                  _const_spec(conv_w.shape), _const_spec((1, SSM_CONV_CH)),
                  _const_spec((1, DT_PAD))],
        out_specs=[pl.BlockSpec((1, ts, SSM_CONV_CH), lambda b, n: (b, n, 0)),
                   pl.BlockSpec((1, ts, DT_PAD), lambda b, n: (b, n, 0))],
        out_shape=[jax.ShapeDtypeStruct((bsz, seq, SSM_CONV_CH), F32),
                   jax.ShapeDtypeStruct((bsz, seq, DT_PAD), F32)],
        scratch_shapes=[pltpu.VMEM((ts + 2 * SSD_HALO, SSM_CONV_CH), F32),
                        pltpu.VMEM((len(phases), ts + SSD_HALO, SSM_CONV_CH), F32)],
        compiler_params=_cparams(("parallel", "parallel")),
        name="ssd_pre",
    )(xv, xv, xv, dv, conv_w, conv_b[None], dtb)


def _ssd_consts(a_log):
    a_row = jnp.zeros((1, DT_PAD), F32).at[0, :2 * SSM_HEADS].set(-jnp.exp(a_log.reshape(-1)))
    tri = jnp.asarray(np.tril(np.ones((SSM_CHUNK, SSM_CHUNK), np.float32)))
    expand = np.zeros((2, DT_PAD, SSM_INNER), np.float32)
    for d in range(2):
        for h in range(SSM_HEADS):
            expand[d, d * SSM_HEADS + h, h * SSM_HEAD_DIM:(h + 1) * SSM_HEAD_DIM] = 1.0
    return a_row, tri, jnp.asarray(expand, BF16)


def _cumsum_both(dta, tri_ref):
    fwd = jnp.dot(tri_ref[...], dta, preferred_element_type=F32, precision=HI)
    bwd = lax.dot_general(tri_ref[...], dta, (((0,), (0,)), ((), ())),
                          preferred_element_type=F32, precision=HI)
    lane = lax.broadcasted_iota(jnp.int32, dta.shape, 1)
    return jnp.where(lane < SSM_HEADS, fwd, bwd)


def _expand(v, ex_ref, d):
    return _split_dot(v, ex_ref[d])


def _ssd_state_kernel(xf_ref, xb_ref, dtf_ref, dtb_ref, arow_ref, tri_ref, ex_ref,
                      pf_ref, pb_ref, sf_ref, sb_ref, *, nch):
    @pl.when(pl.program_id(1) == 0)
    def _():
        sf_ref[...] = jnp.zeros_like(sf_ref)
        sb_ref[...] = jnp.zeros_like(sb_ref)

    def contribution(d, x_ref, dt_ref, k):
        rows = slice(k * SSM_CHUNK, (k + 1) * SSM_CHUNK)
        dtv = dt_ref[0, rows, :]
        acs = _cumsum_both(dtv * arow_ref[...], tri_ref)
        total = acs[SSM_CHUNK - 1:SSM_CHUNK, :] if d == 0 else acs[0:1, :]
        lane = lax.broadcasted_iota(jnp.int32, acs.shape, 1)
        mine = (lane >= d * SSM_HEADS) & (lane < (d + 1) * SSM_HEADS)
        w = dtv * jnp.exp(jnp.where(mine, total - acs, 0.0))
        xw = (x_ref[0, rows, :SSM_INNER] * _expand(w, ex_ref, d)).astype(BF16)
        bmat = x_ref[0, rows, SSM_INNER:SSM_INNER + SSD_BC].astype(BF16)
        new = []
        for g in range(SSM_GROUPS):
            new.append(lax.dot_general(
                bmat[:, g * SSM_STATE:(g + 1) * SSM_STATE],
                xw[:, g * GROUP_LANES:(g + 1) * GROUP_LANES],
                (((0,), (0,)), ((), ())), preferred_element_type=F32))
        carry = _expand(jnp.broadcast_to(jnp.exp(total), (SUBLANES, DT_PAD)), ex_ref, d)[0:1]
        return jnp.concatenate(new, axis=1), carry

    def scan(d, x_ref, dt_ref, st_ref, out_ref):
        order = list(range(nch)) if d == 0 else list(range(nch - 1, -1, -1))
        parts = [contribution(d, x_ref, dt_ref, k) for k in order]
        st = st_ref[...]
        for k, (new, carry) in zip(order, parts):
            out_ref[0, k] = st.astype(out_ref.dtype)
            st = st * carry + new
        st_ref[...] = st

    scan(0, xf_ref, dtf_ref, sf_ref, pf_ref)
    scan(1, xb_ref, dtb_ref, sb_ref, pb_ref)


SSD_STATE_CHUNKS = 4


def _ssd_states(xact, dtv, consts, bsz, seq):
    nc = seq // SSM_CHUNK
    nch = min(SSD_STATE_CHUNKS, nc)
    steps = nc // nch
    a_row, tri, expand = consts
    fwd = lambda b, c: (b, c, 0)
    bwd = lambda b, c: (b, steps - 1 - c, 0)
    st = jax.ShapeDtypeStruct((bsz, nc, SSM_STATE, SSM_INNER), BF16)
    return pl.pallas_call(
        functools.partial(_ssd_state_kernel, nch=nch),
        grid=(bsz, steps),
        in_specs=[pl.BlockSpec((1, nch * SSM_CHUNK, SSM_CONV_CH), fwd),
                  pl.BlockSpec((1, nch * SSM_CHUNK, SSM_CONV_CH), bwd),
                  pl.BlockSpec((1, nch * SSM_CHUNK, DT_PAD), fwd),
                  pl.BlockSpec((1, nch * SSM_CHUNK, DT_PAD), bwd),
                  _const_spec(a_row.shape), _const_spec(tri.shape), _const_spec(expand.shape)],
        out_specs=[pl.BlockSpec((1, nch, SSM_STATE, SSM_INNER), lambda b, c: (b, c, 0, 0)),
                   pl.BlockSpec((1, nch, SSM_STATE, SSM_INNER),
                                lambda b, c: (b, steps - 1 - c, 0, 0))],
        out_shape=[st, st],
        scratch_shapes=[pltpu.VMEM((SSM_STATE, SSM_INNER), F32),
                        pltpu.VMEM((SSM_STATE, SSM_INNER), F32)],
        compiler_params=_cparams(("parallel", "arbitrary")),
        name="ssd_states",
    )(xact, xact, dtv, dtv, a_row, tri, expand)


def _ssd_out_kernel(x_ref, dt_ref, z_ref, pf_ref, pb_ref, arow_ref, tri_ref, ex_ref, dskip_ref,
                    ng_ref, o_ref):
    xs = x_ref[0, :, :SSM_INNER]
    bmat = x_ref[0, :, SSM_INNER:SSM_INNER + SSD_BC].astype(BF16)
    cmat = x_ref[0, :, SSM_INNER + SSD_BC:].astype(BF16)
    dtv = dt_ref[0]
    acs = _cumsum_both(dtv * arow_ref[...], tri_ref)
    acs_t = acs.T
    eacs = jnp.exp(acs)
    row = lax.broadcasted_iota(jnp.int32, (SSM_CHUNK, SSM_CHUNK), 0)
    col = lax.broadcasted_iota(jnp.int32, (SSM_CHUNK, SSM_CHUNK), 1)
    causal = (col <= row, col >= row)
    low = lax.broadcasted_iota(jnp.int32, (SSM_CHUNK, LANES), 1) < SSM_HEAD_DIM
    scores = [lax.dot_general(cmat[:, g * SSM_STATE:(g + 1) * SSM_STATE],
                              bmat[:, g * SSM_STATE:(g + 1) * SSM_STATE],
                              (((1,), (1,)), ((), ())), preferred_element_type=F32)
              for g in range(SSM_GROUPS)]
    y = dskip_ref[...] * xs
    for d, p_ref in enumerate((pf_ref, pb_ref)):
        xd = (xs * _expand(dtv, ex_ref, d)).astype(BF16)
        diag = []
        for pair in range(SSM_HEADS // 2):
            xp = xd[:, pair * LANES:(pair + 1) * LANES]
            res = []
            for half in range(2):
                h = 2 * pair + half
                i = d * SSM_HEADS + h
                seg = acs[:, i:i + 1] - acs_t[i:i + 1, :]
                decay = jnp.exp(jnp.where(causal[d], seg, -jnp.inf))
                m = (scores[h // HEADS_PER_GROUP] * decay).astype(BF16)
                res.append(jnp.dot(m, xp, preferred_element_type=F32))
            diag.append(jnp.where(low, res[0], res[1]))
        off = [jnp.dot(cmat[:, g * SSM_STATE:(g + 1) * SSM_STATE],
                       p_ref[0, 0, :, g * GROUP_LANES:(g + 1) * GROUP_LANES],
                       preferred_element_type=F32) for g in range(SSM_GROUPS)]
        y = y + jnp.concatenate(diag, axis=1) + jnp.concatenate(off, axis=1) * _expand(eacs, ex_ref, d)
    z = z_ref[0]
    y = y * (z * jax.nn.sigmoid(z))
    o_ref[0] = _rms(y, ng_ref[...]).astype(o_ref.dtype)


def _ssd_out(xact, dtv, z, prev_f, prev_b, consts, d_skip, norm_g, bsz, seq):
    nc = seq // SSM_CHUNK
    a_row, tri, expand = consts
    dsk = jnp.repeat(d_skip, SSM_HEAD_DIM)[None]
    chunk = lambda n: pl.BlockSpec((1, SSM_CHUNK, n), lambda b, c: (b, c, 0))
    state = pl.BlockSpec((1, 1, SSM_STATE, SSM_INNER), lambda b, c: (b, c, 0, 0))
    out = pl.pallas_call(
        _ssd_out_kernel,
        grid=(bsz, nc),
        in_specs=[chunk(SSM_CONV_CH), chunk(DT_PAD), chunk(SSM_INNER), state, state,
                  _const_spec(a_row.shape), _const_spec(tri.shape), _const_spec(expand.shape),
                  _const_spec(dsk.shape), _const_spec((1, SSM_INNER))],
        out_specs=chunk(SSM_INNER),
        out_shape=jax.ShapeDtypeStruct((bsz, seq, SSM_INNER), BF16),
        compiler_params=_cparams(("parallel", "parallel")),
        name="ssd_out",
    )(xact, dtv, z.reshape(bsz, seq, SSM_INNER), prev_f, prev_b, a_row, tri, expand, dsk,
      norm_g[None])
    return out.reshape(bsz * seq, SSM_INNER)


def _ssd(z, xbc, dt_raw, conv_w, conv_b, a_log, dt_bias, d_skip, norm_g, bsz, seq):
    xact, dtv = _ssd_pre(xbc, dt_raw, conv_w, conv_b, dt_bias, bsz, seq)
    consts = _ssd_consts(a_log)
    prev_f, prev_b = _ssd_states(xact, dtv, consts, bsz, seq)
    return _ssd_out(xact, dtv, z, prev_f, prev_b, consts, d_skip, norm_g, bsz, seq)


FNET_COLS = 4096


def _dft_cos_sin(n):
    ang = 2.0 * np.pi * np.outer(np.arange(n), np.arange(n)) / n
    return np.cos(ang), np.sin(ang)


def _fnet_consts(seq):
    c = FNET_GROUP_DIM
    n1 = seq // LANES
    c1, s1 = _dft_cos_sin(n1)
    stage1 = np.concatenate([c1, -s1], axis=0)
    ang = 2.0 * np.pi * np.outer(np.arange(n1), np.arange(LANES)) / seq
    twr = np.repeat(np.cos(ang), c, axis=1)
    twi = np.repeat(-np.sin(ang), c, axis=1)
    cc, sc = _dft_cos_sin(c)
    chan = np.block([[cc, -sc], [sc, cc]])
    c2, s2 = _dft_cos_sin(LANES)
    return (jnp.asarray(stage1, BF16), jnp.asarray(twr, F32), jnp.asarray(twi, F32),
            jnp.asarray(chan, BF16), jnp.asarray(c2, BF16), jnp.asarray(s2, BF16))


def _fnet1_kernel(x_ref, f_ref, twr_ref, twi_ref, o_ref, *, n1):
    c = FNET_GROUP_DIM
    a = jnp.dot(f_ref[...], x_ref[0].astype(BF16), preferred_element_type=F32)
    ar, ai = a[:n1], a[n1:]
    twr, twi = twr_ref[...], twi_ref[...]
    re = (ar * twr - ai * twi).astype(o_ref.dtype)
    im = (ar * twi + ai * twr).astype(o_ref.dtype)
    for j in range(re.shape[1] // c):
        o_ref[0, :, (2 * j) * c:(2 * j + 1) * c] = re[:, j * c:(j + 1) * c]
        o_ref[0, :, (2 * j + 1) * c:(2 * j + 2) * c] = im[:, j * c:(j + 1) * c]


def _fnet2_kernel(a_ref, chan_ref, c2_ref, s2_ref, o_ref, g_ref, scr_ref, *, n1, scale):
    c = FNET_GROUP_DIM
    per = min(8, n1)
    for i in range(n1 // per):
        blk = a_ref[0, i * per:(i + 1) * per].reshape(per * LANES, 2 * c)
        g = jnp.dot(blk, chan_ref[...], preferred_element_type=F32).astype(BF16)
        g_ref[i * per:(i + 1) * per] = g.reshape(per, LANES, 2 * c)

    def body(k1, carry):
        g = g_ref[k1]
        y = (jnp.dot(c2_ref[...], g[:, :c], preferred_element_type=F32)
             + jnp.dot(s2_ref[...], g[:, c:], preferred_element_type=F32))
        scr_ref[pl.ds(k1, LANES, stride=n1), :] = y * scale
        return carry

    lax.fori_loop(0, n1, body, 0, unroll=4)
    o_ref[0] = scr_ref[...].astype(o_ref.dtype)


def _fourier(fn, bsz, seq):
    c = FNET_GROUP_DIM
    assert c == LANES and seq % LANES == 0
    n1 = seq // LANES
    stage1, twr, twi, chan, c2, s2 = _fnet_consts(seq)
    ncols = LANES * c
    nb = min(FNET_COLS, ncols)
    x2 = fn.reshape(bsz * FNET_GROUPS, n1, ncols)
    a = pl.pallas_call(
        functools.partial(_fnet1_kernel, n1=n1),
        grid=(ncols // nb, bsz * FNET_GROUPS),
        in_specs=[pl.BlockSpec((1, n1, nb), lambda j, i: (i, 0, j)),
                  _const_spec(stage1.shape),
                  pl.BlockSpec((n1, nb), lambda j, i: (0, j)),
                  pl.BlockSpec((n1, nb), lambda j, i: (0, j))],
        out_specs=pl.BlockSpec((1, n1, 2 * nb), lambda j, i: (i, 0, j)),
        out_shape=jax.ShapeDtypeStruct((bsz * FNET_GROUPS, n1, 2 * ncols), BF16),
        compiler_params=_cparams(("parallel", "parallel")),
        name="fnet1",
    )(x2, stage1, twr, twi)
    a4 = a.reshape(bsz * FNET_GROUPS, n1, LANES, 2 * c)
    out = pl.pallas_call(
        functools.partial(_fnet2_kernel, n1=n1, scale=1.0 / math.sqrt(seq * c)),
        grid=(bsz, FNET_GROUPS),
        in_specs=[pl.BlockSpec((1, n1, LANES, 2 * c), lambda b, g: (b * FNET_GROUPS + g, 0, 0, 0)),
                  _const_spec(chan.shape), _const_spec(c2.shape), _const_spec(s2.shape)],
        out_specs=pl.BlockSpec((1, seq, c), lambda b, g: (b, 0, g)),
        out_shape=jax.ShapeDtypeStruct((bsz, seq, FNET_WIDTH), BF16),
        scratch_shapes=[pltpu.VMEM((n1, LANES, 2 * c), BF16), pltpu.VMEM((seq, c), F32)],
        compiler_params=_cparams(("parallel", "parallel")),
        name="fnet2",
    )(a4, chan, c2, s2)
    return out.reshape(bsz * seq, FNET_WIDTH)


def kernel(x, p, rel_bias, norm_mix, w_in, conv_dw, conv_dw_b, conv_ln_g, conv_ln_b, conv_out,
           ssm_conv_w, ssm_conv_b, ssm_a_log, ssm_dt_bias, ssm_d, ssm_norm, ssm_out,
           attn_out, fnet_out, w_gate, b_gate, w_out, norm_ffn, ffn_w1, ffn_w3, ffn_w2,
           moe_router, moe_w1, moe_w3, moe_w2, ple_gate, ple_proj, final_norm):
    bsz, seq, d = x.shape
    depth = w_in.shape[0]
    t = bsz * seq
    h = x.reshape(t, d)
    bias_tables = [_att_bias_tables(rel_bias, g, dil) for g, (_, dil) in enumerate(ATT_PATTERNS)]
    for l in range(depth):
        *qkv, conv_u, z, xbc, fn, dt = _inproj(h, norm_mix[l][None], _reorder_w_in(w_in[l]), bsz, seq)
        att = [_attention_group(qkv[g], bias_tables[g], dil, bsz, seq, f"attn{g}")
               for g, (_, dil) in enumerate(ATT_PATTERNS)]
        cnf = _conformer(conv_u, conv_dw[l], conv_dw_b[l], conv_ln_g[l], conv_ln_b[l], bsz, seq)
        ssd = _ssd(z, xbc, dt, ssm_conv_w[l], ssm_conv_b[l], ssm_a_log[l], ssm_dt_bias[l],
                   ssm_d[l], ssm_norm[l], bsz, seq)
        fnt = _fourier(fn, bsz, seq)
        wbr = jnp.stack([attn_out[l], conv_out[l], ssm_out[l], fnet_out[l]]).astype(BF16)
        h = _mix(h, norm_mix[l][None], att, (cnf, ssd, fnt), wbr, w_gate[l].astype(BF16),
                 b_gate[l][:, None, :], w_out[l].astype(BF16))
        pl_in = p[l].reshape(t, -1)
        wpg, wpp = ple_gate[l].astype(BF16), ple_proj[l].astype(BF16)
        i = l // 2
        if l % 2 == 0:
            h = _ffn(h, norm_ffn[l][None], ffn_w1[i].astype(BF16), ffn_w3[i].astype(BF16),
                     ffn_w2[i].astype(BF16), pl_in, wpg, wpp)
        else:
            h = _moe(h, norm_ffn[l][None], moe_router[i], moe_w1[i].astype(BF16),
                     moe_w3[i].astype(BF16), moe_w2[i].astype(BF16), pl_in, wpg, wpp)
    return _final_norm(h, final_norm[None]).reshape(bsz, seq, d)
```

```python
import functools
import math

import numpy as np
import jax
import jax.numpy as jnp
from jax import lax
from jax.experimental import pallas as pl
from jax.experimental.pallas import tpu as pltpu

F32 = jnp.float32
BF16 = jnp.bfloat16

EPS = 1e-6
N_BRANCHES = 4
CONV_CH = 512
CONV_K = 31
SSM_HEADS = 8
SSM_HEAD_DIM = 64
SSM_INNER = SSM_HEADS * SSM_HEAD_DIM
SSM_GROUPS = 2
SSM_STATE = 64
SSM_CONV = 5
SSM_CONV_CH = SSM_INNER + 2 * SSM_GROUPS * SSM_STATE
SSM_CHUNK = 128
ATT_PATTERNS = ((128, 1), (512, 4), (2048, 16))
ATT_GROUPS = len(ATT_PATTERNS)
ATT_HEADS = 8
ATT_HEAD_DIM = 64
ATT_WIDTH = ATT_HEADS * ATT_HEAD_DIM
REL_BUCKETS = 32
REL_MAX_DIST = 1024
FNET_GROUPS = 4
FNET_GROUP_DIM = 128
FNET_WIDTH = FNET_GROUPS * FNET_GROUP_DIM
N_EXPERTS = 8
TOP_K = 2

ATT_IN_COLS = 3 * ATT_GROUPS * ATT_WIDTH
CONV_IN_COLS = 2 * CONV_CH
SSM_IN_COLS = SSM_INNER + SSM_CONV_CH + 2 * SSM_HEADS
OFF_CONV = ATT_IN_COLS
OFF_SSM = OFF_CONV + CONV_IN_COLS
OFF_FNET = OFF_SSM + SSM_IN_COLS

LANES = 128
DT_PAD = LANES
VMEM_LIMIT = 56 * 1024 * 1024


def _cparams(sem):
    return pltpu.CompilerParams(dimension_semantics=sem, vmem_limit_bytes=VMEM_LIMIT)


def _const_spec(shape):
    nd = len(shape)
    return pl.BlockSpec(shape, lambda *_: (0,) * nd, pipeline_mode=pl.Buffered(1))


def _rms(x, g):
    return x * lax.rsqrt(jnp.mean(x * x, axis=-1, keepdims=True) + EPS) * g


_SEC_QKV = (0, ATT_IN_COLS)
_SEC_CONV = (_SEC_QKV[0] + _SEC_QKV[1], CONV_IN_COLS)
_SEC_Z = (_SEC_CONV[0] + _SEC_CONV[1], SSM_INNER)
_SEC_XBC = (_SEC_Z[0] + _SEC_Z[1], SSM_CONV_CH)
_SEC_FNET = (_SEC_XBC[0] + _SEC_XBC[1], FNET_WIDTH)
_SEC_DT = (_SEC_FNET[0] + _SEC_FNET[1], DT_PAD)
_IN_COLS_PAD = _SEC_DT[0] + _SEC_DT[1]
_MM_CHUNK = 512


ATT_TILE = 256
QKV_COLS = 3 * ATT_WIDTH


def _reorder_w_in(w):
    d = w.shape[0]
    qkv = w[:, :OFF_CONV].reshape(d, 3, ATT_GROUPS, ATT_WIDTH).transpose(0, 2, 1, 3)
    ssm = w[:, OFF_SSM:OFF_FNET]
    dt = ssm[:, SSM_INNER + SSM_CONV_CH:]
    parts = [qkv.reshape(d, OFF_CONV), w[:, OFF_CONV:OFF_SSM], ssm[:, :SSM_INNER],
             ssm[:, SSM_INNER:SSM_INNER + SSM_CONV_CH], w[:, OFF_FNET:],
             dt, jnp.zeros((d, DT_PAD - dt.shape[1]), w.dtype)]
    return jnp.concatenate(parts, axis=1).astype(BF16)


def _deinterleave_matrix(dil):
    s = np.arange(ATT_TILE)
    m = np.zeros((ATT_TILE, ATT_TILE), np.float32)
    m[(s % dil) * (ATT_TILE // dil) + s // dil, s] = 1.0
    return m


def _inproj_kernel(h_ref, g_ref, w_ref, perm_ref, q0_ref, q1_ref, q2_ref, conv_ref, z_ref,
                   xbc_ref, fn_ref, dt_ref):
    xn = _rms(h_ref[...], g_ref[...]).astype(BF16)

    def section(x, sec, store):
        start, width = sec
        for c in range(0, width, _MM_CHUNK):
            cw = min(_MM_CHUNK, width - c)
            store(c, cw, jnp.dot(x, w_ref[:, start + c:start + c + cw],
                                 preferred_element_type=F32))

    def to(ref):
        def store(c, cw, val):
            ref[:, c:c + cw] = val.astype(ref.dtype)
        return store

    def to_fnet(c, cw, val):
        for g in range(cw // FNET_GROUP_DIM):
            fn_ref[0, c // FNET_GROUP_DIM + g] = val[:, g * FNET_GROUP_DIM:(g + 1) * FNET_GROUP_DIM]

    for g, q_ref in enumerate((q0_ref, q1_ref, q2_ref)):
        x = xn
        if ATT_PATTERNS[g][1] > 1:
            x = jnp.dot(perm_ref[g], xn, preferred_element_type=F32).astype(BF16)
        section(x, (g * QKV_COLS, QKV_COLS), to(q_ref))
    section(xn, _SEC_CONV, to(conv_ref))
    section(xn, _SEC_Z, to(z_ref))
    section(xn, _SEC_XBC, to(xbc_ref))
    section(xn, _SEC_FNET, to_fnet)
    section(xn, _SEC_DT, to(dt_ref))


def _inproj(h, g, w, bsz, seq):
    t, d = h.shape
    tm = ATT_TILE
    spt = seq // tm
    perm = jnp.asarray(np.stack([_deinterleave_matrix(dil) for _, dil in ATT_PATTERNS]), BF16)
    row = lambda n: pl.BlockSpec((tm, n), lambda i: (i, 0))
    qkv = jax.ShapeDtypeStruct((t, QKV_COLS), BF16)
    return pl.pallas_call(
        _inproj_kernel,
        grid=(t // tm,),
        in_specs=[row(d), _const_spec((1, d)), _const_spec(w.shape), _const_spec(perm.shape)],
        out_specs=[row(QKV_COLS), row(QKV_COLS), row(QKV_COLS), row(CONV_IN_COLS),
                   row(SSM_INNER), row(SSM_CONV_CH),
                   pl.BlockSpec((1, FNET_GROUPS, tm, FNET_GROUP_DIM),
                                lambda i: (i // spt, 0, i % spt, 0)),
                   row(DT_PAD)],
        out_shape=[qkv, qkv, qkv,
                   jax.ShapeDtypeStruct((t, CONV_IN_COLS), F32),
                   jax.ShapeDtypeStruct((t, SSM_INNER), F32),
                   jax.ShapeDtypeStruct((t, SSM_CONV_CH), F32),
                   jax.ShapeDtypeStruct((bsz, FNET_GROUPS, seq, FNET_GROUP_DIM), F32),
                   jax.ShapeDtypeStruct((t, DT_PAD), F32)],
        compiler_params=_cparams(("parallel",)),
        name="inproj",
    )(h, g, w, perm)


def _split_dot(v, m):
    hi = v.astype(BF16)
    lo = (v - hi.astype(F32)).astype(BF16)
    return (jnp.dot(hi, m, preferred_element_type=F32) + jnp.dot(lo, m, preferred_element_type=F32))


def _interleave(pt, v):
    if v.dtype == BF16:
        terms = [v]
    else:
        hi = v.astype(BF16)
        r1 = v - hi.astype(F32)
        mid = r1.astype(BF16)
        terms = [hi, mid, (r1 - mid.astype(F32)).astype(BF16)]
    tiles = []
    for i in range(v.shape[0] // ATT_TILE):
        rows = slice(i * ATT_TILE, (i + 1) * ATT_TILE)
        parts = [jnp.dot(pt, t[rows], preferred_element_type=F32) for t in terms]
        tiles.append(functools.reduce(lambda a, b: a + b, parts))
    return jnp.concatenate(tiles, axis=0)


def _mix_kernel(h_ref, g_ref, o0_ref, o1_ref, o2_ref, l0_ref, l1_ref, l2_ref, b1_ref, b2_ref,
                b3_ref, pt_ref, hx_ref, wbr_ref, wg_ref, cg_ref, wo_ref, o_ref):
    h = h_ref[...]
    xn = _rms(h, g_ref[...]).astype(BF16)
    outs, lses = [], []
    for g, (og_ref, lg_ref) in enumerate(zip((o0_ref, o1_ref, o2_ref), (l0_ref, l1_ref, l2_ref))):
        if ATT_PATTERNS[g][1] > 1:
            outs.append(_interleave(pt_ref[g], og_ref[...]))
            lses.append(_interleave(pt_ref[g], lg_ref[...]))
        else:
            outs.append(og_ref[...].astype(F32))
            lses.append(lg_ref[...])
    top = jnp.maximum(jnp.maximum(lses[0], lses[1]), lses[2])
    es = [jnp.exp(l - top) for l in lses]
    inv = 1.0 / (es[0] + es[1] + es[2])
    att = None
    for e, og in zip(es, outs):
        term = og * _split_dot(e * inv, hx_ref[...])
        att = term if att is None else att + term
    acc = None
    for b, hid in enumerate((att.astype(BF16), b1_ref[...], b2_ref[...], b3_ref[...])):
        gate = jax.nn.sigmoid(jnp.dot(xn, wg_ref[b], preferred_element_type=F32) + cg_ref[b])
        br = jnp.dot(hid, wbr_ref[b], preferred_element_type=F32)
        acc = gate * br if acc is None else acc + gate * br
    o_ref[...] = h + jnp.dot(acc.astype(BF16), wo_ref[...], preferred_element_type=F32)


def _mix(h, g, att, others, wbr, wg, cg, wo):
    t, d = h.shape
    tm = 512
    row = lambda n: pl.BlockSpec((tm, n), lambda i: (i, 0))
    head_expand = np.zeros((LANES, ATT_WIDTH), np.float32)
    for hd in range(ATT_HEADS):
        head_expand[hd, hd * ATT_HEAD_DIM:(hd + 1) * ATT_HEAD_DIM] = 1.0
    head_expand = jnp.asarray(head_expand, BF16)
    unperm = jnp.asarray(np.stack([_deinterleave_matrix(dil).T for _, dil in ATT_PATTERNS]), BF16)
    outs = [o for o, _ in att]
    lses = [l for _, l in att]
    return pl.pallas_call(
        _mix_kernel,
        grid=(t // tm,),
        in_specs=[row(d), _const_spec((1, d))] + [row(a.shape[1]) for a in outs + lses + list(others)]
                 + [_const_spec(unperm.shape), _const_spec(head_expand.shape), _const_spec(wbr.shape),
                    _const_spec(wg.shape), _const_spec(cg.shape), _const_spec(wo.shape)],
        out_specs=row(d),
        out_shape=jax.ShapeDtypeStruct((t, d), F32),
        compiler_params=_cparams(("parallel",)),
        name="mix",
    )(h, g, *outs, *lses, *others, unperm, head_expand, wbr, wg, cg, wo)


def _ple(h2, p_ref, wpg_ref, wpp_ref):
    gate = jax.nn.sigmoid(jnp.dot(h2.astype(BF16), wpg_ref[...], preferred_element_type=F32))
    pe = jnp.dot(p_ref[...].astype(BF16), wpp_ref[...], preferred_element_type=F32)
    return h2 + gate * pe


def _swiglu_partial(xn, w1, w3, w2, scale=None):
    a = jnp.dot(xn, w1, preferred_element_type=F32)
    b = jnp.dot(xn, w3, preferred_element_type=F32)
    hid = a * jax.nn.sigmoid(a) * b
    if scale is not None:
        hid = hid * scale
    return jnp.dot(hid.astype(BF16), w2, preferred_element_type=F32)


def _ffn_kernel(h_ref, g_ref, w1_ref, w3_ref, w2_ref, p_ref, wpg_ref, wpp_ref, o_ref, xn_ref):
    j = pl.program_id(1)

    @pl.when(j == 0)
    def _():
        h = h_ref[...]
        xn_ref[...] = _rms(h, g_ref[...]).astype(BF16)
        o_ref[...] = h

    o_ref[...] += _swiglu_partial(xn_ref[...], w1_ref[...], w3_ref[...], w2_ref[...])

    @pl.when(j == pl.num_programs(1) - 1)
    def _():
        o_ref[...] = _ple(o_ref[...], p_ref, wpg_ref, wpp_ref)


def _ffn(h, g, w1, w3, w2, p, wpg, wpp):
    t, d = h.shape
    f = w1.shape[1]
    tm, tf = 512, 1408
    row = lambda n: pl.BlockSpec((tm, n), lambda i, j: (i, 0))
    return pl.pallas_call(
        _ffn_kernel,
        grid=(t // tm, f // tf),
        in_specs=[row(d), _const_spec((1, d)),
                  pl.BlockSpec((d, tf), lambda i, j: (0, j)),
                  pl.BlockSpec((d, tf), lambda i, j: (0, j)),
                  pl.BlockSpec((tf, d), lambda i, j: (j, 0)),
                  row(p.shape[1]), _const_spec(wpg.shape), _const_spec(wpp.shape)],
        out_specs=row(d),
        out_shape=jax.ShapeDtypeStruct((t, d), F32),
        scratch_shapes=[pltpu.VMEM((tm, d), BF16)],
        compiler_params=_cparams(("parallel", "arbitrary")),
        name="ffn",
    )(h, g, w1, w3, w2, p, wpg, wpp)


MOE_ROWS = 128


def _moe_route(logits):
    ne, tm = logits.shape
    eidx = lax.broadcasted_iota(jnp.int32, logits.shape, 0)
    m1 = jnp.max(logits, axis=0, keepdims=True)
    i1 = jnp.min(jnp.where(logits == m1, eidx, ne), axis=0, keepdims=True)
    rest = jnp.where(eidx == i1, -jnp.inf, logits)
    m2 = jnp.max(rest, axis=0, keepdims=True)
    i2 = jnp.min(jnp.where(rest == m2, eidx, ne), axis=0, keepdims=True)
    e2 = jnp.exp(m2 - m1)
    den = 1.0 + e2
    combine = jnp.where(eidx == i1, 1.0 / den, 0.0) + jnp.where(eidx == i2, e2 / den, 0.0)
    routed = jnp.where((eidx == i1) | (eidx == i2), 1.0, 0.0)
    r = lax.broadcasted_iota(jnp.int32, (LANES, LANES), 0)
    c = lax.broadcasted_iota(jnp.int32, (LANES, LANES), 1)
    before = jnp.where(r < c, 1.0, 0.0).astype(BF16)
    counts = jnp.zeros((ne, 1), F32)
    slots = []
    for k in range(tm // LANES):
        blk = routed[:, k * LANES:(k + 1) * LANES]
        slots.append(jnp.dot(blk.astype(BF16), before, preferred_element_type=F32) + counts)
        counts = counts + jnp.sum(blk, axis=1, keepdims=True)
    slot = jnp.where(routed > 0.0, jnp.concatenate(slots, axis=1), -1.0).astype(jnp.int32)
    return combine, slot, counts


def _moe_kernel(h_ref, g_ref, rt_ref, w1_ref, w3_ref, w2_ref, p_ref, wpg_ref, wpp_ref, o_ref,
                xn_ref, comb_ref, slot_ref, cnt_ref, xe_ref, ye_ref):
    e = pl.program_id(1)
    j = pl.program_id(2)
    ne = pl.num_programs(1)
    tm = xn_ref.shape[0]

    @pl.when((e == 0) & (j == 0))
    def _():
        h = h_ref[...]
        xn = _rms(h, g_ref[...])
        xn_ref[...] = xn.astype(BF16)
        logits = lax.dot_general(rt_ref[...], xn, (((1,), (1,)), ((), ())),
                                 preferred_element_type=F32, precision=HI)
        combine, slot, counts = _moe_route(logits)
        comb_ref[...] = combine
        slot_ref[...] = slot
        for k in range(comb_ref.shape[0]):
            cnt_ref[k] = jnp.sum(counts[k:k + 1, :]).astype(jnp.int32)
        o_ref[...] = h

    n_blocks = (cnt_ref[e] + MOE_ROWS - 1) // MOE_ROWS

    def for_row_blocks(body):
        def pair(i, carry):
            body(pl.multiple_of(i * 2 * MOE_ROWS, 2 * MOE_ROWS), 2 * MOE_ROWS)
            return carry
        lax.fori_loop(0, n_blocks // 2, pair, 0)

        @pl.when(n_blocks % 2 == 1)
        def _():
            body(pl.multiple_of((n_blocks - 1) * MOE_ROWS, MOE_ROWS), MOE_ROWS)

    def one_hot(r0, rows):
        return slot_ref[pl.ds(e, 1), :] == lax.broadcasted_iota(jnp.int32, (rows, tm), 0) + r0

    @pl.when(j == 0)
    def _():
        def gather(r0, rows):
            sel = jnp.where(one_hot(r0, rows), 1.0, 0.0).astype(BF16)
            xe_ref[pl.ds(r0, rows), :] = jnp.dot(
                sel, xn_ref[...], preferred_element_type=F32).astype(BF16)
            ye_ref[pl.ds(r0, rows), :] = jnp.zeros((rows, ye_ref.shape[1]), F32)
        for_row_blocks(gather)

    def expert(r0, rows):
        ye_ref[pl.ds(r0, rows), :] += _swiglu_partial(
            xe_ref[pl.ds(r0, rows), :], w1_ref[0], w3_ref[0], w2_ref[0])
    for_row_blocks(expert)

    @pl.when(j == pl.num_programs(2) - 1)
    def _():
        def scatter(r0, rows):
            hot = one_hot(r0, rows)
            weight = jnp.sum(jnp.where(hot, comb_ref[pl.ds(e, 1), :], 0.0), axis=1, keepdims=True)
            yw = (ye_ref[pl.ds(r0, rows), :] * weight).astype(BF16)
            o_ref[...] += lax.dot_general(jnp.where(hot, 1.0, 0.0).astype(BF16), yw,
                                          (((0,), (0,)), ((), ())), preferred_element_type=F32)
        for_row_blocks(scatter)

    @pl.when((e == ne - 1) & (j == pl.num_programs(2) - 1))
    def _():
        o_ref[...] = _ple(o_ref[...], p_ref, wpg_ref, wpp_ref)


def _moe(h, g, router, w1, w3, w2, p, wpg, wpp):
    t, d = h.shape
    ne, _, f = w1.shape
    tm, tf = 1024, 896
    row = lambda n: pl.BlockSpec((tm, n), lambda i, e, j: (i, 0))
    return pl.pallas_call(
        _moe_kernel,
        grid=(t // tm, ne, f // tf),
        in_specs=[row(d), _const_spec((1, d)), _const_spec((ne, d)),
                  pl.BlockSpec((1, d, tf), lambda i, e, j: (e, 0, j)),
                  pl.BlockSpec((1, d, tf), lambda i, e, j: (e, 0, j)),
                  pl.BlockSpec((1, tf, d), lambda i, e, j: (e, j, 0)),
                  row(p.shape[1]), _const_spec(wpg.shape), _const_spec(wpp.shape)],
        out_specs=row(d),
        out_shape=jax.ShapeDtypeStruct((t, d), F32),
        scratch_shapes=[pltpu.VMEM((tm, d), BF16), pltpu.VMEM((ne, tm), F32),
                        pltpu.VMEM((ne, tm), jnp.int32), pltpu.SMEM((ne,), jnp.int32),
                        pltpu.VMEM((tm, d), BF16), pltpu.VMEM((tm, d), F32)],
        compiler_params=_cparams(("parallel", "arbitrary", "arbitrary")),
        name="moe",
    )(h, g, router.T, w1, w3, w2, p, wpg, wpp)


def _final_norm_kernel(h_ref, g_ref, o_ref):
    o_ref[...] = _rms(h_ref[...], g_ref[...])


def _final_norm(h, g):
    t, d = h.shape
    tm = 1024
    row = pl.BlockSpec((tm, d), lambda i: (i, 0))
    return pl.pallas_call(
        _final_norm_kernel, grid=(t // tm,),
        in_specs=[row, _const_spec((1, d))], out_specs=row,
        out_shape=jax.ShapeDtypeStruct((t, d), F32),
        compiler_params=_cparams(("parallel",)), name="final_norm",
    )(h, g)


ATT_HALF = 64
ATT_QB = 128
ATT_KB = ATT_QB + 2 * ATT_HALF
NEG = -1e30
assert all(w // (2 * d) == ATT_HALF for w, d in ATT_PATTERNS)


def _t5_bucket(rel):
    half = REL_BUCKETS // 2
    max_exact = half // 2
    n = np.abs(rel)
    large = max_exact + (np.log(np.maximum(n, 1) / max_exact) / math.log(REL_MAX_DIST / max_exact)
                         * (half - max_exact)).astype(np.int32)
    large = np.minimum(large, half - 1)
    return np.where(rel > 0, half, 0) + np.where(n < max_exact, n, large)


def _att_bias_tables(rel_bias, g, dil):
    i = np.arange(ATT_QB)[:, None]
    j = np.arange(ATT_KB)[None, :]
    rel = j - ATT_HALF - i
    band = np.abs(rel) <= ATT_HALF
    pick = np.eye(REL_BUCKETS, dtype=np.float32)[_t5_bucket(dil * rel)]
    heads = rel_bias[:, g * ATT_HEADS:(g + 1) * ATT_HEADS].astype(F32)
    bias = jnp.einsum('qkb,bh->hqk', pick, heads, precision=lax.Precision.HIGHEST)
    tables = []
    for v in range(4):
        ok = band
        if v & 1:
            ok = ok & (j >= ATT_HALF)
        if v & 2:
            ok = ok & (j < ATT_QB + ATT_HALF)
        tables.append(jnp.where(ok[None], bias, NEG))
    return jnp.stack(tables)


def _attn_kernel(q_ref, k_ref, v_ref, kp_ref, kn_ref, vp_ref, vn_ref, bias_ref, o_ref, lse_ref,
                 qbuf, kbuf, vbuf, obuf, lbuf, *, tq, n_blocks):
    flat = lambda ref: ref[0].reshape(-1, ref.shape[-1])
    qbuf[...] = flat(q_ref)
    kbuf[0:ATT_HALF] = flat(kp_ref)
    kbuf[ATT_HALF:ATT_HALF + tq] = flat(k_ref)
    kbuf[ATT_HALF + tq:] = flat(kn_ref)
    vbuf[0:ATT_HALF] = flat(vp_ref)
    vbuf[ATT_HALF:ATT_HALF + tq] = flat(v_ref)
    vbuf[ATT_HALF + tq:] = flat(vn_ref)
    nsb = tq // ATT_QB
    first = pl.program_id(2) * nsb
    lane = lax.broadcasted_iota(jnp.int32, (ATT_QB, LANES), 1)
    low = lane < ATT_HEAD_DIM
    lane_row = lax.broadcasted_iota(jnp.int32, (1, LANES), 1)
    keep = [(lane_row < ATT_HEAD_DIM).astype(BF16), (lane_row >= ATT_HEAD_DIM).astype(BF16)]
    ones = jnp.ones((ATT_KB, LANES), BF16)

    def block(sb, carry):
        r0 = pl.multiple_of(sb * ATT_QB, ATT_QB)
        gsb = first + sb
        variant = (gsb == 0).astype(jnp.int32) + 2 * (gsb == n_blocks - 1).astype(jnp.int32)
        q = qbuf[pl.ds(r0, ATT_QB), :] * (ATT_HEAD_DIM ** -0.5)
        lse_all = jnp.zeros((ATT_QB, LANES), F32)
        outs = []
        for pair in range(ATT_HEADS // 2):
            cols = slice(pair * LANES, (pair + 1) * LANES)
            qp = q[:, cols]
            kp = kbuf[pl.ds(r0, ATT_KB), cols]
            vp = jnp.concatenate([vbuf[pl.ds(r0, ATT_KB), cols], ones], axis=1)
            res = []
            for half in range(2):
                h = 2 * pair + half
                s = lax.dot_general(qp * keep[half], kp, (((1,), (1,)), ((), ())),
                                    preferred_element_type=F32)
                s = s + bias_ref[variant, h]
                m = jnp.max(s, axis=-1, keepdims=True)
                e = jnp.exp(s - m)
                pv = jnp.dot(e.astype(BF16), vp, preferred_element_type=F32)
                den = pv[:, LANES:]
                res.append(pv[:, :LANES] / den)
                lse_all = jnp.where(lane == h, m + jnp.log(den), lse_all)
            outs.append(jnp.where(low, res[0], res[1]))
        obuf[pl.ds(r0, ATT_QB), :] = jnp.concatenate(outs, axis=1).astype(obuf.dtype)
        lbuf[pl.ds(r0, ATT_QB), :] = lse_all
        return carry

    lax.fori_loop(0, nsb, block, 0)
    o_ref[0] = obuf[...].reshape(o_ref.shape[1:])
    lse_ref[0] = lbuf[...].reshape(lse_ref.shape[1:])


def _attention_group(qkv, bias_tables, dil, bsz, seq, name):
    sub_len = seq // dil
    assert sub_len % ATT_QB == 0 and seq % ATT_TILE == 0 and ATT_TILE % dil == 0
    rows = ATT_TILE // dil
    tq = min(512, sub_len)
    w = ATT_WIDTH
    hrows = min(rows, ATT_HALF)
    n_hb = sub_len // ATT_HALF

    def view(a, chunk):
        return a.reshape(bsz, (seq // ATT_TILE) * (rows // chunk), dil, chunk, a.shape[-1])

    def main(part):
        return pl.BlockSpec((1, tq // rows, None, rows, w), lambda b, r, n: (b, n, r, 0, part))

    def halo(part, nxt):
        if nxt:
            blk = lambda n: jnp.minimum((n + 1) * (tq // ATT_HALF), n_hb - 1)
        else:
            blk = lambda n: jnp.maximum(n * (tq // ATT_HALF) - 1, 0)
        return pl.BlockSpec((1, ATT_HALF // hrows, None, hrows, w),
                            lambda b, r, n: (b, blk(n), r, 0, part))

    mv, hv = view(qkv, rows), view(qkv, hrows)
    o, lse = pl.pallas_call(
        functools.partial(_attn_kernel, tq=tq, n_blocks=sub_len // ATT_QB),
        grid=(bsz, dil, sub_len // tq),
        in_specs=[main(0), main(1), main(2), halo(1, False), halo(1, True), halo(2, False),
                  halo(2, True), _const_spec(bias_tables.shape)],
        out_specs=[pl.BlockSpec((1, tq // rows, None, rows, w), lambda b, r, n: (b, n, r, 0, 0)),
                   pl.BlockSpec((1, tq // rows, None, rows, LANES), lambda b, r, n: (b, n, r, 0, 0))],
        out_shape=[jax.ShapeDtypeStruct((bsz, seq // ATT_TILE, dil, rows, w), BF16),
                   jax.ShapeDtypeStruct((bsz, seq // ATT_TILE, dil, rows, LANES), F32)],
        scratch_shapes=[pltpu.VMEM((tq, w), BF16),
                        pltpu.VMEM((tq + 2 * ATT_HALF, w), BF16),
                        pltpu.VMEM((tq + 2 * ATT_HALF, w), BF16),
                        pltpu.VMEM((tq, w), BF16), pltpu.VMEM((tq, LANES), F32)],
        compiler_params=_cparams(("parallel", "parallel", "parallel")),
        name=name,
    )(mv, mv, mv, hv, hv, hv, hv, bias_tables)
    return o.reshape(bsz * seq, w), lse.reshape(bsz * seq, LANES)


SUBLANES = 8
CONV_HALO = 16
CONV_ROWS = 64


def _conformer_kernel(u_ref, up_ref, un_ref, dw_ref, dwb_ref, lng_ref, lnb_ref, o_ref,
                      hp_ref, sh_ref, *, ts):
    n = pl.program_id(1)

    def glu(u):
        return u[:, :CONV_CH] * jax.nn.sigmoid(u[:, CONV_CH:])

    hp_ref[0:CONV_HALO] = jnp.where(n > 0, glu(up_ref[0]), 0.0)
    hp_ref[CONV_HALO:CONV_HALO + ts] = glu(u_ref[0])
    hp_ref[CONV_HALO + ts:] = jnp.where(n < pl.num_programs(1) - 1, glu(un_ref[0]), 0.0)
    span = ts + 2 * CONV_HALO - SUBLANES
    for b in range(SUBLANES):
        sh_ref[b] = hp_ref[pl.ds(b, span), :]
    first = CONV_HALO - CONV_K // 2

    def rows(c, carry):
        r0 = pl.multiple_of(c * CONV_ROWS, CONV_ROWS)
        acc = jnp.broadcast_to(dwb_ref[...], (CONV_ROWS, CONV_CH))
        for k in range(CONV_K):
            a, b = divmod(first + k, SUBLANES)
            acc = acc + dw_ref[pl.ds(k, 1), :] * sh_ref[b, pl.ds(r0 + a * SUBLANES, CONV_ROWS), :]
        mu = jnp.mean(acc, axis=-1, keepdims=True)
        cen = acc - mu
        var = jnp.mean(cen * cen, axis=-1, keepdims=True)
        y = cen * lax.rsqrt(var + EPS) * lng_ref[...] + lnb_ref[...]
        o_ref[0, pl.ds(r0, CONV_ROWS), :] = (y * jax.nn.sigmoid(y)).astype(o_ref.dtype)
        return carry

    lax.fori_loop(0, ts // CONV_ROWS, rows, 0)


def _conformer(u, dw, dw_b, ln_g, ln_b, bsz, seq):
    ts = min(512, seq)
    view = u.reshape(bsz, seq, 2 * CONV_CH)
    hb = ts // CONV_HALO
    n_hb = seq // CONV_HALO
    out = pl.pallas_call(
        functools.partial(_conformer_kernel, ts=ts),
        grid=(bsz, seq // ts),
        in_specs=[pl.BlockSpec((1, ts, 2 * CONV_CH), lambda b, n: (b, n, 0)),
                  pl.BlockSpec((1, CONV_HALO, 2 * CONV_CH),
                               lambda b, n: (b, jnp.maximum(n * hb - 1, 0), 0)),
                  pl.BlockSpec((1, CONV_HALO, 2 * CONV_CH),
                               lambda b, n: (b, jnp.minimum((n + 1) * hb, n_hb - 1), 0)),
                  _const_spec(dw.shape), _const_spec((1, CONV_CH)), _const_spec((1, CONV_CH)),
                  _const_spec((1, CONV_CH))],
        out_specs=pl.BlockSpec((1, ts, CONV_CH), lambda b, n: (b, n, 0)),
        out_shape=jax.ShapeDtypeStruct((bsz, seq, CONV_CH), BF16),
        scratch_shapes=[pltpu.VMEM((ts + 2 * CONV_HALO, CONV_CH), F32),
                        pltpu.VMEM((SUBLANES, ts + 2 * CONV_HALO - SUBLANES, CONV_CH), F32)],
        compiler_params=_cparams(("parallel", "parallel")),
        name="conformer",
    )(view, view, view, dw, dw_b[None], ln_g[None], ln_b[None])
    return out.reshape(bsz * seq, CONV_CH)


SSD_HALO = SUBLANES
SSD_ROWS = 64
SSD_BC = SSM_GROUPS * SSM_STATE
HEADS_PER_GROUP = SSM_HEADS // SSM_GROUPS
GROUP_LANES = HEADS_PER_GROUP * SSM_HEAD_DIM
HI = lax.Precision.HIGHEST
assert SSD_BC == LANES and SSM_CHUNK == LANES and 2 * SSM_HEADS <= LANES


def _ssd_pre_kernel(x_ref, xp_ref, xn_ref, dt_ref, cw_ref, cb_ref, dtb_ref, xo_ref, dto_ref,
                    hp_ref, sh_ref, *, ts, phases):
    n = pl.program_id(1)
    hp_ref[0:SSD_HALO] = jnp.where(n > 0, xp_ref[0], 0.0)
    hp_ref[SSD_HALO:SSD_HALO + ts] = x_ref[0]
    hp_ref[SSD_HALO + ts:] = jnp.where(n < pl.num_programs(1) - 1, xn_ref[0], 0.0)
    for i, b in enumerate(phases):
        sh_ref[i] = hp_ref[pl.ds(b, ts + SSD_HALO), :]
    first = SSD_HALO - SSM_CONV // 2

    def rows(c, carry):
        r0 = pl.multiple_of(c * SSD_ROWS, SSD_ROWS)
        acc = jnp.broadcast_to(cb_ref[...], (SSD_ROWS, SSM_CONV_CH))
        for k in range(SSM_CONV):
            a, b = divmod(first + k, SUBLANES)
            acc = acc + cw_ref[pl.ds(k, 1), :] * sh_ref[phases.index(b),
                                                        pl.ds(r0 + a * SUBLANES, SSD_ROWS), :]
        xo_ref[0, pl.ds(r0, SSD_ROWS), :] = acc * jax.nn.sigmoid(acc)
        return carry

    lax.fori_loop(0, ts // SSD_ROWS, rows, 0)
    x = dt_ref[0] + dtb_ref[...]
    softplus = jnp.maximum(x, 0.0) + jnp.log1p(jnp.exp(-jnp.abs(x)))
    lane = lax.broadcasted_iota(jnp.int32, x.shape, 1)
    dto_ref[0] = jnp.where(lane < 2 * SSM_HEADS, softplus, 0.0)


def _ssd_pre(xbc, dt_raw, conv_w, conv_b, dt_bias, bsz, seq):
    ts = min(512, seq)
    xv = xbc.reshape(bsz, seq, SSM_CONV_CH)
    dv = dt_raw.reshape(bsz, seq, DT_PAD)
    hb = ts // SSD_HALO
    n_hb = seq // SSD_HALO
    first = SSD_HALO - SSM_CONV // 2
    phases = tuple(sorted({(first + k) % SUBLANES for k in range(SSM_CONV)}))
    dtb = jnp.zeros((1, DT_PAD), F32).at[0, :2 * SSM_HEADS].set(dt_bias.reshape(-1))
    return pl.pallas_call(
        functools.partial(_ssd_pre_kernel, ts=ts, phases=phases),
        grid=(bsz, seq // ts),
        in_specs=[pl.BlockSpec((1, ts, SSM_CONV_CH), lambda b, n: (b, n, 0)),
                  pl.BlockSpec((1, SSD_HALO, SSM_CONV_CH),
                               lambda b, n: (b, jnp.maximum(n * hb - 1, 0), 0)),
                  pl.BlockSpec((1, SSD_HALO, SSM_CONV_CH),
                               lambda b, n: (b, jnp.minimum((n + 1) * hb, n_hb - 1), 0)),
                  pl.BlockSpec((1, ts, DT_PAD), lambda b, n: (b, n, 0)),
                  _const_spec(conv_w.shape), _const_spec((1, SSM_CONV_CH)),
                  _const_spec((1, DT_PAD))],
        out_specs=[pl.BlockSpec((1, ts, SSM_CONV_CH), lambda b, n: (b, n, 0)),
                   pl.BlockSpec((1, ts, DT_PAD), lambda b, n: (b, n, 0))],
        out_shape=[jax.ShapeDtypeStruct((bsz, seq, SSM_CONV_CH), F32),
                   jax.ShapeDtypeStruct((bsz, seq, DT_PAD), F32)],
        scratch_shapes=[pltpu.VMEM((ts + 2 * SSD_HALO, SSM_CONV_CH), F32),
                        pltpu.VMEM((len(phases), ts + SSD_HALO, SSM_CONV_CH), F32)],
        compiler_params=_cparams(("parallel", "parallel")),
        name="ssd_pre",
    )(xv, xv, xv, dv, conv_w, conv_b[None], dtb)


def _ssd_consts(a_log):
    a_row = jnp.zeros((1, DT_PAD), F32).at[0, :2 * SSM_HEADS].set(-jnp.exp(a_log.reshape(-1)))
    tri = jnp.asarray(np.tril(np.ones((SSM_CHUNK, SSM_CHUNK), np.float32)))
    expand = np.zeros((2, DT_PAD, SSM_INNER), np.float32)
    for d in range(2):
        for h in range(SSM_HEADS):
            expand[d, d * SSM_HEADS + h, h * SSM_HEAD_DIM:(h + 1) * SSM_HEAD_DIM] = 1.0
    return a_row, tri, jnp.asarray(expand, BF16)


def _cumsum_both(dta, tri_ref):
    fwd = jnp.dot(tri_ref[...], dta, preferred_element_type=F32, precision=HI)
    bwd = lax.dot_general(tri_ref[...], dta, (((0,), (0,)), ((), ())),
                          preferred_element_type=F32, precision=HI)
    lane = lax.broadcasted_iota(jnp.int32, dta.shape, 1)
    return jnp.where(lane < SSM_HEADS, fwd, bwd)


def _expand(v, ex_ref, d):
    return _split_dot(v, ex_ref[d])


def _ssd_state_kernel(xf_ref, xb_ref, dtf_ref, dtb_ref, arow_ref, tri_ref, ex_ref,
                      pf_ref, pb_ref, sf_ref, sb_ref, *, nch):
    @pl.when(pl.program_id(1) == 0)
    def _():
        sf_ref[...] = jnp.zeros_like(sf_ref)
        sb_ref[...] = jnp.zeros_like(sb_ref)

    def contribution(d, x_ref, dt_ref, k):
        rows = slice(k * SSM_CHUNK, (k + 1) * SSM_CHUNK)
        dtv = dt_ref[0, rows, :]
        acs = _cumsum_both(dtv * arow_ref[...], tri_ref)
        total = acs[SSM_CHUNK - 1:SSM_CHUNK, :] if d == 0 else acs[0:1, :]
        lane = lax.broadcasted_iota(jnp.int32, acs.shape, 1)
        mine = (lane >= d * SSM_HEADS) & (lane < (d + 1) * SSM_HEADS)
        w = dtv * jnp.exp(jnp.where(mine, total - acs, 0.0))
        xw = (x_ref[0, rows, :SSM_INNER] * _expand(w, ex_ref, d)).astype(BF16)
        bmat = x_ref[0, rows, SSM_INNER:SSM_INNER + SSD_BC].astype(BF16)
        new = []
        for g in range(SSM_GROUPS):
            new.append(lax.dot_general(
                bmat[:, g * SSM_STATE:(g + 1) * SSM_STATE],
                xw[:, g * GROUP_LANES:(g + 1) * GROUP_LANES],
                (((0,), (0,)), ((), ())), preferred_element_type=F32))
        carry = _expand(jnp.broadcast_to(jnp.exp(total), (SUBLANES, DT_PAD)), ex_ref, d)[0:1]
        return jnp.concatenate(new, axis=1), carry

    def scan(d, x_ref, dt_ref, st_ref, out_ref):
        order = list(range(nch)) if d == 0 else list(range(nch - 1, -1, -1))
        parts = [contribution(d, x_ref, dt_ref, k) for k in order]
        st = st_ref[...]
        for k, (new, carry) in zip(order, parts):
            out_ref[0, k] = st.astype(out_ref.dtype)
            st = st * carry + new
        st_ref[...] = st

    scan(0, xf_ref, dtf_ref, sf_ref, pf_ref)
    scan(1, xb_ref, dtb_ref, sb_ref, pb_ref)


SSD_STATE_CHUNKS = 4


def _ssd_states(xact, dtv, consts, bsz, seq):
    nc = seq // SSM_CHUNK
    nch = min(SSD_STATE_CHUNKS, nc)
    steps = nc // nch
    a_row, tri, expand = consts
    fwd = lambda b, c: (b, c, 0)
    bwd = lambda b, c: (b, steps - 1 - c, 0)
    st = jax.ShapeDtypeStruct((bsz, nc, SSM_STATE, SSM_INNER), BF16)
    return pl.pallas_call(
        functools.partial(_ssd_state_kernel, nch=nch),
        grid=(bsz, steps),
        in_specs=[pl.BlockSpec((1, nch * SSM_CHUNK, SSM_CONV_CH), fwd),
                  pl.BlockSpec((1, nch * SSM_CHUNK, SSM_CONV_CH), bwd),
                  pl.BlockSpec((1, nch * SSM_CHUNK, DT_PAD), fwd),
                  pl.BlockSpec((1, nch * SSM_CHUNK, DT_PAD), bwd),
                  _const_spec(a_row.shape), _const_spec(tri.shape), _const_spec(expand.shape)],
        out_specs=[pl.BlockSpec((1, nch, SSM_STATE, SSM_INNER), lambda b, c: (b, c, 0, 0)),
                   pl.BlockSpec((1, nch, SSM_STATE, SSM_INNER),
                                lambda b, c: (b, steps - 1 - c, 0, 0))],
        out_shape=[st, st],
        scratch_shapes=[pltpu.VMEM((SSM_STATE, SSM_INNER), F32),
                        pltpu.VMEM((SSM_STATE, SSM_INNER), F32)],
        compiler_params=_cparams(("parallel", "arbitrary")),
        name="ssd_states",
    )(xact, xact, dtv, dtv, a_row, tri, expand)


def _ssd_out_kernel(x_ref, dt_ref, z_ref, pf_ref, pb_ref, arow_ref, tri_ref, ex_ref, dskip_ref,
                    ng_ref, o_ref):
    xs = x_ref[0, :, :SSM_INNER]
    bmat = x_ref[0, :, SSM_INNER:SSM_INNER + SSD_BC].astype(BF16)
    cmat = x_ref[0, :, SSM_INNER + SSD_BC:].astype(BF16)
    dtv = dt_ref[0]
    acs = _cumsum_both(dtv * arow_ref[...], tri_ref)
    acs_t = acs.T
    dt_t = dtv.T
    eacs = jnp.exp(acs)
    row = lax.broadcasted_iota(jnp.int32, (SSM_CHUNK, SSM_CHUNK), 0)
    col = lax.broadcasted_iota(jnp.int32, (SSM_CHUNK, SSM_CHUNK), 1)
    past, now = col < row, col == row
    low = lax.broadcasted_iota(jnp.int32, (SSM_CHUNK, LANES), 1) < SSM_HEAD_DIM
    scores = [lax.dot_general(cmat[:, g * SSM_STATE:(g + 1) * SSM_STATE],
                              bmat[:, g * SSM_STATE:(g + 1) * SSM_STATE],
                              (((1,), (1,)), ((), ())), preferred_element_type=F32)
              for g in range(SSM_GROUPS)]
    xs_b = xs.astype(BF16)
    diag = []
    for pair in range(SSM_HEADS // 2):
        xp = xs_b[:, pair * LANES:(pair + 1) * LANES]
        res = []
        for half in range(2):
            f = 2 * pair + half
            b = SSM_HEADS + f
            seg = jnp.where(past | now, acs[:, f:f + 1] - acs_t[f:f + 1, :],
                            acs[:, b:b + 1] - acs_t[b:b + 1, :])
            dts = (jnp.where(past, dt_t[f:f + 1, :], dt_t[b:b + 1, :])
                   + jnp.where(now, dt_t[f:f + 1, :], 0.0))
            m = (scores[f // HEADS_PER_GROUP] * jnp.exp(seg) * dts).astype(BF16)
            res.append(jnp.dot(m, xp, preferred_element_type=F32))
        diag.append(jnp.where(low, res[0], res[1]))
    y = dskip_ref[...] * xs + jnp.concatenate(diag, axis=1)
    for d, p_ref in enumerate((pf_ref, pb_ref)):
        off = [jnp.dot(cmat[:, g * SSM_STATE:(g + 1) * SSM_STATE],
                       p_ref[0, 0, :, g * GROUP_LANES:(g + 1) * GROUP_LANES],
                       preferred_element_type=F32) for g in range(SSM_GROUPS)]
        y = y + jnp.concatenate(off, axis=1) * _expand(eacs, ex_ref, d)
    z = z_ref[0]
    y = y * (z * jax.nn.sigmoid(z))
    o_ref[0] = _rms(y, ng_ref[...]).astype(o_ref.dtype)


def _ssd_out(xact, dtv, z, prev_f, prev_b, consts, d_skip, norm_g, bsz, seq):
    nc = seq // SSM_CHUNK
    a_row, tri, expand = consts
    dsk = jnp.repeat(d_skip, SSM_HEAD_DIM)[None]
    chunk = lambda n: pl.BlockSpec((1, SSM_CHUNK, n), lambda b, c: (b, c, 0))
    state = pl.BlockSpec((1, 1, SSM_STATE, SSM_INNER), lambda b, c: (b, c, 0, 0))
    out = pl.pallas_call(
        _ssd_out_kernel,
        grid=(bsz, nc),
        in_specs=[chunk(SSM_CONV_CH), chunk(DT_PAD), chunk(SSM_INNER), state, state,
                  _const_spec(a_row.shape), _const_spec(tri.shape), _const_spec(expand.shape),
                  _const_spec(dsk.shape), _const_spec((1, SSM_INNER))],
        out_specs=chunk(SSM_INNER),
        out_shape=jax.ShapeDtypeStruct((bsz, seq, SSM_INNER), BF16),
        compiler_params=_cparams(("parallel", "parallel")),
        name="ssd_out",
    )(xact, dtv, z.reshape(bsz, seq, SSM_INNER), prev_f, prev_b, a_row, tri, expand, dsk,
      norm_g[None])
    return out.reshape(bsz * seq, SSM_INNER)


def _ssd(z, xbc, dt_raw, conv_w, conv_b, a_log, dt_bias, d_skip, norm_g, bsz, seq):
    xact, dtv = _ssd_pre(xbc, dt_raw, conv_w, conv_b, dt_bias, bsz, seq)
    consts = _ssd_consts(a_log)
    prev_f, prev_b = _ssd_states(xact, dtv, consts, bsz, seq)
    return _ssd_out(xact, dtv, z, prev_f, prev_b, consts, d_skip, norm_g, bsz, seq)


FNET_COLS = 4096


def _dft_cos_sin(n):
    ang = 2.0 * np.pi * np.outer(np.arange(n), np.arange(n)) / n
    return np.cos(ang), np.sin(ang)


def _fnet_consts(seq):
    c = FNET_GROUP_DIM
    n1 = seq // LANES
    c1, s1 = _dft_cos_sin(n1)
    stage1 = np.concatenate([c1, -s1], axis=0)
    ang = 2.0 * np.pi * np.outer(np.arange(n1), np.arange(LANES)) / seq
    twr = np.repeat(np.cos(ang), c, axis=1)
    twi = np.repeat(-np.sin(ang), c, axis=1)
    cc, sc = _dft_cos_sin(c)
    chan = np.block([[cc, -sc], [sc, cc]])
    c2, s2 = _dft_cos_sin(LANES)
    return (jnp.asarray(stage1, BF16), jnp.asarray(twr, F32), jnp.asarray(twi, F32),
            jnp.asarray(chan, BF16), jnp.asarray(c2, BF16), jnp.asarray(s2, BF16))


def _fnet1_kernel(x_ref, f_ref, twr_ref, twi_ref, o_ref, *, n1):
    c = FNET_GROUP_DIM
    a = jnp.dot(f_ref[...], x_ref[0].astype(BF16), preferred_element_type=F32)
    ar, ai = a[:n1], a[n1:]
    twr, twi = twr_ref[...], twi_ref[...]
    re = (ar * twr - ai * twi).astype(o_ref.dtype)
    im = (ar * twi + ai * twr).astype(o_ref.dtype)
    for j in range(re.shape[1] // c):
        o_ref[0, :, (2 * j) * c:(2 * j + 1) * c] = re[:, j * c:(j + 1) * c]
        o_ref[0, :, (2 * j + 1) * c:(2 * j + 2) * c] = im[:, j * c:(j + 1) * c]


def _fnet2_kernel(a_ref, chan_ref, c2_ref, s2_ref, o_ref, g_ref, scr_ref, *, n1, scale):
    c = FNET_GROUP_DIM
    per = min(8, n1)
    for i in range(n1 // per):
        blk = a_ref[0, i * per:(i + 1) * per].reshape(per * LANES, 2 * c)
        g = jnp.dot(blk, chan_ref[...], preferred_element_type=F32).astype(BF16)
        g_ref[i * per:(i + 1) * per] = g.reshape(per, LANES, 2 * c)

    def body(k1, carry):
        g = g_ref[k1]
        y = (jnp.dot(c2_ref[...], g[:, :c], preferred_element_type=F32)
             + jnp.dot(s2_ref[...], g[:, c:], preferred_element_type=F32))
        scr_ref[pl.ds(k1, LANES, stride=n1), :] = y * scale
        return carry

    lax.fori_loop(0, n1, body, 0, unroll=4)
    o_ref[0] = scr_ref[...].astype(o_ref.dtype)


def _fourier(fn, bsz, seq):
    c = FNET_GROUP_DIM
    assert c == LANES and seq % LANES == 0
    n1 = seq // LANES
    stage1, twr, twi, chan, c2, s2 = _fnet_consts(seq)
    ncols = LANES * c
    nb = min(FNET_COLS, ncols)
    x2 = fn.reshape(bsz * FNET_GROUPS, n1, ncols)
    a = pl.pallas_call(
        functools.partial(_fnet1_kernel, n1=n1),
        grid=(ncols // nb, bsz * FNET_GROUPS),
        in_specs=[pl.BlockSpec((1, n1, nb), lambda j, i: (i, 0, j)),
                  _const_spec(stage1.shape),
                  pl.BlockSpec((n1, nb), lambda j, i: (0, j)),
                  pl.BlockSpec((n1, nb), lambda j, i: (0, j))],
        out_specs=pl.BlockSpec((1, n1, 2 * nb), lambda j, i: (i, 0, j)),
        out_shape=jax.ShapeDtypeStruct((bsz * FNET_GROUPS, n1, 2 * ncols), BF16),
        compiler_params=_cparams(("parallel", "parallel")),
        name="fnet1",
    )(x2, stage1, twr, twi)
    a4 = a.reshape(bsz * FNET_GROUPS, n1, LANES, 2 * c)
    out = pl.pallas_call(
        functools.partial(_fnet2_kernel, n1=n1, scale=1.0 / math.sqrt(seq * c)),
        grid=(bsz, FNET_GROUPS),
        in_specs=[pl.BlockSpec((1, n1, LANES, 2 * c), lambda b, g: (b * FNET_GROUPS + g, 0, 0, 0)),
                  _const_spec(chan.shape), _const_spec(c2.shape), _const_spec(s2.shape)],
        out_specs=pl.BlockSpec((1, seq, c), lambda b, g: (b, 0, g)),
        out_shape=jax.ShapeDtypeStruct((bsz, seq, FNET_WIDTH), BF16),
        scratch_shapes=[pltpu.VMEM((n1, LANES, 2 * c), BF16), pltpu.VMEM((seq, c), F32)],
        compiler_params=_cparams(("parallel", "parallel")),
        name="fnet2",
    )(a4, chan, c2, s2)
    return out.reshape(bsz * seq, FNET_WIDTH)


def kernel(x, p, rel_bias, norm_mix, w_in, conv_dw, conv_dw_b, conv_ln_g, conv_ln_b, conv_out,
           ssm_conv_w, ssm_conv_b, ssm_a_log, ssm_dt_bias, ssm_d, ssm_norm, ssm_out,
           attn_out, fnet_out, w_gate, b_gate, w_out, norm_ffn, ffn_w1, ffn_w3, ffn_w2,
           moe_router, moe_w1, moe_w3, moe_w2, ple_gate, ple_proj, final_norm):
    bsz, seq, d = x.shape
    depth = w_in.shape[0]
    t = bsz * seq
    h = x.reshape(t, d)
    bias_tables = [_att_bias_tables(rel_bias, g, dil) for g, (_, dil) in enumerate(ATT_PATTERNS)]
    for l in range(depth):
        *qkv, conv_u, z, xbc, fn, dt = _inproj(h, norm_mix[l][None], _reorder_w_in(w_in[l]), bsz, seq)
        att = [_attention_group(qkv[g], bias_tables[g], dil, bsz, seq, f"attn{g}")
               for g, (_, dil) in enumerate(ATT_PATTERNS)]
        cnf = _conformer(conv_u, conv_dw[l], conv_dw_b[l], conv_ln_g[l], conv_ln_b[l], bsz, seq)
        ssd = _ssd(z, xbc, dt, ssm_conv_w[l], ssm_conv_b[l], ssm_a_log[l], ssm_dt_bias[l],
                   ssm_d[l], ssm_norm[l], bsz, seq)
        fnt = _fourier(fn, bsz, seq)
        wbr = jnp.stack([attn_out[l], conv_out[l], ssm_out[l], fnet_out[l]]).astype(BF16)
        h = _mix(h, norm_mix[l][None], att, (cnf, ssd, fnt), wbr, w_gate[l].astype(BF16),
                 b_gate[l][:, None, :], w_out[l].astype(BF16))
        pl_in = p[l].reshape(t, -1)
        wpg, wpp = ple_gate[l].astype(BF16), ple_proj[l].astype(BF16)
        i = l // 2
        if l % 2 == 0:
            h = _ffn(h, norm_ffn[l][None], ffn_w1[i].astype(BF16), ffn_w3[i].astype(BF16),
                     ffn_w2[i].astype(BF16), pl_in, wpg, wpp)
        else:
            h = _moe(h, norm_ffn[l][None], moe_router[i], moe_w1[i].astype(BF16),
                     moe_w3[i].astype(BF16), moe_w2[i].astype(BF16), pl_in, wpg, wpp)
    return _final_norm(h, final_norm[None]).reshape(bsz, seq, d)
```

```python
import functools
import math

import numpy as np
import jax
import jax.numpy as jnp
from jax import lax
from jax.experimental import pallas as pl
from jax.experimental.pallas import tpu as pltpu

F32 = jnp.float32
BF16 = jnp.bfloat16

EPS = 1e-6
N_BRANCHES = 4
CONV_CH = 512
CONV_K = 31
SSM_HEADS = 8
SSM_HEAD_DIM = 64
SSM_INNER = SSM_HEADS * SSM_HEAD_DIM
SSM_GROUPS = 2
SSM_STATE = 64
SSM_CONV = 5
SSM_CONV_CH = SSM_INNER + 2 * SSM_GROUPS * SSM_STATE
SSM_CHUNK = 128
ATT_PATTERNS = ((128, 1), (512, 4), (2048, 16))
ATT_GROUPS = len(ATT_PATTERNS)
ATT_HEADS = 8
ATT_HEAD_DIM = 64
ATT_WIDTH = ATT_HEADS * ATT_HEAD_DIM
REL_BUCKETS = 32
REL_MAX_DIST = 1024
FNET_GROUPS = 4
FNET_GROUP_DIM = 128
FNET_WIDTH = FNET_GROUPS * FNET_GROUP_DIM
N_EXPERTS = 8
TOP_K = 2

ATT_IN_COLS = 3 * ATT_GROUPS * ATT_WIDTH
CONV_IN_COLS = 2 * CONV_CH
SSM_IN_COLS = SSM_INNER + SSM_CONV_CH + 2 * SSM_HEADS
OFF_CONV = ATT_IN_COLS
OFF_SSM = OFF_CONV + CONV_IN_COLS
OFF_FNET = OFF_SSM + SSM_IN_COLS

LANES = 128
DT_PAD = LANES
VMEM_LIMIT = 56 * 1024 * 1024


def _cparams(sem):
    return pltpu.CompilerParams(dimension_semantics=sem, vmem_limit_bytes=VMEM_LIMIT)


def _const_spec(shape):
    nd = len(shape)
    return pl.BlockSpec(shape, lambda *_: (0,) * nd, pipeline_mode=pl.Buffered(1))


def _rms(x, g):
    return x * lax.rsqrt(jnp.mean(x * x, axis=-1, keepdims=True) + EPS) * g


_SEC_QKV = (0, ATT_IN_COLS)
_SEC_CONV = (_SEC_QKV[0] + _SEC_QKV[1], CONV_IN_COLS)
_SEC_Z = (_SEC_CONV[0] + _SEC_CONV[1], SSM_INNER)
_SEC_XBC = (_SEC_Z[0] + _SEC_Z[1], SSM_CONV_CH)
_SEC_FNET = (_SEC_XBC[0] + _SEC_XBC[1], FNET_WIDTH)
_SEC_DT = (_SEC_FNET[0] + _SEC_FNET[1], DT_PAD)
_IN_COLS_PAD = _SEC_DT[0] + _SEC_DT[1]
_MM_CHUNK = 512


ATT_TILE = 256
QKV_COLS = 3 * ATT_WIDTH


def _reorder_w_in(w):
    d = w.shape[0]
    qkv = w[:, :OFF_CONV].reshape(d, 3, ATT_GROUPS, ATT_WIDTH).transpose(0, 2, 1, 3)
    ssm = w[:, OFF_SSM:OFF_FNET]
    dt = ssm[:, SSM_INNER + SSM_CONV_CH:]
    parts = [qkv.reshape(d, OFF_CONV), w[:, OFF_CONV:OFF_SSM], ssm[:, :SSM_INNER],
             ssm[:, SSM_INNER:SSM_INNER + SSM_CONV_CH], w[:, OFF_FNET:],
             dt, jnp.zeros((d, DT_PAD - dt.shape[1]), w.dtype)]
    return jnp.concatenate(parts, axis=1).astype(BF16)


def _deinterleave_matrix(dil):
    s = np.arange(ATT_TILE)
    m = np.zeros((ATT_TILE, ATT_TILE), np.float32)
    m[(s % dil) * (ATT_TILE // dil) + s // dil, s] = 1.0
    return m


def _inproj_kernel(h_ref, g_ref, w_ref, perm_ref, q0_ref, q1_ref, q2_ref, conv_ref, z_ref,
                   xbc_ref, fn_ref, dt_ref):
    xn = _rms(h_ref[...], g_ref[...]).astype(BF16)

    def section(x, sec, store):
        start, width = sec
        for c in range(0, width, _MM_CHUNK):
            cw = min(_MM_CHUNK, width - c)
            store(c, cw, jnp.dot(x, w_ref[:, start + c:start + c + cw],
                                 preferred_element_type=F32))

    def to(ref):
        def store(c, cw, val):
            ref[:, c:c + cw] = val.astype(ref.dtype)
        return store

    def to_fnet(c, cw, val):
        for g in range(cw // FNET_GROUP_DIM):
            fn_ref[0, c // FNET_GROUP_DIM + g] = val[:, g * FNET_GROUP_DIM:(g + 1) * FNET_GROUP_DIM]

    for g, q_ref in enumerate((q0_ref, q1_ref, q2_ref)):
        x = xn
        if ATT_PATTERNS[g][1] > 1:
            x = jnp.dot(perm_ref[g], xn, preferred_element_type=F32).astype(BF16)
        section(x, (g * QKV_COLS, QKV_COLS), to(q_ref))
    section(xn, _SEC_CONV, to(conv_ref))
    section(xn, _SEC_Z, to(z_ref))
    section(xn, _SEC_XBC, to(xbc_ref))
    section(xn, _SEC_FNET, to_fnet)
    section(xn, _SEC_DT, to(dt_ref))


def _inproj(h, g, w, bsz, seq):
    t, d = h.shape
    tm = ATT_TILE
    spt = seq // tm
    perm = jnp.asarray(np.stack([_deinterleave_matrix(dil) for _, dil in ATT_PATTERNS]), BF16)
    row = lambda n: pl.BlockSpec((tm, n), lambda i: (i, 0))
    qkv = jax.ShapeDtypeStruct((t, QKV_COLS), BF16)
    return pl.pallas_call(
        _inproj_kernel,
        grid=(t // tm,),
        in_specs=[row(d), _const_spec((1, d)), _const_spec(w.shape), _const_spec(perm.shape)],
        out_specs=[row(QKV_COLS), row(QKV_COLS), row(QKV_COLS), row(CONV_IN_COLS),
                   row(SSM_INNER), row(SSM_CONV_CH),
                   pl.BlockSpec((1, FNET_GROUPS, tm, FNET_GROUP_DIM),
                                lambda i: (i // spt, 0, i % spt, 0)),
                   row(DT_PAD)],
        out_shape=[qkv, qkv, qkv,
                   jax.ShapeDtypeStruct((t, CONV_IN_COLS), F32),
                   jax.ShapeDtypeStruct((t, SSM_INNER), F32),
                   jax.ShapeDtypeStruct((t, SSM_CONV_CH), F32),
                   jax.ShapeDtypeStruct((bsz, FNET_GROUPS, seq, FNET_GROUP_DIM), F32),
                   jax.ShapeDtypeStruct((t, DT_PAD), F32)],
        compiler_params=_cparams(("parallel",)),
        name="inproj",
    )(h, g, w, perm)


def _split_dot(v, m):
    hi = v.astype(BF16)
    lo = (v - hi.astype(F32)).astype(BF16)
    return (jnp.dot(hi, m, preferred_element_type=F32) + jnp.dot(lo, m, preferred_element_type=F32))


def _interleave(pt, v):
    if v.dtype == BF16:
        terms = [v]
    else:
        hi = v.astype(BF16)
        r1 = v - hi.astype(F32)
        mid = r1.astype(BF16)
        terms = [hi, mid, (r1 - mid.astype(F32)).astype(BF16)]
    tiles = []
    for i in range(v.shape[0] // ATT_TILE):
        rows = slice(i * ATT_TILE, (i + 1) * ATT_TILE)
        parts = [jnp.dot(pt, t[rows], preferred_element_type=F32) for t in terms]
        tiles.append(functools.reduce(lambda a, b: a + b, parts))
    return jnp.concatenate(tiles, axis=0)


def _mix_kernel(h_ref, g_ref, o0_ref, o1_ref, o2_ref, l0_ref, l1_ref, l2_ref, b1_ref, b2_ref,
                b3_ref, pt_ref, hx_ref, wbr_ref, wg_ref, cg_ref, wo_ref, o_ref):
    h = h_ref[...]
    xn = _rms(h, g_ref[...]).astype(BF16)
    outs, lses = [], []
    for g, (og_ref, lg_ref) in enumerate(zip((o0_ref, o1_ref, o2_ref), (l0_ref, l1_ref, l2_ref))):
        if ATT_PATTERNS[g][1] > 1:
            outs.append(_interleave(pt_ref[g], og_ref[...]))
            lses.append(_interleave(pt_ref[g], lg_ref[...]))
        else:
            outs.append(og_ref[...].astype(F32))
            lses.append(lg_ref[...])
    top = jnp.maximum(jnp.maximum(lses[0], lses[1]), lses[2])
    es = [jnp.exp(l - top) for l in lses]
    inv = 1.0 / (es[0] + es[1] + es[2])
    att = None
    for e, og in zip(es, outs):
        term = og * _split_dot(e * inv, hx_ref[...])
        att = term if att is None else att + term
    acc = None
    for b, hid in enumerate((att.astype(BF16), b1_ref[...], b2_ref[...], b3_ref[...])):
        gate = jax.nn.sigmoid(jnp.dot(xn, wg_ref[b], preferred_element_type=F32) + cg_ref[b])
        br = jnp.dot(hid, wbr_ref[b], preferred_element_type=F32)
        acc = gate * br if acc is None else acc + gate * br
    o_ref[...] = h + jnp.dot(acc.astype(BF16), wo_ref[...], preferred_element_type=F32)


def _mix(h, g, att, others, wbr, wg, cg, wo):
    t, d = h.shape
    tm = 512
    row = lambda n: pl.BlockSpec((tm, n), lambda i: (i, 0))
    head_expand = np.zeros((LANES, ATT_WIDTH), np.float32)
    for hd in range(ATT_HEADS):
        head_expand[hd, hd * ATT_HEAD_DIM:(hd + 1) * ATT_HEAD_DIM] = 1.0
    head_expand = jnp.asarray(head_expand, BF16)
    unperm = jnp.asarray(np.stack([_deinterleave_matrix(dil).T for _, dil in ATT_PATTERNS]), BF16)
    outs = [o for o, _ in att]
    lses = [l for _, l in att]
    return pl.pallas_call(
        _mix_kernel,
        grid=(t // tm,),
        in_specs=[row(d), _const_spec((1, d))] + [row(a.shape[1]) for a in outs + lses + list(others)]
                 + [_const_spec(unperm.shape), _const_spec(head_expand.shape), _const_spec(wbr.shape),
                    _const_spec(wg.shape), _const_spec(cg.shape), _const_spec(wo.shape)],
        out_specs=row(d),
        out_shape=jax.ShapeDtypeStruct((t, d), F32),
        compiler_params=_cparams(("parallel",)),
        name="mix",
    )(h, g, *outs, *lses, *others, unperm, head_expand, wbr, wg, cg, wo)


def _ple(h2, p_ref, wpg_ref, wpp_ref):
    gate = jax.nn.sigmoid(jnp.dot(h2.astype(BF16), wpg_ref[...], preferred_element_type=F32))
    pe = jnp.dot(p_ref[...].astype(BF16), wpp_ref[...], preferred_element_type=F32)
    return h2 + gate * pe


def _swiglu_partial(xn, w1, w3, w2, scale=None):
    a = jnp.dot(xn, w1, preferred_element_type=F32)
    b = jnp.dot(xn, w3, preferred_element_type=F32)
    hid = a * jax.nn.sigmoid(a) * b
    if scale is not None:
        hid = hid * scale
    return jnp.dot(hid.astype(BF16), w2, preferred_element_type=F32)


def _ffn_kernel(h_ref, g_ref, w1_ref, w3_ref, w2_ref, p_ref, wpg_ref, wpp_ref, o_ref, xn_ref):
    j = pl.program_id(1)

    @pl.when(j == 0)
    def _():
        h = h_ref[...]
        xn_ref[...] = _rms(h, g_ref[...]).astype(BF16)
        o_ref[...] = h

    o_ref[...] += _swiglu_partial(xn_ref[...], w1_ref[...], w3_ref[...], w2_ref[...])

    @pl.when(j == pl.num_programs(1) - 1)
    def _():
        o_ref[...] = _ple(o_ref[...], p_ref, wpg_ref, wpp_ref)


def _ffn(h, g, w1, w3, w2, p, wpg, wpp):
    t, d = h.shape
    f = w1.shape[1]
    tm, tf = 512, 1408
    row = lambda n: pl.BlockSpec((tm, n), lambda i, j: (i, 0))
    return pl.pallas_call(
        _ffn_kernel,
        grid=(t // tm, f // tf),
        in_specs=[row(d), _const_spec((1, d)),
                  pl.BlockSpec((d, tf), lambda i, j: (0, j)),
                  pl.BlockSpec((d, tf), lambda i, j: (0, j)),
                  pl.BlockSpec((tf, d), lambda i, j: (j, 0)),
                  row(p.shape[1]), _const_spec(wpg.shape), _const_spec(wpp.shape)],
        out_specs=row(d),
        out_shape=jax.ShapeDtypeStruct((t, d), F32),
        scratch_shapes=[pltpu.VMEM((tm, d), BF16)],
        compiler_params=_cparams(("parallel", "arbitrary")),
        name="ffn",
    )(h, g, w1, w3, w2, p, wpg, wpp)


MOE_ROWS = 128


def _moe_route(logits):
    ne, tm = logits.shape
    eidx = lax.broadcasted_iota(jnp.int32, logits.shape, 0)
    m1 = jnp.max(logits, axis=0, keepdims=True)
    i1 = jnp.min(jnp.where(logits == m1, eidx, ne), axis=0, keepdims=True)
    rest = jnp.where(eidx == i1, -jnp.inf, logits)
    m2 = jnp.max(rest, axis=0, keepdims=True)
    i2 = jnp.min(jnp.where(rest == m2, eidx, ne), axis=0, keepdims=True)
    e2 = jnp.exp(m2 - m1)
    den = 1.0 + e2
    combine = jnp.where(eidx == i1, 1.0 / den, 0.0) + jnp.where(eidx == i2, e2 / den, 0.0)
    routed = jnp.where((eidx == i1) | (eidx == i2), 1.0, 0.0)
    r = lax.broadcasted_iota(jnp.int32, (LANES, LANES), 0)
    c = lax.broadcasted_iota(jnp.int32, (LANES, LANES), 1)
    before = jnp.where(r < c, 1.0, 0.0).astype(BF16)
    counts = jnp.zeros((ne, 1), F32)
    slots = []
    for k in range(tm // LANES):
        blk = routed[:, k * LANES:(k + 1) * LANES]
        slots.append(jnp.dot(blk.astype(BF16), before, preferred_element_type=F32) + counts)
        counts = counts + jnp.sum(blk, axis=1, keepdims=True)
    slot = jnp.where(routed > 0.0, jnp.concatenate(slots, axis=1), -1.0).astype(jnp.int32)
    return combine, slot, counts


def _moe_kernel(h_ref, g_ref, rt_ref, w1_ref, w3_ref, w2_ref, p_ref, wpg_ref, wpp_ref, o_ref,
                xn_ref, comb_ref, slot_ref, cnt_ref, xe_ref, ye_ref):
    e = pl.program_id(1)
    j = pl.program_id(2)
    ne = pl.num_programs(1)
    tm = xn_ref.shape[0]

    @pl.when((e == 0) & (j == 0))
    def _():
        h = h_ref[...]
        xn = _rms(h, g_ref[...])
        xn_ref[...] = xn.astype(BF16)
        logits = lax.dot_general(rt_ref[...], xn, (((1,), (1,)), ((), ())),
                                 preferred_element_type=F32, precision=HI)
        combine, slot, counts = _moe_route(logits)
        comb_ref[...] = combine
        slot_ref[...] = slot
        for k in range(comb_ref.shape[0]):
            cnt_ref[k] = jnp.sum(counts[k:k + 1, :]).astype(jnp.int32)
        o_ref[...] = h

    n_blocks = (cnt_ref[e] + MOE_ROWS - 1) // MOE_ROWS

    def for_row_blocks(body):
        def pair(i, carry):
            body(pl.multiple_of(i * 2 * MOE_ROWS, 2 * MOE_ROWS), 2 * MOE_ROWS)
            return carry
        lax.fori_loop(0, n_blocks // 2, pair, 0)

        @pl.when(n_blocks % 2 == 1)
        def _():
            body(pl.multiple_of((n_blocks - 1) * MOE_ROWS, MOE_ROWS), MOE_ROWS)

    def one_hot(r0, rows):
        return slot_ref[pl.ds(e, 1), :] == lax.broadcasted_iota(jnp.int32, (rows, tm), 0) + r0

    @pl.when(j == 0)
    def _():
        def gather(r0, rows):
            sel = jnp.where(one_hot(r0, rows), 1.0, 0.0).astype(BF16)
            xe_ref[pl.ds(r0, rows), :] = jnp.dot(
                sel, xn_ref[...], preferred_element_type=F32).astype(BF16)
            ye_ref[pl.ds(r0, rows), :] = jnp.zeros((rows, ye_ref.shape[1]), F32)
        for_row_blocks(gather)

    def expert(r0, rows):
        ye_ref[pl.ds(r0, rows), :] += _swiglu_partial(
            xe_ref[pl.ds(r0, rows), :], w1_ref[0], w3_ref[0], w2_ref[0])
    for_row_blocks(expert)

    @pl.when(j == pl.num_programs(2) - 1)
    def _():
        def scatter(r0, rows):
            hot = one_hot(r0, rows)
            weight = jnp.sum(jnp.where(hot, comb_ref[pl.ds(e, 1), :], 0.0), axis=1, keepdims=True)
            yw = (ye_ref[pl.ds(r0, rows), :] * weight).astype(BF16)
            o_ref[...] += lax.dot_general(jnp.where(hot, 1.0, 0.0).astype(BF16), yw,
                                          (((0,), (0,)), ((), ())), preferred_element_type=F32)
        for_row_blocks(scatter)

    @pl.when((e == ne - 1) & (j == pl.num_programs(2) - 1))
    def _():
        o_ref[...] = _ple(o_ref[...], p_ref, wpg_ref, wpp_ref)


def _moe(h, g, router, w1, w3, w2, p, wpg, wpp):
    t, d = h.shape
    ne, _, f = w1.shape
    tm, tf = 1024, 1792
    row = lambda n: pl.BlockSpec((tm, n), lambda i, e, j: (i, 0))
    return pl.pallas_call(
        _moe_kernel,
        grid=(t // tm, ne, f // tf),
        in_specs=[row(d), _const_spec((1, d)), _const_spec((ne, d)),
                  pl.BlockSpec((1, d, tf), lambda i, e, j: (e, 0, j)),
                  pl.BlockSpec((1, d, tf), lambda i, e, j: (e, 0, j)),
                  pl.BlockSpec((1, tf, d), lambda i, e, j: (e, j, 0)),
                  row(p.shape[1]), _const_spec(wpg.shape), _const_spec(wpp.shape)],
        out_specs=row(d),
        out_shape=jax.ShapeDtypeStruct((t, d), F32),
        scratch_shapes=[pltpu.VMEM((tm, d), BF16), pltpu.VMEM((ne, tm), F32),
                        pltpu.VMEM((ne, tm), jnp.int32), pltpu.SMEM((ne,), jnp.int32),
                        pltpu.VMEM((tm, d), BF16), pltpu.VMEM((tm, d), F32)],
        compiler_params=_cparams(("parallel", "arbitrary", "arbitrary")),
        name="moe",
    )(h, g, router.T, w1, w3, w2, p, wpg, wpp)


def _final_norm_kernel(h_ref, g_ref, o_ref):
    o_ref[...] = _rms(h_ref[...], g_ref[...])


def _final_norm(h, g):
    t, d = h.shape
    tm = 1024
    row = pl.BlockSpec((tm, d), lambda i: (i, 0))
    return pl.pallas_call(
        _final_norm_kernel, grid=(t // tm,),
        in_specs=[row, _const_spec((1, d))], out_specs=row,
        out_shape=jax.ShapeDtypeStruct((t, d), F32),
        compiler_params=_cparams(("parallel",)), name="final_norm",
    )(h, g)


ATT_HALF = 64
ATT_QB = 128
ATT_KB = ATT_QB + 2 * ATT_HALF
NEG = -1e30
assert all(w // (2 * d) == ATT_HALF for w, d in ATT_PATTERNS)


def _t5_bucket(rel):
    half = REL_BUCKETS // 2
    max_exact = half // 2
    n = np.abs(rel)
    large = max_exact + (np.log(np.maximum(n, 1) / max_exact) / math.log(REL_MAX_DIST / max_exact)
                         * (half - max_exact)).astype(np.int32)
    large = np.minimum(large, half - 1)
    return np.where(rel > 0, half, 0) + np.where(n < max_exact, n, large)


def _att_bias_tables(rel_bias, g, dil):
    i = np.arange(ATT_QB)[:, None]
    j = np.arange(ATT_KB)[None, :]
    rel = j - ATT_HALF - i
    band = np.abs(rel) <= ATT_HALF
    pick = np.eye(REL_BUCKETS, dtype=np.float32)[_t5_bucket(dil * rel)]
    heads = rel_bias[:, g * ATT_HEADS:(g + 1) * ATT_HEADS].astype(F32)
    bias = jnp.einsum('qkb,bh->hqk', pick, heads, precision=lax.Precision.HIGHEST)
    tables = []
    for v in range(4):
        ok = band
        if v & 1:
            ok = ok & (j >= ATT_HALF)
        if v & 2:
            ok = ok & (j < ATT_QB + ATT_HALF)
        tables.append(jnp.where(ok[None], bias, NEG))
    return jnp.stack(tables)


def _attn_kernel(q_ref, k_ref, v_ref, kp_ref, kn_ref, vp_ref, vn_ref, bias_ref, o_ref, lse_ref,
                 qbuf, kbuf, vbuf, obuf, lbuf, *, tq, n_blocks):
    flat = lambda ref: ref[0].reshape(-1, ref.shape[-1])
    qbuf[...] = flat(q_ref)
    kbuf[0:ATT_HALF] = flat(kp_ref)
    kbuf[ATT_HALF:ATT_HALF + tq] = flat(k_ref)
    kbuf[ATT_HALF + tq:] = flat(kn_ref)
    vbuf[0:ATT_HALF] = flat(vp_ref)
    vbuf[ATT_HALF:ATT_HALF + tq] = flat(v_ref)
    vbuf[ATT_HALF + tq:] = flat(vn_ref)
    nsb = tq // ATT_QB
    first = pl.program_id(2) * nsb
    lane = lax.broadcasted_iota(jnp.int32, (ATT_QB, LANES), 1)
    low = lane < ATT_HEAD_DIM
    lane_row = lax.broadcasted_iota(jnp.int32, (1, LANES), 1)
    keep = [(lane_row < ATT_HEAD_DIM).astype(BF16), (lane_row >= ATT_HEAD_DIM).astype(BF16)]
    ones = jnp.ones((ATT_KB, LANES), BF16)

    def block(sb, carry):
        r0 = pl.multiple_of(sb * ATT_QB, ATT_QB)
        gsb = first + sb
        variant = (gsb == 0).astype(jnp.int32) + 2 * (gsb == n_blocks - 1).astype(jnp.int32)
        q = qbuf[pl.ds(r0, ATT_QB), :] * (ATT_HEAD_DIM ** -0.5)
        lse_all = jnp.zeros((ATT_QB, LANES), F32)
        outs = []
        for pair in range(ATT_HEADS // 2):
            cols = slice(pair * LANES, (pair + 1) * LANES)
            qp = q[:, cols]
            kp = kbuf[pl.ds(r0, ATT_KB), cols]
            vp = jnp.concatenate([vbuf[pl.ds(r0, ATT_KB), cols], ones], axis=1)
            res = []
            for half in range(2):
                h = 2 * pair + half
                s = lax.dot_general(qp * keep[half], kp, (((1,), (1,)), ((), ())),
                                    preferred_element_type=F32)
                s = s + bias_ref[variant, h]
                m = jnp.max(s, axis=-1, keepdims=True)
                e = jnp.exp(s - m)
                pv = jnp.dot(e.astype(BF16), vp, preferred_element_type=F32)
                den = pv[:, LANES:]
                res.append(pv[:, :LANES] / den)
                lse_all = jnp.where(lane == h, m + jnp.log(den), lse_all)
            outs.append(jnp.where(low, res[0], res[1]))
        obuf[pl.ds(r0, ATT_QB), :] = jnp.concatenate(outs, axis=1).astype(obuf.dtype)
        lbuf[pl.ds(r0, ATT_QB), :] = lse_all
        return carry

    lax.fori_loop(0, nsb, block, 0)
    o_ref[0] = obuf[...].reshape(o_ref.shape[1:])
    lse_ref[0] = lbuf[...].reshape(lse_ref.shape[1:])


def _attention_group(qkv, bias_tables, dil, bsz, seq, name):
    sub_len = seq // dil
    assert sub_len % ATT_QB == 0 and seq % ATT_TILE == 0 and ATT_TILE % dil == 0
    rows = ATT_TILE // dil
    tq = min(512, sub_len)
    w = ATT_WIDTH
    hrows = min(rows, ATT_HALF)
    n_hb = sub_len // ATT_HALF

    def view(a, chunk):
        return a.reshape(bsz, (seq // ATT_TILE) * (rows // chunk), dil, chunk, a.shape[-1])

    def main(part):
        return pl.BlockSpec((1, tq // rows, None, rows, w), lambda b, r, n: (b, n, r, 0, part))

    def halo(part, nxt):
        if nxt:
            blk = lambda n: jnp.minimum((n + 1) * (tq // ATT_HALF), n_hb - 1)
        else:
            blk = lambda n: jnp.maximum(n * (tq // ATT_HALF) - 1, 0)
        return pl.BlockSpec((1, ATT_HALF // hrows, None, hrows, w),
                            lambda b, r, n: (b, blk(n), r, 0, part))

    mv, hv = view(qkv, rows), view(qkv, hrows)
    o, lse = pl.pallas_call(
        functools.partial(_attn_kernel, tq=tq, n_blocks=sub_len // ATT_QB),
        grid=(bsz, dil, sub_len // tq),
        in_specs=[main(0), main(1), main(2), halo(1, False), halo(1, True), halo(2, False),
                  halo(2, True), _const_spec(bias_tables.shape)],
        out_specs=[pl.BlockSpec((1, tq // rows, None, rows, w), lambda b, r, n: (b, n, r, 0, 0)),
                   pl.BlockSpec((1, tq // rows, None, rows, LANES), lambda b, r, n: (b, n, r, 0, 0))],
        out_shape=[jax.ShapeDtypeStruct((bsz, seq // ATT_TILE, dil, rows, w), BF16),
                   jax.ShapeDtypeStruct((bsz, seq // ATT_TILE, dil, rows, LANES), F32)],
        scratch_shapes=[pltpu.VMEM((tq, w), BF16),
                        pltpu.VMEM((tq + 2 * ATT_HALF, w), BF16),
                        pltpu.VMEM((tq + 2 * ATT_HALF, w), BF16),
                        pltpu.VMEM((tq, w), BF16), pltpu.VMEM((tq, LANES), F32)],
        compiler_params=_cparams(("parallel", "parallel", "parallel")),
        name=name,
    )(mv, mv, mv, hv, hv, hv, hv, bias_tables)
    return o.reshape(bsz * seq, w), lse.reshape(bsz * seq, LANES)


SUBLANES = 8
CONV_HALO = 16
CONV_ROWS = 64


def _conformer_kernel(u_ref, up_ref, un_ref, dw_ref, dwb_ref, lng_ref, lnb_ref, o_ref,
                      hp_ref, sh_ref, *, ts):
    n = pl.program_id(1)

    def glu(u):
        return u[:, :CONV_CH] * jax.nn.sigmoid(u[:, CONV_CH:])

    hp_ref[0:CONV_HALO] = jnp.where(n > 0, glu(up_ref[0]), 0.0)
    hp_ref[CONV_HALO:CONV_HALO + ts] = glu(u_ref[0])
    hp_ref[CONV_HALO + ts:] = jnp.where(n < pl.num_programs(1) - 1, glu(un_ref[0]), 0.0)
    span = ts + 2 * CONV_HALO - SUBLANES
    for b in range(SUBLANES):
        sh_ref[b] = hp_ref[pl.ds(b, span), :]
    first = CONV_HALO - CONV_K // 2

    def rows(c, carry):
        r0 = pl.multiple_of(c * CONV_ROWS, CONV_ROWS)
        acc = jnp.broadcast_to(dwb_ref[...], (CONV_ROWS, CONV_CH))
        for k in range(CONV_K):
            a, b = divmod(first + k, SUBLANES)
            acc = acc + dw_ref[pl.ds(k, 1), :] * sh_ref[b, pl.ds(r0 + a * SUBLANES, CONV_ROWS), :]
        mu = jnp.mean(acc, axis=-1, keepdims=True)
        cen = acc - mu
        var = jnp.mean(cen * cen, axis=-1, keepdims=True)
        y = cen * lax.rsqrt(var + EPS) * lng_ref[...] + lnb_ref[...]
        o_ref[0, pl.ds(r0, CONV_ROWS), :] = (y * jax.nn.sigmoid(y)).astype(o_ref.dtype)
        return carry

    lax.fori_loop(0, ts // CONV_ROWS, rows, 0)


def _conformer(u, dw, dw_b, ln_g, ln_b, bsz, seq):
    ts = min(512, seq)
    view = u.reshape(bsz, seq, 2 * CONV_CH)
    hb = ts // CONV_HALO
    n_hb = seq // CONV_HALO
    out = pl.pallas_call(
        functools.partial(_conformer_kernel, ts=ts),
        grid=(bsz, seq // ts),
        in_specs=[pl.BlockSpec((1, ts, 2 * CONV_CH), lambda b, n: (b, n, 0)),
                  pl.BlockSpec((1, CONV_HALO, 2 * CONV_CH),
                               lambda b, n: (b, jnp.maximum(n * hb - 1, 0), 0)),
                  pl.BlockSpec((1, CONV_HALO, 2 * CONV_CH),
                               lambda b, n: (b, jnp.minimum((n + 1) * hb, n_hb - 1), 0)),
                  _const_spec(dw.shape), _const_spec((1, CONV_CH)), _const_spec((1, CONV_CH)),
                  _const_spec((1, CONV_CH))],
        out_specs=pl.BlockSpec((1, ts, CONV_CH), lambda b, n: (b, n, 0)),
        out_shape=jax.ShapeDtypeStruct((bsz, seq, CONV_CH), BF16),
        scratch_shapes=[pltpu.VMEM((ts + 2 * CONV_HALO, CONV_CH), F32),
                        pltpu.VMEM((SUBLANES, ts + 2 * CONV_HALO - SUBLANES, CONV_CH), F32)],
        compiler_params=_cparams(("parallel", "parallel")),
        name="conformer",
    )(view, view, view, dw, dw_b[None], ln_g[None], ln_b[None])
    return out.reshape(bsz * seq, CONV_CH)


SSD_HALO = SUBLANES
SSD_ROWS = 64
SSD_BC = SSM_GROUPS * SSM_STATE
HEADS_PER_GROUP = SSM_HEADS // SSM_GROUPS
GROUP_LANES = HEADS_PER_GROUP * SSM_HEAD_DIM
HI = lax.Precision.HIGHEST
assert SSD_BC == LANES and SSM_CHUNK == LANES and 2 * SSM_HEADS <= LANES


def _ssd_pre_kernel(x_ref, xp_ref, xn_ref, dt_ref, cw_ref, cb_ref, dtb_ref, arow_ref, tri_ref,
                    ex_ref, xo_ref, dto_ref, nf_ref, nb_ref, cy_ref, hp_ref, sh_ref, *, ts, phases):
    n = pl.program_id(1)
    hp_ref[0:SSD_HALO] = jnp.where(n > 0, xp_ref[0], 0.0)
    hp_ref[SSD_HALO:SSD_HALO + ts] = x_ref[0]
    hp_ref[SSD_HALO + ts:] = jnp.where(n < pl.num_programs(1) - 1, xn_ref[0], 0.0)
    for i, b in enumerate(phases):
        sh_ref[i] = hp_ref[pl.ds(b, ts + SSD_HALO), :]
    first = SSD_HALO - SSM_CONV // 2

    def rows(c, carry):
        r0 = pl.multiple_of(c * SSD_ROWS, SSD_ROWS)
        acc = jnp.broadcast_to(cb_ref[...], (SSD_ROWS, SSM_CONV_CH))
        for k in range(SSM_CONV):
            a, b = divmod(first + k, SUBLANES)
            acc = acc + cw_ref[pl.ds(k, 1), :] * sh_ref[phases.index(b),
                                                        pl.ds(r0 + a * SUBLANES, SSD_ROWS), :]
        xo_ref[0, pl.ds(r0, SSD_ROWS), :] = acc * jax.nn.sigmoid(acc)
        return carry

    lax.fori_loop(0, ts // SSD_ROWS, rows, 0)
    x = dt_ref[0] + dtb_ref[...]
    softplus = jnp.maximum(x, 0.0) + jnp.log1p(jnp.exp(-jnp.abs(x)))
    lane = lax.broadcasted_iota(jnp.int32, x.shape, 1)
    dtv = jnp.where(lane < 2 * SSM_HEADS, softplus, 0.0)
    dto_ref[0] = dtv

    nch = ts // SSM_CHUNK
    chunk = lambda k: slice(k * SSM_CHUNK, (k + 1) * SSM_CHUNK)
    dta = dtv * arow_ref[...]
    acs_all = _cumsum_both(jnp.concatenate([dta[chunk(k)] for k in range(nch)], axis=1), tri_ref)
    clane = lax.broadcasted_iota(jnp.int32, (SSM_CHUNK, DT_PAD), 1)
    ws, totals = [], [[], []]
    for k in range(nch):
        acs = acs_all[:, chunk(k)]
        tot_f, tot_b = acs[SSM_CHUNK - 1:SSM_CHUNK, :], acs[0:1, :]
        to_end = jnp.where(clane < SSM_HEADS, tot_f - acs,
                           jnp.where(clane < 2 * SSM_HEADS, tot_b - acs, 0.0))
        ws.append(dtv[chunk(k)] * jnp.exp(to_end))
        totals[0].append(jnp.exp(tot_f))
        totals[1].append(jnp.exp(tot_b))
    w_all = jnp.concatenate(ws, axis=0)
    xs = xo_ref[0, :, :SSM_INNER]
    hl = lax.broadcasted_iota(jnp.int32, (SSM_STATE, SSM_INNER), 1)
    pad_rows = jnp.zeros((SUBLANES - nch % SUBLANES, DT_PAD), F32)
    for d, new_ref in enumerate((nf_ref, nb_ref)):
        xw = (xs * _expand(w_all, ex_ref, d)).astype(BF16)
        carry = _expand(jnp.concatenate(totals[d] + [pad_rows], axis=0), ex_ref, d)
        for k in range(nch):
            bmat = xo_ref[0, chunk(k), SSM_INNER:SSM_INNER + SSD_BC].astype(BF16)
            full = lax.dot_general(bmat, xw[chunk(k)], (((0,), (0,)), ((), ())),
                                   preferred_element_type=F32)
            new = full[:SSM_STATE]
            for g in range(1, SSM_GROUPS):
                new = jnp.where(hl >= g * GROUP_LANES, full[g * SSM_STATE:(g + 1) * SSM_STATE], new)
            new_ref[0, k] = new
            cy_ref[0, k, d:d + 1, :] = carry[k:k + 1]


def _ssd_pre(xbc, dt_raw, conv_w, conv_b, dt_bias, consts, bsz, seq):
    ts = min(512, seq)
    nc = seq // SSM_CHUNK
    nch = ts // SSM_CHUNK
    a_row, tri, expand = consts
    new = jax.ShapeDtypeStruct((bsz, nc, SSM_STATE, SSM_INNER), F32)
    new_spec = pl.BlockSpec((1, nch, SSM_STATE, SSM_INNER), lambda b, n: (b, n, 0, 0))
    xv = xbc.reshape(bsz, seq, SSM_CONV_CH)
    dv = dt_raw.reshape(bsz, seq, DT_PAD)
    hb = ts // SSD_HALO
    n_hb = seq // SSD_HALO
    first = SSD_HALO - SSM_CONV // 2
    phases = tuple(sorted({(first + k) % SUBLANES for k in range(SSM_CONV)}))
    dtb = jnp.zeros((1, DT_PAD), F32).at[0, :2 * SSM_HEADS].set(dt_bias.reshape(-1))
    return pl.pallas_call(
        functools.partial(_ssd_pre_kernel, ts=ts, phases=phases),
        grid=(bsz, seq // ts),
        in_specs=[pl.BlockSpec((1, ts, SSM_CONV_CH), lambda b, n: (b, n, 0)),
                  pl.BlockSpec((1, SSD_HALO, SSM_CONV_CH),
                               lambda b, n: (b, jnp.maximum(n * hb - 1, 0), 0)),
                  pl.BlockSpec((1, SSD_HALO, SSM_CONV_CH),
                               lambda b, n: (b, jnp.minimum((n + 1) * hb, n_hb - 1), 0)),
                  pl.BlockSpec((1, ts, DT_PAD), lambda b, n: (b, n, 0)),
                  _const_spec(conv_w.shape), _const_spec((1, SSM_CONV_CH)),
                  _const_spec((1, DT_PAD)), _const_spec(a_row.shape), _const_spec(tri.shape),
                  _const_spec(expand.shape)],
        out_specs=[pl.BlockSpec((1, ts, SSM_CONV_CH), lambda b, n: (b, n, 0)),
                   pl.BlockSpec((1, ts, DT_PAD), lambda b, n: (b, n, 0)),
                   new_spec, new_spec,
                   pl.BlockSpec((1, nch, 2, SSM_INNER), lambda b, n: (b, n, 0, 0))],
        out_shape=[jax.ShapeDtypeStruct((bsz, seq, SSM_CONV_CH), F32),
                   jax.ShapeDtypeStruct((bsz, seq, DT_PAD), F32),
                   new, new, jax.ShapeDtypeStruct((bsz, nc, 2, SSM_INNER), F32)],
        scratch_shapes=[pltpu.VMEM((ts + 2 * SSD_HALO, SSM_CONV_CH), F32),
                        pltpu.VMEM((len(phases), ts + SSD_HALO, SSM_CONV_CH), F32)],
        compiler_params=_cparams(("parallel", "parallel")),
        name="ssd_pre",
    )(xv, xv, xv, dv, conv_w, conv_b[None], dtb, a_row, tri, expand)


def _ssd_consts(a_log):
    a_row = jnp.zeros((1, DT_PAD), F32).at[0, :2 * SSM_HEADS].set(-jnp.exp(a_log.reshape(-1)))
    lower = np.tril(np.ones((SSM_CHUNK, SSM_CHUNK), np.float32))
    tri = jnp.asarray(np.concatenate([lower, lower.T], axis=0), BF16)
    expand = np.zeros((2, DT_PAD, SSM_INNER), np.float32)
    for d in range(2):
        for h in range(SSM_HEADS):
            expand[d, d * SSM_HEADS + h, h * SSM_HEAD_DIM:(h + 1) * SSM_HEAD_DIM] = 1.0
    return a_row, tri, jnp.asarray(expand, BF16)


def _cumsum_both(dta, tri_ref):
    hi = dta.astype(BF16)
    r1 = dta - hi.astype(F32)
    mid = r1.astype(BF16)
    lo = (r1 - mid.astype(F32)).astype(BF16)
    both = (jnp.dot(tri_ref[...], hi, preferred_element_type=F32)
            + jnp.dot(tri_ref[...], mid, preferred_element_type=F32)
            + jnp.dot(tri_ref[...], lo, preferred_element_type=F32))
    lane = lax.broadcasted_iota(jnp.int32, dta.shape, 1) % DT_PAD
    return jnp.where(lane < SSM_HEADS, both[:SSM_CHUNK], both[SSM_CHUNK:])


def _expand(v, ex_ref, d):
    return _split_dot(v, ex_ref[d])


def _ssd_scan_kernel(nf_ref, nb_ref, cf_ref, cb_ref, pf_ref, pb_ref, sf_ref, sb_ref, *, nch):
    @pl.when(pl.program_id(1) == 0)
    def _():
        sf_ref[...] = jnp.zeros_like(sf_ref)
        sb_ref[...] = jnp.zeros_like(sb_ref)

    def scan(d, order, new_ref, carry_ref, st_ref, out_ref):
        st = st_ref[...]
        for k in order:
            out_ref[0, k] = st.astype(out_ref.dtype)
            st = st * carry_ref[0, k, d:d + 1, :] + new_ref[0, k]
        st_ref[...] = st

    scan(0, range(nch), nf_ref, cf_ref, sf_ref, pf_ref)
    scan(1, range(nch - 1, -1, -1), nb_ref, cb_ref, sb_ref, pb_ref)


SSD_SCAN_CHUNKS = 16


def _ssd_states(new_f, new_b, carry, bsz, seq):
    nc = seq // SSM_CHUNK
    nch = min(SSD_SCAN_CHUNKS, nc)
    steps = nc // nch
    fwd = lambda b, c: (b, c, 0, 0)
    bwd = lambda b, c: (b, steps - 1 - c, 0, 0)
    st = jax.ShapeDtypeStruct((bsz, nc, SSM_STATE, SSM_INNER), BF16)
    blk = (1, nch, SSM_STATE, SSM_INNER)
    return pl.pallas_call(
        functools.partial(_ssd_scan_kernel, nch=nch),
        grid=(bsz, steps),
        in_specs=[pl.BlockSpec(blk, fwd), pl.BlockSpec(blk, bwd),
                  pl.BlockSpec((1, nch, 2, SSM_INNER), fwd),
                  pl.BlockSpec((1, nch, 2, SSM_INNER), bwd)],
        out_specs=[pl.BlockSpec(blk, fwd), pl.BlockSpec(blk, bwd)],
        out_shape=[st, st],
        scratch_shapes=[pltpu.VMEM((SSM_STATE, SSM_INNER), F32),
                        pltpu.VMEM((SSM_STATE, SSM_INNER), F32)],
        compiler_params=_cparams(("parallel", "arbitrary")),
        name="ssd_scan",
    )(new_f, new_b, carry, carry)


def _ssd_out_kernel(x_ref, dt_ref, z_ref, pf_ref, pb_ref, arow_ref, tri_ref, ex_ref, dskip_ref,
                    ng_ref, o_ref):
    xs = x_ref[0, :, :SSM_INNER]
    bmat = x_ref[0, :, SSM_INNER:SSM_INNER + SSD_BC].astype(BF16)
    cmat = x_ref[0, :, SSM_INNER + SSD_BC:].astype(BF16)
    dtv = dt_ref[0]
    acs = _cumsum_both(dtv * arow_ref[...], tri_ref)
    acs_t = acs.T
    dt_t = dtv.T
    eacs = jnp.exp(acs)
    row = lax.broadcasted_iota(jnp.int32, (SSM_CHUNK, SSM_CHUNK), 0)
    col = lax.broadcasted_iota(jnp.int32, (SSM_CHUNK, SSM_CHUNK), 1)
    past, now = col < row, col == row
    low = lax.broadcasted_iota(jnp.int32, (SSM_CHUNK, LANES), 1) < SSM_HEAD_DIM
    glane = lax.broadcasted_iota(jnp.int32, (1, SSD_BC), 1) // SSM_STATE
    scores = [lax.dot_general(cmat * (glane == g).astype(BF16), bmat,
                              (((1,), (1,)), ((), ())), preferred_element_type=F32)
              for g in range(SSM_GROUPS)]
    xs_b = xs.astype(BF16)
    keep = [low.astype(BF16), (~low).astype(BF16)]
    diag = []
    for pair in range(SSM_HEADS // 2):
        xp = xs_b[:, pair * LANES:(pair + 1) * LANES]
        mats = []
        for half in range(2):
            f = 2 * pair + half
            b = SSM_HEADS + f
            seg = jnp.where(past | now, acs[:, f:f + 1] - acs_t[f:f + 1, :],
                            acs[:, b:b + 1] - acs_t[b:b + 1, :])
            dts = (jnp.where(past, dt_t[f:f + 1, :], dt_t[b:b + 1, :])
                   + jnp.where(now, dt_t[f:f + 1, :], 0.0))
            mats.append((scores[f // HEADS_PER_GROUP] * jnp.exp(seg) * dts).astype(BF16))
        diag.append(jnp.dot(jnp.concatenate(mats, axis=1),
                            jnp.concatenate([xp * keep[0], xp * keep[1]], axis=0),
                            preferred_element_type=F32))
    y = dskip_ref[...] * xs + jnp.concatenate(diag, axis=1)
    hgroup = lax.broadcasted_iota(jnp.int32, (1, SSM_INNER), 1) // GROUP_LANES
    for d, p_ref in enumerate((pf_ref, pb_ref)):
        prev = p_ref[0, 0]
        stacked = jnp.concatenate([prev * (hgroup == g).astype(BF16) for g in range(SSM_GROUPS)],
                                  axis=0)
        off = jnp.dot(cmat, stacked, preferred_element_type=F32)
        y = y + off * _expand(eacs, ex_ref, d)
    z = z_ref[0]
    y = y * (z * jax.nn.sigmoid(z))
    o_ref[0] = _rms(y, ng_ref[...]).astype(o_ref.dtype)


def _ssd_out(xact, dtv, z, prev_f, prev_b, consts, d_skip, norm_g, bsz, seq):
    nc = seq // SSM_CHUNK
    a_row, tri, expand = consts
    dsk = jnp.repeat(d_skip, SSM_HEAD_DIM)[None]
    chunk = lambda n: pl.BlockSpec((1, SSM_CHUNK, n), lambda b, c: (b, c, 0))
    state = pl.BlockSpec((1, 1, SSM_STATE, SSM_INNER), lambda b, c: (b, c, 0, 0))
    out = pl.pallas_call(
        _ssd_out_kernel,
        grid=(bsz, nc),
        in_specs=[chunk(SSM_CONV_CH), chunk(DT_PAD), chunk(SSM_INNER), state, state,
                  _const_spec(a_row.shape), _const_spec(tri.shape), _const_spec(expand.shape),
                  _const_spec(dsk.shape), _const_spec((1, SSM_INNER))],
        out_specs=chunk(SSM_INNER),
        out_shape=jax.ShapeDtypeStruct((bsz, seq, SSM_INNER), BF16),
        compiler_params=_cparams(("parallel", "parallel")),
        name="ssd_out",
    )(xact, dtv, z.reshape(bsz, seq, SSM_INNER), prev_f, prev_b, a_row, tri, expand, dsk,
      norm_g[None])
    return out.reshape(bsz * seq, SSM_INNER)


def _ssd(z, xbc, dt_raw, conv_w, conv_b, a_log, dt_bias, d_skip, norm_g, bsz, seq):
    consts = _ssd_consts(a_log)
    xact, dtv, new_f, new_b, carry = _ssd_pre(xbc, dt_raw, conv_w, conv_b, dt_bias, consts, bsz, seq)
    prev_f, prev_b = _ssd_states(new_f, new_b, carry, bsz, seq)
    return _ssd_out(xact, dtv, z, prev_f, prev_b, consts, d_skip, norm_g, bsz, seq)


FNET_COLS = 4096


def _dft_cos_sin(n):
    ang = 2.0 * np.pi * np.outer(np.arange(n), np.arange(n)) / n
    return np.cos(ang), np.sin(ang)


def _fnet_consts(seq):
    c = FNET_GROUP_DIM
    n1 = seq // LANES
    c1, s1 = _dft_cos_sin(n1)
    stage1 = np.concatenate([c1, -s1], axis=0)
    ang = 2.0 * np.pi * np.outer(np.arange(n1), np.arange(LANES)) / seq
    twr = np.repeat(np.cos(ang), c, axis=1)
    twi = np.repeat(-np.sin(ang), c, axis=1)
    cc, sc = _dft_cos_sin(c)
    chan = np.block([[cc, -sc], [sc, cc]])
    c2, s2 = _dft_cos_sin(LANES)
    return (jnp.asarray(stage1, BF16), jnp.asarray(twr, F32), jnp.asarray(twi, F32),
            jnp.asarray(chan, BF16), jnp.asarray(c2, BF16), jnp.asarray(s2, BF16))


def _fnet1_kernel(x_ref, f_ref, twr_ref, twi_ref, o_ref, *, n1):
    c = FNET_GROUP_DIM
    a = jnp.dot(f_ref[...], x_ref[0].astype(BF16), preferred_element_type=F32)
    ar, ai = a[:n1], a[n1:]
    twr, twi = twr_ref[...], twi_ref[...]
    re = (ar * twr - ai * twi).astype(o_ref.dtype)
    im = (ar * twi + ai * twr).astype(o_ref.dtype)
    for j in range(re.shape[1] // c):
        o_ref[0, :, (2 * j) * c:(2 * j + 1) * c] = re[:, j * c:(j + 1) * c]
        o_ref[0, :, (2 * j + 1) * c:(2 * j + 2) * c] = im[:, j * c:(j + 1) * c]


def _fnet2_kernel(a_ref, chan_ref, c2_ref, s2_ref, o_ref, g_ref, scr_ref, *, n1, scale):
    c = FNET_GROUP_DIM
    per = min(8, n1)
    for i in range(n1 // per):
        blk = a_ref[0, i * per:(i + 1) * per].reshape(per * LANES, 2 * c)
        g = jnp.dot(blk, chan_ref[...], preferred_element_type=F32).astype(BF16)
        g_ref[i * per:(i + 1) * per] = g.reshape(per, LANES, 2 * c)

    def body(k1, carry):
        g = g_ref[k1]
        y = (jnp.dot(c2_ref[...], g[:, :c], preferred_element_type=F32)
             + jnp.dot(s2_ref[...], g[:, c:], preferred_element_type=F32))
        scr_ref[pl.ds(k1, LANES, stride=n1), :] = y * scale
        return carry

    lax.fori_loop(0, n1, body, 0, unroll=4)
    o_ref[0] = scr_ref[...].astype(o_ref.dtype)


def _fourier(fn, bsz, seq):
    c = FNET_GROUP_DIM
    assert c == LANES and seq % LANES == 0
    n1 = seq // LANES
    stage1, twr, twi, chan, c2, s2 = _fnet_consts(seq)
    ncols = LANES * c
    nb = min(FNET_COLS, ncols)
    x2 = fn.reshape(bsz * FNET_GROUPS, n1, ncols)
    a = pl.pallas_call(
        functools.partial(_fnet1_kernel, n1=n1),
        grid=(ncols // nb, bsz * FNET_GROUPS),
        in_specs=[pl.BlockSpec((1, n1, nb), lambda j, i: (i, 0, j)),
                  _const_spec(stage1.shape),
                  pl.BlockSpec((n1, nb), lambda j, i: (0, j)),
                  pl.BlockSpec((n1, nb), lambda j, i: (0, j))],
        out_specs=pl.BlockSpec((1, n1, 2 * nb), lambda j, i: (i, 0, j)),
        out_shape=jax.ShapeDtypeStruct((bsz * FNET_GROUPS, n1, 2 * ncols), BF16),
        compiler_params=_cparams(("parallel", "parallel")),
        name="fnet1",
    )(x2, stage1, twr, twi)
    a4 = a.reshape(bsz * FNET_GROUPS, n1, LANES, 2 * c)
    out = pl.pallas_call(
        functools.partial(_fnet2_kernel, n1=n1, scale=1.0 / math.sqrt(seq * c)),
        grid=(bsz, FNET_GROUPS),
        in_specs=[pl.BlockSpec((1, n1, LANES, 2 * c), lambda b, g: (b * FNET_GROUPS + g, 0, 0, 0)),
                  _const_spec(chan.shape), _const_spec(c2.shape), _const_spec(s2.shape)],
        out_specs=pl.BlockSpec((1, seq, c), lambda b, g: (b, 0, g)),
        out_shape=jax.ShapeDtypeStruct((bsz, seq, FNET_WIDTH), BF16),
        scratch_shapes=[pltpu.VMEM((n1, LANES, 2 * c), BF16), pltpu.VMEM((seq, c), F32)],
        compiler_params=_cparams(("parallel", "parallel")),
        name="fnet2",
    )(a4, chan, c2, s2)
    return out.reshape(bsz * seq, FNET_WIDTH)


def kernel(x, p, rel_bias, norm_mix, w_in, conv_dw, conv_dw_b, conv_ln_g, conv_ln_b, conv_out,
           ssm_conv_w, ssm_conv_b, ssm_a_log, ssm_dt_bias, ssm_d, ssm_norm, ssm_out,
           attn_out, fnet_out, w_gate, b_gate, w_out, norm_ffn, ffn_w1, ffn_w3, ffn_w2,
           moe_router, moe_w1, moe_w3, moe_w2, ple_gate, ple_proj, final_norm):
    bsz, seq, d = x.shape
    depth = w_in.shape[0]
    t = bsz * seq
    h = x.reshape(t, d)
    bias_tables = [_att_bias_tables(rel_bias, g, dil) for g, (_, dil) in enumerate(ATT_PATTERNS)]
    for l in range(depth):
        *qkv, conv_u, z, xbc, fn, dt = _inproj(h, norm_mix[l][None], _reorder_w_in(w_in[l]), bsz, seq)
        att = [_attention_group(qkv[g], bias_tables[g], dil, bsz, seq, f"attn{g}")
               for g, (_, dil) in enumerate(ATT_PATTERNS)]
        cnf = _conformer(conv_u, conv_dw[l], conv_dw_b[l], conv_ln_g[l], conv_ln_b[l], bsz, seq)
        ssd = _ssd(z, xbc, dt, ssm_conv_w[l], ssm_conv_b[l], ssm_a_log[l], ssm_dt_bias[l],
                   ssm_d[l], ssm_norm[l], bsz, seq)
        fnt = _fourier(fn, bsz, seq)
        wbr = jnp.stack([attn_out[l], conv_out[l], ssm_out[l], fnet_out[l]]).astype(BF16)
        h = _mix(h, norm_mix[l][None], att, (cnf, ssd, fnt), wbr, w_gate[l].astype(BF16),
                 b_gate[l][:, None, :], w_out[l].astype(BF16))
        pl_in = p[l].reshape(t, -1)
        wpg, wpp = ple_gate[l].astype(BF16), ple_proj[l].astype(BF16)
        i = l // 2
        if l % 2 == 0:
            h = _ffn(h, norm_ffn[l][None], ffn_w1[i].astype(BF16), ffn_w3[i].astype(BF16),
                     ffn_w2[i].astype(BF16), pl_in, wpg, wpp)
        else:
            h = _moe(h, norm_ffn[l][None], moe_router[i], moe_w1[i].astype(BF16),
                     moe_w3[i].astype(BF16), moe_w2[i].astype(BF16), pl_in, wpg, wpp)
    return _final_norm(h, final_norm[None]).reshape(bsz, seq, d)
```

```python
import functools
import math

import numpy as np
import jax
import jax.numpy as jnp
from jax import lax
from jax.experimental import pallas as pl
from jax.experimental.pallas import tpu as pltpu

F32 = jnp.float32
BF16 = jnp.bfloat16

EPS = 1e-6
N_BRANCHES = 4
CONV_CH = 512
CONV_K = 31
SSM_HEADS = 8
SSM_HEAD_DIM = 64
SSM_INNER = SSM_HEADS * SSM_HEAD_DIM
SSM_GROUPS = 2
SSM_STATE = 64
SSM_CONV = 5
SSM_CONV_CH = SSM_INNER + 2 * SSM_GROUPS * SSM_STATE
SSM_CHUNK = 128
ATT_PATTERNS = ((128, 1), (512, 4), (2048, 16))
ATT_GROUPS = len(ATT_PATTERNS)
ATT_HEADS = 8
ATT_HEAD_DIM = 64
ATT_WIDTH = ATT_HEADS * ATT_HEAD_DIM
REL_BUCKETS = 32
REL_MAX_DIST = 1024
FNET_GROUPS = 4
FNET_GROUP_DIM = 128
FNET_WIDTH = FNET_GROUPS * FNET_GROUP_DIM
N_EXPERTS = 8
TOP_K = 2

ATT_IN_COLS = 3 * ATT_GROUPS * ATT_WIDTH
CONV_IN_COLS = 2 * CONV_CH
SSM_IN_COLS = SSM_INNER + SSM_CONV_CH + 2 * SSM_HEADS
OFF_CONV = ATT_IN_COLS
OFF_SSM = OFF_CONV + CONV_IN_COLS
OFF_FNET = OFF_SSM + SSM_IN_COLS

LANES = 128
DT_PAD = LANES
VMEM_LIMIT = 56 * 1024 * 1024


def _cparams(sem):
    return pltpu.CompilerParams(dimension_semantics=sem, vmem_limit_bytes=VMEM_LIMIT)


def _const_spec(shape):
    nd = len(shape)
    return pl.BlockSpec(shape, lambda *_: (0,) * nd, pipeline_mode=pl.Buffered(1))


def _rms(x, g):
    return x * lax.rsqrt(jnp.mean(x * x, axis=-1, keepdims=True) + EPS) * g


_SEC_QKV = (0, ATT_IN_COLS)
_SEC_CONV = (_SEC_QKV[0] + _SEC_QKV[1], CONV_IN_COLS)
_SEC_Z = (_SEC_CONV[0] + _SEC_CONV[1], SSM_INNER)
_SEC_XBC = (_SEC_Z[0] + _SEC_Z[1], SSM_CONV_CH)
_SEC_FNET = (_SEC_XBC[0] + _SEC_XBC[1], FNET_WIDTH)
_SEC_DT = (_SEC_FNET[0] + _SEC_FNET[1], DT_PAD)
_IN_COLS_PAD = _SEC_DT[0] + _SEC_DT[1]
_MM_CHUNK = 512


ATT_TILE = 256
QKV_COLS = 3 * ATT_WIDTH


def _reorder_w_in(w):
    d = w.shape[0]
    qkv = w[:, :OFF_CONV].reshape(d, 3, ATT_GROUPS, ATT_WIDTH).transpose(0, 2, 1, 3)
    ssm = w[:, OFF_SSM:OFF_FNET]
    dt = ssm[:, SSM_INNER + SSM_CONV_CH:]
    parts = [qkv.reshape(d, OFF_CONV), w[:, OFF_CONV:OFF_SSM], ssm[:, :SSM_INNER],
             ssm[:, SSM_INNER:SSM_INNER + SSM_CONV_CH], w[:, OFF_FNET:],
             dt, jnp.zeros((d, DT_PAD - dt.shape[1]), w.dtype)]
    return jnp.concatenate(parts, axis=1).astype(BF16)


def _deinterleave_matrix(dil):
    s = np.arange(ATT_TILE)
    m = np.zeros((ATT_TILE, ATT_TILE), np.float32)
    m[(s % dil) * (ATT_TILE // dil) + s // dil, s] = 1.0
    return m


def _inproj_kernel(h_ref, g_ref, w_ref, perm_ref, q0_ref, q1_ref, q2_ref, conv_ref, z_ref,
                   xbc_ref, fn_ref, dt_ref):
    xn = _rms(h_ref[...], g_ref[...]).astype(BF16)

    def section(x, sec, store):
        start, width = sec
        for c in range(0, width, _MM_CHUNK):
            cw = min(_MM_CHUNK, width - c)
            store(c, cw, jnp.dot(x, w_ref[:, start + c:start + c + cw],
                                 preferred_element_type=F32))

    def to(ref):
        def store(c, cw, val):
            ref[:, c:c + cw] = val.astype(ref.dtype)
        return store

    def to_fnet(c, cw, val):
        for g in range(cw // FNET_GROUP_DIM):
            fn_ref[0, c // FNET_GROUP_DIM + g] = val[:, g * FNET_GROUP_DIM:(g + 1) * FNET_GROUP_DIM]

    for g, q_ref in enumerate((q0_ref, q1_ref, q2_ref)):
        x = xn
        if ATT_PATTERNS[g][1] > 1:
            x = jnp.dot(perm_ref[g], xn, preferred_element_type=F32).astype(BF16)
        section(x, (g * QKV_COLS, QKV_COLS), to(q_ref))
    section(xn, _SEC_CONV, to(conv_ref))
    section(xn, _SEC_Z, to(z_ref))
    section(xn, _SEC_XBC, to(xbc_ref))
    section(xn, _SEC_FNET, to_fnet)
    section(xn, _SEC_DT, to(dt_ref))


def _inproj(h, g, w, bsz, seq):
    t, d = h.shape
    tm = ATT_TILE
    spt = seq // tm
    perm = jnp.asarray(np.stack([_deinterleave_matrix(dil) for _, dil in ATT_PATTERNS]), BF16)
    row = lambda n: pl.BlockSpec((tm, n), lambda i: (i, 0))
    qkv = jax.ShapeDtypeStruct((t, QKV_COLS), BF16)
    return pl.pallas_call(
        _inproj_kernel,
        grid=(t // tm,),
        in_specs=[row(d), _const_spec((1, d)), _const_spec(w.shape), _const_spec(perm.shape)],
        out_specs=[row(QKV_COLS), row(QKV_COLS), row(QKV_COLS), row(CONV_IN_COLS),
                   row(SSM_INNER), row(SSM_CONV_CH),
                   pl.BlockSpec((1, FNET_GROUPS, tm, FNET_GROUP_DIM),
                                lambda i: (i // spt, 0, i % spt, 0)),
                   row(DT_PAD)],
        out_shape=[qkv, qkv, qkv,
                   jax.ShapeDtypeStruct((t, CONV_IN_COLS), F32),
                   jax.ShapeDtypeStruct((t, SSM_INNER), F32),
                   jax.ShapeDtypeStruct((t, SSM_CONV_CH), F32),
                   jax.ShapeDtypeStruct((bsz, FNET_GROUPS, seq, FNET_GROUP_DIM), F32),
                   jax.ShapeDtypeStruct((t, DT_PAD), F32)],
        compiler_params=_cparams(("parallel",)),
        name="inproj",
    )(h, g, w, perm)


def _split_dot(v, m):
    hi = v.astype(BF16)
    lo = (v - hi.astype(F32)).astype(BF16)
    return (jnp.dot(hi, m, preferred_element_type=F32) + jnp.dot(lo, m, preferred_element_type=F32))


def _interleave(pt, v):
    n = v.shape[1]
    if v.dtype != BF16:
        hi = v.astype(BF16)
        v = jnp.concatenate([hi, (v - hi.astype(F32)).astype(BF16)], axis=1)
    tiles = []
    for i in range(v.shape[0] // ATT_TILE):
        r = jnp.dot(pt, v[i * ATT_TILE:(i + 1) * ATT_TILE], preferred_element_type=F32)
        tiles.append(r if r.shape[1] == n else r[:, :n] + r[:, n:])
    return jnp.concatenate(tiles, axis=0)


def _mix_kernel(h_ref, g_ref, o0_ref, o1_ref, o2_ref, l0_ref, l1_ref, l2_ref, b1_ref, b2_ref,
                b3_ref, pt_ref, hx_ref, wbr_ref, wg_ref, cg_ref, wo_ref, o_ref):
    h = h_ref[...]
    xn = _rms(h, g_ref[...]).astype(BF16)
    outs, lses = [], []
    for g, (og_ref, lg_ref) in enumerate(zip((o0_ref, o1_ref, o2_ref), (l0_ref, l1_ref, l2_ref))):
        if ATT_PATTERNS[g][1] > 1:
            outs.append(_interleave(pt_ref[g], og_ref[...]))
            lses.append(_interleave(pt_ref[g], lg_ref[...]))
        else:
            outs.append(og_ref[...].astype(F32))
            lses.append(lg_ref[...])
    top = jnp.maximum(jnp.maximum(lses[0], lses[1]), lses[2])
    es = [jnp.exp(l - top) for l in lses]
    inv = 1.0 / (es[0] + es[1] + es[2])
    att = None
    for e, og in zip(es, outs):
        term = og * jnp.dot((e * inv).astype(BF16), hx_ref[...], preferred_element_type=F32)
        att = term if att is None else att + term
    acc = None
    for b, hid in enumerate((att.astype(BF16), b1_ref[...], b2_ref[...], b3_ref[...])):
        gate = jax.nn.sigmoid(jnp.dot(xn, wg_ref[b], preferred_element_type=F32) + cg_ref[b])
        br = jnp.dot(hid, wbr_ref[b], preferred_element_type=F32)
        acc = gate * br if acc is None else acc + gate * br
    o_ref[...] = h + jnp.dot(acc.astype(BF16), wo_ref[...], preferred_element_type=F32)


def _mix(h, g, att, others, wbr, wg, cg, wo):
    t, d = h.shape
    tm = 512
    row = lambda n: pl.BlockSpec((tm, n), lambda i: (i, 0))
    head_expand = np.zeros((LANES, ATT_WIDTH), np.float32)
    for hd in range(ATT_HEADS):
        head_expand[hd, hd * ATT_HEAD_DIM:(hd + 1) * ATT_HEAD_DIM] = 1.0
    head_expand = jnp.asarray(head_expand, BF16)
    unperm = jnp.asarray(np.stack([_deinterleave_matrix(dil).T for _, dil in ATT_PATTERNS]), BF16)
    outs = [o for o, _ in att]
    lses = [l for _, l in att]
    return pl.pallas_call(
        _mix_kernel,
        grid=(t // tm,),
        in_specs=[row(d), _const_spec((1, d))] + [row(a.shape[1]) for a in outs + lses + list(others)]
                 + [_const_spec(unperm.shape), _const_spec(head_expand.shape), _const_spec(wbr.shape),
                    _const_spec(wg.shape), _const_spec(cg.shape), _const_spec(wo.shape)],
        out_specs=row(d),
        out_shape=jax.ShapeDtypeStruct((t, d), F32),
        compiler_params=_cparams(("parallel",)),
        name="mix",
    )(h, g, *outs, *lses, *others, unperm, head_expand, wbr, wg, cg, wo)


def _ple(h2, p_ref, wpg_ref, wpp_ref):
    gate = jax.nn.sigmoid(jnp.dot(h2.astype(BF16), wpg_ref[...], preferred_element_type=F32))
    pe = jnp.dot(p_ref[...].astype(BF16), wpp_ref[...], preferred_element_type=F32)
    return h2 + gate * pe


def _swiglu_partial(xn, w1, w3, w2, scale=None):
    a = jnp.dot(xn, w1, preferred_element_type=F32)
    b = jnp.dot(xn, w3, preferred_element_type=F32)
    hid = a * jax.nn.sigmoid(a) * b
    if scale is not None:
        hid = hid * scale
    return jnp.dot(hid.astype(BF16), w2, preferred_element_type=F32)


def _ffn_kernel(h_ref, g_ref, w1_ref, w3_ref, w2_ref, p_ref, wpg_ref, wpp_ref, o_ref, xn_ref):
    j = pl.program_id(1)

    @pl.when(j == 0)
    def _():
        h = h_ref[...]
        xn_ref[...] = _rms(h, g_ref[...]).astype(BF16)
        o_ref[...] = h

    o_ref[...] += _swiglu_partial(xn_ref[...], w1_ref[...], w3_ref[...], w2_ref[...])

    @pl.when(j == pl.num_programs(1) - 1)
    def _():
        o_ref[...] = _ple(o_ref[...], p_ref, wpg_ref, wpp_ref)


def _ffn(h, g, w1, w3, w2, p, wpg, wpp):
    t, d = h.shape
    f = w1.shape[1]
    tm, tf = 512, 1408
    row = lambda n: pl.BlockSpec((tm, n), lambda i, j: (i, 0))
    return pl.pallas_call(
        _ffn_kernel,
        grid=(t // tm, f // tf),
        in_specs=[row(d), _const_spec((1, d)),
                  pl.BlockSpec((d, tf), lambda i, j: (0, j)),
                  pl.BlockSpec((d, tf), lambda i, j: (0, j)),
                  pl.BlockSpec((tf, d), lambda i, j: (j, 0)),
                  row(p.shape[1]), _const_spec(wpg.shape), _const_spec(wpp.shape)],
        out_specs=row(d),
        out_shape=jax.ShapeDtypeStruct((t, d), F32),
        scratch_shapes=[pltpu.VMEM((tm, d), BF16)],
        compiler_params=_cparams(("parallel", "arbitrary")),
        name="ffn",
    )(h, g, w1, w3, w2, p, wpg, wpp)


MOE_UNIT = 64


def _moe_route(logits):
    ne, tm = logits.shape
    eidx = lax.broadcasted_iota(jnp.int32, logits.shape, 0)
    m1 = jnp.max(logits, axis=0, keepdims=True)
    i1 = jnp.min(jnp.where(logits == m1, eidx, ne), axis=0, keepdims=True)
    rest = jnp.where(eidx == i1, -jnp.inf, logits)
    m2 = jnp.max(rest, axis=0, keepdims=True)
    i2 = jnp.min(jnp.where(rest == m2, eidx, ne), axis=0, keepdims=True)
    e2 = jnp.exp(m2 - m1)
    den = 1.0 + e2
    combine = jnp.where(eidx == i1, 1.0 / den, 0.0) + jnp.where(eidx == i2, e2 / den, 0.0)
    routed = jnp.where((eidx == i1) | (eidx == i2), 1.0, 0.0)
    r = lax.broadcasted_iota(jnp.int32, (LANES, LANES), 0)
    c = lax.broadcasted_iota(jnp.int32, (LANES, LANES), 1)
    before = jnp.where(r < c, 1.0, 0.0).astype(BF16)
    counts = jnp.zeros((ne, 1), F32)
    slots = []
    for k in range(tm // LANES):
        blk = routed[:, k * LANES:(k + 1) * LANES]
        slots.append(jnp.dot(blk.astype(BF16), before, preferred_element_type=F32) + counts)
        counts = counts + jnp.sum(blk, axis=1, keepdims=True)
    slot = jnp.where(routed > 0.0, jnp.concatenate(slots, axis=1), -1.0).astype(jnp.int32)
    return combine, slot, counts


def _moe_kernel(h_ref, g_ref, rt_ref, w1_ref, w3_ref, w2_ref, p_ref, wpg_ref, wpp_ref, o_ref,
                xn_ref, comb_ref, slot_ref, cnt_ref, xe_ref, ye_ref):
    e = pl.program_id(1)
    j = pl.program_id(2)
    ne = pl.num_programs(1)
    tm = xn_ref.shape[0]

    @pl.when((e == 0) & (j == 0))
    def _():
        h = h_ref[...]
        xn = _rms(h, g_ref[...])
        xn_ref[...] = xn.astype(BF16)
        logits = lax.dot_general(rt_ref[...], xn, (((1,), (1,)), ((), ())),
                                 preferred_element_type=F32, precision=HI)
        combine, slot, counts = _moe_route(logits)
        comb_ref[...] = combine
        slot_ref[...] = slot
        for k in range(comb_ref.shape[0]):
            cnt_ref[k] = jnp.sum(counts[k:k + 1, :]).astype(jnp.int32)
        o_ref[...] = h

    units = (cnt_ref[e] + MOE_UNIT - 1) // MOE_UNIT
    big = 4 * MOE_UNIT

    def for_row_blocks(body):
        def quad(i, carry):
            body(pl.multiple_of(i * big, big), big)
            return carry
        lax.fori_loop(0, units // 4, quad, 0)
        rem = units % 4
        base = (units // 4) * big

        @pl.when(rem >= 2)
        def _():
            body(pl.multiple_of(base, big), 2 * MOE_UNIT)

        @pl.when(rem % 2 == 1)
        def _():
            body(pl.multiple_of(base + (rem // 2) * 2 * MOE_UNIT, MOE_UNIT), MOE_UNIT)

    def one_hot(r0, rows):
        return slot_ref[pl.ds(e, 1), :] == lax.broadcasted_iota(jnp.int32, (rows, tm), 0) + r0

    @pl.when(j == 0)
    def _():
        def gather(r0, rows):
            sel = jnp.where(one_hot(r0, rows), 1.0, 0.0).astype(BF16)
            xe_ref[pl.ds(r0, rows), :] = jnp.dot(
                sel, xn_ref[...], preferred_element_type=F32).astype(BF16)
            ye_ref[pl.ds(r0, rows), :] = jnp.zeros((rows, ye_ref.shape[1]), F32)
        for_row_blocks(gather)

    def expert(r0, rows):
        ye_ref[pl.ds(r0, rows), :] += _swiglu_partial(
            xe_ref[pl.ds(r0, rows), :], w1_ref[0], w3_ref[0], w2_ref[0])
    for_row_blocks(expert)

    @pl.when(j == pl.num_programs(2) - 1)
    def _():
        def scatter(r0, rows):
            hot = one_hot(r0, rows)
            weight = jnp.sum(jnp.where(hot, comb_ref[pl.ds(e, 1), :], 0.0), axis=1, keepdims=True)
            yw = (ye_ref[pl.ds(r0, rows), :] * weight).astype(BF16)
            o_ref[...] += lax.dot_general(jnp.where(hot, 1.0, 0.0).astype(BF16), yw,
                                          (((0,), (0,)), ((), ())), preferred_element_type=F32)
        for_row_blocks(scatter)

    @pl.when((e == ne - 1) & (j == pl.num_programs(2) - 1))
    def _():
        o_ref[...] = _ple(o_ref[...], p_ref, wpg_ref, wpp_ref)


def _moe(h, g, router, w1, w3, w2, p, wpg, wpp):
    t, d = h.shape
    ne, _, f = w1.shape
    tm, tf = 1024, 1792
    row = lambda n: pl.BlockSpec((tm, n), lambda i, e, j: (i, 0))
    return pl.pallas_call(
        _moe_kernel,
        grid=(t // tm, ne, f // tf),
        in_specs=[row(d), _const_spec((1, d)), _const_spec((ne, d)),
                  pl.BlockSpec((1, d, tf), lambda i, e, j: (e, 0, j)),
                  pl.BlockSpec((1, d, tf), lambda i, e, j: (e, 0, j)),
                  pl.BlockSpec((1, tf, d), lambda i, e, j: (e, j, 0)),
                  row(p.shape[1]), _const_spec(wpg.shape), _const_spec(wpp.shape)],
        out_specs=row(d),
        out_shape=jax.ShapeDtypeStruct((t, d), F32),
        scratch_shapes=[pltpu.VMEM((tm, d), BF16), pltpu.VMEM((ne, tm), F32),
                        pltpu.VMEM((ne, tm), jnp.int32), pltpu.SMEM((ne,), jnp.int32),
                        pltpu.VMEM((tm, d), BF16), pltpu.VMEM((tm, d), F32)],
        compiler_params=_cparams(("parallel", "arbitrary", "arbitrary")),
        name="moe",
    )(h, g, router.T, w1, w3, w2, p, wpg, wpp)


def _final_norm_kernel(h_ref, g_ref, o_ref):
    o_ref[...] = _rms(h_ref[...], g_ref[...])


def _final_norm(h, g):
    t, d = h.shape
    tm = 1024
    row = pl.BlockSpec((tm, d), lambda i: (i, 0))
    return pl.pallas_call(
        _final_norm_kernel, grid=(t // tm,),
        in_specs=[row, _const_spec((1, d))], out_specs=row,
        out_shape=jax.ShapeDtypeStruct((t, d), F32),
        compiler_params=_cparams(("parallel",)), name="final_norm",
    )(h, g)


ATT_HALF = 64
ATT_QB = 128
ATT_KB = ATT_QB + 2 * ATT_HALF
NEG = -1e30
assert all(w // (2 * d) == ATT_HALF for w, d in ATT_PATTERNS)


def _t5_bucket(rel):
    half = REL_BUCKETS // 2
    max_exact = half // 2
    n = np.abs(rel)
    large = max_exact + (np.log(np.maximum(n, 1) / max_exact) / math.log(REL_MAX_DIST / max_exact)
                         * (half - max_exact)).astype(np.int32)
    large = np.minimum(large, half - 1)
    return np.where(rel > 0, half, 0) + np.where(n < max_exact, n, large)


def _att_bias_tables(rel_bias, g, dil):
    i = np.arange(ATT_QB)[:, None]
    j = np.arange(ATT_KB)[None, :]
    rel = j - ATT_HALF - i
    band = np.abs(rel) <= ATT_HALF
    pick = np.eye(REL_BUCKETS, dtype=np.float32)[_t5_bucket(dil * rel)]
    heads = rel_bias[:, g * ATT_HEADS:(g + 1) * ATT_HEADS].astype(F32)
    bias = jnp.einsum('qkb,bh->hqk', pick, heads, precision=lax.Precision.HIGHEST)
    tables = []
    for v in range(4):
        ok = band
        if v & 1:
            ok = ok & (j >= ATT_HALF)
        if v & 2:
            ok = ok & (j < ATT_QB + ATT_HALF)
        tables.append(jnp.where(ok[None], bias, NEG))
    return jnp.stack(tables)


def _attn_kernel(q_ref, k_ref, v_ref, kp_ref, kn_ref, vp_ref, vn_ref, bias_ref, o_ref, lse_ref,
                 qbuf, kbuf, vbuf, obuf, lbuf, *, tq, n_blocks):
    flat = lambda ref: ref[0].reshape(-1, ref.shape[-1])
    qbuf[...] = flat(q_ref)
    kbuf[0:ATT_HALF] = flat(kp_ref)
    kbuf[ATT_HALF:ATT_HALF + tq] = flat(k_ref)
    kbuf[ATT_HALF + tq:] = flat(kn_ref)
    vbuf[0:ATT_HALF] = flat(vp_ref)
    vbuf[ATT_HALF:ATT_HALF + tq] = flat(v_ref)
    vbuf[ATT_HALF + tq:] = flat(vn_ref)
    nsb = tq // ATT_QB
    first = pl.program_id(2) * nsb
    lane = lax.broadcasted_iota(jnp.int32, (ATT_QB, LANES), 1)
    low = lane < ATT_HEAD_DIM
    lane_row = lax.broadcasted_iota(jnp.int32, (1, LANES), 1)
    keep = [(lane_row < ATT_HEAD_DIM).astype(BF16), (lane_row >= ATT_HEAD_DIM).astype(BF16)]
    ones = jnp.ones((ATT_KB, LANES), BF16)

    def block(sb, carry):
        r0 = pl.multiple_of(sb * ATT_QB, ATT_QB)
        gsb = first + sb
        variant = (gsb == 0).astype(jnp.int32) + 2 * (gsb == n_blocks - 1).astype(jnp.int32)
        q = qbuf[pl.ds(r0, ATT_QB), :] * (ATT_HEAD_DIM ** -0.5)
        lse_all = jnp.zeros((ATT_QB, LANES), F32)
        outs = []
        for pair in range(ATT_HEADS // 2):
            cols = slice(pair * LANES, (pair + 1) * LANES)
            qp = q[:, cols]
            kp = kbuf[pl.ds(r0, ATT_KB), cols]
            vp = jnp.concatenate([vbuf[pl.ds(r0, ATT_KB), cols], ones], axis=1)
            res = []
            for half in range(2):
                h = 2 * pair + half
                s = lax.dot_general(qp * keep[half], kp, (((1,), (1,)), ((), ())),
                                    preferred_element_type=F32)
                s = s + bias_ref[variant, h]
                m = jnp.max(s, axis=-1, keepdims=True)
                e = jnp.exp(s - m)
                pv = jnp.dot(e.astype(BF16), vp, preferred_element_type=F32)
                den = pv[:, LANES:]
                res.append(pv[:, :LANES] / den)
                lse_all = jnp.where(lane == h, m + jnp.log(den), lse_all)
            outs.append(jnp.where(low, res[0], res[1]))
        obuf[pl.ds(r0, ATT_QB), :] = jnp.concatenate(outs, axis=1).astype(obuf.dtype)
        lbuf[pl.ds(r0, ATT_QB), :] = lse_all
        return carry

    lax.fori_loop(0, nsb, block, 0)
    o_ref[0] = obuf[...].reshape(o_ref.shape[1:])
    lse_ref[0] = lbuf[...].reshape(lse_ref.shape[1:])


def _attention_group(qkv, bias_tables, dil, bsz, seq, name):
    sub_len = seq // dil
    assert sub_len % ATT_QB == 0 and seq % ATT_TILE == 0 and ATT_TILE % dil == 0
    rows = ATT_TILE // dil
    tq = min(512, sub_len)
    w = ATT_WIDTH
    hrows = min(rows, ATT_HALF)
    n_hb = sub_len // ATT_HALF

    def view(a, chunk):
        return a.reshape(bsz, (seq // ATT_TILE) * (rows // chunk), dil, chunk, a.shape[-1])

    def main(part):
        return pl.BlockSpec((1, tq // rows, None, rows, w), lambda b, r, n: (b, n, r, 0, part))

    def halo(part, nxt):
        if nxt:
            blk = lambda n: jnp.minimum((n + 1) * (tq // ATT_HALF), n_hb - 1)
        else:
            blk = lambda n: jnp.maximum(n * (tq // ATT_HALF) - 1, 0)
        return pl.BlockSpec((1, ATT_HALF // hrows, None, hrows, w),
                            lambda b, r, n: (b, blk(n), r, 0, part))

    mv, hv = view(qkv, rows), view(qkv, hrows)
    o, lse = pl.pallas_call(
        functools.partial(_attn_kernel, tq=tq, n_blocks=sub_len // ATT_QB),
        grid=(bsz, dil, sub_len // tq),
        in_specs=[main(0), main(1), main(2), halo(1, False), halo(1, True), halo(2, False),
                  halo(2, True), _const_spec(bias_tables.shape)],
        out_specs=[pl.BlockSpec((1, tq // rows, None, rows, w), lambda b, r, n: (b, n, r, 0, 0)),
                   pl.BlockSpec((1, tq // rows, None, rows, LANES), lambda b, r, n: (b, n, r, 0, 0))],
        out_shape=[jax.ShapeDtypeStruct((bsz, seq // ATT_TILE, dil, rows, w), BF16),
                   jax.ShapeDtypeStruct((bsz, seq // ATT_TILE, dil, rows, LANES), F32)],
        scratch_shapes=[pltpu.VMEM((tq, w), BF16),
                        pltpu.VMEM((tq + 2 * ATT_HALF, w), BF16),
                        pltpu.VMEM((tq + 2 * ATT_HALF, w), BF16),
                        pltpu.VMEM((tq, w), BF16), pltpu.VMEM((tq, LANES), F32)],
        compiler_params=_cparams(("parallel", "parallel", "parallel")),
        name=name,
    )(mv, mv, mv, hv, hv, hv, hv, bias_tables)
    return o.reshape(bsz * seq, w), lse.reshape(bsz * seq, LANES)


SUBLANES = 8
CONV_HALO = 16
CONV_ROWS = 64


def _conformer_kernel(u_ref, up_ref, un_ref, dw_ref, dwb_ref, lng_ref, lnb_ref, o_ref,
                      hp_ref, sh_ref, *, ts):
    n = pl.program_id(1)

    def glu(u):
        return u[:, :CONV_CH] * jax.nn.sigmoid(u[:, CONV_CH:])

    hp_ref[0:CONV_HALO] = jnp.where(n > 0, glu(up_ref[0]), 0.0)
    hp_ref[CONV_HALO:CONV_HALO + ts] = glu(u_ref[0])
    hp_ref[CONV_HALO + ts:] = jnp.where(n < pl.num_programs(1) - 1, glu(un_ref[0]), 0.0)
    span = ts + 2 * CONV_HALO - SUBLANES
    for b in range(SUBLANES):
        sh_ref[b] = hp_ref[pl.ds(b, span), :]
    first = CONV_HALO - CONV_K // 2

    def rows(c, carry):
        r0 = pl.multiple_of(c * CONV_ROWS, CONV_ROWS)
        acc = jnp.broadcast_to(dwb_ref[...], (CONV_ROWS, CONV_CH))
        for k in range(CONV_K):
            a, b = divmod(first + k, SUBLANES)
            acc = acc + dw_ref[pl.ds(k, 1), :] * sh_ref[b, pl.ds(r0 + a * SUBLANES, CONV_ROWS), :]
        mu = jnp.mean(acc, axis=-1, keepdims=True)
        cen = acc - mu
        var = jnp.mean(cen * cen, axis=-1, keepdims=True)
        y = cen * lax.rsqrt(var + EPS) * lng_ref[...] + lnb_ref[...]
        o_ref[0, pl.ds(r0, CONV_ROWS), :] = (y * jax.nn.sigmoid(y)).astype(o_ref.dtype)
        return carry

    lax.fori_loop(0, ts // CONV_ROWS, rows, 0)


def _conformer(u, dw, dw_b, ln_g, ln_b, bsz, seq):
    ts = min(512, seq)
    view = u.reshape(bsz, seq, 2 * CONV_CH)
    hb = ts // CONV_HALO
    n_hb = seq // CONV_HALO
    out = pl.pallas_call(
        functools.partial(_conformer_kernel, ts=ts),
        grid=(bsz, seq // ts),
        in_specs=[pl.BlockSpec((1, ts, 2 * CONV_CH), lambda b, n: (b, n, 0)),
                  pl.BlockSpec((1, CONV_HALO, 2 * CONV_CH),
                               lambda b, n: (b, jnp.maximum(n * hb - 1, 0), 0)),
                  pl.BlockSpec((1, CONV_HALO, 2 * CONV_CH),
                               lambda b, n: (b, jnp.minimum((n + 1) * hb, n_hb - 1), 0)),
                  _const_spec(dw.shape), _const_spec((1, CONV_CH)), _const_spec((1, CONV_CH)),
                  _const_spec((1, CONV_CH))],
        out_specs=pl.BlockSpec((1, ts, CONV_CH), lambda b, n: (b, n, 0)),
        out_shape=jax.ShapeDtypeStruct((bsz, seq, CONV_CH), BF16),
        scratch_shapes=[pltpu.VMEM((ts + 2 * CONV_HALO, CONV_CH), F32),
                        pltpu.VMEM((SUBLANES, ts + 2 * CONV_HALO - SUBLANES, CONV_CH), F32)],
        compiler_params=_cparams(("parallel", "parallel")),
        name="conformer",
    )(view, view, view, dw, dw_b[None], ln_g[None], ln_b[None])
    return out.reshape(bsz * seq, CONV_CH)


SSD_HALO = SUBLANES
SSD_ROWS = 64
SSD_BC = SSM_GROUPS * SSM_STATE
HEADS_PER_GROUP = SSM_HEADS // SSM_GROUPS
GROUP_LANES = HEADS_PER_GROUP * SSM_HEAD_DIM
HI = lax.Precision.HIGHEST
assert SSD_BC == LANES and SSM_CHUNK == LANES and 2 * SSM_HEADS <= LANES


def _ssd_pre_kernel(x_ref, xp_ref, xn_ref, dt_ref, cw_ref, cb_ref, dtb_ref, arow_ref, tri_ref,
                    ex_ref, xo_ref, dto_ref, nf_ref, nb_ref, cy_ref, hp_ref, sh_ref, *, ts, phases):
    n = pl.program_id(1)
    hp_ref[0:SSD_HALO] = jnp.where(n > 0, xp_ref[0], 0.0)
    hp_ref[SSD_HALO:SSD_HALO + ts] = x_ref[0]
    hp_ref[SSD_HALO + ts:] = jnp.where(n < pl.num_programs(1) - 1, xn_ref[0], 0.0)
    for i, b in enumerate(phases):
        sh_ref[i] = hp_ref[pl.ds(b, ts + SSD_HALO), :]
    first = SSD_HALO - SSM_CONV // 2

    def rows(c, carry):
        r0 = pl.multiple_of(c * SSD_ROWS, SSD_ROWS)
        acc = jnp.broadcast_to(cb_ref[...], (SSD_ROWS, SSM_CONV_CH))
        for k in range(SSM_CONV):
            a, b = divmod(first + k, SUBLANES)
            acc = acc + cw_ref[pl.ds(k, 1), :] * sh_ref[phases.index(b),
                                                        pl.ds(r0 + a * SUBLANES, SSD_ROWS), :]
        xo_ref[0, pl.ds(r0, SSD_ROWS), :] = acc * jax.nn.sigmoid(acc)
        return carry

    lax.fori_loop(0, ts // SSD_ROWS, rows, 0)
    x = dt_ref[0] + dtb_ref[...]
    softplus = jnp.maximum(x, 0.0) + jnp.log1p(jnp.exp(-jnp.abs(x)))
    lane = lax.broadcasted_iota(jnp.int32, x.shape, 1)
    dtv = jnp.where(lane < 2 * SSM_HEADS, softplus, 0.0)
    dto_ref[0] = dtv

    nch = ts // SSM_CHUNK
    chunk = lambda k: slice(k * SSM_CHUNK, (k + 1) * SSM_CHUNK)
    dta = dtv * arow_ref[...]
    acs_all = _cumsum_both(jnp.concatenate([dta[chunk(k)] for k in range(nch)], axis=1), tri_ref)
    clane = lax.broadcasted_iota(jnp.int32, (SSM_CHUNK, DT_PAD), 1)
    ws, totals = [], [[], []]
    for k in range(nch):
        acs = acs_all[:, chunk(k)]
        tot_f, tot_b = acs[SSM_CHUNK - 1:SSM_CHUNK, :], acs[0:1, :]
        to_end = jnp.where(clane < SSM_HEADS, tot_f - acs,
                           jnp.where(clane < 2 * SSM_HEADS, tot_b - acs, 0.0))
        ws.append(dtv[chunk(k)] * jnp.exp(to_end))
        totals[0].append(jnp.exp(tot_f))
        totals[1].append(jnp.exp(tot_b))
    w_all = jnp.concatenate(ws, axis=0)
    xs = xo_ref[0, :, :SSM_INNER]
    hl = lax.broadcasted_iota(jnp.int32, (SSM_STATE, SSM_INNER), 1)
    pad_rows = jnp.zeros((SUBLANES - nch % SUBLANES, DT_PAD), F32)
    for d, new_ref in enumerate((nf_ref, nb_ref)):
        xw = (xs * _expand(w_all, ex_ref, d)).astype(BF16)
        carry = _expand(jnp.concatenate(totals[d] + [pad_rows], axis=0), ex_ref, d)
        for k in range(nch):
            bmat = xo_ref[0, chunk(k), SSM_INNER:SSM_INNER + SSD_BC].astype(BF16)
            full = lax.dot_general(bmat, xw[chunk(k)], (((0,), (0,)), ((), ())),
                                   preferred_element_type=F32)
            new = full[:SSM_STATE]
            for g in range(1, SSM_GROUPS):
                new = jnp.where(hl >= g * GROUP_LANES, full[g * SSM_STATE:(g + 1) * SSM_STATE], new)
            new_ref[0, k] = new
            cy_ref[0, k, d:d + 1, :] = carry[k:k + 1]


def _ssd_pre(xbc, dt_raw, conv_w, conv_b, dt_bias, consts, bsz, seq):
    ts = min(512, seq)
    nc = seq // SSM_CHUNK
    nch = ts // SSM_CHUNK
    a_row, tri, expand = consts
    new = jax.ShapeDtypeStruct((bsz, nc, SSM_STATE, SSM_INNER), F32)
    new_spec = pl.BlockSpec((1, nch, SSM_STATE, SSM_INNER), lambda b, n: (b, n, 0, 0))
    xv = xbc.reshape(bsz, seq, SSM_CONV_CH)
    dv = dt_raw.reshape(bsz, seq, DT_PAD)
    hb = ts // SSD_HALO
    n_hb = seq // SSD_HALO
    first = SSD_HALO - SSM_CONV // 2
    phases = tuple(sorted({(first + k) % SUBLANES for k in range(SSM_CONV)}))
    dtb = jnp.zeros((1, DT_PAD), F32).at[0, :2 * SSM_HEADS].set(dt_bias.reshape(-1))
    return pl.pallas_call(
        functools.partial(_ssd_pre_kernel, ts=ts, phases=phases),
        grid=(bsz, seq // ts),
        in_specs=[pl.BlockSpec((1, ts, SSM_CONV_CH), lambda b, n: (b, n, 0)),
                  pl.BlockSpec((1, SSD_HALO, SSM_CONV_CH),
                               lambda b, n: (b, jnp.maximum(n * hb - 1, 0), 0)),
                  pl.BlockSpec((1, SSD_HALO, SSM_CONV_CH),
                               lambda b, n: (b, jnp.minimum((n + 1) * hb, n_hb - 1), 0)),
                  pl.BlockSpec((1, ts, DT_PAD), lambda b, n: (b, n, 0)),
                  _const_spec(conv_w.shape), _const_spec((1, SSM_CONV_CH)),
                  _const_spec((1, DT_PAD)), _const_spec(a_row.shape), _const_spec(tri.shape),
                  _const_spec(expand.shape)],
        out_specs=[pl.BlockSpec((1, ts, SSM_CONV_CH), lambda b, n: (b, n, 0)),
                   pl.BlockSpec((1, ts, DT_PAD), lambda b, n: (b, n, 0)),
                   new_spec, new_spec,
                   pl.BlockSpec((1, nch, 2, SSM_INNER), lambda b, n: (b, n, 0, 0))],
        out_shape=[jax.ShapeDtypeStruct((bsz, seq, SSM_CONV_CH), F32),
                   jax.ShapeDtypeStruct((bsz, seq, DT_PAD), F32),
                   new, new, jax.ShapeDtypeStruct((bsz, nc, 2, SSM_INNER), F32)],
        scratch_shapes=[pltpu.VMEM((ts + 2 * SSD_HALO, SSM_CONV_CH), F32),
                        pltpu.VMEM((len(phases), ts + SSD_HALO, SSM_CONV_CH), F32)],
        compiler_params=_cparams(("parallel", "parallel")),
        name="ssd_pre",
    )(xv, xv, xv, dv, conv_w, conv_b[None], dtb, a_row, tri, expand)


def _ssd_consts(a_log):
    a_row = jnp.zeros((1, DT_PAD), F32).at[0, :2 * SSM_HEADS].set(-jnp.exp(a_log.reshape(-1)))
    lower = np.tril(np.ones((SSM_CHUNK, SSM_CHUNK), np.float32))
    tri = jnp.asarray(np.concatenate([lower, lower.T], axis=0), BF16)
    expand = np.zeros((2, DT_PAD, SSM_INNER), np.float32)
    for d in range(2):
        for h in range(SSM_HEADS):
            expand[d, d * SSM_HEADS + h, h * SSM_HEAD_DIM:(h + 1) * SSM_HEAD_DIM] = 1.0
    return a_row, tri, jnp.asarray(expand, BF16)


def _cumsum_both(dta, tri_ref):
    hi = dta.astype(BF16)
    r1 = dta - hi.astype(F32)
    mid = r1.astype(BF16)
    lo = (r1 - mid.astype(F32)).astype(BF16)
    both = (jnp.dot(tri_ref[...], hi, preferred_element_type=F32)
            + jnp.dot(tri_ref[...], mid, preferred_element_type=F32)
            + jnp.dot(tri_ref[...], lo, preferred_element_type=F32))
    lane = lax.broadcasted_iota(jnp.int32, dta.shape, 1) % DT_PAD
    return jnp.where(lane < SSM_HEADS, both[:SSM_CHUNK], both[SSM_CHUNK:])


def _expand(v, ex_ref, d):
    return _split_dot(v, ex_ref[d])


def _ssd_scan_kernel(nf_ref, nb_ref, cf_ref, cb_ref, pf_ref, pb_ref, sf_ref, sb_ref, *, nch):
    @pl.when(pl.program_id(1) == 0)
    def _():
        sf_ref[...] = jnp.zeros_like(sf_ref)
        sb_ref[...] = jnp.zeros_like(sb_ref)

    def scan(d, order, new_ref, carry_ref, st_ref, out_ref):
        st = st_ref[...]
        for k in order:
            out_ref[0, k] = st.astype(out_ref.dtype)
            st = st * carry_ref[0, k, d:d + 1, :] + new_ref[0, k]
        st_ref[...] = st

    scan(0, range(nch), nf_ref, cf_ref, sf_ref, pf_ref)
    scan(1, range(nch - 1, -1, -1), nb_ref, cb_ref, sb_ref, pb_ref)


SSD_SCAN_CHUNKS = 16


def _ssd_states(new_f, new_b, carry, bsz, seq):
    nc = seq // SSM_CHUNK
    nch = min(SSD_SCAN_CHUNKS, nc)
    steps = nc // nch
    fwd = lambda b, c: (b, c, 0, 0)
    bwd = lambda b, c: (b, steps - 1 - c, 0, 0)
    st = jax.ShapeDtypeStruct((bsz, nc, SSM_STATE, SSM_INNER), BF16)
    blk = (1, nch, SSM_STATE, SSM_INNER)
    return pl.pallas_call(
        functools.partial(_ssd_scan_kernel, nch=nch),
        grid=(bsz, steps),
        in_specs=[pl.BlockSpec(blk, fwd), pl.BlockSpec(blk, bwd),
                  pl.BlockSpec((1, nch, 2, SSM_INNER), fwd),
                  pl.BlockSpec((1, nch, 2, SSM_INNER), bwd)],
        out_specs=[pl.BlockSpec(blk, fwd), pl.BlockSpec(blk, bwd)],
        out_shape=[st, st],
        scratch_shapes=[pltpu.VMEM((SSM_STATE, SSM_INNER), F32),
                        pltpu.VMEM((SSM_STATE, SSM_INNER), F32)],
        compiler_params=_cparams(("parallel", "arbitrary")),
        name="ssd_scan",
    )(new_f, new_b, carry, carry)


def _ssd_out_kernel(x_ref, dt_ref, z_ref, pf_ref, pb_ref, arow_ref, tri_ref, ex_ref, dskip_ref,
                    ng_ref, o_ref):
    xs = x_ref[0, :, :SSM_INNER]
    bmat = x_ref[0, :, SSM_INNER:SSM_INNER + SSD_BC].astype(BF16)
    cmat = x_ref[0, :, SSM_INNER + SSD_BC:].astype(BF16)
    dtv = dt_ref[0]
    acs = _cumsum_both(dtv * arow_ref[...], tri_ref)
    acs_t = acs.T
    dt_t = dtv.T
    eacs = jnp.exp(acs)
    row = lax.broadcasted_iota(jnp.int32, (SSM_CHUNK, SSM_CHUNK), 0)
    col = lax.broadcasted_iota(jnp.int32, (SSM_CHUNK, SSM_CHUNK), 1)
    past, now = col < row, col == row
    low = lax.broadcasted_iota(jnp.int32, (SSM_CHUNK, LANES), 1) < SSM_HEAD_DIM
    glane = lax.broadcasted_iota(jnp.int32, (1, SSD_BC), 1) // SSM_STATE
    scores = [lax.dot_general(cmat * (glane == g).astype(BF16), bmat,
                              (((1,), (1,)), ((), ())), preferred_element_type=F32)
              for g in range(SSM_GROUPS)]
    xs_b = xs.astype(BF16)
    keep = [low.astype(BF16), (~low).astype(BF16)]
    diag = []
    for pair in range(SSM_HEADS // 2):
        xp = xs_b[:, pair * LANES:(pair + 1) * LANES]
        mats = []
        for half in range(2):
            f = 2 * pair + half
            b = SSM_HEADS + f
            seg = jnp.where(past | now, acs[:, f:f + 1] - acs_t[f:f + 1, :],
                            acs[:, b:b + 1] - acs_t[b:b + 1, :])
            dts = (jnp.where(past, dt_t[f:f + 1, :], dt_t[b:b + 1, :])
                   + jnp.where(now, dt_t[f:f + 1, :], 0.0))
            mats.append((scores[f // HEADS_PER_GROUP] * jnp.exp(seg) * dts).astype(BF16))
        diag.append(jnp.dot(jnp.concatenate(mats, axis=1),
                            jnp.concatenate([xp * keep[0], xp * keep[1]], axis=0),
                            preferred_element_type=F32))
    y = dskip_ref[...] * xs + jnp.concatenate(diag, axis=1)
    hgroup = lax.broadcasted_iota(jnp.int32, (1, SSM_INNER), 1) // GROUP_LANES
    for d, p_ref in enumerate((pf_ref, pb_ref)):
        prev = p_ref[0, 0]
        stacked = jnp.concatenate([prev * (hgroup == g).astype(BF16) for g in range(SSM_GROUPS)],
                                  axis=0)
        off = jnp.dot(cmat, stacked, preferred_element_type=F32)
        y = y + off * _expand(eacs, ex_ref, d)
    z = z_ref[0]
    y = y * (z * jax.nn.sigmoid(z))
    o_ref[0] = _rms(y, ng_ref[...]).astype(o_ref.dtype)


def _ssd_out(xact, dtv, z, prev_f, prev_b, consts, d_skip, norm_g, bsz, seq):
    nc = seq // SSM_CHUNK
    a_row, tri, expand = consts
    dsk = jnp.repeat(d_skip, SSM_HEAD_DIM)[None]
    chunk = lambda n: pl.BlockSpec((1, SSM_CHUNK, n), lambda b, c: (b, c, 0))
    state = pl.BlockSpec((1, 1, SSM_STATE, SSM_INNER), lambda b, c: (b, c, 0, 0))
    out = pl.pallas_call(
        _ssd_out_kernel,
        grid=(bsz, nc),
        in_specs=[chunk(SSM_CONV_CH), chunk(DT_PAD), chunk(SSM_INNER), state, state,
                  _const_spec(a_row.shape), _const_spec(tri.shape), _const_spec(expand.shape),
                  _const_spec(dsk.shape), _const_spec((1, SSM_INNER))],
        out_specs=chunk(SSM_INNER),
        out_shape=jax.ShapeDtypeStruct((bsz, seq, SSM_INNER), BF16),
        compiler_params=_cparams(("parallel", "parallel")),
        name="ssd_out",
    )(xact, dtv, z.reshape(bsz, seq, SSM_INNER), prev_f, prev_b, a_row, tri, expand, dsk,
      norm_g[None])
    return out.reshape(bsz * seq, SSM_INNER)


def _ssd(z, xbc, dt_raw, conv_w, conv_b, a_log, dt_bias, d_skip, norm_g, bsz, seq):
    consts = _ssd_consts(a_log)
    xact, dtv, new_f, new_b, carry = _ssd_pre(xbc, dt_raw, conv_w, conv_b, dt_bias, consts, bsz, seq)
    prev_f, prev_b = _ssd_states(new_f, new_b, carry, bsz, seq)
    return _ssd_out(xact, dtv, z, prev_f, prev_b, consts, d_skip, norm_g, bsz, seq)


FNET_COLS = 4096


def _dft_cos_sin(n):
    ang = 2.0 * np.pi * np.outer(np.arange(n), np.arange(n)) / n
    return np.cos(ang), np.sin(ang)


def _fnet_consts(seq):
    c = FNET_GROUP_DIM
    n1 = seq // LANES
    c1, s1 = _dft_cos_sin(n1)
    stage1 = np.concatenate([c1, -s1], axis=0)
    ang = 2.0 * np.pi * np.outer(np.arange(n1), np.arange(LANES)) / seq
    twr = np.repeat(np.cos(ang), c, axis=1)
    twi = np.repeat(-np.sin(ang), c, axis=1)
    cc, sc = _dft_cos_sin(c)
    chan = np.block([[cc, -sc], [sc, cc]])
    c2, s2 = _dft_cos_sin(LANES)
    return (jnp.asarray(stage1, BF16), jnp.asarray(twr, F32), jnp.asarray(twi, F32),
            jnp.asarray(chan, BF16), jnp.asarray(c2, BF16), jnp.asarray(s2, BF16))


def _fnet1_kernel(x_ref, f_ref, twr_ref, twi_ref, o_ref, *, n1):
    c = FNET_GROUP_DIM
    a = jnp.dot(f_ref[...], x_ref[0].astype(BF16), preferred_element_type=F32)
    ar, ai = a[:n1], a[n1:]
    twr, twi = twr_ref[...], twi_ref[...]
    re = (ar * twr - ai * twi).astype(o_ref.dtype)
    im = (ar * twi + ai * twr).astype(o_ref.dtype)
    for j in range(re.shape[1] // c):
        o_ref[0, :, (2 * j) * c:(2 * j + 1) * c] = re[:, j * c:(j + 1) * c]
        o_ref[0, :, (2 * j + 1) * c:(2 * j + 2) * c] = im[:, j * c:(j + 1) * c]


def _fnet2_kernel(a_ref, chan_ref, c2_ref, s2_ref, o_ref, g_ref, scr_ref, *, n1, scale):
    c = FNET_GROUP_DIM
    pitch = scr_ref.shape[0] // LANES
    per = min(8, n1)
    for i in range(n1 // per):
        blk = a_ref[0, i * per:(i + 1) * per].reshape(per * LANES, 2 * c)
        g = jnp.dot(blk, chan_ref[...], preferred_element_type=F32).astype(BF16)
        g_ref[i * per:(i + 1) * per] = g.reshape(per, LANES, 2 * c)

    def body(k1, carry):
        g = g_ref[k1]
        y = (jnp.dot(c2_ref[...], g[:, :c], preferred_element_type=F32)
             + jnp.dot(s2_ref[...], g[:, c:], preferred_element_type=F32))
        scr_ref[pl.ds(k1, LANES, stride=pitch), :] = y * scale
        return carry

    lax.fori_loop(0, n1, body, 0, unroll=4)

    def compact(k2, carry):
        src = pl.multiple_of(k2 * pitch, SUBLANES)
        dst = pl.multiple_of(k2 * n1, n1)
        o_ref[0, pl.ds(dst, n1), :] = scr_ref[pl.ds(src, n1), :].astype(o_ref.dtype)
        return carry

    lax.fori_loop(0, LANES, compact, 0, unroll=8)


def _fourier(fn, bsz, seq):
    c = FNET_GROUP_DIM
    assert c == LANES and seq % LANES == 0
    n1 = seq // LANES
    stage1, twr, twi, chan, c2, s2 = _fnet_consts(seq)
    ncols = LANES * c
    nb = min(FNET_COLS, ncols)
    x2 = fn.reshape(bsz * FNET_GROUPS, n1, ncols)
    a = pl.pallas_call(
        functools.partial(_fnet1_kernel, n1=n1),
        grid=(ncols // nb, bsz * FNET_GROUPS),
        in_specs=[pl.BlockSpec((1, n1, nb), lambda j, i: (i, 0, j)),
                  _const_spec(stage1.shape),
                  pl.BlockSpec((n1, nb), lambda j, i: (0, j)),
                  pl.BlockSpec((n1, nb), lambda j, i: (0, j))],
        out_specs=pl.BlockSpec((1, n1, 2 * nb), lambda j, i: (i, 0, j)),
        out_shape=jax.ShapeDtypeStruct((bsz * FNET_GROUPS, n1, 2 * ncols), BF16),
        compiler_params=_cparams(("parallel", "parallel")),
        name="fnet1",
    )(x2, stage1, twr, twi)
    a4 = a.reshape(bsz * FNET_GROUPS, n1, LANES, 2 * c)
    out = pl.pallas_call(
        functools.partial(_fnet2_kernel, n1=n1, scale=1.0 / math.sqrt(seq * c)),
        grid=(bsz, FNET_GROUPS),
        in_specs=[pl.BlockSpec((1, n1, LANES, 2 * c), lambda b, g: (b * FNET_GROUPS + g, 0, 0, 0)),
                  _const_spec(chan.shape), _const_spec(c2.shape), _const_spec(s2.shape)],
        out_specs=pl.BlockSpec((1, seq, c), lambda b, g: (b, 0, g)),
        out_shape=jax.ShapeDtypeStruct((bsz, seq, FNET_WIDTH), BF16),
        scratch_shapes=[pltpu.VMEM((n1, LANES, 2 * c), BF16),
                        pltpu.VMEM((LANES * (n1 + SUBLANES), c), F32)],
        compiler_params=_cparams(("parallel", "parallel")),
        name="fnet2",
    )(a4, chan, c2, s2)
    return out.reshape(bsz * seq, FNET_WIDTH)


def kernel(x, p, rel_bias, norm_mix, w_in, conv_dw, conv_dw_b, conv_ln_g, conv_ln_b, conv_out,
           ssm_conv_w, ssm_conv_b, ssm_a_log, ssm_dt_bias, ssm_d, ssm_norm, ssm_out,
           attn_out, fnet_out, w_gate, b_gate, w_out, norm_ffn, ffn_w1, ffn_w3, ffn_w2,
           moe_router, moe_w1, moe_w3, moe_w2, ple_gate, ple_proj, final_norm):
    bsz, seq, d = x.shape
    depth = w_in.shape[0]
    t = bsz * seq
    h = x.reshape(t, d)
    bias_tables = [_att_bias_tables(rel_bias, g, dil) for g, (_, dil) in enumerate(ATT_PATTERNS)]
    for l in range(depth):
        *qkv, conv_u, z, xbc, fn, dt = _inproj(h, norm_mix[l][None], _reorder_w_in(w_in[l]), bsz, seq)
        att = [_attention_group(qkv[g], bias_tables[g], dil, bsz, seq, f"attn{g}")
               for g, (_, dil) in enumerate(ATT_PATTERNS)]
        cnf = _conformer(conv_u, conv_dw[l], conv_dw_b[l], conv_ln_g[l], conv_ln_b[l], bsz, seq)
        ssd = _ssd(z, xbc, dt, ssm_conv_w[l], ssm_conv_b[l], ssm_a_log[l], ssm_dt_bias[l],
                   ssm_d[l], ssm_norm[l], bsz, seq)
        fnt = _fourier(fn, bsz, seq)
        wbr = jnp.stack([attn_out[l], conv_out[l], ssm_out[l], fnet_out[l]]).astype(BF16)
        h = _mix(h, norm_mix[l][None], att, (cnf, ssd, fnt), wbr, w_gate[l].astype(BF16),
                 b_gate[l][:, None, :], w_out[l].astype(BF16))
        pl_in = p[l].reshape(t, -1)
        wpg, wpp = ple_gate[l].astype(BF16), ple_proj[l].astype(BF16)
        i = l // 2
        if l % 2 == 0:
            h = _ffn(h, norm_ffn[l][None], ffn_w1[i].astype(BF16), ffn_w3[i].astype(BF16),
                     ffn_w2[i].astype(BF16), pl_in, wpg, wpp)
        else:
            h = _moe(h, norm_ffn[l][None], moe_router[i], moe_w1[i].astype(BF16),
                     moe_w3[i].astype(BF16), moe_w2[i].astype(BF16), pl_in, wpg, wpp)
    return _final_norm(h, final_norm[None]).reshape(bsz, seq, d)
```

```python
import functools
import math

import numpy as np
import jax
import jax.numpy as jnp
from jax import lax
from jax.experimental import pallas as pl
from jax.experimental.pallas import tpu as pltpu

F32 = jnp.float32
BF16 = jnp.bfloat16

EPS = 1e-6
N_BRANCHES = 4
CONV_CH = 512
CONV_K = 31
SSM_HEADS = 8
SSM_HEAD_DIM = 64
SSM_INNER = SSM_HEADS * SSM_HEAD_DIM
SSM_GROUPS = 2
SSM_STATE = 64
SSM_CONV = 5
SSM_CONV_CH = SSM_INNER + 2 * SSM_GROUPS * SSM_STATE
SSM_CHUNK = 128
ATT_PATTERNS = ((128, 1), (512, 4), (2048, 16))
ATT_GROUPS = len(ATT_PATTERNS)
ATT_HEADS = 8
ATT_HEAD_DIM = 64
ATT_WIDTH = ATT_HEADS * ATT_HEAD_DIM
REL_BUCKETS = 32
REL_MAX_DIST = 1024
FNET_GROUPS = 4
FNET_GROUP_DIM = 128
FNET_WIDTH = FNET_GROUPS * FNET_GROUP_DIM
N_EXPERTS = 8
TOP_K = 2

ATT_IN_COLS = 3 * ATT_GROUPS * ATT_WIDTH
CONV_IN_COLS = 2 * CONV_CH
SSM_IN_COLS = SSM_INNER + SSM_CONV_CH + 2 * SSM_HEADS
OFF_CONV = ATT_IN_COLS
OFF_SSM = OFF_CONV + CONV_IN_COLS
OFF_FNET = OFF_SSM + SSM_IN_COLS

LANES = 128
DT_PAD = LANES
VMEM_LIMIT = 56 * 1024 * 1024


def _cparams(sem):
    return pltpu.CompilerParams(dimension_semantics=sem, vmem_limit_bytes=VMEM_LIMIT)


def _const_spec(shape):
    nd = len(shape)
    return pl.BlockSpec(shape, lambda *_: (0,) * nd, pipeline_mode=pl.Buffered(1))


def _rms(x, g):
    return x * lax.rsqrt(jnp.mean(x * x, axis=-1, keepdims=True) + EPS) * g


_SEC_QKV = (0, ATT_IN_COLS)
_SEC_CONV = (_SEC_QKV[0] + _SEC_QKV[1], CONV_IN_COLS)
_SEC_Z = (_SEC_CONV[0] + _SEC_CONV[1], SSM_INNER)
_SEC_XBC = (_SEC_Z[0] + _SEC_Z[1], SSM_CONV_CH)
_SEC_FNET = (_SEC_XBC[0] + _SEC_XBC[1], FNET_WIDTH)
_SEC_DT = (_SEC_FNET[0] + _SEC_FNET[1], DT_PAD)
_IN_COLS_PAD = _SEC_DT[0] + _SEC_DT[1]
_MM_CHUNK = 512


ATT_TILE = 256
QKV_COLS = 3 * ATT_WIDTH
INPROJ_ROWS = 2 * ATT_TILE


def _reorder_w_in(w):
    d = w.shape[0]
    qkv = w[:, :OFF_CONV].reshape(d, 3, ATT_GROUPS, ATT_WIDTH).transpose(0, 2, 1, 3)
    ssm = w[:, OFF_SSM:OFF_FNET]
    dt = ssm[:, SSM_INNER + SSM_CONV_CH:]
    parts = [qkv.reshape(d, OFF_CONV), w[:, OFF_CONV:OFF_SSM], ssm[:, :SSM_INNER],
             ssm[:, SSM_INNER:SSM_INNER + SSM_CONV_CH], w[:, OFF_FNET:],
             dt, jnp.zeros((d, DT_PAD - dt.shape[1]), w.dtype)]
    return jnp.concatenate(parts, axis=1).astype(BF16)


def _deinterleave_matrix(dil):
    s = np.arange(ATT_TILE)
    m = np.zeros((ATT_TILE, ATT_TILE), np.float32)
    m[(s % dil) * (ATT_TILE // dil) + s // dil, s] = 1.0
    return m


def _inproj_kernel(h_ref, g_ref, w_ref, perm_ref, q0_ref, q1_ref, q2_ref, conv_ref, z_ref,
                   xbc_ref, fn_ref, dt_ref):
    xn = _rms(h_ref[...], g_ref[...]).astype(BF16)

    def section(x, sec, store):
        start, width = sec
        for c in range(0, width, _MM_CHUNK):
            cw = min(_MM_CHUNK, width - c)
            store(c, cw, jnp.dot(x, w_ref[:, start + c:start + c + cw],
                                 preferred_element_type=F32))

    def to(ref):
        def store(c, cw, val):
            ref[:, c:c + cw] = val.astype(ref.dtype)
        return store

    def to_fnet(c, cw, val):
        for g in range(cw // FNET_GROUP_DIM):
            fn_ref[0, c // FNET_GROUP_DIM + g] = val[:, g * FNET_GROUP_DIM:(g + 1) * FNET_GROUP_DIM]

    for g, q_ref in enumerate((q0_ref, q1_ref, q2_ref)):
        x = xn
        if ATT_PATTERNS[g][1] > 1:
            x = jnp.concatenate(
                [jnp.dot(perm_ref[g], xn[i * ATT_TILE:(i + 1) * ATT_TILE],
                         preferred_element_type=F32) for i in range(xn.shape[0] // ATT_TILE)],
                axis=0).astype(BF16)
        section(x, (g * QKV_COLS, QKV_COLS), to(q_ref))
    section(xn, _SEC_CONV, to(conv_ref))
    section(xn, _SEC_Z, to(z_ref))
    section(xn, _SEC_XBC, to(xbc_ref))
    section(xn, _SEC_FNET, to_fnet)
    section(xn, _SEC_DT, to(dt_ref))


def _inproj(h, g, w, bsz, seq):
    t, d = h.shape
    tm = INPROJ_ROWS
    spt = seq // tm
    perm = jnp.asarray(np.stack([_deinterleave_matrix(dil) for _, dil in ATT_PATTERNS]), BF16)
    row = lambda n: pl.BlockSpec((tm, n), lambda i: (i, 0))
    qkv = jax.ShapeDtypeStruct((t, QKV_COLS), BF16)
    return pl.pallas_call(
        _inproj_kernel,
        grid=(t // tm,),
        in_specs=[row(d), _const_spec((1, d)), _const_spec(w.shape), _const_spec(perm.shape)],
        out_specs=[row(QKV_COLS), row(QKV_COLS), row(QKV_COLS), row(CONV_IN_COLS),
                   row(SSM_INNER), row(SSM_CONV_CH),
                   pl.BlockSpec((1, FNET_GROUPS, tm, FNET_GROUP_DIM),
                                lambda i: (i // spt, 0, i % spt, 0)),
                   row(DT_PAD)],
        out_shape=[qkv, qkv, qkv,
                   jax.ShapeDtypeStruct((t, CONV_IN_COLS), F32),
                   jax.ShapeDtypeStruct((t, SSM_INNER), F32),
                   jax.ShapeDtypeStruct((t, SSM_CONV_CH), F32),
                   jax.ShapeDtypeStruct((bsz, FNET_GROUPS, seq, FNET_GROUP_DIM), F32),
                   jax.ShapeDtypeStruct((t, DT_PAD), F32)],
        compiler_params=_cparams(("parallel",)),
        name="inproj",
    )(h, g, w, perm)


def _split_dot(v, m):
    hi = v.astype(BF16)
    lo = (v - hi.astype(F32)).astype(BF16)
    return (jnp.dot(hi, m, preferred_element_type=F32) + jnp.dot(lo, m, preferred_element_type=F32))


def _interleave(pt, v):
    n = v.shape[1]
    if v.dtype != BF16:
        hi = v.astype(BF16)
        v = jnp.concatenate([hi, (v - hi.astype(F32)).astype(BF16)], axis=1)
    tiles = []
    for i in range(v.shape[0] // ATT_TILE):
        r = jnp.dot(pt, v[i * ATT_TILE:(i + 1) * ATT_TILE], preferred_element_type=F32)
        tiles.append(r if r.shape[1] == n else r[:, :n] + r[:, n:])
    return jnp.concatenate(tiles, axis=0)


def _mix_kernel(h_ref, g_ref, o0_ref, o1_ref, o2_ref, l0_ref, l1_ref, l2_ref, b1_ref, b2_ref,
                b3_ref, pt_ref, hx_ref, wbr_ref, wg_ref, cg_ref, wo_ref, o_ref):
    h = h_ref[...]
    xn = _rms(h, g_ref[...]).astype(BF16)
    outs, lses = [], []
    for g, (og_ref, lg_ref) in enumerate(zip((o0_ref, o1_ref, o2_ref), (l0_ref, l1_ref, l2_ref))):
        if ATT_PATTERNS[g][1] > 1:
            outs.append(_interleave(pt_ref[g], og_ref[...]))
            lses.append(_interleave(pt_ref[g], lg_ref[...]))
        else:
            outs.append(og_ref[...].astype(F32))
            lses.append(lg_ref[...])
    top = jnp.maximum(jnp.maximum(lses[0], lses[1]), lses[2])
    es = [jnp.exp(l - top) for l in lses]
    inv = 1.0 / (es[0] + es[1] + es[2])
    att = None
    for e, og in zip(es, outs):
        term = og * jnp.dot((e * inv).astype(BF16), hx_ref[...], preferred_element_type=F32)
        att = term if att is None else att + term
    acc = None
    for b, hid in enumerate((att.astype(BF16), b1_ref[...], b2_ref[...], b3_ref[...])):
        gate = jax.nn.sigmoid(jnp.dot(xn, wg_ref[b], preferred_element_type=F32) + cg_ref[b])
        br = jnp.dot(hid, wbr_ref[b], preferred_element_type=F32)
        acc = gate * br if acc is None else acc + gate * br
    o_ref[...] = h + jnp.dot(acc.astype(BF16), wo_ref[...], preferred_element_type=F32)


def _mix(h, g, att, others, wbr, wg, cg, wo):
    t, d = h.shape
    tm = 512
    row = lambda n: pl.BlockSpec((tm, n), lambda i: (i, 0))
    head_expand = np.zeros((LANES, ATT_WIDTH), np.float32)
    for hd in range(ATT_HEADS):
        head_expand[hd, hd * ATT_HEAD_DIM:(hd + 1) * ATT_HEAD_DIM] = 1.0
    head_expand = jnp.asarray(head_expand, BF16)
    unperm = jnp.asarray(np.stack([_deinterleave_matrix(dil).T for _, dil in ATT_PATTERNS]), BF16)
    outs = [o for o, _ in att]
    lses = [l for _, l in att]
    return pl.pallas_call(
        _mix_kernel,
        grid=(t // tm,),
        in_specs=[row(d), _const_spec((1, d))] + [row(a.shape[1]) for a in outs + lses + list(others)]
                 + [_const_spec(unperm.shape), _const_spec(head_expand.shape), _const_spec(wbr.shape),
                    _const_spec(wg.shape), _const_spec(cg.shape), _const_spec(wo.shape)],
        out_specs=row(d),
        out_shape=jax.ShapeDtypeStruct((t, d), F32),
        compiler_params=_cparams(("parallel",)),
        name="mix",
    )(h, g, *outs, *lses, *others, unperm, head_expand, wbr, wg, cg, wo)


def _ple(h2, p_ref, wpg_ref, wpp_ref):
    gate = jax.nn.sigmoid(jnp.dot(h2.astype(BF16), wpg_ref[...], preferred_element_type=F32))
    pe = jnp.dot(p_ref[...].astype(BF16), wpp_ref[...], preferred_element_type=F32)
    return h2 + gate * pe


def _swiglu_partial(xn, w1, w3, w2, scale=None):
    a = jnp.dot(xn, w1, preferred_element_type=F32)
    b = jnp.dot(xn, w3, preferred_element_type=F32)
    hid = a * jax.nn.sigmoid(a) * b
    if scale is not None:
        hid = hid * scale
    return jnp.dot(hid.astype(BF16), w2, preferred_element_type=F32)


def _ffn_kernel(h_ref, g_ref, w1_ref, w3_ref, w2_ref, p_ref, wpg_ref, wpp_ref, o_ref, xn_ref):
    j = pl.program_id(1)

    @pl.when(j == 0)
    def _():
        h = h_ref[...]
        xn_ref[...] = _rms(h, g_ref[...]).astype(BF16)
        o_ref[...] = h

    o_ref[...] += _swiglu_partial(xn_ref[...], w1_ref[...], w3_ref[...], w2_ref[...])

    @pl.when(j == pl.num_programs(1) - 1)
    def _():
        o_ref[...] = _ple(o_ref[...], p_ref, wpg_ref, wpp_ref)


def _ffn(h, g, w1, w3, w2, p, wpg, wpp):
    t, d = h.shape
    f = w1.shape[1]
    tm, tf = 512, 1408
    row = lambda n: pl.BlockSpec((tm, n), lambda i, j: (i, 0))
    return pl.pallas_call(
        _ffn_kernel,
        grid=(t // tm, f // tf),
        in_specs=[row(d), _const_spec((1, d)),
                  pl.BlockSpec((d, tf), lambda i, j: (0, j)),
                  pl.BlockSpec((d, tf), lambda i, j: (0, j)),
                  pl.BlockSpec((tf, d), lambda i, j: (j, 0)),
                  row(p.shape[1]), _const_spec(wpg.shape), _const_spec(wpp.shape)],
        out_specs=row(d),
        out_shape=jax.ShapeDtypeStruct((t, d), F32),
        scratch_shapes=[pltpu.VMEM((tm, d), BF16)],
        compiler_params=_cparams(("parallel", "arbitrary")),
        name="ffn",
    )(h, g, w1, w3, w2, p, wpg, wpp)


MOE_ROWS = 128


def _moe_route(logits):
    ne, tm = logits.shape
    eidx = lax.broadcasted_iota(jnp.int32, logits.shape, 0)
    m1 = jnp.max(logits, axis=0, keepdims=True)
    i1 = jnp.min(jnp.where(logits == m1, eidx, ne), axis=0, keepdims=True)
    rest = jnp.where(eidx == i1, -jnp.inf, logits)
    m2 = jnp.max(rest, axis=0, keepdims=True)
    i2 = jnp.min(jnp.where(rest == m2, eidx, ne), axis=0, keepdims=True)
    e2 = jnp.exp(m2 - m1)
    den = 1.0 + e2
    combine = jnp.where(eidx == i1, 1.0 / den, 0.0) + jnp.where(eidx == i2, e2 / den, 0.0)
    routed = jnp.where((eidx == i1) | (eidx == i2), 1.0, 0.0)
    r = lax.broadcasted_iota(jnp.int32, (LANES, LANES), 0)
    c = lax.broadcasted_iota(jnp.int32, (LANES, LANES), 1)
    before = jnp.where(r < c, 1.0, 0.0).astype(BF16)
    counts = jnp.zeros((ne, 1), F32)
    slots = []
    for k in range(tm // LANES):
        blk = routed[:, k * LANES:(k + 1) * LANES]
        slots.append(jnp.dot(blk.astype(BF16), before, preferred_element_type=F32) + counts)
        counts = counts + jnp.sum(blk, axis=1, keepdims=True)
    slot = jnp.where(routed > 0.0, jnp.concatenate(slots, axis=1), -1.0).astype(jnp.int32)
    return combine, slot, counts


def _moe_kernel(h_ref, g_ref, rt_ref, w1_ref, w3_ref, w2_ref, p_ref, wpg_ref, wpp_ref, o_ref,
                xn_ref, comb_ref, slot_ref, cnt_ref, xe_ref, ye_ref):
    e = pl.program_id(1)
    j = pl.program_id(2)
    ne = pl.num_programs(1)
    tm = xn_ref.shape[0]

    @pl.when((e == 0) & (j == 0))
    def _():
        h = h_ref[...]
        xn = _rms(h, g_ref[...])
        xn_ref[...] = xn.astype(BF16)
        logits = lax.dot_general(rt_ref[...], xn, (((1,), (1,)), ((), ())),
                                 preferred_element_type=F32, precision=HI)
        combine, slot, counts = _moe_route(logits)
        comb_ref[...] = combine
        slot_ref[...] = slot
        for k in range(comb_ref.shape[0]):
            cnt_ref[k] = jnp.sum(counts[k:k + 1, :]).astype(jnp.int32)
        o_ref[...] = h

    n_blocks = (cnt_ref[e] + MOE_ROWS - 1) // MOE_ROWS

    def for_row_blocks(body):
        def pair(i, carry):
            body(pl.multiple_of(i * 2 * MOE_ROWS, 2 * MOE_ROWS), 2 * MOE_ROWS)
            return carry
        lax.fori_loop(0, n_blocks // 2, pair, 0)

        @pl.when(n_blocks % 2 == 1)
        def _():
            body(pl.multiple_of((n_blocks - 1) * MOE_ROWS, MOE_ROWS), MOE_ROWS)

    def one_hot(r0, rows):
        return slot_ref[pl.ds(e, 1), :] == lax.broadcasted_iota(jnp.int32, (rows, tm), 0) + r0

    @pl.when(j == 0)
    def _():
        def gather(r0, rows):
            sel = jnp.where(one_hot(r0, rows), 1.0, 0.0).astype(BF16)
            xe_ref[pl.ds(r0, rows), :] = jnp.dot(
                sel, xn_ref[...], preferred_element_type=F32).astype(BF16)
            ye_ref[pl.ds(r0, rows), :] = jnp.zeros((rows, ye_ref.shape[1]), F32)
        for_row_blocks(gather)

    def expert(r0, rows):
        ye_ref[pl.ds(r0, rows), :] += _swiglu_partial(
            xe_ref[pl.ds(r0, rows), :], w1_ref[0], w3_ref[0], w2_ref[0])
    for_row_blocks(expert)

    @pl.when(j == pl.num_programs(2) - 1)
    def _():
        def scatter(r0, rows):
            hot = one_hot(r0, rows)
            weight = jnp.sum(jnp.where(hot, comb_ref[pl.ds(e, 1), :], 0.0), axis=1, keepdims=True)
            yw = (ye_ref[pl.ds(r0, rows), :] * weight).astype(BF16)
            o_ref[...] += lax.dot_general(jnp.where(hot, 1.0, 0.0).astype(BF16), yw,
                                          (((0,), (0,)), ((), ())), preferred_element_type=F32)
        for_row_blocks(scatter)

    @pl.when((e == ne - 1) & (j == pl.num_programs(2) - 1))
    def _():
        o_ref[...] = _ple(o_ref[...], p_ref, wpg_ref, wpp_ref)


def _moe(h, g, router, w1, w3, w2, p, wpg, wpp):
    t, d = h.shape
    ne, _, f = w1.shape
    tm, tf = 1024, 1792
    row = lambda n: pl.BlockSpec((tm, n), lambda i, e, j: (i, 0))
    return pl.pallas_call(
        _moe_kernel,
        grid=(t // tm, ne, f // tf),
        in_specs=[row(d), _const_spec((1, d)), _const_spec((ne, d)),
                  pl.BlockSpec((1, d, tf), lambda i, e, j: (e, 0, j)),
                  pl.BlockSpec((1, d, tf), lambda i, e, j: (e, 0, j)),
                  pl.BlockSpec((1, tf, d), lambda i, e, j: (e, j, 0)),
                  row(p.shape[1]), _const_spec(wpg.shape), _const_spec(wpp.shape)],
        out_specs=row(d),
        out_shape=jax.ShapeDtypeStruct((t, d), F32),
        scratch_shapes=[pltpu.VMEM((tm, d), BF16), pltpu.VMEM((ne, tm), F32),
                        pltpu.VMEM((ne, tm), jnp.int32), pltpu.SMEM((ne,), jnp.int32),
                        pltpu.VMEM((tm, d), BF16), pltpu.VMEM((tm, d), F32)],
        compiler_params=_cparams(("parallel", "arbitrary", "arbitrary")),
        name="moe",
    )(h, g, router.T, w1, w3, w2, p, wpg, wpp)


def _final_norm_kernel(h_ref, g_ref, o_ref):
    o_ref[...] = _rms(h_ref[...], g_ref[...])


def _final_norm(h, g):
    t, d = h.shape
    tm = 1024
    row = pl.BlockSpec((tm, d), lambda i: (i, 0))
    return pl.pallas_call(
        _final_norm_kernel, grid=(t // tm,),
        in_specs=[row, _const_spec((1, d))], out_specs=row,
        out_shape=jax.ShapeDtypeStruct((t, d), F32),
        compiler_params=_cparams(("parallel",)), name="final_norm",
    )(h, g)


ATT_HALF = 64
ATT_QB = 128
ATT_KB = ATT_QB + 2 * ATT_HALF
NEG = -1e30
assert all(w // (2 * d) == ATT_HALF for w, d in ATT_PATTERNS)


def _t5_bucket(rel):
    half = REL_BUCKETS // 2
    max_exact = half // 2
    n = np.abs(rel)
    large = max_exact + (np.log(np.maximum(n, 1) / max_exact) / math.log(REL_MAX_DIST / max_exact)
                         * (half - max_exact)).astype(np.int32)
    large = np.minimum(large, half - 1)
    return np.where(rel > 0, half, 0) + np.where(n < max_exact, n, large)


def _att_bias_tables(rel_bias, g, dil):
    i = np.arange(ATT_QB)[:, None]
    j = np.arange(ATT_KB)[None, :]
    rel = j - ATT_HALF - i
    band = np.abs(rel) <= ATT_HALF
    pick = np.eye(REL_BUCKETS, dtype=np.float32)[_t5_bucket(dil * rel)]
    heads = rel_bias[:, g * ATT_HEADS:(g + 1) * ATT_HEADS].astype(F32)
    bias = jnp.einsum('qkb,bh->hqk', pick, heads, precision=lax.Precision.HIGHEST)
    tables = []
    for v in range(4):
        ok = band
        if v & 1:
            ok = ok & (j >= ATT_HALF)
        if v & 2:
            ok = ok & (j < ATT_QB + ATT_HALF)
        tables.append(jnp.where(ok[None], bias, NEG))
    return jnp.stack(tables)


def _attn_kernel(q_ref, k_ref, v_ref, kp_ref, kn_ref, vp_ref, vn_ref, bias_ref, o_ref, lse_ref,
                 qbuf, kbuf, vbuf, obuf, lbuf, *, tq, n_blocks):
    flat = lambda ref: ref[0].reshape(-1, ref.shape[-1])
    qbuf[...] = flat(q_ref)
    kbuf[0:ATT_HALF] = flat(kp_ref)
    kbuf[ATT_HALF:ATT_HALF + tq] = flat(k_ref)
    kbuf[ATT_HALF + tq:] = flat(kn_ref)
    vbuf[0:ATT_HALF] = flat(vp_ref)
    vbuf[ATT_HALF:ATT_HALF + tq] = flat(v_ref)
    vbuf[ATT_HALF + tq:] = flat(vn_ref)
    nsb = tq // ATT_QB
    first = pl.program_id(2) * nsb
    lane = lax.broadcasted_iota(jnp.int32, (ATT_QB, LANES), 1)
    low = lane < ATT_HEAD_DIM
    lane_row = lax.broadcasted_iota(jnp.int32, (1, LANES), 1)
    keep = [(lane_row < ATT_HEAD_DIM).astype(BF16), (lane_row >= ATT_HEAD_DIM).astype(BF16)]
    ones = jnp.ones((ATT_KB, LANES), BF16)

    def block(sb, carry):
        r0 = pl.multiple_of(sb * ATT_QB, ATT_QB)
        gsb = first + sb
        variant = (gsb == 0).astype(jnp.int32) + 2 * (gsb == n_blocks - 1).astype(jnp.int32)
        q = qbuf[pl.ds(r0, ATT_QB), :] * (ATT_HEAD_DIM ** -0.5)
        lse_all = jnp.zeros((ATT_QB, LANES), F32)
        outs = []
        for pair in range(ATT_HEADS // 2):
            cols = slice(pair * LANES, (pair + 1) * LANES)
            qp = q[:, cols]
            kp = kbuf[pl.ds(r0, ATT_KB), cols]
            vp = jnp.concatenate([vbuf[pl.ds(r0, ATT_KB), cols], ones], axis=1)
            res = []
            for half in range(2):
                h = 2 * pair + half
                s = lax.dot_general(qp * keep[half], kp, (((1,), (1,)), ((), ())),
                                    preferred_element_type=F32)
                s = s + bias_ref[variant, h]
                m = jnp.max(s, axis=-1, keepdims=True)
                e = jnp.exp(s - m)
                pv = jnp.dot(e.astype(BF16), vp, preferred_element_type=F32)
                den = pv[:, LANES:]
                res.append(pv[:, :LANES] / den)
                lse_all = jnp.where(lane == h, m + jnp.log(den), lse_all)
            outs.append(jnp.where(low, res[0], res[1]))
        obuf[pl.ds(r0, ATT_QB), :] = jnp.concatenate(outs, axis=1).astype(obuf.dtype)
        lbuf[pl.ds(r0, ATT_QB), :] = lse_all
        return carry

    lax.fori_loop(0, nsb, block, 0, unroll=4)
    o_ref[0] = obuf[...].reshape(o_ref.shape[1:])
    lse_ref[0] = lbuf[...].reshape(lse_ref.shape[1:])


def _attention_group(qkv, bias_tables, dil, bsz, seq, name):
    sub_len = seq // dil
    assert sub_len % ATT_QB == 0 and seq % ATT_TILE == 0 and ATT_TILE % dil == 0
    rows = ATT_TILE // dil
    tq = min(512, sub_len)
    w = ATT_WIDTH
    hrows = min(rows, ATT_HALF)
    n_hb = sub_len // ATT_HALF

    def view(a, chunk):
        return a.reshape(bsz, (seq // ATT_TILE) * (rows // chunk), dil, chunk, a.shape[-1])

    def main(part):
        return pl.BlockSpec((1, tq // rows, None, rows, w), lambda b, r, n: (b, n, r, 0, part))

    def halo(part, nxt):
        if nxt:
            blk = lambda n: jnp.minimum((n + 1) * (tq // ATT_HALF), n_hb - 1)
        else:
            blk = lambda n: jnp.maximum(n * (tq // ATT_HALF) - 1, 0)
        return pl.BlockSpec((1, ATT_HALF // hrows, None, hrows, w),
                            lambda b, r, n: (b, blk(n), r, 0, part))

    mv, hv = view(qkv, rows), view(qkv, hrows)
    o, lse = pl.pallas_call(
        functools.partial(_attn_kernel, tq=tq, n_blocks=sub_len // ATT_QB),
        grid=(bsz, dil, sub_len // tq),
        in_specs=[main(0), main(1), main(2), halo(1, False), halo(1, True), halo(2, False),
                  halo(2, True), _const_spec(bias_tables.shape)],
        out_specs=[pl.BlockSpec((1, tq // rows, None, rows, w), lambda b, r, n: (b, n, r, 0, 0)),
                   pl.BlockSpec((1, tq // rows, None, rows, LANES), lambda b, r, n: (b, n, r, 0, 0))],
        out_shape=[jax.ShapeDtypeStruct((bsz, seq // ATT_TILE, dil, rows, w), BF16),
                   jax.ShapeDtypeStruct((bsz, seq // ATT_TILE, dil, rows, LANES), F32)],
        scratch_shapes=[pltpu.VMEM((tq, w), BF16),
                        pltpu.VMEM((tq + 2 * ATT_HALF, w), BF16),
                        pltpu.VMEM((tq + 2 * ATT_HALF, w), BF16),
                        pltpu.VMEM((tq, w), BF16), pltpu.VMEM((tq, LANES), F32)],
        compiler_params=_cparams(("parallel", "parallel", "parallel")),
        name=name,
    )(mv, mv, mv, hv, hv, hv, hv, bias_tables)
    return o.reshape(bsz * seq, w), lse.reshape(bsz * seq, LANES)


SUBLANES = 8
CONV_HALO = 16
CONV_ROWS = 64


def _conformer_kernel(u_ref, up_ref, un_ref, dw_ref, dwb_ref, lng_ref, lnb_ref, o_ref,
                      hp_ref, sh_ref, *, ts):
    n = pl.program_id(1)

    def glu(u):
        return u[:, :CONV_CH] * jax.nn.sigmoid(u[:, CONV_CH:])

    hp_ref[0:CONV_HALO] = jnp.where(n > 0, glu(up_ref[0]), 0.0)
    hp_ref[CONV_HALO:CONV_HALO + ts] = glu(u_ref[0])
    hp_ref[CONV_HALO + ts:] = jnp.where(n < pl.num_programs(1) - 1, glu(un_ref[0]), 0.0)
    span = ts + 2 * CONV_HALO - SUBLANES
    for b in range(SUBLANES):
        sh_ref[b] = hp_ref[pl.ds(b, span), :]
    first = CONV_HALO - CONV_K // 2

    def rows(c, carry):
        r0 = pl.multiple_of(c * CONV_ROWS, CONV_ROWS)
        acc = jnp.broadcast_to(dwb_ref[...], (CONV_ROWS, CONV_CH))
        for k in range(CONV_K):
            a, b = divmod(first + k, SUBLANES)
            acc = acc + dw_ref[pl.ds(k, 1), :] * sh_ref[b, pl.ds(r0 + a * SUBLANES, CONV_ROWS), :]
        mu = jnp.mean(acc, axis=-1, keepdims=True)
        cen = acc - mu
        var = jnp.mean(cen * cen, axis=-1, keepdims=True)
        y = cen * lax.rsqrt(var + EPS) * lng_ref[...] + lnb_ref[...]
        o_ref[0, pl.ds(r0, CONV_ROWS), :] = (y * jax.nn.sigmoid(y)).astype(o_ref.dtype)
        return carry

    lax.fori_loop(0, ts // CONV_ROWS, rows, 0, unroll=2)


def _conformer(u, dw, dw_b, ln_g, ln_b, bsz, seq):
    ts = min(512, seq)
    view = u.reshape(bsz, seq, 2 * CONV_CH)
    hb = ts // CONV_HALO
    n_hb = seq // CONV_HALO
    out = pl.pallas_call(
        functools.partial(_conformer_kernel, ts=ts),
        grid=(bsz, seq // ts),
        in_specs=[pl.BlockSpec((1, ts, 2 * CONV_CH), lambda b, n: (b, n, 0)),
                  pl.BlockSpec((1, CONV_HALO, 2 * CONV_CH),
                               lambda b, n: (b, jnp.maximum(n * hb - 1, 0), 0)),
                  pl.BlockSpec((1, CONV_HALO, 2 * CONV_CH),
                               lambda b, n: (b, jnp.minimum((n + 1) * hb, n_hb - 1), 0)),
                  _const_spec(dw.shape), _const_spec((1, CONV_CH)), _const_spec((1, CONV_CH)),
                  _const_spec((1, CONV_CH))],
        out_specs=pl.BlockSpec((1, ts, CONV_CH), lambda b, n: (b, n, 0)),
        out_shape=jax.ShapeDtypeStruct((bsz, seq, CONV_CH), BF16),
        scratch_shapes=[pltpu.VMEM((ts + 2 * CONV_HALO, CONV_CH), F32),
                        pltpu.VMEM((SUBLANES, ts + 2 * CONV_HALO - SUBLANES, CONV_CH), F32)],
        compiler_params=_cparams(("parallel", "parallel")),
        name="conformer",
    )(view, view, view, dw, dw_b[None], ln_g[None], ln_b[None])
    return out.reshape(bsz * seq, CONV_CH)


SSD_HALO = SUBLANES
SSD_ROWS = 64
SSD_BC = SSM_GROUPS * SSM_STATE
HEADS_PER_GROUP = SSM_HEADS // SSM_GROUPS
GROUP_LANES = HEADS_PER_GROUP * SSM_HEAD_DIM
HI = lax.Precision.HIGHEST
assert SSD_BC == LANES and SSM_CHUNK == LANES and 2 * SSM_HEADS <= LANES


def _ssd_pre_kernel(x_ref, xp_ref, xn_ref, dt_ref, cw_ref, cb_ref, dtb_ref, arow_ref, tri_ref,
                    ex_ref, xo_ref, dto_ref, acs_ref, nf_ref, nb_ref, cy_ref, hp_ref, sh_ref,
                    *, ts, phases):
    n = pl.program_id(1)
    hp_ref[0:SSD_HALO] = jnp.where(n > 0, xp_ref[0], 0.0)
    hp_ref[SSD_HALO:SSD_HALO + ts] = x_ref[0]
    hp_ref[SSD_HALO + ts:] = jnp.where(n < pl.num_programs(1) - 1, xn_ref[0], 0.0)
    for i, b in enumerate(phases):
        sh_ref[i] = hp_ref[pl.ds(b, ts + SSD_HALO), :]
    first = SSD_HALO - SSM_CONV // 2

    def rows(c, carry):
        r0 = pl.multiple_of(c * SSD_ROWS, SSD_ROWS)
        acc = jnp.broadcast_to(cb_ref[...], (SSD_ROWS, SSM_CONV_CH))
        for k in range(SSM_CONV):
            a, b = divmod(first + k, SUBLANES)
            acc = acc + cw_ref[pl.ds(k, 1), :] * sh_ref[phases.index(b),
                                                        pl.ds(r0 + a * SUBLANES, SSD_ROWS), :]
        xo_ref[0, pl.ds(r0, SSD_ROWS), :] = acc * jax.nn.sigmoid(acc)
        return carry

    lax.fori_loop(0, ts // SSD_ROWS, rows, 0)
    x = dt_ref[0] + dtb_ref[...]
    softplus = jnp.maximum(x, 0.0) + jnp.log1p(jnp.exp(-jnp.abs(x)))
    lane = lax.broadcasted_iota(jnp.int32, x.shape, 1)
    dtv = jnp.where(lane < 2 * SSM_HEADS, softplus, 0.0)
    dto_ref[0] = dtv

    nch = ts // SSM_CHUNK
    chunk = lambda k: slice(k * SSM_CHUNK, (k + 1) * SSM_CHUNK)
    dta = dtv * arow_ref[...]
    acs_all = _cumsum_both(jnp.concatenate([dta[chunk(k)] for k in range(nch)], axis=1), tri_ref)
    clane = lax.broadcasted_iota(jnp.int32, (SSM_CHUNK, DT_PAD), 1)
    ws, totals = [], [[], []]
    for k in range(nch):
        acs = acs_all[:, chunk(k)]
        acs_ref[0, chunk(k), :] = acs
        tot_f, tot_b = acs[SSM_CHUNK - 1:SSM_CHUNK, :], acs[0:1, :]
        to_end = jnp.where(clane < SSM_HEADS, tot_f - acs,
                           jnp.where(clane < 2 * SSM_HEADS, tot_b - acs, 0.0))
        ws.append(dtv[chunk(k)] * jnp.exp(to_end))
        totals[0].append(jnp.exp(tot_f))
        totals[1].append(jnp.exp(tot_b))
    w_all = jnp.concatenate(ws, axis=0)
    xs = xo_ref[0, :, :SSM_INNER]
    hl = lax.broadcasted_iota(jnp.int32, (SSM_STATE, SSM_INNER), 1)
    pad_rows = jnp.zeros((SUBLANES - nch % SUBLANES, DT_PAD), F32)
    for d, new_ref in enumerate((nf_ref, nb_ref)):
        xw = (xs * _expand(w_all, ex_ref, d)).astype(BF16)
        carry = _expand(jnp.concatenate(totals[d] + [pad_rows], axis=0), ex_ref, d)
        for k in range(nch):
            bmat = xo_ref[0, chunk(k), SSM_INNER:SSM_INNER + SSD_BC].astype(BF16)
            full = lax.dot_general(bmat, xw[chunk(k)], (((0,), (0,)), ((), ())),
                                   preferred_element_type=F32)
            new = full[:SSM_STATE]
            for g in range(1, SSM_GROUPS):
                new = jnp.where(hl >= g * GROUP_LANES, full[g * SSM_STATE:(g + 1) * SSM_STATE], new)
            new_ref[0, k] = new
            cy_ref[0, k, d:d + 1, :] = carry[k:k + 1]


def _ssd_pre(xbc, dt_raw, conv_w, conv_b, dt_bias, consts, bsz, seq):
    ts = min(512, seq)
    nc = seq // SSM_CHUNK
    nch = ts // SSM_CHUNK
    a_row, tri, expand = consts
    new = jax.ShapeDtypeStruct((bsz, nc, SSM_STATE, SSM_INNER), F32)
    new_spec = pl.BlockSpec((1, nch, SSM_STATE, SSM_INNER), lambda b, n: (b, n, 0, 0))
    xv = xbc.reshape(bsz, seq, SSM_CONV_CH)
    dv = dt_raw.reshape(bsz, seq, DT_PAD)
    hb = ts // SSD_HALO
    n_hb = seq // SSD_HALO
    first = SSD_HALO - SSM_CONV // 2
    phases = tuple(sorted({(first + k) % SUBLANES for k in range(SSM_CONV)}))
    dtb = jnp.zeros((1, DT_PAD), F32).at[0, :2 * SSM_HEADS].set(dt_bias.reshape(-1))
    return pl.pallas_call(
        functools.partial(_ssd_pre_kernel, ts=ts, phases=phases),
        grid=(bsz, seq // ts),
        in_specs=[pl.BlockSpec((1, ts, SSM_CONV_CH), lambda b, n: (b, n, 0)),
                  pl.BlockSpec((1, SSD_HALO, SSM_CONV_CH),
                               lambda b, n: (b, jnp.maximum(n * hb - 1, 0), 0)),
                  pl.BlockSpec((1, SSD_HALO, SSM_CONV_CH),
                               lambda b, n: (b, jnp.minimum((n + 1) * hb, n_hb - 1), 0)),
                  pl.BlockSpec((1, ts, DT_PAD), lambda b, n: (b, n, 0)),
                  _const_spec(conv_w.shape), _const_spec((1, SSM_CONV_CH)),
                  _const_spec((1, DT_PAD)), _const_spec(a_row.shape), _const_spec(tri.shape),
                  _const_spec(expand.shape)],
        out_specs=[pl.BlockSpec((1, ts, SSM_CONV_CH), lambda b, n: (b, n, 0)),
                   pl.BlockSpec((1, ts, DT_PAD), lambda b, n: (b, n, 0)),
                   pl.BlockSpec((1, ts, DT_PAD), lambda b, n: (b, n, 0)),
                   new_spec, new_spec,
                   pl.BlockSpec((1, nch, 2, SSM_INNER), lambda b, n: (b, n, 0, 0))],
        out_shape=[jax.ShapeDtypeStruct((bsz, seq, SSM_CONV_CH), F32),
                   jax.ShapeDtypeStruct((bsz, seq, DT_PAD), F32),
                   jax.ShapeDtypeStruct((bsz, seq, DT_PAD), F32),
                   new, new, jax.ShapeDtypeStruct((bsz, nc, 2, SSM_INNER), F32)],
        scratch_shapes=[pltpu.VMEM((ts + 2 * SSD_HALO, SSM_CONV_CH), F32),
                        pltpu.VMEM((len(phases), ts + SSD_HALO, SSM_CONV_CH), F32)],
        compiler_params=_cparams(("parallel", "parallel")),
        name="ssd_pre",
    )(xv, xv, xv, dv, conv_w, conv_b[None], dtb, a_row, tri, expand)


def _ssd_consts(a_log):
    a_row = jnp.zeros((1, DT_PAD), F32).at[0, :2 * SSM_HEADS].set(-jnp.exp(a_log.reshape(-1)))
    lower = np.tril(np.ones((SSM_CHUNK, SSM_CHUNK), np.float32))
    tri = jnp.asarray(np.concatenate([lower, lower.T], axis=0), BF16)
    expand = np.zeros((2, DT_PAD, SSM_INNER), np.float32)
    for d in range(2):
        for h in range(SSM_HEADS):
            expand[d, d * SSM_HEADS + h, h * SSM_HEAD_DIM:(h + 1) * SSM_HEAD_DIM] = 1.0
    return a_row, tri, jnp.asarray(expand, BF16)


def _cumsum_both(dta, tri_ref):
    hi = dta.astype(BF16)
    r1 = dta - hi.astype(F32)
    mid = r1.astype(BF16)
    lo = (r1 - mid.astype(F32)).astype(BF16)
    both = (jnp.dot(tri_ref[...], hi, preferred_element_type=F32)
            + jnp.dot(tri_ref[...], mid, preferred_element_type=F32)
            + jnp.dot(tri_ref[...], lo, preferred_element_type=F32))
    lane = lax.broadcasted_iota(jnp.int32, dta.shape, 1) % DT_PAD
    return jnp.where(lane < SSM_HEADS, both[:SSM_CHUNK], both[SSM_CHUNK:])


def _expand(v, ex_ref, d):
    return _split_dot(v, ex_ref[d])


def _ssd_scan_kernel(nf_ref, nb_ref, cf_ref, cb_ref, pf_ref, pb_ref, sf_ref, sb_ref, *, nch):
    @pl.when(pl.program_id(1) == 0)
    def _():
        sf_ref[...] = jnp.zeros_like(sf_ref)
        sb_ref[...] = jnp.zeros_like(sb_ref)

    def scan(d, order, new_ref, carry_ref, st_ref, out_ref):
        st = st_ref[...]
        for k in order:
            out_ref[0, k] = st.astype(out_ref.dtype)
            st = st * carry_ref[0, k, d:d + 1, :] + new_ref[0, k]
        st_ref[...] = st

    scan(0, range(nch), nf_ref, cf_ref, sf_ref, pf_ref)
    scan(1, range(nch - 1, -1, -1), nb_ref, cb_ref, sb_ref, pb_ref)


SSD_SCAN_CHUNKS = 16


def _ssd_states(new_f, new_b, carry, bsz, seq):
    nc = seq // SSM_CHUNK
    nch = min(SSD_SCAN_CHUNKS, nc)
    steps = nc // nch
    fwd = lambda b, c: (b, c, 0, 0)
    bwd = lambda b, c: (b, steps - 1 - c, 0, 0)
    st = jax.ShapeDtypeStruct((bsz, nc, SSM_STATE, SSM_INNER), BF16)
    blk = (1, nch, SSM_STATE, SSM_INNER)
    return pl.pallas_call(
        functools.partial(_ssd_scan_kernel, nch=nch),
        grid=(bsz, steps),
        in_specs=[pl.BlockSpec(blk, fwd), pl.BlockSpec(blk, bwd),
                  pl.BlockSpec((1, nch, 2, SSM_INNER), fwd),
                  pl.BlockSpec((1, nch, 2, SSM_INNER), bwd)],
        out_specs=[pl.BlockSpec(blk, fwd), pl.BlockSpec(blk, bwd)],
        out_shape=[st, st],
        scratch_shapes=[pltpu.VMEM((SSM_STATE, SSM_INNER), F32),
                        pltpu.VMEM((SSM_STATE, SSM_INNER), F32)],
        compiler_params=_cparams(("parallel", "arbitrary")),
        name="ssd_scan",
    )(new_f, new_b, carry, carry)


def _ssd_out_kernel(x_ref, dt_ref, acs_ref, z_ref, pf_ref, pb_ref, ex_ref, dskip_ref, ng_ref,
                    o_ref, *, nck):
    for k in range(nck):
        _ssd_out_chunk(k, x_ref, dt_ref, acs_ref, z_ref, pf_ref, pb_ref, ex_ref, dskip_ref,
                       ng_ref, o_ref)


def _ssd_out_chunk(k, x_ref, dt_ref, acs_ref, z_ref, pf_ref, pb_ref, ex_ref, dskip_ref, ng_ref,
                   o_ref):
    rows = slice(k * SSM_CHUNK, (k + 1) * SSM_CHUNK)
    xs = x_ref[0, rows, :SSM_INNER]
    bmat = x_ref[0, rows, SSM_INNER:SSM_INNER + SSD_BC].astype(BF16)
    cmat = x_ref[0, rows, SSM_INNER + SSD_BC:].astype(BF16)
    dtv = dt_ref[0, rows, :]
    acs = acs_ref[0, rows, :]
    acs_t = acs.T
    dt_t = dtv.T
    eacs = jnp.exp(acs)
    row = lax.broadcasted_iota(jnp.int32, (SSM_CHUNK, SSM_CHUNK), 0)
    col = lax.broadcasted_iota(jnp.int32, (SSM_CHUNK, SSM_CHUNK), 1)
    past, now = col < row, col == row
    low = lax.broadcasted_iota(jnp.int32, (SSM_CHUNK, LANES), 1) < SSM_HEAD_DIM
    glane = lax.broadcasted_iota(jnp.int32, (1, SSD_BC), 1) // SSM_STATE
    scores = [lax.dot_general(cmat * (glane == g).astype(BF16), bmat,
                              (((1,), (1,)), ((), ())), preferred_element_type=F32)
              for g in range(SSM_GROUPS)]
    xs_b = xs.astype(BF16)
    keep = [low.astype(BF16), (~low).astype(BF16)]
    diag = []
    for pair in range(SSM_HEADS // 2):
        xp = xs_b[:, pair * LANES:(pair + 1) * LANES]
        mats = []
        for half in range(2):
            f = 2 * pair + half
            b = SSM_HEADS + f
            seg = jnp.where(past | now, acs[:, f:f + 1] - acs_t[f:f + 1, :],
                            acs[:, b:b + 1] - acs_t[b:b + 1, :])
            dts = (jnp.where(past, dt_t[f:f + 1, :], dt_t[b:b + 1, :])
                   + jnp.where(now, dt_t[f:f + 1, :], 0.0))
            mats.append((scores[f // HEADS_PER_GROUP] * jnp.exp(seg) * dts).astype(BF16))
        diag.append(jnp.dot(jnp.concatenate(mats, axis=1),
                            jnp.concatenate([xp * keep[0], xp * keep[1]], axis=0),
                            preferred_element_type=F32))
    y = dskip_ref[...] * xs + jnp.concatenate(diag, axis=1)
    hgroup = lax.broadcasted_iota(jnp.int32, (1, SSM_INNER), 1) // GROUP_LANES
    for d, p_ref in enumerate((pf_ref, pb_ref)):
        prev = p_ref[0, k]
        stacked = jnp.concatenate([prev * (hgroup == g).astype(BF16) for g in range(SSM_GROUPS)],
                                  axis=0)
        off = jnp.dot(cmat, stacked, preferred_element_type=F32)
        y = y + off * _expand(eacs, ex_ref, d)
    z = z_ref[0, rows, :]
    y = y * (z * jax.nn.sigmoid(z))
    o_ref[0, rows, :] = _rms(y, ng_ref[...]).astype(o_ref.dtype)


SSD_OUT_CHUNKS = 4


def _ssd_out(xact, dtv, acs, z, prev_f, prev_b, expand, d_skip, norm_g, bsz, seq):
    nc = seq // SSM_CHUNK
    nck = min(SSD_OUT_CHUNKS, nc)
    dsk = jnp.repeat(d_skip, SSM_HEAD_DIM)[None]
    chunk = lambda n: pl.BlockSpec((1, nck * SSM_CHUNK, n), lambda b, c: (b, c, 0))
    state = pl.BlockSpec((1, nck, SSM_STATE, SSM_INNER), lambda b, c: (b, c, 0, 0))
    out = pl.pallas_call(
        functools.partial(_ssd_out_kernel, nck=nck),
        grid=(bsz, nc // nck),
        in_specs=[chunk(SSM_CONV_CH), chunk(DT_PAD), chunk(DT_PAD), chunk(SSM_INNER), state, state,
                  _const_spec(expand.shape), _const_spec(dsk.shape), _const_spec((1, SSM_INNER))],
        out_specs=chunk(SSM_INNER),
        out_shape=jax.ShapeDtypeStruct((bsz, seq, SSM_INNER), BF16),
        compiler_params=_cparams(("parallel", "parallel")),
        name="ssd_out",
    )(xact, dtv, acs, z.reshape(bsz, seq, SSM_INNER), prev_f, prev_b, expand, dsk, norm_g[None])
    return out.reshape(bsz * seq, SSM_INNER)


def _ssd(z, xbc, dt_raw, conv_w, conv_b, a_log, dt_bias, d_skip, norm_g, bsz, seq):
    consts = _ssd_consts(a_log)
    xact, dtv, acs, new_f, new_b, carry = _ssd_pre(xbc, dt_raw, conv_w, conv_b, dt_bias, consts,
                                                   bsz, seq)
    prev_f, prev_b = _ssd_states(new_f, new_b, carry, bsz, seq)
    return _ssd_out(xact, dtv, acs, z, prev_f, prev_b, consts[2], d_skip, norm_g, bsz, seq)


FNET_COLS = 4096


def _dft_cos_sin(n):
    ang = 2.0 * np.pi * np.outer(np.arange(n), np.arange(n)) / n
    return np.cos(ang), np.sin(ang)


def _fnet_consts(seq):
    c = FNET_GROUP_DIM
    n1 = seq // LANES
    c1, s1 = _dft_cos_sin(n1)
    stage1 = np.concatenate([c1, -s1], axis=0)
    ang = 2.0 * np.pi * np.outer(np.arange(n1), np.arange(LANES)) / seq
    twr = np.repeat(np.cos(ang), c, axis=1)
    twi = np.repeat(-np.sin(ang), c, axis=1)
    cc, sc = _dft_cos_sin(c)
    chan = np.block([[cc, -sc], [sc, cc]])
    c2, s2 = _dft_cos_sin(LANES)
    return (jnp.asarray(stage1, BF16), jnp.asarray(twr, F32), jnp.asarray(twi, F32),
            jnp.asarray(chan, BF16), jnp.asarray(c2, BF16), jnp.asarray(s2, BF16))


def _fnet1_kernel(x_ref, f_ref, twr_ref, twi_ref, o_ref, *, n1):
    c = FNET_GROUP_DIM
    a = jnp.dot(f_ref[...], x_ref[0].astype(BF16), preferred_element_type=F32)
    ar, ai = a[:n1], a[n1:]
    twr, twi = twr_ref[...], twi_ref[...]
    re = (ar * twr - ai * twi).astype(o_ref.dtype)
    im = (ar * twi + ai * twr).astype(o_ref.dtype)
    for j in range(re.shape[1] // c):
        o_ref[0, :, (2 * j) * c:(2 * j + 1) * c] = re[:, j * c:(j + 1) * c]
        o_ref[0, :, (2 * j + 1) * c:(2 * j + 2) * c] = im[:, j * c:(j + 1) * c]


def _fnet2_kernel(a_ref, chan_ref, c2_ref, s2_ref, o_ref, g_ref, scr_ref, *, n1, scale):
    c = FNET_GROUP_DIM
    pitch = scr_ref.shape[0] // LANES
    per = min(8, n1)
    for i in range(n1 // per):
        blk = a_ref[0, i * per:(i + 1) * per].reshape(per * LANES, 2 * c)
        g = jnp.dot(blk, chan_ref[...], preferred_element_type=F32).astype(BF16)
        g_ref[i * per:(i + 1) * per] = g.reshape(per, LANES, 2 * c)

    def body(k1, carry):
        g = g_ref[k1]
        y = (jnp.dot(c2_ref[...], g[:, :c], preferred_element_type=F32)
             + jnp.dot(s2_ref[...], g[:, c:], preferred_element_type=F32))
        scr_ref[pl.ds(k1, LANES, stride=pitch), :] = y * scale
        return carry

    lax.fori_loop(0, n1, body, 0, unroll=4)

    def compact(k2, carry):
        src = pl.multiple_of(k2 * pitch, SUBLANES)
        dst = pl.multiple_of(k2 * n1, n1)
        o_ref[0, pl.ds(dst, n1), :] = scr_ref[pl.ds(src, n1), :].astype(o_ref.dtype)
        return carry

    lax.fori_loop(0, LANES, compact, 0, unroll=8)


def _fourier(fn, bsz, seq):
    c = FNET_GROUP_DIM
    assert c == LANES and seq % LANES == 0
    n1 = seq // LANES
    stage1, twr, twi, chan, c2, s2 = _fnet_consts(seq)
    ncols = LANES * c
    nb = min(FNET_COLS, ncols)
    x2 = fn.reshape(bsz * FNET_GROUPS, n1, ncols)
    a = pl.pallas_call(
        functools.partial(_fnet1_kernel, n1=n1),
        grid=(ncols // nb, bsz * FNET_GROUPS),
        in_specs=[pl.BlockSpec((1, n1, nb), lambda j, i: (i, 0, j)),
                  _const_spec(stage1.shape),
                  pl.BlockSpec((n1, nb), lambda j, i: (0, j)),
                  pl.BlockSpec((n1, nb), lambda j, i: (0, j))],
        out_specs=pl.BlockSpec((1, n1, 2 * nb), lambda j, i: (i, 0, j)),
        out_shape=jax.ShapeDtypeStruct((bsz * FNET_GROUPS, n1, 2 * ncols), BF16),
        compiler_params=_cparams(("parallel", "parallel")),
        name="fnet1",
    )(x2, stage1, twr, twi)
    a4 = a.reshape(bsz * FNET_GROUPS, n1, LANES, 2 * c)
    out = pl.pallas_call(
        functools.partial(_fnet2_kernel, n1=n1, scale=1.0 / math.sqrt(seq * c)),
        grid=(bsz, FNET_GROUPS),
        in_specs=[pl.BlockSpec((1, n1, LANES, 2 * c), lambda b, g: (b * FNET_GROUPS + g, 0, 0, 0)),
                  _const_spec(chan.shape), _const_spec(c2.shape), _const_spec(s2.shape)],
        out_specs=pl.BlockSpec((1, seq, c), lambda b, g: (b, 0, g)),
        out_shape=jax.ShapeDtypeStruct((bsz, seq, FNET_WIDTH), BF16),
        scratch_shapes=[pltpu.VMEM((n1, LANES, 2 * c), BF16),
                        pltpu.VMEM((LANES * (n1 + SUBLANES), c), F32)],
        compiler_params=_cparams(("parallel", "parallel")),
        name="fnet2",
    )(a4, chan, c2, s2)
    return out.reshape(bsz * seq, FNET_WIDTH)


def kernel(x, p, rel_bias, norm_mix, w_in, conv_dw, conv_dw_b, conv_ln_g, conv_ln_b, conv_out,
           ssm_conv_w, ssm_conv_b, ssm_a_log, ssm_dt_bias, ssm_d, ssm_norm, ssm_out,
           attn_out, fnet_out, w_gate, b_gate, w_out, norm_ffn, ffn_w1, ffn_w3, ffn_w2,
           moe_router, moe_w1, moe_w3, moe_w2, ple_gate, ple_proj, final_norm):
    bsz, seq, d = x.shape
    depth = w_in.shape[0]
    t = bsz * seq
    h = x.reshape(t, d)
    bias_tables = [_att_bias_tables(rel_bias, g, dil) for g, (_, dil) in enumerate(ATT_PATTERNS)]
    for l in range(depth):
        *qkv, conv_u, z, xbc, fn, dt = _inproj(h, norm_mix[l][None], _reorder_w_in(w_in[l]), bsz, seq)
        att = [_attention_group(qkv[g], bias_tables[g], dil, bsz, seq, f"attn{g}")
               for g, (_, dil) in enumerate(ATT_PATTERNS)]
        cnf = _conformer(conv_u, conv_dw[l], conv_dw_b[l], conv_ln_g[l], conv_ln_b[l], bsz, seq)
        ssd = _ssd(z, xbc, dt, ssm_conv_w[l], ssm_conv_b[l], ssm_a_log[l], ssm_dt_bias[l],
                   ssm_d[l], ssm_norm[l], bsz, seq)
        fnt = _fourier(fn, bsz, seq)
        wbr = jnp.stack([attn_out[l], conv_out[l], ssm_out[l], fnet_out[l]]).astype(BF16)
        h = _mix(h, norm_mix[l][None], att, (cnf, ssd, fnt), wbr, w_gate[l].astype(BF16),
                 b_gate[l][:, None, :], w_out[l].astype(BF16))
        pl_in = p[l].reshape(t, -1)
        wpg, wpp = ple_gate[l].astype(BF16), ple_proj[l].astype(BF16)
        i = l // 2
        if l % 2 == 0:
            h = _ffn(h, norm_ffn[l][None], ffn_w1[i].astype(BF16), ffn_w3[i].astype(BF16),
                     ffn_w2[i].astype(BF16), pl_in, wpg, wpp)
        else:
            h = _moe(h, norm_ffn[l][None], moe_router[i], moe_w1[i].astype(BF16),
                     moe_w3[i].astype(BF16), moe_w2[i].astype(BF16), pl_in, wpg, wpp)
    return _final_norm(h, final_norm[None]).reshape(bsz, seq, d)
```

```python
import functools
import math

import numpy as np
import jax
import jax.numpy as jnp
from jax import lax
from jax.experimental import pallas as pl
from jax.experimental.pallas import tpu as pltpu

F32 = jnp.float32
BF16 = jnp.bfloat16

EPS = 1e-6
N_BRANCHES = 4
CONV_CH = 512
CONV_K = 31
SSM_HEADS = 8
SSM_HEAD_DIM = 64
SSM_INNER = SSM_HEADS * SSM_HEAD_DIM
SSM_GROUPS = 2
SSM_STATE = 64
SSM_CONV = 5
SSM_CONV_CH = SSM_INNER + 2 * SSM_GROUPS * SSM_STATE
SSM_CHUNK = 128
ATT_PATTERNS = ((128, 1), (512, 4), (2048, 16))
ATT_GROUPS = len(ATT_PATTERNS)
ATT_HEADS = 8
ATT_HEAD_DIM = 64
ATT_WIDTH = ATT_HEADS * ATT_HEAD_DIM
REL_BUCKETS = 32
REL_MAX_DIST = 1024
FNET_GROUPS = 4
FNET_GROUP_DIM = 128
FNET_WIDTH = FNET_GROUPS * FNET_GROUP_DIM
N_EXPERTS = 8
TOP_K = 2

ATT_IN_COLS = 3 * ATT_GROUPS * ATT_WIDTH
CONV_IN_COLS = 2 * CONV_CH
SSM_IN_COLS = SSM_INNER + SSM_CONV_CH + 2 * SSM_HEADS
OFF_CONV = ATT_IN_COLS
OFF_SSM = OFF_CONV + CONV_IN_COLS
OFF_FNET = OFF_SSM + SSM_IN_COLS

LANES = 128
DT_PAD = LANES
VMEM_LIMIT = 56 * 1024 * 1024


def _cparams(sem):
    return pltpu.CompilerParams(dimension_semantics=sem, vmem_limit_bytes=VMEM_LIMIT)


def _const_spec(shape):
    nd = len(shape)
    return pl.BlockSpec(shape, lambda *_: (0,) * nd, pipeline_mode=pl.Buffered(1))


def _rms(x, g):
    return x * lax.rsqrt(jnp.mean(x * x, axis=-1, keepdims=True) + EPS) * g


_SEC_QKV = (0, ATT_IN_COLS)
_SEC_CONV = (_SEC_QKV[0] + _SEC_QKV[1], CONV_IN_COLS)
_SEC_Z = (_SEC_CONV[0] + _SEC_CONV[1], SSM_INNER)
_SEC_XBC = (_SEC_Z[0] + _SEC_Z[1], SSM_CONV_CH)
_SEC_FNET = (_SEC_XBC[0] + _SEC_XBC[1], FNET_WIDTH)
_SEC_DT = (_SEC_FNET[0] + _SEC_FNET[1], DT_PAD)
_IN_COLS_PAD = _SEC_DT[0] + _SEC_DT[1]
_MM_CHUNK = 512


ATT_TILE = 256
QKV_COLS = 3 * ATT_WIDTH
INPROJ_ROWS = 2 * ATT_TILE


def _reorder_w_in(w):
    d = w.shape[0]
    qkv = w[:, :OFF_CONV].reshape(d, 3, ATT_GROUPS, ATT_WIDTH).transpose(0, 2, 1, 3)
    ssm = w[:, OFF_SSM:OFF_FNET]
    dt = ssm[:, SSM_INNER + SSM_CONV_CH:]
    parts = [qkv.reshape(d, OFF_CONV), w[:, OFF_CONV:OFF_SSM], ssm[:, :SSM_INNER],
             ssm[:, SSM_INNER:SSM_INNER + SSM_CONV_CH], w[:, OFF_FNET:],
             dt, jnp.zeros((d, DT_PAD - dt.shape[1]), w.dtype)]
    return jnp.concatenate(parts, axis=1).astype(BF16)


def _deinterleave_matrix(dil):
    s = np.arange(ATT_TILE)
    m = np.zeros((ATT_TILE, ATT_TILE), np.float32)
    m[(s % dil) * (ATT_TILE // dil) + s // dil, s] = 1.0
    return m


def _inproj_kernel(h_ref, g_ref, w_ref, perm_ref, q0_ref, q1_ref, q2_ref, conv_ref, z_ref,
                   xbc_ref, fn_ref, dt_ref):
    xn = _rms(h_ref[...], g_ref[...]).astype(BF16)

    def section(x, sec, store):
        start, width = sec
        for c in range(0, width, _MM_CHUNK):
            cw = min(_MM_CHUNK, width - c)
            store(c, cw, jnp.dot(x, w_ref[:, start + c:start + c + cw],
                                 preferred_element_type=F32))

    def to(ref):
        def store(c, cw, val):
            ref[:, c:c + cw] = val.astype(ref.dtype)
        return store

    def to_fnet(c, cw, val):
        for g in range(cw // FNET_GROUP_DIM):
            fn_ref[0, c // FNET_GROUP_DIM + g] = val[:, g * FNET_GROUP_DIM:(g + 1) * FNET_GROUP_DIM]

    for g, q_ref in enumerate((q0_ref, q1_ref, q2_ref)):
        x = xn
        if ATT_PATTERNS[g][1] > 1:
            x = jnp.concatenate(
                [jnp.dot(perm_ref[g], xn[i * ATT_TILE:(i + 1) * ATT_TILE],
                         preferred_element_type=F32) for i in range(xn.shape[0] // ATT_TILE)],
                axis=0).astype(BF16)
        section(x, (g * QKV_COLS, QKV_COLS), to(q_ref))
    section(xn, _SEC_CONV, to(conv_ref))
    section(xn, _SEC_Z, to(z_ref))
    section(xn, _SEC_XBC, to(xbc_ref))
    section(xn, _SEC_FNET, to_fnet)
    section(xn, _SEC_DT, to(dt_ref))


def _inproj(h, g, w, bsz, seq):
    t, d = h.shape
    tm = INPROJ_ROWS
    spt = seq // tm
    perm = jnp.asarray(np.stack([_deinterleave_matrix(dil) for _, dil in ATT_PATTERNS]), BF16)
    row = lambda n: pl.BlockSpec((tm, n), lambda i: (i, 0))
    qkv = jax.ShapeDtypeStruct((t, QKV_COLS), BF16)
    return pl.pallas_call(
        _inproj_kernel,
        grid=(t // tm,),
        in_specs=[row(d), _const_spec((1, d)), _const_spec(w.shape), _const_spec(perm.shape)],
        out_specs=[row(QKV_COLS), row(QKV_COLS), row(QKV_COLS), row(CONV_IN_COLS),
                   row(SSM_INNER), row(SSM_CONV_CH),
                   pl.BlockSpec((1, FNET_GROUPS, tm, FNET_GROUP_DIM),
                                lambda i: (i // spt, 0, i % spt, 0)),
                   row(DT_PAD)],
        out_shape=[qkv, qkv, qkv,
                   jax.ShapeDtypeStruct((t, CONV_IN_COLS), F32),
                   jax.ShapeDtypeStruct((t, SSM_INNER), F32),
                   jax.ShapeDtypeStruct((t, SSM_CONV_CH), F32),
                   jax.ShapeDtypeStruct((bsz, FNET_GROUPS, seq, FNET_GROUP_DIM), F32),
                   jax.ShapeDtypeStruct((t, DT_PAD), F32)],
        compiler_params=_cparams(("parallel",)),
        name="inproj",
    )(h, g, w, perm)


def _split_dot(v, m):
    hi = v.astype(BF16)
    lo = (v - hi.astype(F32)).astype(BF16)
    return (jnp.dot(hi, m, preferred_element_type=F32) + jnp.dot(lo, m, preferred_element_type=F32))


def _interleave(pt, v):
    n = v.shape[1]
    if v.dtype != BF16:
        hi = v.astype(BF16)
        v = jnp.concatenate([hi, (v - hi.astype(F32)).astype(BF16)], axis=1)
    tiles = []
    for i in range(v.shape[0] // ATT_TILE):
        r = jnp.dot(pt, v[i * ATT_TILE:(i + 1) * ATT_TILE], preferred_element_type=F32)
        tiles.append(r if r.shape[1] == n else r[:, :n] + r[:, n:])
    return jnp.concatenate(tiles, axis=0)


def _mix_kernel(h_ref, g_ref, o0_ref, o1_ref, o2_ref, l0_ref, l1_ref, l2_ref, b1_ref, b2_ref,
                b3_ref, pt_ref, hx_ref, wbr_ref, wg_ref, cg_ref, wo_ref, o_ref):
    h = h_ref[...]
    xn = _rms(h, g_ref[...]).astype(BF16)
    outs, lses = [], []
    for g, (og_ref, lg_ref) in enumerate(zip((o0_ref, o1_ref, o2_ref), (l0_ref, l1_ref, l2_ref))):
        if ATT_PATTERNS[g][1] > 1:
            outs.append(_interleave(pt_ref[g], og_ref[...]))
            lses.append(_interleave(pt_ref[g], lg_ref[...]))
        else:
            outs.append(og_ref[...].astype(F32))
            lses.append(lg_ref[...])
    top = jnp.maximum(jnp.maximum(lses[0], lses[1]), lses[2])
    es = [jnp.exp(l - top) for l in lses]
    inv = 1.0 / (es[0] + es[1] + es[2])
    att = None
    for e, og in zip(es, outs):
        term = og * jnp.dot((e * inv).astype(BF16), hx_ref[...], preferred_element_type=F32)
        att = term if att is None else att + term
    acc = None
    for b, hid in enumerate((att.astype(BF16), b1_ref[...], b2_ref[...], b3_ref[...])):
        gate = jax.nn.sigmoid(jnp.dot(xn, wg_ref[b], preferred_element_type=F32) + cg_ref[b])
        br = jnp.dot(hid, wbr_ref[b], preferred_element_type=F32)
        acc = gate * br if acc is None else acc + gate * br
    o_ref[...] = h + jnp.dot(acc.astype(BF16), wo_ref[...], preferred_element_type=F32)


def _mix(h, g, att, others, wbr, wg, cg, wo):
    t, d = h.shape
    tm = 512
    row = lambda n: pl.BlockSpec((tm, n), lambda i: (i, 0))
    head_expand = np.zeros((LANES, ATT_WIDTH), np.float32)
    for hd in range(ATT_HEADS):
        head_expand[hd, hd * ATT_HEAD_DIM:(hd + 1) * ATT_HEAD_DIM] = 1.0
    head_expand = jnp.asarray(head_expand, BF16)
    unperm = jnp.asarray(np.stack([_deinterleave_matrix(dil).T for _, dil in ATT_PATTERNS]), BF16)
    outs = [o for o, _ in att]
    lses = [l for _, l in att]
    return pl.pallas_call(
        _mix_kernel,
        grid=(t // tm,),
        in_specs=[row(d), _const_spec((1, d))] + [row(a.shape[1]) for a in outs + lses + list(others)]
                 + [_const_spec(unperm.shape), _const_spec(head_expand.shape), _const_spec(wbr.shape),
                    _const_spec(wg.shape), _const_spec(cg.shape), _const_spec(wo.shape)],
        out_specs=row(d),
        out_shape=jax.ShapeDtypeStruct((t, d), F32),
        compiler_params=_cparams(("parallel",)),
        name="mix",
    )(h, g, *outs, *lses, *others, unperm, head_expand, wbr, wg, cg, wo)


def _ple(h2, p_ref, wpg_ref, wpp_ref):
    gate = jax.nn.sigmoid(jnp.dot(h2.astype(BF16), wpg_ref[...], preferred_element_type=F32))
    pe = jnp.dot(p_ref[...].astype(BF16), wpp_ref[...], preferred_element_type=F32)
    return h2 + gate * pe


def _swiglu_partial(xn, w1, w3, w2, scale=None):
    a = jnp.dot(xn, w1, preferred_element_type=F32)
    b = jnp.dot(xn, w3, preferred_element_type=F32)
    hid = a * jax.nn.sigmoid(a) * b
    if scale is not None:
        hid = hid * scale
    return jnp.dot(hid.astype(BF16), w2, preferred_element_type=F32)


def _ffn_kernel(h_ref, g_ref, w1_ref, w3_ref, w2_ref, p_ref, wpg_ref, wpp_ref, o_ref, xn_ref):
    j = pl.program_id(1)

    @pl.when(j == 0)
    def _():
        h = h_ref[...]
        xn_ref[...] = _rms(h, g_ref[...]).astype(BF16)
        o_ref[...] = h

    o_ref[...] += _swiglu_partial(xn_ref[...], w1_ref[...], w3_ref[...], w2_ref[...])

    @pl.when(j == pl.num_programs(1) - 1)
    def _():
        o_ref[...] = _ple(o_ref[...], p_ref, wpg_ref, wpp_ref)


def _ffn(h, g, w1, w3, w2, p, wpg, wpp):
    t, d = h.shape
    f = w1.shape[1]
    tm, tf = 512, 1408
    row = lambda n: pl.BlockSpec((tm, n), lambda i, j: (i, 0))
    return pl.pallas_call(
        _ffn_kernel,
        grid=(t // tm, f // tf),
        in_specs=[row(d), _const_spec((1, d)),
                  pl.BlockSpec((d, tf), lambda i, j: (0, j)),
                  pl.BlockSpec((d, tf), lambda i, j: (0, j)),
                  pl.BlockSpec((tf, d), lambda i, j: (j, 0)),
                  row(p.shape[1]), _const_spec(wpg.shape), _const_spec(wpp.shape)],
        out_specs=row(d),
        out_shape=jax.ShapeDtypeStruct((t, d), F32),
        scratch_shapes=[pltpu.VMEM((tm, d), BF16)],
        compiler_params=_cparams(("parallel", "arbitrary")),
        name="ffn",
    )(h, g, w1, w3, w2, p, wpg, wpp)


MOE_ROWS = 128


def _moe_route(logits):
    ne, tm = logits.shape
    eidx = lax.broadcasted_iota(jnp.int32, logits.shape, 0)
    m1 = jnp.max(logits, axis=0, keepdims=True)
    i1 = jnp.min(jnp.where(logits == m1, eidx, ne), axis=0, keepdims=True)
    rest = jnp.where(eidx == i1, -jnp.inf, logits)
    m2 = jnp.max(rest, axis=0, keepdims=True)
    i2 = jnp.min(jnp.where(rest == m2, eidx, ne), axis=0, keepdims=True)
    e2 = jnp.exp(m2 - m1)
    den = 1.0 + e2
    combine = jnp.where(eidx == i1, 1.0 / den, 0.0) + jnp.where(eidx == i2, e2 / den, 0.0)
    routed = jnp.where((eidx == i1) | (eidx == i2), 1.0, 0.0)
    r = lax.broadcasted_iota(jnp.int32, (LANES, LANES), 0)
    c = lax.broadcasted_iota(jnp.int32, (LANES, LANES), 1)
    before = jnp.where(r < c, 1.0, 0.0).astype(BF16)
    counts = jnp.zeros((ne, 1), F32)
    slots = []
    for k in range(tm // LANES):
        blk = routed[:, k * LANES:(k + 1) * LANES]
        slots.append(jnp.dot(blk.astype(BF16), before, preferred_element_type=F32) + counts)
        counts = counts + jnp.sum(blk, axis=1, keepdims=True)
    slot = jnp.where(routed > 0.0, jnp.concatenate(slots, axis=1), -1.0).astype(jnp.int32)
    return combine, slot, counts


def _moe_kernel(h_ref, g_ref, rt_ref, w1_ref, w3_ref, w2_ref, p_ref, wpg_ref, wpp_ref, o_ref,
                xn_ref, comb_ref, slot_ref, cnt_ref, xe_ref, ye_ref):
    e = pl.program_id(1)
    j = pl.program_id(2)
    ne = pl.num_programs(1)
    tm = xn_ref.shape[0]

    @pl.when((e == 0) & (j == 0))
    def _():
        h = h_ref[...]
        xn = _rms(h, g_ref[...])
        xn_ref[...] = xn.astype(BF16)
        logits = lax.dot_general(rt_ref[...], xn, (((1,), (1,)), ((), ())),
                                 preferred_element_type=F32, precision=HI)
        combine, slot, counts = _moe_route(logits)
        comb_ref[...] = combine
        slot_ref[...] = slot
        for k in range(comb_ref.shape[0]):
            cnt_ref[k] = jnp.sum(counts[k:k + 1, :]).astype(jnp.int32)
        o_ref[...] = h

    n_blocks = (cnt_ref[e] + MOE_ROWS - 1) // MOE_ROWS

    def for_row_blocks(body):
        def pair(i, carry):
            body(pl.multiple_of(i * 2 * MOE_ROWS, 2 * MOE_ROWS), 2 * MOE_ROWS)
            return carry
        lax.fori_loop(0, n_blocks // 2, pair, 0)

        @pl.when(n_blocks % 2 == 1)
        def _():
            body(pl.multiple_of((n_blocks - 1) * MOE_ROWS, MOE_ROWS), MOE_ROWS)

    def one_hot(r0, rows):
        return slot_ref[pl.ds(e, 1), :] == lax.broadcasted_iota(jnp.int32, (rows, tm), 0) + r0

    def swiglu(x):
        return _swiglu_partial(x, w1_ref[0], w3_ref[0], w2_ref[0])

    last_j = pl.num_programs(2) - 1

    @pl.when(j == 0)
    def _():
        def gather_first(r0, rows):
            sel = jnp.where(one_hot(r0, rows), 1.0, 0.0).astype(BF16)
            x = jnp.dot(sel, xn_ref[...], preferred_element_type=F32).astype(BF16)
            xe_ref[pl.ds(r0, rows), :] = x
            ye_ref[pl.ds(r0, rows), :] = swiglu(x)
        for_row_blocks(gather_first)

    @pl.when((j > 0) & (j < last_j))
    def _():
        def middle(r0, rows):
            ye_ref[pl.ds(r0, rows), :] += swiglu(xe_ref[pl.ds(r0, rows), :])
        for_row_blocks(middle)

    @pl.when((j > 0) & (j == last_j))
    def _():
        def last_scatter(r0, rows):
            y = ye_ref[pl.ds(r0, rows), :] + swiglu(xe_ref[pl.ds(r0, rows), :])
            hot = one_hot(r0, rows)
            weight = jnp.sum(jnp.where(hot, comb_ref[pl.ds(e, 1), :], 0.0), axis=1, keepdims=True)
            o_ref[...] += lax.dot_general(jnp.where(hot, 1.0, 0.0).astype(BF16),
                                          (y * weight).astype(BF16),
                                          (((0,), (0,)), ((), ())), preferred_element_type=F32)
        for_row_blocks(last_scatter)

    @pl.when((e == ne - 1) & (j == pl.num_programs(2) - 1))
    def _():
        o_ref[...] = _ple(o_ref[...], p_ref, wpg_ref, wpp_ref)


def _moe(h, g, router, w1, w3, w2, p, wpg, wpp):
    t, d = h.shape
    ne, _, f = w1.shape
    tm, tf = 1024, 1792
    assert f % tf == 0 and f // tf >= 2
    row = lambda n: pl.BlockSpec((tm, n), lambda i, e, j: (i, 0))
    return pl.pallas_call(
        _moe_kernel,
        grid=(t // tm, ne, f // tf),
        in_specs=[row(d), _const_spec((1, d)), _const_spec((ne, d)),
                  pl.BlockSpec((1, d, tf), lambda i, e, j: (e, 0, j)),
                  pl.BlockSpec((1, d, tf), lambda i, e, j: (e, 0, j)),
                  pl.BlockSpec((1, tf, d), lambda i, e, j: (e, j, 0)),
                  row(p.shape[1]), _const_spec(wpg.shape), _const_spec(wpp.shape)],
        out_specs=row(d),
        out_shape=jax.ShapeDtypeStruct((t, d), F32),
        scratch_shapes=[pltpu.VMEM((tm, d), BF16), pltpu.VMEM((ne, tm), F32),
                        pltpu.VMEM((ne, tm), jnp.int32), pltpu.SMEM((ne,), jnp.int32),
                        pltpu.VMEM((tm, d), BF16), pltpu.VMEM((tm, d), F32)],
        compiler_params=_cparams(("parallel", "arbitrary", "arbitrary")),
        name="moe",
    )(h, g, router.T, w1, w3, w2, p, wpg, wpp)


def _final_norm_kernel(h_ref, g_ref, o_ref):
    o_ref[...] = _rms(h_ref[...], g_ref[...])


def _final_norm(h, g):
    t, d = h.shape
    tm = 1024
    row = pl.BlockSpec((tm, d), lambda i: (i, 0))
    return pl.pallas_call(
        _final_norm_kernel, grid=(t // tm,),
        in_specs=[row, _const_spec((1, d))], out_specs=row,
        out_shape=jax.ShapeDtypeStruct((t, d), F32),
        compiler_params=_cparams(("parallel",)), name="final_norm",
    )(h, g)


ATT_HALF = 64
ATT_QB = 128
ATT_KB = ATT_QB + 2 * ATT_HALF
NEG = -1e30
assert all(w // (2 * d) == ATT_HALF for w, d in ATT_PATTERNS)


def _t5_bucket(rel):
    half = REL_BUCKETS // 2
    max_exact = half // 2
    n = np.abs(rel)
    large = max_exact + (np.log(np.maximum(n, 1) / max_exact) / math.log(REL_MAX_DIST / max_exact)
                         * (half - max_exact)).astype(np.int32)
    large = np.minimum(large, half - 1)
    return np.where(rel > 0, half, 0) + np.where(n < max_exact, n, large)


def _att_bias_tables(rel_bias, g, dil):
    i = np.arange(ATT_QB)[:, None]
    j = np.arange(ATT_KB)[None, :]
    rel = j - ATT_HALF - i
    band = np.abs(rel) <= ATT_HALF
    pick = np.eye(REL_BUCKETS, dtype=np.float32)[_t5_bucket(dil * rel)]
    heads = rel_bias[:, g * ATT_HEADS:(g + 1) * ATT_HEADS].astype(F32)
    bias = jnp.einsum('qkb,bh->hqk', pick, heads, precision=lax.Precision.HIGHEST)
    tables = []
    for v in range(4):
        ok = band
        if v & 1:
            ok = ok & (j >= ATT_HALF)
        if v & 2:
            ok = ok & (j < ATT_QB + ATT_HALF)
        tables.append(jnp.where(ok[None], bias, NEG))
    return jnp.stack(tables)


def _attn_kernel(q_ref, k_ref, v_ref, kp_ref, kn_ref, vp_ref, vn_ref, bias_ref, o_ref, lse_ref,
                 qbuf, kbuf, vbuf, obuf, lbuf, *, tq, n_blocks):
    flat = lambda ref: ref[0].reshape(-1, ref.shape[-1])
    qbuf[...] = flat(q_ref)
    kbuf[0:ATT_HALF] = flat(kp_ref)
    kbuf[ATT_HALF:ATT_HALF + tq] = flat(k_ref)
    kbuf[ATT_HALF + tq:] = flat(kn_ref)
    vbuf[0:ATT_HALF] = flat(vp_ref)
    vbuf[ATT_HALF:ATT_HALF + tq] = flat(v_ref)
    vbuf[ATT_HALF + tq:] = flat(vn_ref)
    nsb = tq // ATT_QB
    first = pl.program_id(2) * nsb
    lane = lax.broadcasted_iota(jnp.int32, (ATT_QB, LANES), 1)
    low = lane < ATT_HEAD_DIM
    lane_row = lax.broadcasted_iota(jnp.int32, (1, LANES), 1)
    keep = [(lane_row < ATT_HEAD_DIM).astype(BF16), (lane_row >= ATT_HEAD_DIM).astype(BF16)]
    ones = jnp.ones((ATT_KB, LANES), BF16)

    def block(sb, carry):
        r0 = pl.multiple_of(sb * ATT_QB, ATT_QB)
        gsb = first + sb
        variant = (gsb == 0).astype(jnp.int32) + 2 * (gsb == n_blocks - 1).astype(jnp.int32)
        q = qbuf[pl.ds(r0, ATT_QB), :] * (ATT_HEAD_DIM ** -0.5)
        lse_all = jnp.zeros((ATT_QB, LANES), F32)
        outs = []
        for pair in range(ATT_HEADS // 2):
            cols = slice(pair * LANES, (pair + 1) * LANES)
            qp = q[:, cols]
            kp = kbuf[pl.ds(r0, ATT_KB), cols]
            vp = jnp.concatenate([vbuf[pl.ds(r0, ATT_KB), cols], ones], axis=1)
            res = []
            for half in range(2):
                h = 2 * pair + half
                s = lax.dot_general(qp * keep[half], kp, (((1,), (1,)), ((), ())),
                                    preferred_element_type=F32)
                s = s + bias_ref[variant, h]
                m = jnp.max(s, axis=-1, keepdims=True)
                e = jnp.exp(s - m)
                pv = jnp.dot(e.astype(BF16), vp, preferred_element_type=F32)
                den = pv[:, LANES:]
                res.append(pv[:, :LANES] / den)
                lse_all = jnp.where(lane == h, m + jnp.log(den), lse_all)
            outs.append(jnp.where(low, res[0], res[1]))
        obuf[pl.ds(r0, ATT_QB), :] = jnp.concatenate(outs, axis=1).astype(obuf.dtype)
        lbuf[pl.ds(r0, ATT_QB), :] = lse_all
        return carry

    lax.fori_loop(0, nsb, block, 0, unroll=4)
    o_ref[0] = obuf[...].reshape(o_ref.shape[1:])
    lse_ref[0] = lbuf[...].reshape(lse_ref.shape[1:])


def _attention_group(qkv, bias_tables, dil, bsz, seq, name):
    sub_len = seq // dil
    assert sub_len % ATT_QB == 0 and seq % ATT_TILE == 0 and ATT_TILE % dil == 0
    rows = ATT_TILE // dil
    tq = min(512, sub_len)
    w = ATT_WIDTH
    hrows = min(rows, ATT_HALF)
    n_hb = sub_len // ATT_HALF

    def view(a, chunk):
        return a.reshape(bsz, (seq // ATT_TILE) * (rows // chunk), dil, chunk, a.shape[-1])

    def main(part):
        return pl.BlockSpec((1, tq // rows, None, rows, w), lambda b, r, n: (b, n, r, 0, part))

    def halo(part, nxt):
        if nxt:
            blk = lambda n: jnp.minimum((n + 1) * (tq // ATT_HALF), n_hb - 1)
        else:
            blk = lambda n: jnp.maximum(n * (tq // ATT_HALF) - 1, 0)
        return pl.BlockSpec((1, ATT_HALF // hrows, None, hrows, w),
                            lambda b, r, n: (b, blk(n), r, 0, part))

    mv, hv = view(qkv, rows), view(qkv, hrows)
    o, lse = pl.pallas_call(
        functools.partial(_attn_kernel, tq=tq, n_blocks=sub_len // ATT_QB),
        grid=(bsz, dil, sub_len // tq),
        in_specs=[main(0), main(1), main(2), halo(1, False), halo(1, True), halo(2, False),
                  halo(2, True), _const_spec(bias_tables.shape)],
        out_specs=[pl.BlockSpec((1, tq // rows, None, rows, w), lambda b, r, n: (b, n, r, 0, 0)),
                   pl.BlockSpec((1, tq // rows, None, rows, LANES), lambda b, r, n: (b, n, r, 0, 0))],
        out_shape=[jax.ShapeDtypeStruct((bsz, seq // ATT_TILE, dil, rows, w), BF16),
                   jax.ShapeDtypeStruct((bsz, seq // ATT_TILE, dil, rows, LANES), F32)],
        scratch_shapes=[pltpu.VMEM((tq, w), BF16),
                        pltpu.VMEM((tq + 2 * ATT_HALF, w), BF16),
                        pltpu.VMEM((tq + 2 * ATT_HALF, w), BF16),
                        pltpu.VMEM((tq, w), BF16), pltpu.VMEM((tq, LANES), F32)],
        compiler_params=_cparams(("parallel", "parallel", "parallel")),
        name=name,
    )(mv, mv, mv, hv, hv, hv, hv, bias_tables)
    return o.reshape(bsz * seq, w), lse.reshape(bsz * seq, LANES)


SUBLANES = 8
CONV_HALO = 16
CONV_ROWS = 64


def _conformer_kernel(u_ref, up_ref, un_ref, dw_ref, dwb_ref, lng_ref, lnb_ref, o_ref,
                      hp_ref, sh_ref, *, ts):
    n = pl.program_id(1)

    def glu(u):
        return u[:, :CONV_CH] * jax.nn.sigmoid(u[:, CONV_CH:])

    hp_ref[0:CONV_HALO] = jnp.where(n > 0, glu(up_ref[0]), 0.0)
    hp_ref[CONV_HALO:CONV_HALO + ts] = glu(u_ref[0])
    hp_ref[CONV_HALO + ts:] = jnp.where(n < pl.num_programs(1) - 1, glu(un_ref[0]), 0.0)
    span = ts + 2 * CONV_HALO - SUBLANES
    for b in range(SUBLANES):
        sh_ref[b] = hp_ref[pl.ds(b, span), :]
    first = CONV_HALO - CONV_K // 2

    def rows(c, carry):
        r0 = pl.multiple_of(c * CONV_ROWS, CONV_ROWS)
        acc = jnp.broadcast_to(dwb_ref[...], (CONV_ROWS, CONV_CH))
        for k in range(CONV_K):
            a, b = divmod(first + k, SUBLANES)
            acc = acc + dw_ref[pl.ds(k, 1), :] * sh_ref[b, pl.ds(r0 + a * SUBLANES, CONV_ROWS), :]
        mu = jnp.mean(acc, axis=-1, keepdims=True)
        cen = acc - mu
        var = jnp.mean(cen * cen, axis=-1, keepdims=True)
        y = cen * lax.rsqrt(var + EPS) * lng_ref[...] + lnb_ref[...]
        o_ref[0, pl.ds(r0, CONV_ROWS), :] = (y * jax.nn.sigmoid(y)).astype(o_ref.dtype)
        return carry

    lax.fori_loop(0, ts // CONV_ROWS, rows, 0, unroll=2)


def _conformer(u, dw, dw_b, ln_g, ln_b, bsz, seq):
    ts = min(512, seq)
    view = u.reshape(bsz, seq, 2 * CONV_CH)
    hb = ts // CONV_HALO
    n_hb = seq // CONV_HALO
    out = pl.pallas_call(
        functools.partial(_conformer_kernel, ts=ts),
        grid=(bsz, seq // ts),
        in_specs=[pl.BlockSpec((1, ts, 2 * CONV_CH), lambda b, n: (b, n, 0)),
                  pl.BlockSpec((1, CONV_HALO, 2 * CONV_CH),
                               lambda b, n: (b, jnp.maximum(n * hb - 1, 0), 0)),
                  pl.BlockSpec((1, CONV_HALO, 2 * CONV_CH),
                               lambda b, n: (b, jnp.minimum((n + 1) * hb, n_hb - 1), 0)),
                  _const_spec(dw.shape), _const_spec((1, CONV_CH)), _const_spec((1, CONV_CH)),
                  _const_spec((1, CONV_CH))],
        out_specs=pl.BlockSpec((1, ts, CONV_CH), lambda b, n: (b, n, 0)),
        out_shape=jax.ShapeDtypeStruct((bsz, seq, CONV_CH), BF16),
        scratch_shapes=[pltpu.VMEM((ts + 2 * CONV_HALO, CONV_CH), F32),
                        pltpu.VMEM((SUBLANES, ts + 2 * CONV_HALO - SUBLANES, CONV_CH), F32)],
        compiler_params=_cparams(("parallel", "parallel")),
        name="conformer",
    )(view, view, view, dw, dw_b[None], ln_g[None], ln_b[None])
    return out.reshape(bsz * seq, CONV_CH)


SSD_HALO = SUBLANES
SSD_ROWS = 64
SSD_BC = SSM_GROUPS * SSM_STATE
HEADS_PER_GROUP = SSM_HEADS // SSM_GROUPS
GROUP_LANES = HEADS_PER_GROUP * SSM_HEAD_DIM
HI = lax.Precision.HIGHEST
assert SSD_BC == LANES and SSM_CHUNK == LANES and 2 * SSM_HEADS <= LANES


def _ssd_pre_kernel(x_ref, xp_ref, xn_ref, dt_ref, cw_ref, cb_ref, dtb_ref, arow_ref, tri_ref,
                    ex_ref, xo_ref, dto_ref, acs_ref, nf_ref, nb_ref, cy_ref, hp_ref, sh_ref,
                    *, ts, phases):
    n = pl.program_id(1)
    hp_ref[0:SSD_HALO] = jnp.where(n > 0, xp_ref[0], 0.0)
    hp_ref[SSD_HALO:SSD_HALO + ts] = x_ref[0]
    hp_ref[SSD_HALO + ts:] = jnp.where(n < pl.num_programs(1) - 1, xn_ref[0], 0.0)
    for i, b in enumerate(phases):
        sh_ref[i] = hp_ref[pl.ds(b, ts + SSD_HALO), :]
    first = SSD_HALO - SSM_CONV // 2

    def rows(c, carry):
        r0 = pl.multiple_of(c * SSD_ROWS, SSD_ROWS)
        acc = jnp.broadcast_to(cb_ref[...], (SSD_ROWS, SSM_CONV_CH))
        for k in range(SSM_CONV):
            a, b = divmod(first + k, SUBLANES)
            acc = acc + cw_ref[pl.ds(k, 1), :] * sh_ref[phases.index(b),
                                                        pl.ds(r0 + a * SUBLANES, SSD_ROWS), :]
        xo_ref[0, pl.ds(r0, SSD_ROWS), :] = acc * jax.nn.sigmoid(acc)
        return carry

    lax.fori_loop(0, ts // SSD_ROWS, rows, 0, unroll=True)
    x = dt_ref[0] + dtb_ref[...]
    softplus = jnp.maximum(x, 0.0) + jnp.log1p(jnp.exp(-jnp.abs(x)))
    lane = lax.broadcasted_iota(jnp.int32, x.shape, 1)
    dtv = jnp.where(lane < 2 * SSM_HEADS, softplus, 0.0)
    dto_ref[0] = dtv

    nch = ts // SSM_CHUNK
    chunk = lambda k: slice(k * SSM_CHUNK, (k + 1) * SSM_CHUNK)
    dta = dtv * arow_ref[...]
    acs_all = _cumsum_both(jnp.concatenate([dta[chunk(k)] for k in range(nch)], axis=1), tri_ref)
    clane = lax.broadcasted_iota(jnp.int32, (SSM_CHUNK, DT_PAD), 1)
    ws, totals = [], [[], []]
    for k in range(nch):
        acs = acs_all[:, chunk(k)]
        acs_ref[0, chunk(k), :] = acs
        tot_f, tot_b = acs[SSM_CHUNK - 1:SSM_CHUNK, :], acs[0:1, :]
        to_end = jnp.where(clane < SSM_HEADS, tot_f - acs,
                           jnp.where(clane < 2 * SSM_HEADS, tot_b - acs, 0.0))
        ws.append(dtv[chunk(k)] * jnp.exp(to_end))
        totals[0].append(jnp.exp(tot_f))
        totals[1].append(jnp.exp(tot_b))
    w_all = jnp.concatenate(ws, axis=0)
    xs = xo_ref[0, :, :SSM_INNER]
    hl = lax.broadcasted_iota(jnp.int32, (SSM_STATE, SSM_INNER), 1)
    pad_rows = jnp.zeros((SUBLANES - nch % SUBLANES, DT_PAD), F32)
    for d, new_ref in enumerate((nf_ref, nb_ref)):
        xw = (xs * _expand(w_all, ex_ref, d)).astype(BF16)
        carry = _expand(jnp.concatenate(totals[d] + [pad_rows], axis=0), ex_ref, d)
        for k in range(nch):
            bmat = xo_ref[0, chunk(k), SSM_INNER:SSM_INNER + SSD_BC].astype(BF16)
            full = lax.dot_general(bmat, xw[chunk(k)], (((0,), (0,)), ((), ())),
                                   preferred_element_type=F32)
            new = full[:SSM_STATE]
            for g in range(1, SSM_GROUPS):
                new = jnp.where(hl >= g * GROUP_LANES, full[g * SSM_STATE:(g + 1) * SSM_STATE], new)
            new_ref[0, k] = new
            cy_ref[0, k, d:d + 1, :] = carry[k:k + 1]


def _ssd_pre(xbc, dt_raw, conv_w, conv_b, dt_bias, consts, bsz, seq):
    ts = min(512, seq)
    nc = seq // SSM_CHUNK
    nch = ts // SSM_CHUNK
    a_row, tri, expand = consts
    new = jax.ShapeDtypeStruct((bsz, nc, SSM_STATE, SSM_INNER), F32)
    new_spec = pl.BlockSpec((1, nch, SSM_STATE, SSM_INNER), lambda b, n: (b, n, 0, 0))
    xv = xbc.reshape(bsz, seq, SSM_CONV_CH)
    dv = dt_raw.reshape(bsz, seq, DT_PAD)
    hb = ts // SSD_HALO
    n_hb = seq // SSD_HALO
    first = SSD_HALO - SSM_CONV // 2
    phases = tuple(sorted({(first + k) % SUBLANES for k in range(SSM_CONV)}))
    dtb = jnp.zeros((1, DT_PAD), F32).at[0, :2 * SSM_HEADS].set(dt_bias.reshape(-1))
    return pl.pallas_call(
        functools.partial(_ssd_pre_kernel, ts=ts, phases=phases),
        grid=(bsz, seq // ts),
        in_specs=[pl.BlockSpec((1, ts, SSM_CONV_CH), lambda b, n: (b, n, 0)),
                  pl.BlockSpec((1, SSD_HALO, SSM_CONV_CH),
                               lambda b, n: (b, jnp.maximum(n * hb - 1, 0), 0)),
                  pl.BlockSpec((1, SSD_HALO, SSM_CONV_CH),
                               lambda b, n: (b, jnp.minimum((n + 1) * hb, n_hb - 1), 0)),
                  pl.BlockSpec((1, ts, DT_PAD), lambda b, n: (b, n, 0)),
                  _const_spec(conv_w.shape), _const_spec((1, SSM_CONV_CH)),
                  _const_spec((1, DT_PAD)), _const_spec(a_row.shape), _const_spec(tri.shape),
                  _const_spec(expand.shape)],
        out_specs=[pl.BlockSpec((1, ts, SSM_CONV_CH), lambda b, n: (b, n, 0)),
                   pl.BlockSpec((1, ts, DT_PAD), lambda b, n: (b, n, 0)),
                   pl.BlockSpec((1, ts, DT_PAD), lambda b, n: (b, n, 0)),
                   new_spec, new_spec,
                   pl.BlockSpec((1, nch, 2, SSM_INNER), lambda b, n: (b, n, 0, 0))],
        out_shape=[jax.ShapeDtypeStruct((bsz, seq, SSM_CONV_CH), F32),
                   jax.ShapeDtypeStruct((bsz, seq, DT_PAD), F32),
                   jax.ShapeDtypeStruct((bsz, seq, DT_PAD), F32),
                   new, new, jax.ShapeDtypeStruct((bsz, nc, 2, SSM_INNER), F32)],
        scratch_shapes=[pltpu.VMEM((ts + 2 * SSD_HALO, SSM_CONV_CH), F32),
                        pltpu.VMEM((len(phases), ts + SSD_HALO, SSM_CONV_CH), F32)],
        compiler_params=_cparams(("parallel", "parallel")),
        name="ssd_pre",
    )(xv, xv, xv, dv, conv_w, conv_b[None], dtb, a_row, tri, expand)


def _ssd_consts(a_log):
    a_row = jnp.zeros((1, DT_PAD), F32).at[0, :2 * SSM_HEADS].set(-jnp.exp(a_log.reshape(-1)))
    lower = np.tril(np.ones((SSM_CHUNK, SSM_CHUNK), np.float32))
    tri = jnp.asarray(np.concatenate([lower, lower.T], axis=0), BF16)
    expand = np.zeros((2, DT_PAD, SSM_INNER), np.float32)
    for d in range(2):
        for h in range(SSM_HEADS):
            expand[d, d * SSM_HEADS + h, h * SSM_HEAD_DIM:(h + 1) * SSM_HEAD_DIM] = 1.0
    return a_row, tri, jnp.asarray(expand, BF16)


def _cumsum_both(dta, tri_ref):
    hi = dta.astype(BF16)
    r1 = dta - hi.astype(F32)
    mid = r1.astype(BF16)
    lo = (r1 - mid.astype(F32)).astype(BF16)
    both = (jnp.dot(tri_ref[...], hi, preferred_element_type=F32)
            + jnp.dot(tri_ref[...], mid, preferred_element_type=F32)
            + jnp.dot(tri_ref[...], lo, preferred_element_type=F32))
    lane = lax.broadcasted_iota(jnp.int32, dta.shape, 1) % DT_PAD
    return jnp.where(lane < SSM_HEADS, both[:SSM_CHUNK], both[SSM_CHUNK:])


def _expand(v, ex_ref, d):
    return _split_dot(v, ex_ref[d])


def _ssd_scan_kernel(nf_ref, nb_ref, cf_ref, cb_ref, pf_ref, pb_ref, sf_ref, sb_ref, *, nch):
    @pl.when(pl.program_id(1) == 0)
    def _():
        sf_ref[...] = jnp.zeros_like(sf_ref)
        sb_ref[...] = jnp.zeros_like(sb_ref)

    def scan(d, order, new_ref, carry_ref, st_ref, out_ref):
        st = st_ref[...]
        for k in order:
            out_ref[0, k] = st.astype(out_ref.dtype)
            st = st * carry_ref[0, k, d:d + 1, :] + new_ref[0, k]
        st_ref[...] = st

    scan(0, range(nch), nf_ref, cf_ref, sf_ref, pf_ref)
    scan(1, range(nch - 1, -1, -1), nb_ref, cb_ref, sb_ref, pb_ref)


SSD_SCAN_CHUNKS = 16


def _ssd_states(new_f, new_b, carry, bsz, seq):
    nc = seq // SSM_CHUNK
    nch = min(SSD_SCAN_CHUNKS, nc)
    steps = nc // nch
    fwd = lambda b, c: (b, c, 0, 0)
    bwd = lambda b, c: (b, steps - 1 - c, 0, 0)
    st = jax.ShapeDtypeStruct((bsz, nc, SSM_STATE, SSM_INNER), BF16)
    blk = (1, nch, SSM_STATE, SSM_INNER)
    return pl.pallas_call(
        functools.partial(_ssd_scan_kernel, nch=nch),
        grid=(bsz, steps),
        in_specs=[pl.BlockSpec(blk, fwd), pl.BlockSpec(blk, bwd),
                  pl.BlockSpec((1, nch, 2, SSM_INNER), fwd),
                  pl.BlockSpec((1, nch, 2, SSM_INNER), bwd)],
        out_specs=[pl.BlockSpec(blk, fwd), pl.BlockSpec(blk, bwd)],
        out_shape=[st, st],
        scratch_shapes=[pltpu.VMEM((SSM_STATE, SSM_INNER), F32),
                        pltpu.VMEM((SSM_STATE, SSM_INNER), F32)],
        compiler_params=_cparams(("parallel", "arbitrary")),
        name="ssd_scan",
    )(new_f, new_b, carry, carry)


def _ssd_out_kernel(x_ref, dt_ref, acs_ref, z_ref, pf_ref, pb_ref, ex_ref, dskip_ref, ng_ref,
                    o_ref, *, nck):
    for k in range(nck):
        _ssd_out_chunk(k, x_ref, dt_ref, acs_ref, z_ref, pf_ref, pb_ref, ex_ref, dskip_ref,
                       ng_ref, o_ref)


def _ssd_out_chunk(k, x_ref, dt_ref, acs_ref, z_ref, pf_ref, pb_ref, ex_ref, dskip_ref, ng_ref,
                   o_ref):
    rows = slice(k * SSM_CHUNK, (k + 1) * SSM_CHUNK)
    xs = x_ref[0, rows, :SSM_INNER]
    bmat = x_ref[0, rows, SSM_INNER:SSM_INNER + SSD_BC].astype(BF16)
    cmat = x_ref[0, rows, SSM_INNER + SSD_BC:].astype(BF16)
    dtv = dt_ref[0, rows, :]
    acs = acs_ref[0, rows, :]
    acs_t = acs.T
    dt_t = dtv.T
    eacs = jnp.exp(acs)
    row = lax.broadcasted_iota(jnp.int32, (SSM_CHUNK, SSM_CHUNK), 0)
    col = lax.broadcasted_iota(jnp.int32, (SSM_CHUNK, SSM_CHUNK), 1)
    past, now = col < row, col == row
    low = lax.broadcasted_iota(jnp.int32, (SSM_CHUNK, LANES), 1) < SSM_HEAD_DIM
    glane = lax.broadcasted_iota(jnp.int32, (1, SSD_BC), 1) // SSM_STATE
    scores = [lax.dot_general(cmat * (glane == g).astype(BF16), bmat,
                              (((1,), (1,)), ((), ())), preferred_element_type=F32)
              for g in range(SSM_GROUPS)]
    xs_b = xs.astype(BF16)
    keep = [low.astype(BF16), (~low).astype(BF16)]
    diag = []
    for pair in range(SSM_HEADS // 2):
        xp = xs_b[:, pair * LANES:(pair + 1) * LANES]
        mats = []
        for half in range(2):
            f = 2 * pair + half
            b = SSM_HEADS + f
            seg = jnp.where(past | now, acs[:, f:f + 1] - acs_t[f:f + 1, :],
                            acs[:, b:b + 1] - acs_t[b:b + 1, :])
            dts = (jnp.where(past, dt_t[f:f + 1, :], dt_t[b:b + 1, :])
                   + jnp.where(now, dt_t[f:f + 1, :], 0.0))
            mats.append((scores[f // HEADS_PER_GROUP] * jnp.exp(seg) * dts).astype(BF16))
        diag.append(jnp.dot(jnp.concatenate(mats, axis=1),
                            jnp.concatenate([xp * keep[0], xp * keep[1]], axis=0),
                            preferred_element_type=F32))
    y = dskip_ref[...] * xs + jnp.concatenate(diag, axis=1)
    hgroup = lax.broadcasted_iota(jnp.int32, (1, SSM_INNER), 1) // GROUP_LANES
    for d, p_ref in enumerate((pf_ref, pb_ref)):
        prev = p_ref[0, k]
        stacked = jnp.concatenate([prev * (hgroup == g).astype(BF16) for g in range(SSM_GROUPS)],
                                  axis=0)
        off = jnp.dot(cmat, stacked, preferred_element_type=F32)
        y = y + off * _expand(eacs, ex_ref, d)
    z = z_ref[0, rows, :]
    y = y * (z * jax.nn.sigmoid(z))
    o_ref[0, rows, :] = _rms(y, ng_ref[...]).astype(o_ref.dtype)


SSD_OUT_CHUNKS = 4


def _ssd_out(xact, dtv, acs, z, prev_f, prev_b, expand, d_skip, norm_g, bsz, seq):
    nc = seq // SSM_CHUNK
    nck = min(SSD_OUT_CHUNKS, nc)
    dsk = jnp.repeat(d_skip, SSM_HEAD_DIM)[None]
    chunk = lambda n: pl.BlockSpec((1, nck * SSM_CHUNK, n), lambda b, c: (b, c, 0))
    state = pl.BlockSpec((1, nck, SSM_STATE, SSM_INNER), lambda b, c: (b, c, 0, 0))
    out = pl.pallas_call(
        functools.partial(_ssd_out_kernel, nck=nck),
        grid=(bsz, nc // nck),
        in_specs=[chunk(SSM_CONV_CH), chunk(DT_PAD), chunk(DT_PAD), chunk(SSM_INNER), state, state,
                  _const_spec(expand.shape), _const_spec(dsk.shape), _const_spec((1, SSM_INNER))],
        out_specs=chunk(SSM_INNER),
        out_shape=jax.ShapeDtypeStruct((bsz, seq, SSM_INNER), BF16),
        compiler_params=_cparams(("parallel", "parallel")),
        name="ssd_out",
    )(xact, dtv, acs, z.reshape(bsz, seq, SSM_INNER), prev_f, prev_b, expand, dsk, norm_g[None])
    return out.reshape(bsz * seq, SSM_INNER)


def _ssd(z, xbc, dt_raw, conv_w, conv_b, a_log, dt_bias, d_skip, norm_g, bsz, seq):
    consts = _ssd_consts(a_log)
    xact, dtv, acs, new_f, new_b, carry = _ssd_pre(xbc, dt_raw, conv_w, conv_b, dt_bias, consts,
                                                   bsz, seq)
    prev_f, prev_b = _ssd_states(new_f, new_b, carry, bsz, seq)
    return _ssd_out(xact, dtv, acs, z, prev_f, prev_b, consts[2], d_skip, norm_g, bsz, seq)


FNET_COLS = 4096


def _dft_cos_sin(n):
    ang = 2.0 * np.pi * np.outer(np.arange(n), np.arange(n)) / n
    return np.cos(ang), np.sin(ang)


def _fnet_consts(seq):
    c = FNET_GROUP_DIM
    n1 = seq // LANES
    c1, s1 = _dft_cos_sin(n1)
    stage1 = np.concatenate([c1, -s1], axis=0)
    ang = 2.0 * np.pi * np.outer(np.arange(n1), np.arange(LANES)) / seq
    twr = np.repeat(np.cos(ang), c, axis=1)
    twi = np.repeat(-np.sin(ang), c, axis=1)
    cc, sc = _dft_cos_sin(c)
    chan = np.block([[cc, -sc], [sc, cc]])
    c2, s2 = _dft_cos_sin(LANES)
    return (jnp.asarray(stage1, BF16), jnp.asarray(twr, F32), jnp.asarray(twi, F32),
            jnp.asarray(chan, BF16), jnp.asarray(c2, BF16), jnp.asarray(s2, BF16))


def _fnet1_kernel(x_ref, f_ref, twr_ref, twi_ref, o_ref, *, n1):
    c = FNET_GROUP_DIM
    a = jnp.dot(f_ref[...], x_ref[0].astype(BF16), preferred_element_type=F32)
    ar, ai = a[:n1], a[n1:]
    twr, twi = twr_ref[...], twi_ref[...]
    re = (ar * twr - ai * twi).astype(o_ref.dtype)
    im = (ar * twi + ai * twr).astype(o_ref.dtype)
    for j in range(re.shape[1] // c):
        o_ref[0, :, (2 * j) * c:(2 * j + 1) * c] = re[:, j * c:(j + 1) * c]
        o_ref[0, :, (2 * j + 1) * c:(2 * j + 2) * c] = im[:, j * c:(j + 1) * c]


def _fnet2_kernel(a_ref, chan_ref, c2_ref, s2_ref, o_ref, g_ref, scr_ref, *, n1, scale):
    c = FNET_GROUP_DIM
    pitch = scr_ref.shape[0] // LANES
    per = min(8, n1)
    for i in range(n1 // per):
        blk = a_ref[0, i * per:(i + 1) * per].reshape(per * LANES, 2 * c)
        g = jnp.dot(blk, chan_ref[...], preferred_element_type=F32).astype(BF16)
        g_ref[i * per:(i + 1) * per] = g.reshape(per, LANES, 2 * c)

    def body(k1, carry):
        g = g_ref[k1]
        y = (jnp.dot(c2_ref[...], g[:, :c], preferred_element_type=F32)
             + jnp.dot(s2_ref[...], g[:, c:], preferred_element_type=F32))
        scr_ref[pl.ds(k1, LANES, stride=pitch), :] = y * scale
        return carry

    lax.fori_loop(0, n1, body, 0, unroll=4)

    def compact(k2, carry):
        src = pl.multiple_of(k2 * pitch, SUBLANES)
        dst = pl.multiple_of(k2 * n1, n1)
        o_ref[0, pl.ds(dst, n1), :] = scr_ref[pl.ds(src, n1), :].astype(o_ref.dtype)
        return carry

    lax.fori_loop(0, LANES, compact, 0, unroll=8)


def _fourier(fn, bsz, seq):
    c = FNET_GROUP_DIM
    assert c == LANES and seq % LANES == 0
    n1 = seq // LANES
    stage1, twr, twi, chan, c2, s2 = _fnet_consts(seq)
    ncols = LANES * c
    nb = min(FNET_COLS, ncols)
    x2 = fn.reshape(bsz * FNET_GROUPS, n1, ncols)
    a = pl.pallas_call(
        functools.partial(_fnet1_kernel, n1=n1),
        grid=(ncols // nb, bsz * FNET_GROUPS),
        in_specs=[pl.BlockSpec((1, n1, nb), lambda j, i: (i, 0, j)),
                  _const_spec(stage1.shape),
                  pl.BlockSpec((n1, nb), lambda j, i: (0, j)),
                  pl.BlockSpec((n1, nb), lambda j, i: (0, j))],
        out_specs=pl.BlockSpec((1, n1, 2 * nb), lambda j, i: (i, 0, j)),
        out_shape=jax.ShapeDtypeStruct((bsz * FNET_GROUPS, n1, 2 * ncols), BF16),
        compiler_params=_cparams(("parallel", "parallel")),
        name="fnet1",
    )(x2, stage1, twr, twi)
    a4 = a.reshape(bsz * FNET_GROUPS, n1, LANES, 2 * c)
    out = pl.pallas_call(
        functools.partial(_fnet2_kernel, n1=n1, scale=1.0 / math.sqrt(seq * c)),
        grid=(bsz, FNET_GROUPS),
        in_specs=[pl.BlockSpec((1, n1, LANES, 2 * c), lambda b, g: (b * FNET_GROUPS + g, 0, 0, 0)),
                  _const_spec(chan.shape), _const_spec(c2.shape), _const_spec(s2.shape)],
        out_specs=pl.BlockSpec((1, seq, c), lambda b, g: (b, 0, g)),
        out_shape=jax.ShapeDtypeStruct((bsz, seq, FNET_WIDTH), BF16),
        scratch_shapes=[pltpu.VMEM((n1, LANES, 2 * c), BF16),
                        pltpu.VMEM((LANES * (n1 + SUBLANES), c), F32)],
        compiler_params=_cparams(("parallel", "parallel")),
        name="fnet2",
    )(a4, chan, c2, s2)
    return out.reshape(bsz * seq, FNET_WIDTH)


def kernel(x, p, rel_bias, norm_mix, w_in, conv_dw, conv_dw_b, conv_ln_g, conv_ln_b, conv_out,
           ssm_conv_w, ssm_conv_b, ssm_a_log, ssm_dt_bias, ssm_d, ssm_norm, ssm_out,
           attn_out, fnet_out, w_gate, b_gate, w_out, norm_ffn, ffn_w1, ffn_w3, ffn_w2,
           moe_router, moe_w1, moe_w3, moe_w2, ple_gate, ple_proj, final_norm):
    bsz, seq, d = x.shape
    depth = w_in.shape[0]
    t = bsz * seq
    h = x.reshape(t, d)
    bias_tables = [_att_bias_tables(rel_bias, g, dil) for g, (_, dil) in enumerate(ATT_PATTERNS)]
    for l in range(depth):
        *qkv, conv_u, z, xbc, fn, dt = _inproj(h, norm_mix[l][None], _reorder_w_in(w_in[l]), bsz, seq)
        att = [_attention_group(qkv[g], bias_tables[g], dil, bsz, seq, f"attn{g}")
               for g, (_, dil) in enumerate(ATT_PATTERNS)]
        cnf = _conformer(conv_u, conv_dw[l], conv_dw_b[l], conv_ln_g[l], conv_ln_b[l], bsz, seq)
        ssd = _ssd(z, xbc, dt, ssm_conv_w[l], ssm_conv_b[l], ssm_a_log[l], ssm_dt_bias[l],
                   ssm_d[l], ssm_norm[l], bsz, seq)
        fnt = _fourier(fn, bsz, seq)
        wbr = jnp.stack([attn_out[l], conv_out[l], ssm_out[l], fnet_out[l]]).astype(BF16)
        h = _mix(h, norm_mix[l][None], att, (cnf, ssd, fnt), wbr, w_gate[l].astype(BF16),
                 b_gate[l][:, None, :], w_out[l].astype(BF16))
        pl_in = p[l].reshape(t, -1)
        wpg, wpp = ple_gate[l].astype(BF16), ple_proj[l].astype(BF16)
        i = l // 2
        if l % 2 == 0:
            h = _ffn(h, norm_ffn[l][None], ffn_w1[i].astype(BF16), ffn_w3[i].astype(BF16),
                     ffn_w2[i].astype(BF16), pl_in, wpg, wpp)
        else:
            h = _moe(h, norm_ffn[l][None], moe_router[i], moe_w1[i].astype(BF16),
                     moe_w3[i].astype(BF16), moe_w2[i].astype(BF16), pl_in, wpg, wpp)
    return _final_norm(h, final_norm[None]).reshape(bsz, seq, d)
```

```python
import functools
import math

import numpy as np
import jax
import jax.numpy as jnp
from jax import lax
from jax.experimental import pallas as pl
from jax.experimental.pallas import tpu as pltpu

F32 = jnp.float32
BF16 = jnp.bfloat16
HI = lax.Precision.HIGHEST

EPS = 1e-6
N_BRANCHES = 4
CONV_CH = 512
CONV_K = 31
SSM_HEADS = 8
SSM_HEAD_DIM = 64
SSM_INNER = SSM_HEADS * SSM_HEAD_DIM
SSM_GROUPS = 2
SSM_STATE = 64
SSM_CONV = 5
SSM_CONV_CH = SSM_INNER + 2 * SSM_GROUPS * SSM_STATE
SSM_CHUNK = 128
ATT_PATTERNS = ((128, 1), (512, 4), (2048, 16))
ATT_GROUPS = len(ATT_PATTERNS)
ATT_HEADS = 8
ATT_HEAD_DIM = 64
ATT_WIDTH = ATT_HEADS * ATT_HEAD_DIM
REL_BUCKETS = 32
REL_MAX_DIST = 1024
FNET_GROUPS = 4
FNET_GROUP_DIM = 128
FNET_WIDTH = FNET_GROUPS * FNET_GROUP_DIM
N_EXPERTS = 8
TOP_K = 2

ATT_IN_COLS = 3 * ATT_GROUPS * ATT_WIDTH
CONV_IN_COLS = 2 * CONV_CH
SSM_IN_COLS = SSM_INNER + SSM_CONV_CH + 2 * SSM_HEADS
OFF_CONV = ATT_IN_COLS
OFF_SSM = OFF_CONV + CONV_IN_COLS
OFF_FNET = OFF_SSM + SSM_IN_COLS

LANES = 128
DT_PAD = LANES
V7X_VMEM_BYTES = 64 * 1024 * 1024
VMEM_LIMIT = V7X_VMEM_BYTES - 8 * 1024 * 1024

MIX_ROWS = 512
FFN_ROWS, FFN_COLS = 512, 1408
MOE_TOKENS, MOE_COLS = 1024, 1792
NORM_ROWS = 1024
SEQ_ROWS = 512


def _cparams(sem):
    return pltpu.CompilerParams(dimension_semantics=sem, vmem_limit_bytes=VMEM_LIMIT)


def _const_spec(shape):
    nd = len(shape)
    return pl.BlockSpec(shape, lambda *_: (0,) * nd, pipeline_mode=pl.Buffered(1))


def _rms(x, g):
    return x * lax.rsqrt(jnp.mean(x * x, axis=-1, keepdims=True) + EPS) * g


_SEC_QKV = (0, ATT_IN_COLS)
_SEC_CONV = (_SEC_QKV[0] + _SEC_QKV[1], CONV_IN_COLS)
_SEC_Z = (_SEC_CONV[0] + _SEC_CONV[1], SSM_INNER)
_SEC_XBC = (_SEC_Z[0] + _SEC_Z[1], SSM_CONV_CH)
_SEC_FNET = (_SEC_XBC[0] + _SEC_XBC[1], FNET_WIDTH)
_SEC_DT = (_SEC_FNET[0] + _SEC_FNET[1], DT_PAD)
_IN_COLS_PAD = _SEC_DT[0] + _SEC_DT[1]
_MM_CHUNK = 512


ATT_TILE = 256
QKV_COLS = 3 * ATT_WIDTH
INPROJ_ROWS = 2 * ATT_TILE


def _reorder_w_in(w):
    d = w.shape[0]
    qkv = w[:, :OFF_CONV].reshape(d, 3, ATT_GROUPS, ATT_WIDTH).transpose(0, 2, 1, 3)
    ssm = w[:, OFF_SSM:OFF_FNET]
    dt = ssm[:, SSM_INNER + SSM_CONV_CH:]
    parts = [qkv.reshape(d, OFF_CONV), w[:, OFF_CONV:OFF_SSM], ssm[:, :SSM_INNER],
             ssm[:, SSM_INNER:SSM_INNER + SSM_CONV_CH], w[:, OFF_FNET:],
             dt, jnp.zeros((d, DT_PAD - dt.shape[1]), w.dtype)]
    return jnp.concatenate(parts, axis=1).astype(BF16)


def _deinterleave_matrix(dil):
    s = np.arange(ATT_TILE)
    m = np.zeros((ATT_TILE, ATT_TILE), np.float32)
    m[(s % dil) * (ATT_TILE // dil) + s // dil, s] = 1.0
    return m


def _inproj_kernel(h_ref, g_ref, w_ref, perm_ref, q0_ref, q1_ref, q2_ref, conv_ref, z_ref,
                   xbc_ref, fn_ref, dt_ref):
    xn = _rms(h_ref[...], g_ref[...]).astype(BF16)

    def section(x, sec, store):
        start, width = sec
        for c in range(0, width, _MM_CHUNK):
            cw = min(_MM_CHUNK, width - c)
            store(c, cw, jnp.dot(x, w_ref[:, start + c:start + c + cw],
                                 preferred_element_type=F32))

    def to(ref):
        def store(c, cw, val):
            ref[:, c:c + cw] = val.astype(ref.dtype)
        return store

    def to_fnet(c, cw, val):
        for g in range(cw // FNET_GROUP_DIM):
            fn_ref[0, c // FNET_GROUP_DIM + g] = val[:, g * FNET_GROUP_DIM:(g + 1) * FNET_GROUP_DIM]

    for g, q_ref in enumerate((q0_ref, q1_ref, q2_ref)):
        x = xn
        if ATT_PATTERNS[g][1] > 1:
            x = jnp.concatenate(
                [jnp.dot(perm_ref[g], xn[i * ATT_TILE:(i + 1) * ATT_TILE],
                         preferred_element_type=F32) for i in range(xn.shape[0] // ATT_TILE)],
                axis=0).astype(BF16)
        section(x, (g * QKV_COLS, QKV_COLS), to(q_ref))
    section(xn, _SEC_CONV, to(conv_ref))
    section(xn, _SEC_Z, to(z_ref))
    section(xn, _SEC_XBC, to(xbc_ref))
    section(xn, _SEC_FNET, to_fnet)
    section(xn, _SEC_DT, to(dt_ref))


def _inproj(h, g, w, bsz, seq):
    t, d = h.shape
    tm = INPROJ_ROWS
    spt = seq // tm
    perm = jnp.asarray(np.stack([_deinterleave_matrix(dil) for _, dil in ATT_PATTERNS]), BF16)
    row = lambda n: pl.BlockSpec((tm, n), lambda i: (i, 0))
    qkv = jax.ShapeDtypeStruct((t, QKV_COLS), BF16)
    return pl.pallas_call(
        _inproj_kernel,
        grid=(t // tm,),
        in_specs=[row(d), _const_spec((1, d)), _const_spec(w.shape), _const_spec(perm.shape)],
        out_specs=[row(QKV_COLS), row(QKV_COLS), row(QKV_COLS), row(CONV_IN_COLS),
                   row(SSM_INNER), row(SSM_CONV_CH),
                   pl.BlockSpec((1, FNET_GROUPS, tm, FNET_GROUP_DIM),
                                lambda i: (i // spt, 0, i % spt, 0)),
                   row(DT_PAD)],
        out_shape=[qkv, qkv, qkv,
                   jax.ShapeDtypeStruct((t, CONV_IN_COLS), F32),
                   jax.ShapeDtypeStruct((t, SSM_INNER), F32),
                   jax.ShapeDtypeStruct((t, SSM_CONV_CH), F32),
                   jax.ShapeDtypeStruct((bsz, FNET_GROUPS, seq, FNET_GROUP_DIM), F32),
                   jax.ShapeDtypeStruct((t, DT_PAD), F32)],
        compiler_params=_cparams(("parallel",)),
        name="inproj",
    )(h, g, w, perm)


def _split_dot(v, m):
    hi = v.astype(BF16)
    lo = (v - hi.astype(F32)).astype(BF16)
    return (jnp.dot(hi, m, preferred_element_type=F32) + jnp.dot(lo, m, preferred_element_type=F32))


def _interleave(pt, v):
    n = v.shape[1]
    if v.dtype != BF16:
        hi = v.astype(BF16)
        v = jnp.concatenate([hi, (v - hi.astype(F32)).astype(BF16)], axis=1)
    tiles = []
    for i in range(v.shape[0] // ATT_TILE):
        r = jnp.dot(pt, v[i * ATT_TILE:(i + 1) * ATT_TILE], preferred_element_type=F32)
        tiles.append(r if r.shape[1] == n else r[:, :n] + r[:, n:])
    return jnp.concatenate(tiles, axis=0)


def _mix_kernel(h_ref, g_ref, o0_ref, o1_ref, o2_ref, l0_ref, l1_ref, l2_ref, b1_ref, b2_ref,
                b3_ref, pt_ref, hx_ref, wbr_ref, wg_ref, cg_ref, wo_ref, o_ref):
    h = h_ref[...]
    xn = _rms(h, g_ref[...]).astype(BF16)
    outs, lses = [], []
    for g, (og_ref, lg_ref) in enumerate(zip((o0_ref, o1_ref, o2_ref), (l0_ref, l1_ref, l2_ref))):
        if ATT_PATTERNS[g][1] > 1:
            outs.append(_interleave(pt_ref[g], og_ref[...]))
            lses.append(_interleave(pt_ref[g], lg_ref[...]))
        else:
            outs.append(og_ref[...].astype(F32))
            lses.append(lg_ref[...])
    top = jnp.maximum(jnp.maximum(lses[0], lses[1]), lses[2])
    es = [jnp.exp(l - top) for l in lses]
    inv = 1.0 / (es[0] + es[1] + es[2])
    att = None
    for e, og in zip(es, outs):
        term = og * jnp.dot((e * inv).astype(BF16), hx_ref[...], preferred_element_type=F32)
        att = term if att is None else att + term
    acc = None
    for b, hid in enumerate((att.astype(BF16), b1_ref[...], b2_ref[...], b3_ref[...])):
        gate = jax.nn.sigmoid(jnp.dot(xn, wg_ref[b], preferred_element_type=F32) + cg_ref[b])
        br = jnp.dot(hid, wbr_ref[b], preferred_element_type=F32)
        acc = gate * br if acc is None else acc + gate * br
    o_ref[...] = h + jnp.dot(acc.astype(BF16), wo_ref[...], preferred_element_type=F32)


def _mix(h, g, att, others, wbr, wg, cg, wo):
    t, d = h.shape
    tm = MIX_ROWS
    row = lambda n: pl.BlockSpec((tm, n), lambda i: (i, 0))
    head_expand = np.zeros((LANES, ATT_WIDTH), np.float32)
    for hd in range(ATT_HEADS):
        head_expand[hd, hd * ATT_HEAD_DIM:(hd + 1) * ATT_HEAD_DIM] = 1.0
    head_expand = jnp.asarray(head_expand, BF16)
    unperm = jnp.asarray(np.stack([_deinterleave_matrix(dil).T for _, dil in ATT_PATTERNS]), BF16)
    outs = [o for o, _ in att]
    lses = [l for _, l in att]
    return pl.pallas_call(
        _mix_kernel,
        grid=(t // tm,),
        in_specs=[row(d), _const_spec((1, d))] + [row(a.shape[1]) for a in outs + lses + list(others)]
                 + [_const_spec(unperm.shape), _const_spec(head_expand.shape), _const_spec(wbr.shape),
                    _const_spec(wg.shape), _const_spec(cg.shape), _const_spec(wo.shape)],
        out_specs=row(d),
        out_shape=jax.ShapeDtypeStruct((t, d), F32),
        compiler_params=_cparams(("parallel",)),
        name="mix",
    )(h, g, *outs, *lses, *others, unperm, head_expand, wbr, wg, cg, wo)


def _ple(h2, p_ref, wpg_ref, wpp_ref):
    gate = jax.nn.sigmoid(jnp.dot(h2.astype(BF16), wpg_ref[...], preferred_element_type=F32))
    pe = jnp.dot(p_ref[...].astype(BF16), wpp_ref[...], preferred_element_type=F32)
    return h2 + gate * pe


def _swiglu_partial(xn, w1, w3, w2):
    a = jnp.dot(xn, w1, preferred_element_type=F32)
    b = jnp.dot(xn, w3, preferred_element_type=F32)
    hid = a * jax.nn.sigmoid(a) * b
    return jnp.dot(hid.astype(BF16), w2, preferred_element_type=F32)


def _ffn_kernel(h_ref, g_ref, w1_ref, w3_ref, w2_ref, p_ref, wpg_ref, wpp_ref, o_ref, xn_ref):
    j = pl.program_id(1)

    @pl.when(j == 0)
    def _():
        h = h_ref[...]
        xn_ref[...] = _rms(h, g_ref[...]).astype(BF16)
        o_ref[...] = h

    o_ref[...] += _swiglu_partial(xn_ref[...], w1_ref[...], w3_ref[...], w2_ref[...])

    @pl.when(j == pl.num_programs(1) - 1)
    def _():
        o_ref[...] = _ple(o_ref[...], p_ref, wpg_ref, wpp_ref)


def _ffn(h, g, w1, w3, w2, p, wpg, wpp):
    t, d = h.shape
    f = w1.shape[1]
    tm, tf = FFN_ROWS, FFN_COLS
    row = lambda n: pl.BlockSpec((tm, n), lambda i, j: (i, 0))
    return pl.pallas_call(
        _ffn_kernel,
        grid=(t // tm, f // tf),
        in_specs=[row(d), _const_spec((1, d)),
                  pl.BlockSpec((d, tf), lambda i, j: (0, j)),
                  pl.BlockSpec((d, tf), lambda i, j: (0, j)),
                  pl.BlockSpec((tf, d), lambda i, j: (j, 0)),
                  row(p.shape[1]), _const_spec(wpg.shape), _const_spec(wpp.shape)],
        out_specs=row(d),
        out_shape=jax.ShapeDtypeStruct((t, d), F32),
        scratch_shapes=[pltpu.VMEM((tm, d), BF16)],
        compiler_params=_cparams(("parallel", "arbitrary")),
        name="ffn",
    )(h, g, w1, w3, w2, p, wpg, wpp)


MOE_ROWS = 128


def _moe_route(logits):
    ne, tm = logits.shape
    eidx = lax.broadcasted_iota(jnp.int32, logits.shape, 0)
    m1 = jnp.max(logits, axis=0, keepdims=True)
    i1 = jnp.min(jnp.where(logits == m1, eidx, ne), axis=0, keepdims=True)
    rest = jnp.where(eidx == i1, -jnp.inf, logits)
    m2 = jnp.max(rest, axis=0, keepdims=True)
    i2 = jnp.min(jnp.where(rest == m2, eidx, ne), axis=0, keepdims=True)
    e2 = jnp.exp(m2 - m1)
    den = 1.0 + e2
    combine = jnp.where(eidx == i1, 1.0 / den, 0.0) + jnp.where(eidx == i2, e2 / den, 0.0)
    routed = jnp.where((eidx == i1) | (eidx == i2), 1.0, 0.0)
    r = lax.broadcasted_iota(jnp.int32, (LANES, LANES), 0)
    c = lax.broadcasted_iota(jnp.int32, (LANES, LANES), 1)
    before = jnp.where(r < c, 1.0, 0.0).astype(BF16)
    counts = jnp.zeros((ne, 1), F32)
    slots = []
    for k in range(tm // LANES):
        blk = routed[:, k * LANES:(k + 1) * LANES]
        slots.append(jnp.dot(blk.astype(BF16), before, preferred_element_type=F32) + counts)
        counts = counts + jnp.sum(blk, axis=1, keepdims=True)
    slot = jnp.where(routed > 0.0, jnp.concatenate(slots, axis=1), -1.0).astype(jnp.int32)
    return combine, slot, counts


def _moe_kernel(h_ref, g_ref, rt_ref, w1_ref, w3_ref, w2_ref, p_ref, wpg_ref, wpp_ref, o_ref,
                xn_ref, comb_ref, slot_ref, cnt_ref, xe_ref, ye_ref):
    e = pl.program_id(1)
    j = pl.program_id(2)
    ne = pl.num_programs(1)
    tm = xn_ref.shape[0]

    @pl.when((e == 0) & (j == 0))
    def _():
        h = h_ref[...]
        xn = _rms(h, g_ref[...])
        xn_ref[...] = xn.astype(BF16)
        logits = lax.dot_general(rt_ref[...], xn, (((1,), (1,)), ((), ())),
                                 preferred_element_type=F32, precision=HI)
        combine, slot, counts = _moe_route(logits)
        comb_ref[...] = combine
        slot_ref[...] = slot
        for k in range(comb_ref.shape[0]):
            cnt_ref[k] = jnp.sum(counts[k:k + 1, :]).astype(jnp.int32)
        o_ref[...] = h

    n_blocks = (cnt_ref[e] + MOE_ROWS - 1) // MOE_ROWS

    def for_row_blocks(body):
        def pair(i, carry):
            body(pl.multiple_of(i * 2 * MOE_ROWS, 2 * MOE_ROWS), 2 * MOE_ROWS)
            return carry
        lax.fori_loop(0, n_blocks // 2, pair, 0)

        @pl.when(n_blocks % 2 == 1)
        def _():
            body(pl.multiple_of((n_blocks - 1) * MOE_ROWS, MOE_ROWS), MOE_ROWS)

    def one_hot(r0, rows):
        return slot_ref[pl.ds(e, 1), :] == lax.broadcasted_iota(jnp.int32, (rows, tm), 0) + r0

    def swiglu(x):
        return _swiglu_partial(x, w1_ref[0], w3_ref[0], w2_ref[0])

    last_j = pl.num_programs(2) - 1

    @pl.when(j == 0)
    def _():
        def gather_first(r0, rows):
            sel = jnp.where(one_hot(r0, rows), 1.0, 0.0).astype(BF16)
            x = jnp.dot(sel, xn_ref[...], preferred_element_type=F32).astype(BF16)
            xe_ref[pl.ds(r0, rows), :] = x
            ye_ref[pl.ds(r0, rows), :] = swiglu(x)
        for_row_blocks(gather_first)

    @pl.when((j > 0) & (j < last_j))
    def _():
        def middle(r0, rows):
            ye_ref[pl.ds(r0, rows), :] += swiglu(xe_ref[pl.ds(r0, rows), :])
        for_row_blocks(middle)

    @pl.when((j > 0) & (j == last_j))
    def _():
        def last_scatter(r0, rows):
            y = ye_ref[pl.ds(r0, rows), :] + swiglu(xe_ref[pl.ds(r0, rows), :])
            hot = one_hot(r0, rows)
            weight = jnp.sum(jnp.where(hot, comb_ref[pl.ds(e, 1), :], 0.0), axis=1, keepdims=True)
            o_ref[...] += lax.dot_general(jnp.where(hot, 1.0, 0.0).astype(BF16),
                                          (y * weight).astype(BF16),
                                          (((0,), (0,)), ((), ())), preferred_element_type=F32)
        for_row_blocks(last_scatter)

    @pl.when((e == ne - 1) & (j == pl.num_programs(2) - 1))
    def _():
        o_ref[...] = _ple(o_ref[...], p_ref, wpg_ref, wpp_ref)


def _moe(h, g, router, w1, w3, w2, p, wpg, wpp):
    t, d = h.shape
    ne, _, f = w1.shape
    tm, tf = MOE_TOKENS, MOE_COLS
    assert f % tf == 0 and f // tf >= 2
    row = lambda n: pl.BlockSpec((tm, n), lambda i, e, j: (i, 0))
    return pl.pallas_call(
        _moe_kernel,
        grid=(t // tm, ne, f // tf),
        in_specs=[row(d), _const_spec((1, d)), _const_spec((ne, d)),
                  pl.BlockSpec((1, d, tf), lambda i, e, j: (e, 0, j)),
                  pl.BlockSpec((1, d, tf), lambda i, e, j: (e, 0, j)),
                  pl.BlockSpec((1, tf, d), lambda i, e, j: (e, j, 0)),
                  row(p.shape[1]), _const_spec(wpg.shape), _const_spec(wpp.shape)],
        out_specs=row(d),
        out_shape=jax.ShapeDtypeStruct((t, d), F32),
        scratch_shapes=[pltpu.VMEM((tm, d), BF16), pltpu.VMEM((ne, tm), F32),
                        pltpu.VMEM((ne, tm), jnp.int32), pltpu.SMEM((ne,), jnp.int32),
                        pltpu.VMEM((tm, d), BF16), pltpu.VMEM((tm, d), F32)],
        compiler_params=_cparams(("parallel", "arbitrary", "arbitrary")),
        name="moe",
    )(h, g, router.T, w1, w3, w2, p, wpg, wpp)


def _final_norm_kernel(h_ref, g_ref, o_ref):
    o_ref[...] = _rms(h_ref[...], g_ref[...])


def _final_norm(h, g):
    t, d = h.shape
    tm = NORM_ROWS
    row = pl.BlockSpec((tm, d), lambda i: (i, 0))
    return pl.pallas_call(
        _final_norm_kernel, grid=(t // tm,),
        in_specs=[row, _const_spec((1, d))], out_specs=row,
        out_shape=jax.ShapeDtypeStruct((t, d), F32),
        compiler_params=_cparams(("parallel",)), name="final_norm",
    )(h, g)


ATT_HALF = 64
ATT_QB = 128
ATT_KB = ATT_QB + 2 * ATT_HALF
NEG = -1e30
assert all(w // (2 * d) == ATT_HALF for w, d in ATT_PATTERNS)


def _t5_bucket(rel):
    half = REL_BUCKETS // 2
    max_exact = half // 2
    n = np.abs(rel)
    large = max_exact + (np.log(np.maximum(n, 1) / max_exact) / math.log(REL_MAX_DIST / max_exact)
                         * (half - max_exact)).astype(np.int32)
    large = np.minimum(large, half - 1)
    return np.where(rel > 0, half, 0) + np.where(n < max_exact, n, large)


def _att_bias_tables(rel_bias, g, dil):
    i = np.arange(ATT_QB)[:, None]
    j = np.arange(ATT_KB)[None, :]
    rel = j - ATT_HALF - i
    band = np.abs(rel) <= ATT_HALF
    pick = np.eye(REL_BUCKETS, dtype=np.float32)[_t5_bucket(dil * rel)]
    heads = rel_bias[:, g * ATT_HEADS:(g + 1) * ATT_HEADS].astype(F32)
    bias = jnp.einsum('qkb,bh->hqk', pick, heads, precision=lax.Precision.HIGHEST)
    tables = []
    for v in range(4):
        ok = band
        if v & 1:
            ok = ok & (j >= ATT_HALF)
        if v & 2:
            ok = ok & (j < ATT_QB + ATT_HALF)
        tables.append(jnp.where(ok[None], bias, NEG))
    return jnp.stack(tables)


def _attn_kernel(q_ref, k_ref, v_ref, kp_ref, kn_ref, vp_ref, vn_ref, bias_ref, o_ref, lse_ref,
                 qbuf, kbuf, vbuf, obuf, lbuf, *, tq, n_blocks):
    flat = lambda ref: ref[0].reshape(-1, ref.shape[-1])
    qbuf[...] = flat(q_ref)
    kbuf[0:ATT_HALF] = flat(kp_ref)
    kbuf[ATT_HALF:ATT_HALF + tq] = flat(k_ref)
    kbuf[ATT_HALF + tq:] = flat(kn_ref)
    vbuf[0:ATT_HALF] = flat(vp_ref)
    vbuf[ATT_HALF:ATT_HALF + tq] = flat(v_ref)
    vbuf[ATT_HALF + tq:] = flat(vn_ref)
    nsb = tq // ATT_QB
    first = pl.program_id(2) * nsb
    lane = lax.broadcasted_iota(jnp.int32, (ATT_QB, LANES), 1)
    low = lane < ATT_HEAD_DIM
    lane_row = lax.broadcasted_iota(jnp.int32, (1, LANES), 1)
    keep = [(lane_row < ATT_HEAD_DIM).astype(BF16), (lane_row >= ATT_HEAD_DIM).astype(BF16)]
    ones = jnp.ones((ATT_KB, LANES), BF16)

    def block(sb, carry):
        r0 = pl.multiple_of(sb * ATT_QB, ATT_QB)
        gsb = first + sb
        variant = (gsb == 0).astype(jnp.int32) + 2 * (gsb == n_blocks - 1).astype(jnp.int32)
        q = qbuf[pl.ds(r0, ATT_QB), :] * (ATT_HEAD_DIM ** -0.5)
        lse_all = jnp.zeros((ATT_QB, LANES), F32)
        outs = []
        for pair in range(ATT_HEADS // 2):
            cols = slice(pair * LANES, (pair + 1) * LANES)
            qp = q[:, cols]
            kp = kbuf[pl.ds(r0, ATT_KB), cols]
            vp = jnp.concatenate([vbuf[pl.ds(r0, ATT_KB), cols], ones], axis=1)
            res = []
            for half in range(2):
                h = 2 * pair + half
                s = lax.dot_general(qp * keep[half], kp, (((1,), (1,)), ((), ())),
                                    preferred_element_type=F32)
                s = s + bias_ref[variant, h]
                m = jnp.max(s, axis=-1, keepdims=True)
                e = jnp.exp(s - m)
                pv = jnp.dot(e.astype(BF16), vp, preferred_element_type=F32)
                den = pv[:, LANES:]
                res.append(pv[:, :LANES] / den)
                lse_all = jnp.where(lane == h, m + jnp.log(den), lse_all)
            outs.append(jnp.where(low, res[0], res[1]))
        obuf[pl.ds(r0, ATT_QB), :] = jnp.concatenate(outs, axis=1).astype(obuf.dtype)
        lbuf[pl.ds(r0, ATT_QB), :] = lse_all
        return carry

    lax.fori_loop(0, nsb, block, 0, unroll=4)
    o_ref[0] = obuf[...].reshape(o_ref.shape[1:])
    lse_ref[0] = lbuf[...].reshape(lse_ref.shape[1:])


def _attention_group(qkv, bias_tables, dil, bsz, seq, name):
    sub_len = seq // dil
    assert sub_len % ATT_QB == 0 and seq % ATT_TILE == 0 and ATT_TILE % dil == 0
    rows = ATT_TILE // dil
    tq = min(SEQ_ROWS, sub_len)
    w = ATT_WIDTH
    hrows = min(rows, ATT_HALF)
    n_hb = sub_len // ATT_HALF

    def view(a, chunk):
        return a.reshape(bsz, (seq // ATT_TILE) * (rows // chunk), dil, chunk, a.shape[-1])

    def main(part):
        return pl.BlockSpec((1, tq // rows, None, rows, w), lambda b, r, n: (b, n, r, 0, part))

    def halo(part, nxt):
        if nxt:
            blk = lambda n: jnp.minimum((n + 1) * (tq // ATT_HALF), n_hb - 1)
        else:
            blk = lambda n: jnp.maximum(n * (tq // ATT_HALF) - 1, 0)
        return pl.BlockSpec((1, ATT_HALF // hrows, None, hrows, w),
                            lambda b, r, n: (b, blk(n), r, 0, part))

    mv, hv = view(qkv, rows), view(qkv, hrows)
    o, lse = pl.pallas_call(
        functools.partial(_attn_kernel, tq=tq, n_blocks=sub_len // ATT_QB),
        grid=(bsz, dil, sub_len // tq),
        in_specs=[main(0), main(1), main(2), halo(1, False), halo(1, True), halo(2, False),
                  halo(2, True), _const_spec(bias_tables.shape)],
        out_specs=[pl.BlockSpec((1, tq // rows, None, rows, w), lambda b, r, n: (b, n, r, 0, 0)),
                   pl.BlockSpec((1, tq // rows, None, rows, LANES), lambda b, r, n: (b, n, r, 0, 0))],
        out_shape=[jax.ShapeDtypeStruct((bsz, seq // ATT_TILE, dil, rows, w), BF16),
                   jax.ShapeDtypeStruct((bsz, seq // ATT_TILE, dil, rows, LANES), F32)],
        scratch_shapes=[pltpu.VMEM((tq, w), BF16),
                        pltpu.VMEM((tq + 2 * ATT_HALF, w), BF16),
                        pltpu.VMEM((tq + 2 * ATT_HALF, w), BF16),
                        pltpu.VMEM((tq, w), BF16), pltpu.VMEM((tq, LANES), F32)],
        compiler_params=_cparams(("parallel", "parallel", "parallel")),
        name=name,
    )(mv, mv, mv, hv, hv, hv, hv, bias_tables)
    return o.reshape(bsz * seq, w), lse.reshape(bsz * seq, LANES)


SUBLANES = 8
CONV_HALO = 16
CONV_ROWS = 64


def _conformer_kernel(u_ref, up_ref, un_ref, dw_ref, dwb_ref, lng_ref, lnb_ref, o_ref,
                      hp_ref, sh_ref, *, ts):
    n = pl.program_id(1)

    def glu(u):
        return u[:, :CONV_CH] * jax.nn.sigmoid(u[:, CONV_CH:])

    hp_ref[0:CONV_HALO] = jnp.where(n > 0, glu(up_ref[0]), 0.0)
    hp_ref[CONV_HALO:CONV_HALO + ts] = glu(u_ref[0])
    hp_ref[CONV_HALO + ts:] = jnp.where(n < pl.num_programs(1) - 1, glu(un_ref[0]), 0.0)
    span = ts + 2 * CONV_HALO - SUBLANES
    for b in range(1, SUBLANES):
        sh_ref[b - 1] = hp_ref[pl.ds(b, span), :]
    first = CONV_HALO - CONV_K // 2

    def rows(c, carry):
        r0 = pl.multiple_of(c * CONV_ROWS, CONV_ROWS)
        acc = jnp.broadcast_to(dwb_ref[...], (CONV_ROWS, CONV_CH))
        for k in range(CONV_K):
            a, b = divmod(first + k, SUBLANES)
            at = pl.ds(r0 + a * SUBLANES, CONV_ROWS)
            tap = hp_ref[at, :] if b == 0 else sh_ref[b - 1, at, :]
            acc = acc + dw_ref[pl.ds(k, 1), :] * tap
        mu = jnp.mean(acc, axis=-1, keepdims=True)
        cen = acc - mu
        var = jnp.mean(cen * cen, axis=-1, keepdims=True)
        y = cen * lax.rsqrt(var + EPS) * lng_ref[...] + lnb_ref[...]
        o_ref[0, pl.ds(r0, CONV_ROWS), :] = (y * jax.nn.sigmoid(y)).astype(o_ref.dtype)
        return carry

    lax.fori_loop(0, ts // CONV_ROWS, rows, 0, unroll=2)


def _conformer(u, dw, dw_b, ln_g, ln_b, bsz, seq):
    ts = min(SEQ_ROWS, seq)
    view = u.reshape(bsz, seq, 2 * CONV_CH)
    hb = ts // CONV_HALO
    n_hb = seq // CONV_HALO
    out = pl.pallas_call(
        functools.partial(_conformer_kernel, ts=ts),
        grid=(bsz, seq // ts),
        in_specs=[pl.BlockSpec((1, ts, 2 * CONV_CH), lambda b, n: (b, n, 0)),
                  pl.BlockSpec((1, CONV_HALO, 2 * CONV_CH),
                               lambda b, n: (b, jnp.maximum(n * hb - 1, 0), 0)),
                  pl.BlockSpec((1, CONV_HALO, 2 * CONV_CH),
                               lambda b, n: (b, jnp.minimum((n + 1) * hb, n_hb - 1), 0)),
                  _const_spec(dw.shape), _const_spec((1, CONV_CH)), _const_spec((1, CONV_CH)),
                  _const_spec((1, CONV_CH))],
        out_specs=pl.BlockSpec((1, ts, CONV_CH), lambda b, n: (b, n, 0)),
        out_shape=jax.ShapeDtypeStruct((bsz, seq, CONV_CH), BF16),
        scratch_shapes=[pltpu.VMEM((ts + 2 * CONV_HALO, CONV_CH), F32),
                        pltpu.VMEM((SUBLANES - 1, ts + 2 * CONV_HALO - SUBLANES, CONV_CH), F32)],
        compiler_params=_cparams(("parallel", "parallel")),
        name="conformer",
    )(view, view, view, dw, dw_b[None], ln_g[None], ln_b[None])
    return out.reshape(bsz * seq, CONV_CH)


SSD_HALO = SUBLANES
SSD_ROWS = 64
SSD_BC = SSM_GROUPS * SSM_STATE
HEADS_PER_GROUP = SSM_HEADS // SSM_GROUPS
GROUP_LANES = HEADS_PER_GROUP * SSM_HEAD_DIM
assert SSD_BC == LANES and SSM_CHUNK == LANES and 2 * SSM_HEADS <= LANES


def _ssd_pre_kernel(x_ref, xp_ref, xn_ref, dt_ref, cw_ref, cb_ref, dtb_ref, arow_ref, tri_ref,
                    ex_ref, xo_ref, dto_ref, acs_ref, nf_ref, nb_ref, cy_ref, hp_ref, sh_ref,
                    *, ts, phases):
    n = pl.program_id(1)
    hp_ref[0:SSD_HALO] = jnp.where(n > 0, xp_ref[0], 0.0)
    hp_ref[SSD_HALO:SSD_HALO + ts] = x_ref[0]
    hp_ref[SSD_HALO + ts:] = jnp.where(n < pl.num_programs(1) - 1, xn_ref[0], 0.0)
    for i, b in enumerate(phases):
        sh_ref[i] = hp_ref[pl.ds(b, ts + SSD_HALO), :]
    first = SSD_HALO - SSM_CONV // 2

    def rows(c, carry):
        r0 = pl.multiple_of(c * SSD_ROWS, SSD_ROWS)
        acc = jnp.broadcast_to(cb_ref[...], (SSD_ROWS, SSM_CONV_CH))
        for k in range(SSM_CONV):
            a, b = divmod(first + k, SUBLANES)
            acc = acc + cw_ref[pl.ds(k, 1), :] * sh_ref[phases.index(b),
                                                        pl.ds(r0 + a * SUBLANES, SSD_ROWS), :]
        xo_ref[0, pl.ds(r0, SSD_ROWS), :] = acc * jax.nn.sigmoid(acc)
        return carry

    lax.fori_loop(0, ts // SSD_ROWS, rows, 0, unroll=True)
    x = dt_ref[0] + dtb_ref[...]
    softplus = jnp.maximum(x, 0.0) + jnp.log1p(jnp.exp(-jnp.abs(x)))
    lane = lax.broadcasted_iota(jnp.int32, x.shape, 1)
    dtv = jnp.where(lane < 2 * SSM_HEADS, softplus, 0.0)
    dto_ref[0] = dtv

    nch = ts // SSM_CHUNK
    chunk = lambda k: slice(k * SSM_CHUNK, (k + 1) * SSM_CHUNK)
    dta = dtv * arow_ref[...]
    acs_all = _cumsum_both(jnp.concatenate([dta[chunk(k)] for k in range(nch)], axis=1), tri_ref)
    clane = lax.broadcasted_iota(jnp.int32, (SSM_CHUNK, DT_PAD), 1)
    ws, totals = [], [[], []]
    for k in range(nch):
        acs = acs_all[:, chunk(k)]
        acs_ref[0, chunk(k), :] = acs
        tot_f, tot_b = acs[SSM_CHUNK - 1:SSM_CHUNK, :], acs[0:1, :]
        to_end = jnp.where(clane < SSM_HEADS, tot_f - acs,
                           jnp.where(clane < 2 * SSM_HEADS, tot_b - acs, 0.0))
        ws.append(dtv[chunk(k)] * jnp.exp(to_end))
        totals[0].append(jnp.exp(tot_f))
        totals[1].append(jnp.exp(tot_b))
    w_all = jnp.concatenate(ws, axis=0)
    xs = xo_ref[0, :, :SSM_INNER]
    hl = lax.broadcasted_iota(jnp.int32, (SSM_STATE, SSM_INNER), 1)
    pad_rows = jnp.zeros((SUBLANES - nch % SUBLANES, DT_PAD), F32)
    for d, new_ref in enumerate((nf_ref, nb_ref)):
        xw = (xs * _expand(w_all, ex_ref, d)).astype(BF16)
        carry = _expand(jnp.concatenate(totals[d] + [pad_rows], axis=0), ex_ref, d)
        for k in range(nch):
            bmat = xo_ref[0, chunk(k), SSM_INNER:SSM_INNER + SSD_BC].astype(BF16)
            full = lax.dot_general(bmat, xw[chunk(k)], (((0,), (0,)), ((), ())),
                                   preferred_element_type=F32)
            new = full[:SSM_STATE]
            for g in range(1, SSM_GROUPS):
                new = jnp.where(hl >= g * GROUP_LANES, full[g * SSM_STATE:(g + 1) * SSM_STATE], new)
            new_ref[0, k] = new
            cy_ref[0, k, d:d + 1, :] = carry[k:k + 1]


def _ssd_pre(xbc, dt_raw, conv_w, conv_b, dt_bias, consts, bsz, seq):
    ts = min(SEQ_ROWS, seq)
    nc = seq // SSM_CHUNK
    nch = ts // SSM_CHUNK
    a_row, tri, expand = consts
    new = jax.ShapeDtypeStruct((bsz, nc, SSM_STATE, SSM_INNER), F32)
    new_spec = pl.BlockSpec((1, nch, SSM_STATE, SSM_INNER), lambda b, n: (b, n, 0, 0))
    xv = xbc.reshape(bsz, seq, SSM_CONV_CH)
    dv = dt_raw.reshape(bsz, seq, DT_PAD)
    hb = ts // SSD_HALO
    n_hb = seq // SSD_HALO
    first = SSD_HALO - SSM_CONV // 2
    phases = tuple(sorted({(first + k) % SUBLANES for k in range(SSM_CONV)}))
    dtb = jnp.zeros((1, DT_PAD), F32).at[0, :2 * SSM_HEADS].set(dt_bias.reshape(-1))
    return pl.pallas_call(
        functools.partial(_ssd_pre_kernel, ts=ts, phases=phases),
        grid=(bsz, seq // ts),
        in_specs=[pl.BlockSpec((1, ts, SSM_CONV_CH), lambda b, n: (b, n, 0)),
                  pl.BlockSpec((1, SSD_HALO, SSM_CONV_CH),
                               lambda b, n: (b, jnp.maximum(n * hb - 1, 0), 0)),
                  pl.BlockSpec((1, SSD_HALO, SSM_CONV_CH),
                               lambda b, n: (b, jnp.minimum((n + 1) * hb, n_hb - 1), 0)),
                  pl.BlockSpec((1, ts, DT_PAD), lambda b, n: (b, n, 0)),
                  _const_spec(conv_w.shape), _const_spec((1, SSM_CONV_CH)),
                  _const_spec((1, DT_PAD)), _const_spec(a_row.shape), _const_spec(tri.shape),
                  _const_spec(expand.shape)],
        out_specs=[pl.BlockSpec((1, ts, SSM_CONV_CH), lambda b, n: (b, n, 0)),
                   pl.BlockSpec((1, ts, DT_PAD), lambda b, n: (b, n, 0)),
                   pl.BlockSpec((1, ts, DT_PAD), lambda b, n: (b, n, 0)),
                   new_spec, new_spec,
                   pl.BlockSpec((1, nch, 2, SSM_INNER), lambda b, n: (b, n, 0, 0))],
        out_shape=[jax.ShapeDtypeStruct((bsz, seq, SSM_CONV_CH), F32),
                   jax.ShapeDtypeStruct((bsz, seq, DT_PAD), F32),
                   jax.ShapeDtypeStruct((bsz, seq, DT_PAD), F32),
                   new, new, jax.ShapeDtypeStruct((bsz, nc, 2, SSM_INNER), F32)],
        scratch_shapes=[pltpu.VMEM((ts + 2 * SSD_HALO, SSM_CONV_CH), F32),
                        pltpu.VMEM((len(phases), ts + SSD_HALO, SSM_CONV_CH), F32)],
        compiler_params=_cparams(("parallel", "parallel")),
        name="ssd_pre",
    )(xv, xv, xv, dv, conv_w, conv_b[None], dtb, a_row, tri, expand)


def _ssd_consts(a_log):
    a_row = jnp.zeros((1, DT_PAD), F32).at[0, :2 * SSM_HEADS].set(-jnp.exp(a_log.reshape(-1)))
    lower = np.tril(np.ones((SSM_CHUNK, SSM_CHUNK), np.float32))
    tri = jnp.asarray(np.concatenate([lower, lower.T], axis=0), BF16)
    expand = np.zeros((2, DT_PAD, SSM_INNER), np.float32)
    for d in range(2):
        for h in range(SSM_HEADS):
            expand[d, d * SSM_HEADS + h, h * SSM_HEAD_DIM:(h + 1) * SSM_HEAD_DIM] = 1.0
    return a_row, tri, jnp.asarray(expand, BF16)


def _cumsum_both(dta, tri_ref):
    hi = dta.astype(BF16)
    r1 = dta - hi.astype(F32)
    mid = r1.astype(BF16)
    lo = (r1 - mid.astype(F32)).astype(BF16)
    both = (jnp.dot(tri_ref[...], hi, preferred_element_type=F32)
            + jnp.dot(tri_ref[...], mid, preferred_element_type=F32)
            + jnp.dot(tri_ref[...], lo, preferred_element_type=F32))
    lane = lax.broadcasted_iota(jnp.int32, dta.shape, 1) % DT_PAD
    return jnp.where(lane < SSM_HEADS, both[:SSM_CHUNK], both[SSM_CHUNK:])


def _expand(v, ex_ref, d):
    return _split_dot(v, ex_ref[d])


def _ssd_scan_kernel(nf_ref, nb_ref, cf_ref, cb_ref, pf_ref, pb_ref, sf_ref, sb_ref, *, nch):
    @pl.when(pl.program_id(1) == 0)
    def _():
        sf_ref[...] = jnp.zeros_like(sf_ref)
        sb_ref[...] = jnp.zeros_like(sb_ref)

    def scan(d, order, new_ref, carry_ref, st_ref, out_ref):
        st = st_ref[...]
        for k in order:
            out_ref[0, k] = st.astype(out_ref.dtype)
            st = st * carry_ref[0, k, d:d + 1, :] + new_ref[0, k]
        st_ref[...] = st

    scan(0, range(nch), nf_ref, cf_ref, sf_ref, pf_ref)
    scan(1, range(nch - 1, -1, -1), nb_ref, cb_ref, sb_ref, pb_ref)


SSD_SCAN_CHUNKS = 16


def _ssd_states(new_f, new_b, carry, bsz, seq):
    nc = seq // SSM_CHUNK
    nch = min(SSD_SCAN_CHUNKS, nc)
    steps = nc // nch
    fwd = lambda b, c: (b, c, 0, 0)
    bwd = lambda b, c: (b, steps - 1 - c, 0, 0)
    st = jax.ShapeDtypeStruct((bsz, nc, SSM_STATE, SSM_INNER), BF16)
    blk = (1, nch, SSM_STATE, SSM_INNER)
    return pl.pallas_call(
        functools.partial(_ssd_scan_kernel, nch=nch),
        grid=(bsz, steps),
        in_specs=[pl.BlockSpec(blk, fwd), pl.BlockSpec(blk, bwd),
                  pl.BlockSpec((1, nch, 2, SSM_INNER), fwd),
                  pl.BlockSpec((1, nch, 2, SSM_INNER), bwd)],
        out_specs=[pl.BlockSpec(blk, fwd), pl.BlockSpec(blk, bwd)],
        out_shape=[st, st],
        scratch_shapes=[pltpu.VMEM((SSM_STATE, SSM_INNER), F32),
                        pltpu.VMEM((SSM_STATE, SSM_INNER), F32)],
        compiler_params=_cparams(("parallel", "arbitrary")),
        name="ssd_scan",
    )(new_f, new_b, carry, carry)


def _ssd_out_kernel(x_ref, dt_ref, acs_ref, z_ref, pf_ref, pb_ref, ex_ref, dskip_ref, ng_ref,
                    o_ref, *, nck):
    for k in range(nck):
        _ssd_out_chunk(k, x_ref, dt_ref, acs_ref, z_ref, pf_ref, pb_ref, ex_ref, dskip_ref,
                       ng_ref, o_ref)


def _ssd_out_chunk(k, x_ref, dt_ref, acs_ref, z_ref, pf_ref, pb_ref, ex_ref, dskip_ref, ng_ref,
                   o_ref):
    rows = slice(k * SSM_CHUNK, (k + 1) * SSM_CHUNK)
    xs = x_ref[0, rows, :SSM_INNER]
    bmat = x_ref[0, rows, SSM_INNER:SSM_INNER + SSD_BC].astype(BF16)
    cmat = x_ref[0, rows, SSM_INNER + SSD_BC:].astype(BF16)
    dtv = dt_ref[0, rows, :]
    acs = acs_ref[0, rows, :]
    acs_t = acs.T
    dt_t = dtv.T
    eacs = jnp.exp(acs)
    row = lax.broadcasted_iota(jnp.int32, (SSM_CHUNK, SSM_CHUNK), 0)
    col = lax.broadcasted_iota(jnp.int32, (SSM_CHUNK, SSM_CHUNK), 1)
    past, now = col < row, col == row
    low = lax.broadcasted_iota(jnp.int32, (SSM_CHUNK, LANES), 1) < SSM_HEAD_DIM
    glane = lax.broadcasted_iota(jnp.int32, (1, SSD_BC), 1) // SSM_STATE
    scores = [lax.dot_general(cmat * (glane == g).astype(BF16), bmat,
                              (((1,), (1,)), ((), ())), preferred_element_type=F32)
              for g in range(SSM_GROUPS)]
    xs_b = xs.astype(BF16)
    keep = [low.astype(BF16), (~low).astype(BF16)]
    diag = []
    for pair in range(SSM_HEADS // 2):
        xp = xs_b[:, pair * LANES:(pair + 1) * LANES]
        mats = []
        for half in range(2):
            f = 2 * pair + half
            b = SSM_HEADS + f
            seg = jnp.where(past | now, acs[:, f:f + 1] - acs_t[f:f + 1, :],
                            acs[:, b:b + 1] - acs_t[b:b + 1, :])
            dts = (jnp.where(past, dt_t[f:f + 1, :], dt_t[b:b + 1, :])
                   + jnp.where(now, dt_t[f:f + 1, :], 0.0))
            mats.append((scores[f // HEADS_PER_GROUP] * jnp.exp(seg) * dts).astype(BF16))
        diag.append(jnp.dot(jnp.concatenate(mats, axis=1),
                            jnp.concatenate([xp * keep[0], xp * keep[1]], axis=0),
                            preferred_element_type=F32))
    y = dskip_ref[...] * xs + jnp.concatenate(diag, axis=1)
    hgroup = lax.broadcasted_iota(jnp.int32, (1, SSM_INNER), 1) // GROUP_LANES
    for d, p_ref in enumerate((pf_ref, pb_ref)):
        prev = p_ref[0, k]
        stacked = jnp.concatenate([prev * (hgroup == g).astype(BF16) for g in range(SSM_GROUPS)],
                                  axis=0)
        off = jnp.dot(cmat, stacked, preferred_element_type=F32)
        y = y + off * _expand(eacs, ex_ref, d)
    z = z_ref[0, rows, :]
    y = y * (z * jax.nn.sigmoid(z))
    o_ref[0, rows, :] = _rms(y, ng_ref[...]).astype(o_ref.dtype)


SSD_OUT_CHUNKS = 4


def _ssd_out(xact, dtv, acs, z, prev_f, prev_b, expand, d_skip, norm_g, bsz, seq):
    nc = seq // SSM_CHUNK
    nck = min(SSD_OUT_CHUNKS, nc)
    dsk = jnp.repeat(d_skip, SSM_HEAD_DIM)[None]
    chunk = lambda n: pl.BlockSpec((1, nck * SSM_CHUNK, n), lambda b, c: (b, c, 0))
    state = pl.BlockSpec((1, nck, SSM_STATE, SSM_INNER), lambda b, c: (b, c, 0, 0))
    out = pl.pallas_call(
        functools.partial(_ssd_out_kernel, nck=nck),
        grid=(bsz, nc // nck),
        in_specs=[chunk(SSM_CONV_CH), chunk(DT_PAD), chunk(DT_PAD), chunk(SSM_INNER), state, state,
                  _const_spec(expand.shape), _const_spec(dsk.shape), _const_spec((1, SSM_INNER))],
        out_specs=chunk(SSM_INNER),
        out_shape=jax.ShapeDtypeStruct((bsz, seq, SSM_INNER), BF16),
        compiler_params=_cparams(("parallel", "parallel")),
        name="ssd_out",
    )(xact, dtv, acs, z.reshape(bsz, seq, SSM_INNER), prev_f, prev_b, expand, dsk, norm_g[None])
    return out.reshape(bsz * seq, SSM_INNER)


def _ssd(z, xbc, dt_raw, conv_w, conv_b, a_log, dt_bias, d_skip, norm_g, bsz, seq):
    consts = _ssd_consts(a_log)
    xact, dtv, acs, new_f, new_b, carry = _ssd_pre(xbc, dt_raw, conv_w, conv_b, dt_bias, consts,
                                                   bsz, seq)
    prev_f, prev_b = _ssd_states(new_f, new_b, carry, bsz, seq)
    return _ssd_out(xact, dtv, acs, z, prev_f, prev_b, consts[2], d_skip, norm_g, bsz, seq)


FNET_COLS = 4096


def _dft_cos_sin(n):
    ang = 2.0 * np.pi * np.outer(np.arange(n), np.arange(n)) / n
    return np.cos(ang), np.sin(ang)


def _fnet_consts(seq):
    c = FNET_GROUP_DIM
    n1 = seq // LANES
    c1, s1 = _dft_cos_sin(n1)
    stage1 = np.concatenate([c1, -s1], axis=0)
    ang = 2.0 * np.pi * np.outer(np.arange(n1), np.arange(LANES)) / seq
    twr = np.repeat(np.cos(ang), c, axis=1)
    twi = np.repeat(-np.sin(ang), c, axis=1)
    cc, sc = _dft_cos_sin(c)
    chan = np.block([[cc, -sc], [sc, cc]])
    c2, s2 = _dft_cos_sin(LANES)
    return (jnp.asarray(stage1, BF16), jnp.asarray(twr, F32), jnp.asarray(twi, F32),
            jnp.asarray(chan, BF16), jnp.asarray(c2, BF16), jnp.asarray(s2, BF16))


def _fnet1_kernel(x_ref, f_ref, twr_ref, twi_ref, o_ref, *, n1):
    c = FNET_GROUP_DIM
    a = jnp.dot(f_ref[...], x_ref[0].astype(BF16), preferred_element_type=F32)
    ar, ai = a[:n1], a[n1:]
    twr, twi = twr_ref[...], twi_ref[...]
    re = (ar * twr - ai * twi).astype(o_ref.dtype)
    im = (ar * twi + ai * twr).astype(o_ref.dtype)
    for j in range(re.shape[1] // c):
        o_ref[0, :, (2 * j) * c:(2 * j + 1) * c] = re[:, j * c:(j + 1) * c]
        o_ref[0, :, (2 * j + 1) * c:(2 * j + 2) * c] = im[:, j * c:(j + 1) * c]


def _fnet2_kernel(a_ref, chan_ref, c2_ref, s2_ref, o_ref, g_ref, scr_ref, *, n1, scale):
    c = FNET_GROUP_DIM
    pitch = scr_ref.shape[0] // LANES
    per = min(8, n1)
    for i in range(n1 // per):
        blk = a_ref[0, i * per:(i + 1) * per].reshape(per * LANES, 2 * c)
        g = jnp.dot(blk, chan_ref[...], preferred_element_type=F32).astype(BF16)
        g_ref[i * per:(i + 1) * per] = g.reshape(per, LANES, 2 * c)

    def body(k1, carry):
        g = g_ref[k1]
        y = (jnp.dot(c2_ref[...], g[:, :c], preferred_element_type=F32)
             + jnp.dot(s2_ref[...], g[:, c:], preferred_element_type=F32))
        scr_ref[pl.ds(k1, LANES, stride=pitch), :] = y * scale
        return carry

    lax.fori_loop(0, n1, body, 0, unroll=4)

    def compact(k2, carry):
        src = pl.multiple_of(k2 * pitch, SUBLANES)
        dst = pl.multiple_of(k2 * n1, n1)
        o_ref[0, pl.ds(dst, n1), :] = scr_ref[pl.ds(src, n1), :].astype(o_ref.dtype)
        return carry

    lax.fori_loop(0, LANES, compact, 0, unroll=8)


def _fourier(fn, bsz, seq):
    c = FNET_GROUP_DIM
    assert c == LANES and seq % LANES == 0
    n1 = seq // LANES
    stage1, twr, twi, chan, c2, s2 = _fnet_consts(seq)
    ncols = LANES * c
    nb = min(FNET_COLS, ncols)
    x2 = fn.reshape(bsz * FNET_GROUPS, n1, ncols)
    a = pl.pallas_call(
        functools.partial(_fnet1_kernel, n1=n1),
        grid=(ncols // nb, bsz * FNET_GROUPS),
        in_specs=[pl.BlockSpec((1, n1, nb), lambda j, i: (i, 0, j)),
                  _const_spec(stage1.shape),
                  pl.BlockSpec((n1, nb), lambda j, i: (0, j)),
                  pl.BlockSpec((n1, nb), lambda j, i: (0, j))],
        out_specs=pl.BlockSpec((1, n1, 2 * nb), lambda j, i: (i, 0, j)),
        out_shape=jax.ShapeDtypeStruct((bsz * FNET_GROUPS, n1, 2 * ncols), BF16),
        compiler_params=_cparams(("parallel", "parallel")),
        name="fnet1",
    )(x2, stage1, twr, twi)
    a4 = a.reshape(bsz * FNET_GROUPS, n1, LANES, 2 * c)
    out = pl.pallas_call(
        functools.partial(_fnet2_kernel, n1=n1, scale=1.0 / math.sqrt(seq * c)),
        grid=(bsz, FNET_GROUPS),
        in_specs=[pl.BlockSpec((1, n1, LANES, 2 * c), lambda b, g: (b * FNET_GROUPS + g, 0, 0, 0)),
                  _const_spec(chan.shape), _const_spec(c2.shape), _const_spec(s2.shape)],
        out_specs=pl.BlockSpec((1, seq, c), lambda b, g: (b, 0, g)),
        out_shape=jax.ShapeDtypeStruct((bsz, seq, FNET_WIDTH), BF16),
        scratch_shapes=[pltpu.VMEM((n1, LANES, 2 * c), BF16),
                        pltpu.VMEM((LANES * (n1 + SUBLANES), c), F32)],
        compiler_params=_cparams(("parallel", "parallel")),
        name="fnet2",
    )(a4, chan, c2, s2)
    return out.reshape(bsz * seq, FNET_WIDTH)


def kernel(x, p, rel_bias, norm_mix, w_in, conv_dw, conv_dw_b, conv_ln_g, conv_ln_b, conv_out,
           ssm_conv_w, ssm_conv_b, ssm_a_log, ssm_dt_bias, ssm_d, ssm_norm, ssm_out,
           attn_out, fnet_out, w_gate, b_gate, w_out, norm_ffn, ffn_w1, ffn_w3, ffn_w2,
           moe_router, moe_w1, moe_w3, moe_w2, ple_gate, ple_proj, final_norm):
    bsz, seq, d = x.shape
    depth = w_in.shape[0]
    t = bsz * seq
    h = x.reshape(t, d)
    bias_tables = [_att_bias_tables(rel_bias, g, dil) for g, (_, dil) in enumerate(ATT_PATTERNS)]
    for l in range(depth):
        *qkv, conv_u, z, xbc, fn, dt = _inproj(h, norm_mix[l][None], _reorder_w_in(w_in[l]), bsz, seq)
        att = [_attention_group(qkv[g], bias_tables[g], dil, bsz, seq, f"attn{g}")
               for g, (_, dil) in enumerate(ATT_PATTERNS)]
        cnf = _conformer(conv_u, conv_dw[l], conv_dw_b[l], conv_ln_g[l], conv_ln_b[l], bsz, seq)
        ssd = _ssd(z, xbc, dt, ssm_conv_w[l], ssm_conv_b[l], ssm_a_log[l], ssm_dt_bias[l],
                   ssm_d[l], ssm_norm[l], bsz, seq)
        fnt = _fourier(fn, bsz, seq)
        wbr = jnp.stack([attn_out[l], conv_out[l], ssm_out[l], fnet_out[l]]).astype(BF16)
        h = _mix(h, norm_mix[l][None], att, (cnf, ssd, fnt), wbr, w_gate[l].astype(BF16),
                 b_gate[l][:, None, :], w_out[l].astype(BF16))
        pl_in = p[l].reshape(t, -1)
        wpg, wpp = ple_gate[l].astype(BF16), ple_proj[l].astype(BF16)
        i = l // 2
        if l % 2 == 0:
            h = _ffn(h, norm_ffn[l][None], ffn_w1[i].astype(BF16), ffn_w3[i].astype(BF16),
                     ffn_w2[i].astype(BF16), pl_in, wpg, wpp)
        else:
            h = _moe(h, norm_ffn[l][None], moe_router[i], moe_w1[i].astype(BF16),
                     moe_w3[i].astype(BF16), moe_w2[i].astype(BF16), pl_in, wpg, wpp)
    return _final_norm(h, final_norm[None]).reshape(bsz, seq, d)
```

```python
import functools
import math

import numpy as np
import jax
import jax.numpy as jnp
from jax import lax
from jax.experimental import pallas as pl
from jax.experimental.pallas import tpu as pltpu

F32 = jnp.float32
BF16 = jnp.bfloat16
HI = lax.Precision.HIGHEST

EPS = 1e-6
N_BRANCHES = 4
CONV_CH = 512
CONV_K = 31
SSM_HEADS = 8
SSM_HEAD_DIM = 64
SSM_INNER = SSM_HEADS * SSM_HEAD_DIM
SSM_GROUPS = 2
SSM_STATE = 64
SSM_CONV = 5
SSM_CONV_CH = SSM_INNER + 2 * SSM_GROUPS * SSM_STATE
SSM_CHUNK = 128
ATT_PATTERNS = ((128, 1), (512, 4), (2048, 16))
ATT_GROUPS = len(ATT_PATTERNS)
ATT_HEADS = 8
ATT_HEAD_DIM = 64
ATT_WIDTH = ATT_HEADS * ATT_HEAD_DIM
REL_BUCKETS = 32
REL_MAX_DIST = 1024
FNET_GROUPS = 4
FNET_GROUP_DIM = 128
FNET_WIDTH = FNET_GROUPS * FNET_GROUP_DIM
N_EXPERTS = 8
TOP_K = 2

ATT_IN_COLS = 3 * ATT_GROUPS * ATT_WIDTH
CONV_IN_COLS = 2 * CONV_CH
SSM_IN_COLS = SSM_INNER + SSM_CONV_CH + 2 * SSM_HEADS
OFF_CONV = ATT_IN_COLS
OFF_SSM = OFF_CONV + CONV_IN_COLS
OFF_FNET = OFF_SSM + SSM_IN_COLS

LANES = 128
DT_PAD = LANES
V7X_VMEM_BYTES = 64 * 1024 * 1024
VMEM_LIMIT = V7X_VMEM_BYTES - 8 * 1024 * 1024

MIX_ROWS = 512
FFN_ROWS, FFN_COLS = 512, 1536
MOE_TOKENS, MOE_COLS = 1024, 1792
NORM_ROWS = 1024
SEQ_ROWS = 512


def _cparams(sem):
    return pltpu.CompilerParams(dimension_semantics=sem, vmem_limit_bytes=VMEM_LIMIT)


def _const_spec(shape):
    nd = len(shape)
    return pl.BlockSpec(shape, lambda *_: (0,) * nd, pipeline_mode=pl.Buffered(1))


def _rms(x, g):
    return x * lax.rsqrt(jnp.mean(x * x, axis=-1, keepdims=True) + EPS) * g


_SEC_QKV = (0, ATT_IN_COLS)
_SEC_CONV = (_SEC_QKV[0] + _SEC_QKV[1], CONV_IN_COLS)
_SEC_Z = (_SEC_CONV[0] + _SEC_CONV[1], SSM_INNER)
_SEC_XBC = (_SEC_Z[0] + _SEC_Z[1], SSM_CONV_CH)
_SEC_FNET = (_SEC_XBC[0] + _SEC_XBC[1], FNET_WIDTH)
_SEC_DT = (_SEC_FNET[0] + _SEC_FNET[1], DT_PAD)
_IN_COLS_PAD = _SEC_DT[0] + _SEC_DT[1]
_MM_CHUNK = 512


ATT_TILE = 256
QKV_COLS = 3 * ATT_WIDTH
INPROJ_ROWS = 2 * ATT_TILE


def _reorder_w_in(w):
    d = w.shape[0]
    qkv = w[:, :OFF_CONV].reshape(d, 3, ATT_GROUPS, ATT_WIDTH).transpose(0, 2, 1, 3)
    ssm = w[:, OFF_SSM:OFF_FNET]
    dt = ssm[:, SSM_INNER + SSM_CONV_CH:]
    parts = [qkv.reshape(d, OFF_CONV), w[:, OFF_CONV:OFF_SSM], ssm[:, :SSM_INNER],
             ssm[:, SSM_INNER:SSM_INNER + SSM_CONV_CH], w[:, OFF_FNET:],
             dt, jnp.zeros((d, DT_PAD - dt.shape[1]), w.dtype)]
    return jnp.concatenate(parts, axis=1).astype(BF16)


def _deinterleave_matrix(dil):
    s = np.arange(ATT_TILE)
    m = np.zeros((ATT_TILE, ATT_TILE), np.float32)
    m[(s % dil) * (ATT_TILE // dil) + s // dil, s] = 1.0
    return m


def _inproj_kernel(h_ref, g_ref, w_ref, perm_ref, q0_ref, q1_ref, q2_ref, conv_ref, z_ref,
                   xbc_ref, fn_ref, dt_ref):
    xn = _rms(h_ref[...], g_ref[...]).astype(BF16)

    def section(x, sec, store):
        start, width = sec
        for c in range(0, width, _MM_CHUNK):
            cw = min(_MM_CHUNK, width - c)
            store(c, cw, jnp.dot(x, w_ref[:, start + c:start + c + cw],
                                 preferred_element_type=F32))

    def to(ref):
        def store(c, cw, val):
            ref[:, c:c + cw] = val.astype(ref.dtype)
        return store

    def to_fnet(c, cw, val):
        for g in range(cw // FNET_GROUP_DIM):
            fn_ref[0, c // FNET_GROUP_DIM + g] = val[:, g * FNET_GROUP_DIM:(g + 1) * FNET_GROUP_DIM]

    for g, q_ref in enumerate((q0_ref, q1_ref, q2_ref)):
        x = xn
        if ATT_PATTERNS[g][1] > 1:
            x = jnp.concatenate(
                [jnp.dot(perm_ref[g], xn[i * ATT_TILE:(i + 1) * ATT_TILE],
                         preferred_element_type=F32) for i in range(xn.shape[0] // ATT_TILE)],
                axis=0).astype(BF16)
        section(x, (g * QKV_COLS, QKV_COLS), to(q_ref))
    section(xn, _SEC_CONV, to(conv_ref))
    section(xn, _SEC_Z, to(z_ref))
    section(xn, _SEC_XBC, to(xbc_ref))
    section(xn, _SEC_FNET, to_fnet)
    section(xn, _SEC_DT, to(dt_ref))


def _inproj(h, g, w, bsz, seq):
    t, d = h.shape
    tm = INPROJ_ROWS
    spt = seq // tm
    perm = jnp.asarray(np.stack([_deinterleave_matrix(dil) for _, dil in ATT_PATTERNS]), BF16)
    row = lambda n: pl.BlockSpec((tm, n), lambda i: (i, 0))
    qkv = jax.ShapeDtypeStruct((t, QKV_COLS), BF16)
    return pl.pallas_call(
        _inproj_kernel,
        grid=(t // tm,),
        in_specs=[row(d), _const_spec((1, d)), _const_spec(w.shape), _const_spec(perm.shape)],
        out_specs=[row(QKV_COLS), row(QKV_COLS), row(QKV_COLS), row(CONV_IN_COLS),
                   row(SSM_INNER), row(SSM_CONV_CH),
                   pl.BlockSpec((1, FNET_GROUPS, tm, FNET_GROUP_DIM),
                                lambda i: (i // spt, 0, i % spt, 0)),
                   row(DT_PAD)],
        out_shape=[qkv, qkv, qkv,
                   jax.ShapeDtypeStruct((t, CONV_IN_COLS), F32),
                   jax.ShapeDtypeStruct((t, SSM_INNER), F32),
                   jax.ShapeDtypeStruct((t, SSM_CONV_CH), F32),
                   jax.ShapeDtypeStruct((bsz, FNET_GROUPS, seq, FNET_GROUP_DIM), F32),
                   jax.ShapeDtypeStruct((t, DT_PAD), F32)],
        compiler_params=_cparams(("parallel",)),
        name="inproj",
    )(h, g, w, perm)


def _split_dot(v, m):
    hi = v.astype(BF16)
    lo = (v - hi.astype(F32)).astype(BF16)
    return (jnp.dot(hi, m, preferred_element_type=F32) + jnp.dot(lo, m, preferred_element_type=F32))


def _interleave(pt, v):
    n = v.shape[1]
    if v.dtype != BF16:
        hi = v.astype(BF16)
        v = jnp.concatenate([hi, (v - hi.astype(F32)).astype(BF16)], axis=1)
    tiles = []
    for i in range(v.shape[0] // ATT_TILE):
        r = jnp.dot(pt, v[i * ATT_TILE:(i + 1) * ATT_TILE], preferred_element_type=F32)
        tiles.append(r if r.shape[1] == n else r[:, :n] + r[:, n:])
    return jnp.concatenate(tiles, axis=0)


def _mix_kernel(h_ref, g_ref, o0_ref, o1_ref, o2_ref, l0_ref, l1_ref, l2_ref, b1_ref, b2_ref,
                b3_ref, pt_ref, hx_ref, wbr_ref, wg_ref, cg_ref, wo_ref, o_ref):
    h = h_ref[...]
    xn = _rms(h, g_ref[...]).astype(BF16)
    outs, lses = [], []
    for g, (og_ref, lg_ref) in enumerate(zip((o0_ref, o1_ref, o2_ref), (l0_ref, l1_ref, l2_ref))):
        if ATT_PATTERNS[g][1] > 1:
            outs.append(_interleave(pt_ref[g], og_ref[...]))
            lses.append(_interleave(pt_ref[g], lg_ref[...]))
        else:
            outs.append(og_ref[...].astype(F32))
            lses.append(lg_ref[...])
    top = jnp.maximum(jnp.maximum(lses[0], lses[1]), lses[2])
    es = [jnp.exp(l - top) for l in lses]
    inv = 1.0 / (es[0] + es[1] + es[2])
    att = None
    for e, og in zip(es, outs):
        term = og * jnp.dot((e * inv).astype(BF16), hx_ref[...], preferred_element_type=F32)
        att = term if att is None else att + term
    acc = None
    for b, hid in enumerate((att.astype(BF16), b1_ref[...], b2_ref[...], b3_ref[...])):
        gate = jax.nn.sigmoid(jnp.dot(xn, wg_ref[b], preferred_element_type=F32) + cg_ref[b])
        br = jnp.dot(hid, wbr_ref[b], preferred_element_type=F32)
        acc = gate * br if acc is None else acc + gate * br
    o_ref[...] = h + jnp.dot(acc.astype(BF16), wo_ref[...], preferred_element_type=F32)


def _mix(h, g, att, others, wbr, wg, cg, wo):
    t, d = h.shape
    tm = MIX_ROWS
    row = lambda n: pl.BlockSpec((tm, n), lambda i: (i, 0))
    head_expand = np.zeros((LANES, ATT_WIDTH), np.float32)
    for hd in range(ATT_HEADS):
        head_expand[hd, hd * ATT_HEAD_DIM:(hd + 1) * ATT_HEAD_DIM] = 1.0
    head_expand = jnp.asarray(head_expand, BF16)
    unperm = jnp.asarray(np.stack([_deinterleave_matrix(dil).T for _, dil in ATT_PATTERNS]), BF16)
    outs = [o for o, _ in att]
    lses = [l for _, l in att]
    return pl.pallas_call(
        _mix_kernel,
        grid=(t // tm,),
        in_specs=[row(d), _const_spec((1, d))] + [row(a.shape[1]) for a in outs + lses + list(others)]
                 + [_const_spec(unperm.shape), _const_spec(head_expand.shape), _const_spec(wbr.shape),
                    _const_spec(wg.shape), _const_spec(cg.shape), _const_spec(wo.shape)],
        out_specs=row(d),
        out_shape=jax.ShapeDtypeStruct((t, d), F32),
        compiler_params=_cparams(("parallel",)),
        name="mix",
    )(h, g, *outs, *lses, *others, unperm, head_expand, wbr, wg, cg, wo)


def _ple(h2, p_ref, wpg_ref, wpp_ref):
    gate = jax.nn.sigmoid(jnp.dot(h2.astype(BF16), wpg_ref[...], preferred_element_type=F32))
    pe = jnp.dot(p_ref[...].astype(BF16), wpp_ref[...], preferred_element_type=F32)
    return h2 + gate * pe


def _swiglu_partial(xn, w1, w3, w2):
    a = jnp.dot(xn, w1, preferred_element_type=F32)
    b = jnp.dot(xn, w3, preferred_element_type=F32)
    hid = a * jax.nn.sigmoid(a) * b
    return jnp.dot(hid.astype(BF16), w2, preferred_element_type=F32)


def _ffn_kernel(h_ref, g_ref, w1_ref, w3_ref, w2_ref, p_ref, wpg_ref, wpp_ref, o_ref):
    h = h_ref[...]
    xn = _rms(h, g_ref[...]).astype(BF16)
    acc = h
    f = w1_ref.shape[1]
    for c0 in range(0, f, FFN_COLS):
        c1 = min(c0 + FFN_COLS, f)
        acc = acc + _swiglu_partial(xn, w1_ref[:, c0:c1], w3_ref[:, c0:c1], w2_ref[c0:c1, :])
    o_ref[...] = _ple(acc, p_ref, wpg_ref, wpp_ref)


def _ffn(h, g, w1, w3, w2, p, wpg, wpp):
    t, d = h.shape
    tm = FFN_ROWS
    row = lambda n: pl.BlockSpec((tm, n), lambda i: (i, 0))
    return pl.pallas_call(
        _ffn_kernel,
        grid=(t // tm,),
        in_specs=[row(d), _const_spec((1, d)), _const_spec(w1.shape), _const_spec(w3.shape),
                  _const_spec(w2.shape), row(p.shape[1]), _const_spec(wpg.shape),
                  _const_spec(wpp.shape)],
        out_specs=row(d),
        out_shape=jax.ShapeDtypeStruct((t, d), F32),
        compiler_params=_cparams(("parallel",)),
        name="ffn",
    )(h, g, w1, w3, w2, p, wpg, wpp)


MOE_ROWS = 128
MOE_MAX_BLOCKS = 2


def _moe_route(logits):
    ne, tm = logits.shape
    eidx = lax.broadcasted_iota(jnp.int32, logits.shape, 0)
    m1 = jnp.max(logits, axis=0, keepdims=True)
    i1 = jnp.min(jnp.where(logits == m1, eidx, ne), axis=0, keepdims=True)
    rest = jnp.where(eidx == i1, -jnp.inf, logits)
    m2 = jnp.max(rest, axis=0, keepdims=True)
    i2 = jnp.min(jnp.where(rest == m2, eidx, ne), axis=0, keepdims=True)
    e2 = jnp.exp(m2 - m1)
    den = 1.0 + e2
    combine = jnp.where(eidx == i1, 1.0 / den, 0.0) + jnp.where(eidx == i2, e2 / den, 0.0)
    routed = jnp.where((eidx == i1) | (eidx == i2), 1.0, 0.0)
    r = lax.broadcasted_iota(jnp.int32, (LANES, LANES), 0)
    c = lax.broadcasted_iota(jnp.int32, (LANES, LANES), 1)
    before = jnp.where(r < c, 1.0, 0.0).astype(BF16)
    counts = jnp.zeros((ne, 1), F32)
    slots = []
    for k in range(tm // LANES):
        blk = routed[:, k * LANES:(k + 1) * LANES]
        slots.append(jnp.dot(blk.astype(BF16), before, preferred_element_type=F32) + counts)
        counts = counts + jnp.sum(blk, axis=1, keepdims=True)
    slot = jnp.where(routed > 0.0, jnp.concatenate(slots, axis=1), -1.0).astype(jnp.int32)
    return combine, slot, counts


def _moe_kernel(h_ref, g_ref, rt_ref, w1_ref, w3_ref, w2_ref, p_ref, wpg_ref, wpp_ref, o_ref,
                xn_ref, comb_ref, slot_ref, cnt_ref, xe_ref, ye_ref):
    e = pl.program_id(1)
    j = pl.program_id(2)
    ne = pl.num_programs(1)
    tm = xn_ref.shape[0]

    @pl.when((e == 0) & (j == 0))
    def _():
        h = h_ref[...]
        xn = _rms(h, g_ref[...])
        xn_ref[...] = xn.astype(BF16)
        logits = lax.dot_general(rt_ref[...], xn, (((1,), (1,)), ((), ())),
                                 preferred_element_type=F32, precision=HI)
        combine, slot, counts = _moe_route(logits)
        comb_ref[...] = combine
        slot_ref[...] = slot
        for k in range(comb_ref.shape[0]):
            cnt_ref[k] = jnp.sum(counts[k:k + 1, :]).astype(jnp.int32)
        o_ref[...] = h

    n_blocks = (cnt_ref[e] + MOE_ROWS - 1) // MOE_ROWS

    def for_row_blocks(body):
        full = MOE_MAX_BLOCKS * MOE_ROWS

        def whole(i, carry):
            body(pl.multiple_of(i * full, full), full)
            return carry
        lax.fori_loop(0, n_blocks // MOE_MAX_BLOCKS, whole, 0)
        base = pl.multiple_of((n_blocks // MOE_MAX_BLOCKS) * full, full)
        for r in range(1, MOE_MAX_BLOCKS):
            @pl.when(n_blocks % MOE_MAX_BLOCKS == r)
            def _():
                body(base, r * MOE_ROWS)

    def one_hot(r0, rows):
        return slot_ref[pl.ds(e, 1), :] == lax.broadcasted_iota(jnp.int32, (rows, tm), 0) + r0

    def swiglu(x):
        return _swiglu_partial(x, w1_ref[0], w3_ref[0], w2_ref[0])

    last_j = pl.num_programs(2) - 1

    @pl.when(j == 0)
    def _():
        def gather_first(r0, rows):
            sel = jnp.where(one_hot(r0, rows), 1.0, 0.0).astype(BF16)
            x = jnp.dot(sel, xn_ref[...], preferred_element_type=F32).astype(BF16)
            xe_ref[pl.ds(r0, rows), :] = x
            ye_ref[pl.ds(r0, rows), :] = swiglu(x)
        for_row_blocks(gather_first)

    @pl.when((j > 0) & (j < last_j))
    def _():
        def middle(r0, rows):
            ye_ref[pl.ds(r0, rows), :] += swiglu(xe_ref[pl.ds(r0, rows), :])
        for_row_blocks(middle)

    @pl.when((j > 0) & (j == last_j))
    def _():
        def last_scatter(r0, rows):
            y = ye_ref[pl.ds(r0, rows), :] + swiglu(xe_ref[pl.ds(r0, rows), :])
            hot = one_hot(r0, rows)
            weight = jnp.sum(jnp.where(hot, comb_ref[pl.ds(e, 1), :], 0.0), axis=1, keepdims=True)
            o_ref[...] += lax.dot_general(jnp.where(hot, 1.0, 0.0).astype(BF16),
                                          (y * weight).astype(BF16),
                                          (((0,), (0,)), ((), ())), preferred_element_type=F32)
        for_row_blocks(last_scatter)

    @pl.when((e == ne - 1) & (j == pl.num_programs(2) - 1))
    def _():
        o_ref[...] = _ple(o_ref[...], p_ref, wpg_ref, wpp_ref)


def _moe(h, g, router, w1, w3, w2, p, wpg, wpp):
    t, d = h.shape
    ne, _, f = w1.shape
    tm, tf = MOE_TOKENS, MOE_COLS
    assert f % tf == 0 and f // tf >= 2
    row = lambda n: pl.BlockSpec((tm, n), lambda i, e, j: (i, 0))
    return pl.pallas_call(
        _moe_kernel,
        grid=(t // tm, ne, f // tf),
        in_specs=[row(d), _const_spec((1, d)), _const_spec((ne, d)),
                  pl.BlockSpec((1, d, tf), lambda i, e, j: (e, 0, j)),
                  pl.BlockSpec((1, d, tf), lambda i, e, j: (e, 0, j)),
                  pl.BlockSpec((1, tf, d), lambda i, e, j: (e, j, 0)),
                  row(p.shape[1]), _const_spec(wpg.shape), _const_spec(wpp.shape)],
        out_specs=row(d),
        out_shape=jax.ShapeDtypeStruct((t, d), F32),
        scratch_shapes=[pltpu.VMEM((tm, d), BF16), pltpu.VMEM((ne, tm), F32),
                        pltpu.VMEM((ne, tm), jnp.int32), pltpu.SMEM((ne,), jnp.int32),
                        pltpu.VMEM((tm, d), BF16), pltpu.VMEM((tm, d), F32)],
        compiler_params=_cparams(("parallel", "arbitrary", "arbitrary")),
        name="moe",
    )(h, g, router.T, w1, w3, w2, p, wpg, wpp)


def _final_norm_kernel(h_ref, g_ref, o_ref):
    o_ref[...] = _rms(h_ref[...], g_ref[...])


def _final_norm(h, g):
    t, d = h.shape
    tm = NORM_ROWS
    row = pl.BlockSpec((tm, d), lambda i: (i, 0))
    return pl.pallas_call(
        _final_norm_kernel, grid=(t // tm,),
        in_specs=[row, _const_spec((1, d))], out_specs=row,
        out_shape=jax.ShapeDtypeStruct((t, d), F32),
        compiler_params=_cparams(("parallel",)), name="final_norm",
    )(h, g)


ATT_HALF = 64
ATT_QB = 128
ATT_KB = ATT_QB + 2 * ATT_HALF
NEG = -1e30
assert all(w // (2 * d) == ATT_HALF for w, d in ATT_PATTERNS)


def _t5_bucket(rel):
    half = REL_BUCKETS // 2
    max_exact = half // 2
    n = np.abs(rel)
    large = max_exact + (np.log(np.maximum(n, 1) / max_exact) / math.log(REL_MAX_DIST / max_exact)
                         * (half - max_exact)).astype(np.int32)
    large = np.minimum(large, half - 1)
    return np.where(rel > 0, half, 0) + np.where(n < max_exact, n, large)


def _att_bias_tables(rel_bias, g, dil):
    i = np.arange(ATT_QB)[:, None]
    j = np.arange(ATT_KB)[None, :]
    rel = j - ATT_HALF - i
    band = np.abs(rel) <= ATT_HALF
    pick = np.eye(REL_BUCKETS, dtype=np.float32)[_t5_bucket(dil * rel)]
    heads = rel_bias[:, g * ATT_HEADS:(g + 1) * ATT_HEADS].astype(F32)
    bias = jnp.einsum('qkb,bh->hqk', pick, heads, precision=lax.Precision.HIGHEST)
    tables = []
    for v in range(4):
        ok = band
        if v & 1:
            ok = ok & (j >= ATT_HALF)
        if v & 2:
            ok = ok & (j < ATT_QB + ATT_HALF)
        tables.append(jnp.where(ok[None], bias, NEG))
    return jnp.stack(tables)


def _attn_kernel(q_ref, k_ref, v_ref, kp_ref, kn_ref, vp_ref, vn_ref, bias_ref, o_ref, lse_ref,
                 qbuf, kbuf, vbuf, obuf, lbuf, *, tq, n_blocks):
    flat = lambda ref: ref[0].reshape(-1, ref.shape[-1])
    qbuf[...] = flat(q_ref)
    kbuf[0:ATT_HALF] = flat(kp_ref)
    kbuf[ATT_HALF:ATT_HALF + tq] = flat(k_ref)
    kbuf[ATT_HALF + tq:] = flat(kn_ref)
    vbuf[0:ATT_HALF] = flat(vp_ref)
    vbuf[ATT_HALF:ATT_HALF + tq] = flat(v_ref)
    vbuf[ATT_HALF + tq:] = flat(vn_ref)
    nsb = tq // ATT_QB
    first = pl.program_id(2) * nsb
    lane = lax.broadcasted_iota(jnp.int32, (ATT_QB, LANES), 1)
    low = lane < ATT_HEAD_DIM
    lane_row = lax.broadcasted_iota(jnp.int32, (1, LANES), 1)
    keep = [(lane_row < ATT_HEAD_DIM).astype(BF16), (lane_row >= ATT_HEAD_DIM).astype(BF16)]
    ones = jnp.ones((ATT_KB, LANES), BF16)

    def block(sb, carry):
        r0 = pl.multiple_of(sb * ATT_QB, ATT_QB)
        gsb = first + sb
        variant = (gsb == 0).astype(jnp.int32) + 2 * (gsb == n_blocks - 1).astype(jnp.int32)
        q = qbuf[pl.ds(r0, ATT_QB), :] * (ATT_HEAD_DIM ** -0.5)
        lse_all = jnp.zeros((ATT_QB, LANES), F32)
        outs = []
        for pair in range(ATT_HEADS // 2):
            cols = slice(pair * LANES, (pair + 1) * LANES)
            qp = q[:, cols]
            kp = kbuf[pl.ds(r0, ATT_KB), cols]
            vp = jnp.concatenate([vbuf[pl.ds(r0, ATT_KB), cols], ones], axis=1)
            res = []
            for half in range(2):
                h = 2 * pair + half
                s = lax.dot_general(qp * keep[half], kp, (((1,), (1,)), ((), ())),
                                    preferred_element_type=F32)
                s = s + bias_ref[variant, h]
                m = jnp.max(s, axis=-1, keepdims=True)
                e = jnp.exp(s - m)
                pv = jnp.dot(e.astype(BF16), vp, preferred_element_type=F32)
                den = pv[:, LANES:]
                res.append(pv[:, :LANES] / den)
                lse_all = jnp.where(lane == h, m + jnp.log(den), lse_all)
            outs.append(jnp.where(low, res[0], res[1]))
        obuf[pl.ds(r0, ATT_QB), :] = jnp.concatenate(outs, axis=1).astype(obuf.dtype)
        lbuf[pl.ds(r0, ATT_QB), :] = lse_all
        return carry

    lax.fori_loop(0, nsb, block, 0, unroll=4)
    o_ref[0] = obuf[...].reshape(o_ref.shape[1:])
    lse_ref[0] = lbuf[...].reshape(lse_ref.shape[1:])


def _attention_group(qkv, bias_tables, dil, bsz, seq, name):
    sub_len = seq // dil
    assert sub_len % ATT_QB == 0 and seq % ATT_TILE == 0 and ATT_TILE % dil == 0
    rows = ATT_TILE // dil
    tq = min(SEQ_ROWS, sub_len)
    w = ATT_WIDTH
    hrows = min(rows, ATT_HALF)
    n_hb = sub_len // ATT_HALF

    def view(a, chunk):
        return a.reshape(bsz, (seq // ATT_TILE) * (rows // chunk), dil, chunk, a.shape[-1])

    def main(part):
        return pl.BlockSpec((1, tq // rows, None, rows, w), lambda b, r, n: (b, n, r, 0, part))

    def halo(part, nxt):
        if nxt:
            blk = lambda n: jnp.minimum((n + 1) * (tq // ATT_HALF), n_hb - 1)
        else:
            blk = lambda n: jnp.maximum(n * (tq // ATT_HALF) - 1, 0)
        return pl.BlockSpec((1, ATT_HALF // hrows, None, hrows, w),
                            lambda b, r, n: (b, blk(n), r, 0, part))

    mv, hv = view(qkv, rows), view(qkv, hrows)
    o, lse = pl.pallas_call(
        functools.partial(_attn_kernel, tq=tq, n_blocks=sub_len // ATT_QB),
        grid=(bsz, dil, sub_len // tq),
        in_specs=[main(0), main(1), main(2), halo(1, False), halo(1, True), halo(2, False),
                  halo(2, True), _const_spec(bias_tables.shape)],
        out_specs=[pl.BlockSpec((1, tq // rows, None, rows, w), lambda b, r, n: (b, n, r, 0, 0)),
                   pl.BlockSpec((1, tq // rows, None, rows, LANES), lambda b, r, n: (b, n, r, 0, 0))],
        out_shape=[jax.ShapeDtypeStruct((bsz, seq // ATT_TILE, dil, rows, w), BF16),
                   jax.ShapeDtypeStruct((bsz, seq // ATT_TILE, dil, rows, LANES), F32)],
        scratch_shapes=[pltpu.VMEM((tq, w), BF16),
                        pltpu.VMEM((tq + 2 * ATT_HALF, w), BF16),
                        pltpu.VMEM((tq + 2 * ATT_HALF, w), BF16),
                        pltpu.VMEM((tq, w), BF16), pltpu.VMEM((tq, LANES), F32)],
        compiler_params=_cparams(("parallel", "parallel", "parallel")),
        name=name,
    )(mv, mv, mv, hv, hv, hv, hv, bias_tables)
    return o.reshape(bsz * seq, w), lse.reshape(bsz * seq, LANES)


SUBLANES = 8
CONV_HALO = 16
CONV_ROWS = 64


def _conformer_kernel(u_ref, up_ref, un_ref, dw_ref, dwb_ref, lng_ref, lnb_ref, o_ref,
                      hp_ref, sh_ref, *, ts):
    n = pl.program_id(1)

    def glu(u):
        return u[:, :CONV_CH] * jax.nn.sigmoid(u[:, CONV_CH:])

    hp_ref[0:CONV_HALO] = jnp.where(n > 0, glu(up_ref[0]), 0.0)
    hp_ref[CONV_HALO:CONV_HALO + ts] = glu(u_ref[0])
    hp_ref[CONV_HALO + ts:] = jnp.where(n < pl.num_programs(1) - 1, glu(un_ref[0]), 0.0)
    span = ts + 2 * CONV_HALO - SUBLANES
    for b in range(1, SUBLANES):
        sh_ref[b - 1] = hp_ref[pl.ds(b, span), :]
    first = CONV_HALO - CONV_K // 2

    def rows(c, carry):
        r0 = pl.multiple_of(c * CONV_ROWS, CONV_ROWS)
        acc = jnp.broadcast_to(dwb_ref[...], (CONV_ROWS, CONV_CH))
        for k in range(CONV_K):
            a, b = divmod(first + k, SUBLANES)
            at = pl.ds(r0 + a * SUBLANES, CONV_ROWS)
            tap = hp_ref[at, :] if b == 0 else sh_ref[b - 1, at, :]
            acc = acc + dw_ref[pl.ds(k, 1), :] * tap
        mu = jnp.mean(acc, axis=-1, keepdims=True)
        cen = acc - mu
        var = jnp.mean(cen * cen, axis=-1, keepdims=True)
        y = cen * lax.rsqrt(var + EPS) * lng_ref[...] + lnb_ref[...]
        o_ref[0, pl.ds(r0, CONV_ROWS), :] = (y * jax.nn.sigmoid(y)).astype(o_ref.dtype)
        return carry

    lax.fori_loop(0, ts // CONV_ROWS, rows, 0, unroll=2)


def _conformer(u, dw, dw_b, ln_g, ln_b, bsz, seq):
    ts = min(SEQ_ROWS, seq)
    view = u.reshape(bsz, seq, 2 * CONV_CH)
    hb = ts // CONV_HALO
    n_hb = seq // CONV_HALO
    out = pl.pallas_call(
        functools.partial(_conformer_kernel, ts=ts),
        grid=(bsz, seq // ts),
        in_specs=[pl.BlockSpec((1, ts, 2 * CONV_CH), lambda b, n: (b, n, 0)),
                  pl.BlockSpec((1, CONV_HALO, 2 * CONV_CH),
                               lambda b, n: (b, jnp.maximum(n * hb - 1, 0), 0)),
                  pl.BlockSpec((1, CONV_HALO, 2 * CONV_CH),
                               lambda b, n: (b, jnp.minimum((n + 1) * hb, n_hb - 1), 0)),
                  _const_spec(dw.shape), _const_spec((1, CONV_CH)), _const_spec((1, CONV_CH)),
                  _const_spec((1, CONV_CH))],
        out_specs=pl.BlockSpec((1, ts, CONV_CH), lambda b, n: (b, n, 0)),
        out_shape=jax.ShapeDtypeStruct((bsz, seq, CONV_CH), BF16),
        scratch_shapes=[pltpu.VMEM((ts + 2 * CONV_HALO, CONV_CH), F32),
                        pltpu.VMEM((SUBLANES - 1, ts + 2 * CONV_HALO - SUBLANES, CONV_CH), F32)],
        compiler_params=_cparams(("parallel", "parallel")),
        name="conformer",
    )(view, view, view, dw, dw_b[None], ln_g[None], ln_b[None])
    return out.reshape(bsz * seq, CONV_CH)


SSD_HALO = SUBLANES
SSD_ROWS = 64
SSD_BC = SSM_GROUPS * SSM_STATE
HEADS_PER_GROUP = SSM_HEADS // SSM_GROUPS
GROUP_LANES = HEADS_PER_GROUP * SSM_HEAD_DIM
assert SSD_BC == LANES and SSM_CHUNK == LANES and 2 * SSM_HEADS <= LANES


def _ssd_pre_kernel(x_ref, xp_ref, xn_ref, dt_ref, cw_ref, cb_ref, dtb_ref, arow_ref, tri_ref,
                    ex_ref, xo_ref, dto_ref, acs_ref, nf_ref, nb_ref, cy_ref, hp_ref, sh_ref,
                    *, ts, phases):
    n = pl.program_id(1)
    hp_ref[0:SSD_HALO] = jnp.where(n > 0, xp_ref[0], 0.0)
    hp_ref[SSD_HALO:SSD_HALO + ts] = x_ref[0]
    hp_ref[SSD_HALO + ts:] = jnp.where(n < pl.num_programs(1) - 1, xn_ref[0], 0.0)
    for i, b in enumerate(phases):
        sh_ref[i] = hp_ref[pl.ds(b, ts + SSD_HALO), :]
    first = SSD_HALO - SSM_CONV // 2

    def rows(c, carry):
        r0 = pl.multiple_of(c * SSD_ROWS, SSD_ROWS)
        acc = jnp.broadcast_to(cb_ref[...], (SSD_ROWS, SSM_CONV_CH))
        for k in range(SSM_CONV):
            a, b = divmod(first + k, SUBLANES)
            acc = acc + cw_ref[pl.ds(k, 1), :] * sh_ref[phases.index(b),
                                                        pl.ds(r0 + a * SUBLANES, SSD_ROWS), :]
        xo_ref[0, pl.ds(r0, SSD_ROWS), :] = acc * jax.nn.sigmoid(acc)
        return carry

    lax.fori_loop(0, ts // SSD_ROWS, rows, 0, unroll=True)
    x = dt_ref[0] + dtb_ref[...]
    softplus = jnp.maximum(x, 0.0) + jnp.log1p(jnp.exp(-jnp.abs(x)))
    lane = lax.broadcasted_iota(jnp.int32, x.shape, 1)
    dtv = jnp.where(lane < 2 * SSM_HEADS, softplus, 0.0)
    dto_ref[0] = dtv

    nch = ts // SSM_CHUNK
    chunk = lambda k: slice(k * SSM_CHUNK, (k + 1) * SSM_CHUNK)
    dta = dtv * arow_ref[...]
    acs_all = _cumsum_both(jnp.concatenate([dta[chunk(k)] for k in range(nch)], axis=1), tri_ref)
    clane = lax.broadcasted_iota(jnp.int32, (SSM_CHUNK, DT_PAD), 1)
    ws, totals = [], [[], []]
    for k in range(nch):
        acs = acs_all[:, chunk(k)]
        acs_ref[0, chunk(k), :] = acs
        tot_f, tot_b = acs[SSM_CHUNK - 1:SSM_CHUNK, :], acs[0:1, :]
        to_end = jnp.where(clane < SSM_HEADS, tot_f - acs,
                           jnp.where(clane < 2 * SSM_HEADS, tot_b - acs, 0.0))
        ws.append(dtv[chunk(k)] * jnp.exp(to_end))
        totals[0].append(jnp.exp(tot_f))
        totals[1].append(jnp.exp(tot_b))
    w_all = jnp.concatenate(ws, axis=0)
    xs = xo_ref[0, :, :SSM_INNER]
    hl = lax.broadcasted_iota(jnp.int32, (SSM_STATE, SSM_INNER), 1)
    pad_rows = jnp.zeros((SUBLANES - nch % SUBLANES, DT_PAD), F32)
    for d, new_ref in enumerate((nf_ref, nb_ref)):
        xw = (xs * _expand(w_all, ex_ref, d)).astype(BF16)
        carry = _expand(jnp.concatenate(totals[d] + [pad_rows], axis=0), ex_ref, d)
        for k in range(nch):
            bmat = xo_ref[0, chunk(k), SSM_INNER:SSM_INNER + SSD_BC].astype(BF16)
            full = lax.dot_general(bmat, xw[chunk(k)], (((0,), (0,)), ((), ())),
                                   preferred_element_type=F32)
            new = full[:SSM_STATE]
            for g in range(1, SSM_GROUPS):
                new = jnp.where(hl >= g * GROUP_LANES, full[g * SSM_STATE:(g + 1) * SSM_STATE], new)
            new_ref[0, k] = new
            cy_ref[0, k, d:d + 1, :] = carry[k:k + 1]


def _ssd_pre(xbc, dt_raw, conv_w, conv_b, dt_bias, consts, bsz, seq):
    ts = min(SEQ_ROWS, seq)
    nc = seq // SSM_CHUNK
    nch = ts // SSM_CHUNK
    a_row, tri, expand = consts
    new = jax.ShapeDtypeStruct((bsz, nc, SSM_STATE, SSM_INNER), F32)
    new_spec = pl.BlockSpec((1, nch, SSM_STATE, SSM_INNER), lambda b, n: (b, n, 0, 0))
    xv = xbc.reshape(bsz, seq, SSM_CONV_CH)
    dv = dt_raw.reshape(bsz, seq, DT_PAD)
    hb = ts // SSD_HALO
    n_hb = seq // SSD_HALO
    first = SSD_HALO - SSM_CONV // 2
    phases = tuple(sorted({(first + k) % SUBLANES for k in range(SSM_CONV)}))
    dtb = jnp.zeros((1, DT_PAD), F32).at[0, :2 * SSM_HEADS].set(dt_bias.reshape(-1))
    return pl.pallas_call(
        functools.partial(_ssd_pre_kernel, ts=ts, phases=phases),
        grid=(bsz, seq // ts),
        in_specs=[pl.BlockSpec((1, ts, SSM_CONV_CH), lambda b, n: (b, n, 0)),
                  pl.BlockSpec((1, SSD_HALO, SSM_CONV_CH),
                               lambda b, n: (b, jnp.maximum(n * hb - 1, 0), 0)),
                  pl.BlockSpec((1, SSD_HALO, SSM_CONV_CH),
                               lambda b, n: (b, jnp.minimum((n + 1) * hb, n_hb - 1), 0)),
                  pl.BlockSpec((1, ts, DT_PAD), lambda b, n: (b, n, 0)),
                  _const_spec(conv_w.shape), _const_spec((1, SSM_CONV_CH)),
                  _const_spec((1, DT_PAD)), _const_spec(a_row.shape), _const_spec(tri.shape),
                  _const_spec(expand.shape)],
        out_specs=[pl.BlockSpec((1, ts, SSM_CONV_CH), lambda b, n: (b, n, 0)),
                   pl.BlockSpec((1, ts, DT_PAD), lambda b, n: (b, n, 0)),
                   pl.BlockSpec((1, ts, DT_PAD), lambda b, n: (b, n, 0)),
                   new_spec, new_spec,
                   pl.BlockSpec((1, nch, 2, SSM_INNER), lambda b, n: (b, n, 0, 0))],
        out_shape=[jax.ShapeDtypeStruct((bsz, seq, SSM_CONV_CH), F32),
                   jax.ShapeDtypeStruct((bsz, seq, DT_PAD), F32),
                   jax.ShapeDtypeStruct((bsz, seq, DT_PAD), F32),
                   new, new, jax.ShapeDtypeStruct((bsz, nc, 2, SSM_INNER), F32)],
        scratch_shapes=[pltpu.VMEM((ts + 2 * SSD_HALO, SSM_CONV_CH), F32),
                        pltpu.VMEM((len(phases), ts + SSD_HALO, SSM_CONV_CH), F32)],
        compiler_params=_cparams(("parallel", "parallel")),
        name="ssd_pre",
    )(xv, xv, xv, dv, conv_w, conv_b[None], dtb, a_row, tri, expand)


def _ssd_consts(a_log):
    a_row = jnp.zeros((1, DT_PAD), F32).at[0, :2 * SSM_HEADS].set(-jnp.exp(a_log.reshape(-1)))
    lower = np.tril(np.ones((SSM_CHUNK, SSM_CHUNK), np.float32))
    tri = jnp.asarray(np.concatenate([lower, lower.T], axis=0), BF16)
    expand = np.zeros((2, DT_PAD, SSM_INNER), np.float32)
    for d in range(2):
        for h in range(SSM_HEADS):
            expand[d, d * SSM_HEADS + h, h * SSM_HEAD_DIM:(h + 1) * SSM_HEAD_DIM] = 1.0
    return a_row, tri, jnp.asarray(expand, BF16)


def _cumsum_both(dta, tri_ref):
    hi = dta.astype(BF16)
    r1 = dta - hi.astype(F32)
    mid = r1.astype(BF16)
    lo = (r1 - mid.astype(F32)).astype(BF16)
    both = (jnp.dot(tri_ref[...], hi, preferred_element_type=F32)
            + jnp.dot(tri_ref[...], mid, preferred_element_type=F32)
            + jnp.dot(tri_ref[...], lo, preferred_element_type=F32))
    lane = lax.broadcasted_iota(jnp.int32, dta.shape, 1) % DT_PAD
    return jnp.where(lane < SSM_HEADS, both[:SSM_CHUNK], both[SSM_CHUNK:])


def _expand(v, ex_ref, d):
    return _split_dot(v, ex_ref[d])


def _ssd_scan_kernel(nf_ref, nb_ref, cf_ref, cb_ref, pf_ref, pb_ref, sf_ref, sb_ref, *, nch):
    @pl.when(pl.program_id(1) == 0)
    def _():
        sf_ref[...] = jnp.zeros_like(sf_ref)
        sb_ref[...] = jnp.zeros_like(sb_ref)

    def scan(d, order, new_ref, carry_ref, st_ref, out_ref):
        st = st_ref[...]
        for k in order:
            out_ref[0, k] = st.astype(out_ref.dtype)
            st = st * carry_ref[0, k, d:d + 1, :] + new_ref[0, k]
        st_ref[...] = st

    scan(0, range(nch), nf_ref, cf_ref, sf_ref, pf_ref)
    scan(1, range(nch - 1, -1, -1), nb_ref, cb_ref, sb_ref, pb_ref)


SSD_SCAN_CHUNKS = 16


def _ssd_states(new_f, new_b, carry, bsz, seq):
    nc = seq // SSM_CHUNK
    nch = min(SSD_SCAN_CHUNKS, nc)
    steps = nc // nch
    fwd = lambda b, c: (b, c, 0, 0)
    bwd = lambda b, c: (b, steps - 1 - c, 0, 0)
    st = jax.ShapeDtypeStruct((bsz, nc, SSM_STATE, SSM_INNER), BF16)
    blk = (1, nch, SSM_STATE, SSM_INNER)
    return pl.pallas_call(
        functools.partial(_ssd_scan_kernel, nch=nch),
        grid=(bsz, steps),
        in_specs=[pl.BlockSpec(blk, fwd), pl.BlockSpec(blk, bwd),
                  pl.BlockSpec((1, nch, 2, SSM_INNER), fwd),
                  pl.BlockSpec((1, nch, 2, SSM_INNER), bwd)],
        out_specs=[pl.BlockSpec(blk, fwd), pl.BlockSpec(blk, bwd)],
        out_shape=[st, st],
        scratch_shapes=[pltpu.VMEM((SSM_STATE, SSM_INNER), F32),
                        pltpu.VMEM((SSM_STATE, SSM_INNER), F32)],
        compiler_params=_cparams(("parallel", "arbitrary")),
        name="ssd_scan",
    )(new_f, new_b, carry, carry)


def _ssd_out_kernel(x_ref, dt_ref, acs_ref, z_ref, pf_ref, pb_ref, ex_ref, dskip_ref, ng_ref,
                    o_ref, *, nck):
    for k in range(nck):
        _ssd_out_chunk(k, x_ref, dt_ref, acs_ref, z_ref, pf_ref, pb_ref, ex_ref, dskip_ref,
                       ng_ref, o_ref)


def _ssd_out_chunk(k, x_ref, dt_ref, acs_ref, z_ref, pf_ref, pb_ref, ex_ref, dskip_ref, ng_ref,
                   o_ref):
    rows = slice(k * SSM_CHUNK, (k + 1) * SSM_CHUNK)
    xs = x_ref[0, rows, :SSM_INNER]
    bmat = x_ref[0, rows, SSM_INNER:SSM_INNER + SSD_BC].astype(BF16)
    cmat = x_ref[0, rows, SSM_INNER + SSD_BC:].astype(BF16)
    dtv = dt_ref[0, rows, :]
    acs = acs_ref[0, rows, :]
    acs_t = acs.T
    dt_t = dtv.T
    eacs = jnp.exp(acs)
    row = lax.broadcasted_iota(jnp.int32, (SSM_CHUNK, SSM_CHUNK), 0)
    col = lax.broadcasted_iota(jnp.int32, (SSM_CHUNK, SSM_CHUNK), 1)
    past, now = col < row, col == row
    low = lax.broadcasted_iota(jnp.int32, (SSM_CHUNK, LANES), 1) < SSM_HEAD_DIM
    glane = lax.broadcasted_iota(jnp.int32, (1, SSD_BC), 1) // SSM_STATE
    scores = [lax.dot_general(cmat * (glane == g).astype(BF16), bmat,
                              (((1,), (1,)), ((), ())), preferred_element_type=F32)
              for g in range(SSM_GROUPS)]
    xs_b = xs.astype(BF16)
    keep = [low.astype(BF16), (~low).astype(BF16)]
    diag = []
    for pair in range(SSM_HEADS // 2):
        xp = xs_b[:, pair * LANES:(pair + 1) * LANES]
        mats = []
        for half in range(2):
            f = 2 * pair + half
            b = SSM_HEADS + f
            seg = jnp.where(past | now, acs[:, f:f + 1] - acs_t[f:f + 1, :],
                            acs[:, b:b + 1] - acs_t[b:b + 1, :])
            dts = (jnp.where(past, dt_t[f:f + 1, :], dt_t[b:b + 1, :])
                   + jnp.where(now, dt_t[f:f + 1, :], 0.0))
            mats.append((scores[f // HEADS_PER_GROUP] * jnp.exp(seg) * dts).astype(BF16))
        diag.append(jnp.dot(jnp.concatenate(mats, axis=1),
                            jnp.concatenate([xp * keep[0], xp * keep[1]], axis=0),
                            preferred_element_type=F32))
    y = dskip_ref[...] * xs + jnp.concatenate(diag, axis=1)
    hgroup = lax.broadcasted_iota(jnp.int32, (1, SSM_INNER), 1) // GROUP_LANES
    for d, p_ref in enumerate((pf_ref, pb_ref)):
        prev = p_ref[0, k]
        stacked = jnp.concatenate([prev * (hgroup == g).astype(BF16) for g in range(SSM_GROUPS)],
                                  axis=0)
        off = jnp.dot(cmat, stacked, preferred_element_type=F32)
        y = y + off * _expand(eacs, ex_ref, d)
    z = z_ref[0, rows, :]
    y = y * (z * jax.nn.sigmoid(z))
    o_ref[0, rows, :] = _rms(y, ng_ref[...]).astype(o_ref.dtype)


SSD_OUT_CHUNKS = 4


def _ssd_out(xact, dtv, acs, z, prev_f, prev_b, expand, d_skip, norm_g, bsz, seq):
    nc = seq // SSM_CHUNK
    nck = min(SSD_OUT_CHUNKS, nc)
    dsk = jnp.repeat(d_skip, SSM_HEAD_DIM)[None]
    chunk = lambda n: pl.BlockSpec((1, nck * SSM_CHUNK, n), lambda b, c: (b, c, 0))
    state = pl.BlockSpec((1, nck, SSM_STATE, SSM_INNER), lambda b, c: (b, c, 0, 0))
    out = pl.pallas_call(
        functools.partial(_ssd_out_kernel, nck=nck),
        grid=(bsz, nc // nck),
        in_specs=[chunk(SSM_CONV_CH), chunk(DT_PAD), chunk(DT_PAD), chunk(SSM_INNER), state, state,
                  _const_spec(expand.shape), _const_spec(dsk.shape), _const_spec((1, SSM_INNER))],
        out_specs=chunk(SSM_INNER),
        out_shape=jax.ShapeDtypeStruct((bsz, seq, SSM_INNER), BF16),
        compiler_params=_cparams(("parallel", "parallel")),
        name="ssd_out",
    )(xact, dtv, acs, z.reshape(bsz, seq, SSM_INNER), prev_f, prev_b, expand, dsk, norm_g[None])
    return out.reshape(bsz * seq, SSM_INNER)


def _ssd(z, xbc, dt_raw, conv_w, conv_b, a_log, dt_bias, d_skip, norm_g, bsz, seq):
    consts = _ssd_consts(a_log)
    xact, dtv, acs, new_f, new_b, carry = _ssd_pre(xbc, dt_raw, conv_w, conv_b, dt_bias, consts,
                                                   bsz, seq)
    prev_f, prev_b = _ssd_states(new_f, new_b, carry, bsz, seq)
    return _ssd_out(xact, dtv, acs, z, prev_f, prev_b, consts[2], d_skip, norm_g, bsz, seq)


FNET_COLS = 4096


def _dft_cos_sin(n):
    ang = 2.0 * np.pi * np.outer(np.arange(n), np.arange(n)) / n
    return np.cos(ang), np.sin(ang)


def _fnet_consts(seq):
    c = FNET_GROUP_DIM
    n1 = seq // LANES
    c1, s1 = _dft_cos_sin(n1)
    stage1 = np.concatenate([c1, -s1], axis=0)
    ang = 2.0 * np.pi * np.outer(np.arange(n1), np.arange(LANES)) / seq
    twr = np.repeat(np.cos(ang), c, axis=1)
    twi = np.repeat(-np.sin(ang), c, axis=1)
    cc, sc = _dft_cos_sin(c)
    chan = np.block([[cc, -sc], [sc, cc]])
    c2, s2 = _dft_cos_sin(LANES)
    return (jnp.asarray(stage1, BF16), jnp.asarray(twr, F32), jnp.asarray(twi, F32),
            jnp.asarray(chan, BF16), jnp.asarray(c2, BF16), jnp.asarray(s2, BF16))


def _fnet1_kernel(x_ref, f_ref, twr_ref, twi_ref, o_ref, *, n1):
    c = FNET_GROUP_DIM
    a = jnp.dot(f_ref[...], x_ref[0].astype(BF16), preferred_element_type=F32)
    ar, ai = a[:n1], a[n1:]
    twr, twi = twr_ref[...], twi_ref[...]
    re = (ar * twr - ai * twi).astype(o_ref.dtype)
    im = (ar * twi + ai * twr).astype(o_ref.dtype)
    for j in range(re.shape[1] // c):
        o_ref[0, :, (2 * j) * c:(2 * j + 1) * c] = re[:, j * c:(j + 1) * c]
        o_ref[0, :, (2 * j + 1) * c:(2 * j + 2) * c] = im[:, j * c:(j + 1) * c]


def _fnet2_kernel(a_ref, chan_ref, c2_ref, s2_ref, o_ref, g_ref, scr_ref, *, n1, scale):
    c = FNET_GROUP_DIM
    pitch = scr_ref.shape[0] // LANES
    per = min(8, n1)
    for i in range(n1 // per):
        blk = a_ref[0, i * per:(i + 1) * per].reshape(per * LANES, 2 * c)
        g = jnp.dot(blk, chan_ref[...], preferred_element_type=F32).astype(BF16)
        g_ref[i * per:(i + 1) * per] = g.reshape(per, LANES, 2 * c)

    def body(k1, carry):
        g = g_ref[k1]
        y = (jnp.dot(c2_ref[...], g[:, :c], preferred_element_type=F32)
             + jnp.dot(s2_ref[...], g[:, c:], preferred_element_type=F32))
        scr_ref[pl.ds(k1, LANES, stride=pitch), :] = y * scale
        return carry

    lax.fori_loop(0, n1, body, 0, unroll=4)

    def compact(k2, carry):
        src = pl.multiple_of(k2 * pitch, SUBLANES)
        dst = pl.multiple_of(k2 * n1, n1)
        o_ref[0, pl.ds(dst, n1), :] = scr_ref[pl.ds(src, n1), :].astype(o_ref.dtype)
        return carry

    lax.fori_loop(0, LANES, compact, 0, unroll=8)


def _fourier(fn, bsz, seq):
    c = FNET_GROUP_DIM
    assert c == LANES and seq % LANES == 0
    n1 = seq // LANES
    stage1, twr, twi, chan, c2, s2 = _fnet_consts(seq)
    ncols = LANES * c
    nb = min(FNET_COLS, ncols)
    x2 = fn.reshape(bsz * FNET_GROUPS, n1, ncols)
    a = pl.pallas_call(
        functools.partial(_fnet1_kernel, n1=n1),
        grid=(ncols // nb, bsz * FNET_GROUPS),
        in_specs=[pl.BlockSpec((1, n1, nb), lambda j, i: (i, 0, j)),
                  _const_spec(stage1.shape),
                  pl.BlockSpec((n1, nb), lambda j, i: (0, j)),
                  pl.BlockSpec((n1, nb), lambda j, i: (0, j))],
        out_specs=pl.BlockSpec((1, n1, 2 * nb), lambda j, i: (i, 0, j)),
        out_shape=jax.ShapeDtypeStruct((bsz * FNET_GROUPS, n1, 2 * ncols), BF16),
        compiler_params=_cparams(("parallel", "parallel")),
        name="fnet1",
    )(x2, stage1, twr, twi)
    a4 = a.reshape(bsz * FNET_GROUPS, n1, LANES, 2 * c)
    out = pl.pallas_call(
        functools.partial(_fnet2_kernel, n1=n1, scale=1.0 / math.sqrt(seq * c)),
        grid=(bsz, FNET_GROUPS),
        in_specs=[pl.BlockSpec((1, n1, LANES, 2 * c), lambda b, g: (b * FNET_GROUPS + g, 0, 0, 0)),
                  _const_spec(chan.shape), _const_spec(c2.shape), _const_spec(s2.shape)],
        out_specs=pl.BlockSpec((1, seq, c), lambda b, g: (b, 0, g)),
        out_shape=jax.ShapeDtypeStruct((bsz, seq, FNET_WIDTH), BF16),
        scratch_shapes=[pltpu.VMEM((n1, LANES, 2 * c), BF16),
                        pltpu.VMEM((LANES * (n1 + SUBLANES), c), F32)],
        compiler_params=_cparams(("parallel", "parallel")),
        name="fnet2",
    )(a4, chan, c2, s2)
    return out.reshape(bsz * seq, FNET_WIDTH)


def kernel(x, p, rel_bias, norm_mix, w_in, conv_dw, conv_dw_b, conv_ln_g, conv_ln_b, conv_out,
           ssm_conv_w, ssm_conv_b, ssm_a_log, ssm_dt_bias, ssm_d, ssm_norm, ssm_out,
           attn_out, fnet_out, w_gate, b_gate, w_out, norm_ffn, ffn_w1, ffn_w3, ffn_w2,
           moe_router, moe_w1, moe_w3, moe_w2, ple_gate, ple_proj, final_norm):
    bsz, seq, d = x.shape
    depth = w_in.shape[0]
    t = bsz * seq
    h = x.reshape(t, d)
    bias_tables = [_att_bias_tables(rel_bias, g, dil) for g, (_, dil) in enumerate(ATT_PATTERNS)]
    for l in range(depth):
        *qkv, conv_u, z, xbc, fn, dt = _inproj(h, norm_mix[l][None], _reorder_w_in(w_in[l]), bsz, seq)
        att = [_attention_group(qkv[g], bias_tables[g], dil, bsz, seq, f"attn{g}")
               for g, (_, dil) in enumerate(ATT_PATTERNS)]
        cnf = _conformer(conv_u, conv_dw[l], conv_dw_b[l], conv_ln_g[l], conv_ln_b[l], bsz, seq)
        ssd = _ssd(z, xbc, dt, ssm_conv_w[l], ssm_conv_b[l], ssm_a_log[l], ssm_dt_bias[l],
                   ssm_d[l], ssm_norm[l], bsz, seq)
        fnt = _fourier(fn, bsz, seq)
        wbr = jnp.stack([attn_out[l], conv_out[l], ssm_out[l], fnet_out[l]]).astype(BF16)
        h = _mix(h, norm_mix[l][None], att, (cnf, ssd, fnt), wbr, w_gate[l].astype(BF16),
                 b_gate[l][:, None, :], w_out[l].astype(BF16))
        pl_in = p[l].reshape(t, -1)
        wpg, wpp = ple_gate[l].astype(BF16), ple_proj[l].astype(BF16)
        i = l // 2
        if l % 2 == 0:
            h = _ffn(h, norm_ffn[l][None], ffn_w1[i].astype(BF16), ffn_w3[i].astype(BF16),
                     ffn_w2[i].astype(BF16), pl_in, wpg, wpp)
        else:
            h = _moe(h, norm_ffn[l][None], moe_router[i], moe_w1[i].astype(BF16),
                     moe_w3[i].astype(BF16), moe_w2[i].astype(BF16), pl_in, wpg, wpp)
    return _final_norm(h, final_norm[None]).reshape(bsz, seq, d)
```

```python
import functools
import math

import numpy as np
import jax
import jax.numpy as jnp
from jax import lax
from jax.experimental import pallas as pl
from jax.experimental.pallas import tpu as pltpu

F32 = jnp.float32
BF16 = jnp.bfloat16
HI = lax.Precision.HIGHEST

EPS = 1e-6
N_BRANCHES = 4
CONV_CH = 512
CONV_K = 31
SSM_HEADS = 8
SSM_HEAD_DIM = 64
SSM_INNER = SSM_HEADS * SSM_HEAD_DIM
SSM_GROUPS = 2
SSM_STATE = 64
SSM_CONV = 5
SSM_CONV_CH = SSM_INNER + 2 * SSM_GROUPS * SSM_STATE
SSM_CHUNK = 128
ATT_PATTERNS = ((128, 1), (512, 4), (2048, 16))
ATT_GROUPS = len(ATT_PATTERNS)
ATT_HEADS = 8
ATT_HEAD_DIM = 64
ATT_WIDTH = ATT_HEADS * ATT_HEAD_DIM
REL_BUCKETS = 32
REL_MAX_DIST = 1024
FNET_GROUPS = 4
FNET_GROUP_DIM = 128
FNET_WIDTH = FNET_GROUPS * FNET_GROUP_DIM
N_EXPERTS = 8
TOP_K = 2

ATT_IN_COLS = 3 * ATT_GROUPS * ATT_WIDTH
CONV_IN_COLS = 2 * CONV_CH
SSM_IN_COLS = SSM_INNER + SSM_CONV_CH + 2 * SSM_HEADS
OFF_CONV = ATT_IN_COLS
OFF_SSM = OFF_CONV + CONV_IN_COLS
OFF_FNET = OFF_SSM + SSM_IN_COLS

LANES = 128
DT_PAD = LANES
V7X_VMEM_BYTES = 64 * 1024 * 1024
VMEM_LIMIT = V7X_VMEM_BYTES - 8 * 1024 * 1024

MIX_ROWS = 512
FFN_ROWS, FFN_COLS = 512, 1536
MOE_TOKENS, MOE_COLS = 1024, 1792
NORM_ROWS = 1024
SEQ_ROWS = 512


def _cparams(sem):
    return pltpu.CompilerParams(dimension_semantics=sem, vmem_limit_bytes=VMEM_LIMIT)


def _const_spec(shape):
    nd = len(shape)
    return pl.BlockSpec(shape, lambda *_: (0,) * nd, pipeline_mode=pl.Buffered(1))


def _rms(x, g):
    return x * lax.rsqrt(jnp.mean(x * x, axis=-1, keepdims=True) + EPS) * g


_SEC_QKV = (0, ATT_IN_COLS)
_SEC_CONV = (_SEC_QKV[0] + _SEC_QKV[1], CONV_IN_COLS)
_SEC_Z = (_SEC_CONV[0] + _SEC_CONV[1], SSM_INNER)
_SEC_XBC = (_SEC_Z[0] + _SEC_Z[1], SSM_CONV_CH)
_SEC_FNET = (_SEC_XBC[0] + _SEC_XBC[1], FNET_WIDTH)
_SEC_DT = (_SEC_FNET[0] + _SEC_FNET[1], DT_PAD)
_IN_COLS_PAD = _SEC_DT[0] + _SEC_DT[1]
_MM_CHUNK = 512


ATT_TILE = 256
QKV_COLS = 3 * ATT_WIDTH
INPROJ_ROWS = 2 * ATT_TILE


def _reorder_w_in(w):
    d = w.shape[0]
    qkv = w[:, :OFF_CONV].reshape(d, 3, ATT_GROUPS, ATT_WIDTH).transpose(0, 2, 1, 3)
    ssm = w[:, OFF_SSM:OFF_FNET]
    dt = ssm[:, SSM_INNER + SSM_CONV_CH:]
    parts = [qkv.reshape(d, OFF_CONV), w[:, OFF_CONV:OFF_SSM], ssm[:, :SSM_INNER],
             ssm[:, SSM_INNER:SSM_INNER + SSM_CONV_CH], w[:, OFF_FNET:],
             dt, jnp.zeros((d, DT_PAD - dt.shape[1]), w.dtype)]
    return jnp.concatenate(parts, axis=1).astype(BF16)


def _deinterleave_matrix(dil):
    s = np.arange(ATT_TILE)
    m = np.zeros((ATT_TILE, ATT_TILE), np.float32)
    m[(s % dil) * (ATT_TILE // dil) + s // dil, s] = 1.0
    return m


def _inproj_kernel(h_ref, hp_ref, hn_ref, g_ref, w_ref, perm_ref, dw_ref, dwb_ref, lng_ref,
                   lnb_ref, q0_ref, q1_ref, q2_ref, cnf_ref, z_ref, xbc_ref, fn_ref, dt_ref,
                   pad_ref, sh_ref, *, spt):
    tm = h_ref.shape[0]
    pos = pl.program_id(0) % spt
    norm = lambda ref: _rms(ref[...], g_ref[...]).astype(BF16)
    xn = norm(h_ref)

    xe = jnp.concatenate([norm(hp_ref), xn, norm(hn_ref)], axis=0)
    c0 = _SEC_CONV[0]
    lin = jnp.dot(xe, w_ref[:, c0:c0 + CONV_CH], preferred_element_type=F32)
    gate = jnp.dot(xe, w_ref[:, c0 + CONV_CH:c0 + 2 * CONV_CH], preferred_element_type=F32)
    hid = lin * jax.nn.sigmoid(gate)
    pad_ref[0:CONV_HALO] = jnp.where(pos > 0, hid[:CONV_HALO], 0.0)
    pad_ref[CONV_HALO:CONV_HALO + tm] = hid[CONV_HALO:CONV_HALO + tm]
    pad_ref[CONV_HALO + tm:] = jnp.where(pos < spt - 1, hid[CONV_HALO + tm:], 0.0)
    span = tm + 2 * CONV_HALO - SUBLANES
    for b in range(1, SUBLANES):
        sh_ref[b - 1] = pad_ref[pl.ds(b, span), :]
    first = CONV_HALO - CONV_K // 2

    def conv_rows(c):
        acc = jnp.broadcast_to(dwb_ref[...], (CONV_ROWS, CONV_CH))
        for k in range(CONV_K):
            a, b = divmod(first + k, SUBLANES)
            at = pl.ds(c * CONV_ROWS + a * SUBLANES, CONV_ROWS)
            tap = pad_ref[at, :] if b == 0 else sh_ref[b - 1, at, :]
            acc = acc + dw_ref[pl.ds(k, 1), :] * tap
        mu = jnp.mean(acc, axis=-1, keepdims=True)
        cen = acc - mu
        var = jnp.mean(cen * cen, axis=-1, keepdims=True)
        y = cen * lax.rsqrt(var + EPS) * lng_ref[...] + lnb_ref[...]
        cnf_ref[pl.ds(c * CONV_ROWS, CONV_ROWS), :] = (y * jax.nn.sigmoid(y)).astype(cnf_ref.dtype)

    pending = list(range(tm // CONV_ROWS))

    def section(x, sec, store):
        start, width = sec
        for c in range(0, width, _MM_CHUNK):
            cw = min(_MM_CHUNK, width - c)
            store(c, cw, jnp.dot(x, w_ref[:, start + c:start + c + cw],
                                 preferred_element_type=F32))
            if pending:
                conv_rows(pending.pop(0))

    def to(ref):
        def store(c, cw, val):
            ref[:, c:c + cw] = val.astype(ref.dtype)
        return store

    def to_fnet(c, cw, val):
        for g in range(cw // FNET_GROUP_DIM):
            fn_ref[0, c // FNET_GROUP_DIM + g] = val[:, g * FNET_GROUP_DIM:(g + 1) * FNET_GROUP_DIM]

    for g, q_ref in enumerate((q0_ref, q1_ref, q2_ref)):
        x = xn
        if ATT_PATTERNS[g][1] > 1:
            x = jnp.concatenate(
                [jnp.dot(perm_ref[g], xn[i * ATT_TILE:(i + 1) * ATT_TILE],
                         preferred_element_type=F32) for i in range(xn.shape[0] // ATT_TILE)],
                axis=0).astype(BF16)
        section(x, (g * QKV_COLS, QKV_COLS), to(q_ref))
    section(xn, _SEC_Z, to(z_ref))
    section(xn, _SEC_XBC, to(xbc_ref))
    section(xn, _SEC_FNET, to_fnet)
    section(xn, _SEC_DT, to(dt_ref))
    while pending:
        conv_rows(pending.pop(0))


def _inproj(h, g, w, conv, bsz, seq):
    t, d = h.shape
    tm = INPROJ_ROWS
    spt = seq // tm
    dw, dw_b, ln_g, ln_b = conv
    perm = jnp.asarray(np.stack([_deinterleave_matrix(dil) for _, dil in ATT_PATTERNS]), BF16)
    row = lambda n: pl.BlockSpec((tm, n), lambda i: (i, 0))
    hb = tm // CONV_HALO
    n_hb = t // CONV_HALO
    qkv = jax.ShapeDtypeStruct((t, QKV_COLS), BF16)
    vec = _const_spec((1, CONV_CH))
    return pl.pallas_call(
        functools.partial(_inproj_kernel, spt=spt),
        grid=(t // tm,),
        in_specs=[row(d),
                  pl.BlockSpec((CONV_HALO, d), lambda i: (jnp.maximum(i * hb - 1, 0), 0)),
                  pl.BlockSpec((CONV_HALO, d), lambda i: (jnp.minimum((i + 1) * hb, n_hb - 1), 0)),
                  _const_spec((1, d)), _const_spec(w.shape), _const_spec(perm.shape),
                  _const_spec(dw.shape), vec, vec, vec],
        out_specs=[row(QKV_COLS), row(QKV_COLS), row(QKV_COLS), row(CONV_CH),
                   row(SSM_INNER), row(SSM_CONV_CH),
                   pl.BlockSpec((1, FNET_GROUPS, tm, FNET_GROUP_DIM),
                                lambda i: (i // spt, 0, i % spt, 0)),
                   row(DT_PAD)],
        out_shape=[qkv, qkv, qkv,
                   jax.ShapeDtypeStruct((t, CONV_CH), BF16),
                   jax.ShapeDtypeStruct((t, SSM_INNER), F32),
                   jax.ShapeDtypeStruct((t, SSM_CONV_CH), F32),
                   jax.ShapeDtypeStruct((bsz, FNET_GROUPS, seq, FNET_GROUP_DIM), F32),
                   jax.ShapeDtypeStruct((t, DT_PAD), F32)],
        scratch_shapes=[pltpu.VMEM((tm + 2 * CONV_HALO, CONV_CH), F32),
                        pltpu.VMEM((SUBLANES - 1, tm + 2 * CONV_HALO - SUBLANES, CONV_CH), F32)],
        compiler_params=_cparams(("parallel",)),
        name="inproj",
    )(h, h, h, g, w, perm, dw, dw_b[None], ln_g[None], ln_b[None])


def _split_dot(v, m):
    hi = v.astype(BF16)
    lo = (v - hi.astype(F32)).astype(BF16)
    return (jnp.dot(hi, m, preferred_element_type=F32) + jnp.dot(lo, m, preferred_element_type=F32))


def _interleave(pt, v):
    n = v.shape[1]
    if v.dtype != BF16:
        hi = v.astype(BF16)
        v = jnp.concatenate([hi, (v - hi.astype(F32)).astype(BF16)], axis=1)
    tiles = []
    for i in range(v.shape[0] // ATT_TILE):
        r = jnp.dot(pt, v[i * ATT_TILE:(i + 1) * ATT_TILE], preferred_element_type=F32)
        tiles.append(r if r.shape[1] == n else r[:, :n] + r[:, n:])
    return jnp.concatenate(tiles, axis=0)


def _mix_kernel(h_ref, g_ref, o0_ref, o1_ref, o2_ref, l0_ref, l1_ref, l2_ref, b1_ref, b2_ref,
                b3_ref, pt_ref, hx_ref, wbr_ref, wg_ref, cg_ref, wo_ref, o_ref):
    h = h_ref[...]
    xn = _rms(h, g_ref[...]).astype(BF16)
    outs, lses = [], []
    for g, (og_ref, lg_ref) in enumerate(zip((o0_ref, o1_ref, o2_ref), (l0_ref, l1_ref, l2_ref))):
        if ATT_PATTERNS[g][1] > 1:
            outs.append(_interleave(pt_ref[g], og_ref[...]))
            lses.append(_interleave(pt_ref[g], lg_ref[...]))
        else:
            outs.append(og_ref[...].astype(F32))
            lses.append(lg_ref[...])
    top = jnp.maximum(jnp.maximum(lses[0], lses[1]), lses[2])
    es = [jnp.exp(l - top) for l in lses]
    inv = 1.0 / (es[0] + es[1] + es[2])
    att = None
    for e, og in zip(es, outs):
        term = og * jnp.dot((e * inv).astype(BF16), hx_ref[...], preferred_element_type=F32)
        att = term if att is None else att + term
    acc = None
    for b, hid in enumerate((att.astype(BF16), b1_ref[...], b2_ref[...], b3_ref[...])):
        gate = jax.nn.sigmoid(jnp.dot(xn, wg_ref[b], preferred_element_type=F32) + cg_ref[b])
        br = jnp.dot(hid, wbr_ref[b], preferred_element_type=F32)
        acc = gate * br if acc is None else acc + gate * br
    o_ref[...] = h + jnp.dot(acc.astype(BF16), wo_ref[...], preferred_element_type=F32)


def _mix(h, g, att, others, wbr, wg, cg, wo):
    t, d = h.shape
    tm = MIX_ROWS
    row = lambda n: pl.BlockSpec((tm, n), lambda i: (i, 0))
    head_expand = np.zeros((LANES, ATT_WIDTH), np.float32)
    for hd in range(ATT_HEADS):
        head_expand[hd, hd * ATT_HEAD_DIM:(hd + 1) * ATT_HEAD_DIM] = 1.0
    head_expand = jnp.asarray(head_expand, BF16)
    unperm = jnp.asarray(np.stack([_deinterleave_matrix(dil).T for _, dil in ATT_PATTERNS]), BF16)
    outs = [o for o, _ in att]
    lses = [l for _, l in att]
    return pl.pallas_call(
        _mix_kernel,
        grid=(t // tm,),
        in_specs=[row(d), _const_spec((1, d))] + [row(a.shape[1]) for a in outs + lses + list(others)]
                 + [_const_spec(unperm.shape), _const_spec(head_expand.shape), _const_spec(wbr.shape),
                    _const_spec(wg.shape), _const_spec(cg.shape), _const_spec(wo.shape)],
        out_specs=row(d),
        out_shape=jax.ShapeDtypeStruct((t, d), F32),
        compiler_params=_cparams(("parallel",)),
        name="mix",
    )(h, g, *outs, *lses, *others, unperm, head_expand, wbr, wg, cg, wo)


def _ple(h2, p_ref, wpg_ref, wpp_ref):
    gate = jax.nn.sigmoid(jnp.dot(h2.astype(BF16), wpg_ref[...], preferred_element_type=F32))
    pe = jnp.dot(p_ref[...].astype(BF16), wpp_ref[...], preferred_element_type=F32)
    return h2 + gate * pe


def _swiglu_partial(xn, w1, w3, w2):
    a = jnp.dot(xn, w1, preferred_element_type=F32)
    b = jnp.dot(xn, w3, preferred_element_type=F32)
    hid = a * jax.nn.sigmoid(a) * b
    return jnp.dot(hid.astype(BF16), w2, preferred_element_type=F32)


def _ffn_kernel(h_ref, g_ref, w1_ref, w3_ref, w2_ref, p_ref, wpg_ref, wpp_ref, o_ref):
    h = h_ref[...]
    xn = _rms(h, g_ref[...]).astype(BF16)
    acc = h
    f = w1_ref.shape[1]
    for c0 in range(0, f, FFN_COLS):
        c1 = min(c0 + FFN_COLS, f)
        acc = acc + _swiglu_partial(xn, w1_ref[:, c0:c1], w3_ref[:, c0:c1], w2_ref[c0:c1, :])
    o_ref[...] = _ple(acc, p_ref, wpg_ref, wpp_ref)


def _ffn(h, g, w1, w3, w2, p, wpg, wpp):
    t, d = h.shape
    tm = FFN_ROWS
    row = lambda n: pl.BlockSpec((tm, n), lambda i: (i, 0))
    return pl.pallas_call(
        _ffn_kernel,
        grid=(t // tm,),
        in_specs=[row(d), _const_spec((1, d)), _const_spec(w1.shape), _const_spec(w3.shape),
                  _const_spec(w2.shape), row(p.shape[1]), _const_spec(wpg.shape),
                  _const_spec(wpp.shape)],
        out_specs=row(d),
        out_shape=jax.ShapeDtypeStruct((t, d), F32),
        compiler_params=_cparams(("parallel",)),
        name="ffn",
    )(h, g, w1, w3, w2, p, wpg, wpp)


MOE_ROWS = 128
MOE_MAX_BLOCKS = 2


def _moe_route(logits):
    ne, tm = logits.shape
    eidx = lax.broadcasted_iota(jnp.int32, logits.shape, 0)
    m1 = jnp.max(logits, axis=0, keepdims=True)
    i1 = jnp.min(jnp.where(logits == m1, eidx, ne), axis=0, keepdims=True)
    rest = jnp.where(eidx == i1, -jnp.inf, logits)
    m2 = jnp.max(rest, axis=0, keepdims=True)
    i2 = jnp.min(jnp.where(rest == m2, eidx, ne), axis=0, keepdims=True)
    e2 = jnp.exp(m2 - m1)
    den = 1.0 + e2
    combine = jnp.where(eidx == i1, 1.0 / den, 0.0) + jnp.where(eidx == i2, e2 / den, 0.0)
    routed = jnp.where((eidx == i1) | (eidx == i2), 1.0, 0.0)
    r = lax.broadcasted_iota(jnp.int32, (LANES, LANES), 0)
    c = lax.broadcasted_iota(jnp.int32, (LANES, LANES), 1)
    before = jnp.where(r < c, 1.0, 0.0).astype(BF16)
    counts = jnp.zeros((ne, 1), F32)
    slots = []
    for k in range(tm // LANES):
        blk = routed[:, k * LANES:(k + 1) * LANES]
        slots.append(jnp.dot(blk.astype(BF16), before, preferred_element_type=F32) + counts)
        counts = counts + jnp.sum(blk, axis=1, keepdims=True)
    slot = jnp.where(routed > 0.0, jnp.concatenate(slots, axis=1), -1.0).astype(jnp.int32)
    return combine, slot, counts


def _moe_kernel(h_ref, g_ref, rt_ref, w1_ref, w3_ref, w2_ref, p_ref, wpg_ref, wpp_ref, o_ref,
                xn_ref, comb_ref, slot_ref, cnt_ref, xe_ref, ye_ref):
    e = pl.program_id(1)
    j = pl.program_id(2)
    ne = pl.num_programs(1)
    tm = xn_ref.shape[0]

    @pl.when((e == 0) & (j == 0))
    def _():
        h = h_ref[...]
        xn = _rms(h, g_ref[...])
        xn_ref[...] = xn.astype(BF16)
        logits = lax.dot_general(rt_ref[...], xn, (((1,), (1,)), ((), ())),
                                 preferred_element_type=F32, precision=HI)
        combine, slot, counts = _moe_route(logits)
        comb_ref[...] = combine
        slot_ref[...] = slot
        for k in range(comb_ref.shape[0]):
            cnt_ref[k] = jnp.sum(counts[k:k + 1, :]).astype(jnp.int32)
        o_ref[...] = h

    n_blocks = (cnt_ref[e] + MOE_ROWS - 1) // MOE_ROWS

    def for_row_blocks(body):
        full = MOE_MAX_BLOCKS * MOE_ROWS

        def whole(i, carry):
            body(pl.multiple_of(i * full, full), full)
            return carry
        lax.fori_loop(0, n_blocks // MOE_MAX_BLOCKS, whole, 0)
        base = pl.multiple_of((n_blocks // MOE_MAX_BLOCKS) * full, full)
        for r in range(1, MOE_MAX_BLOCKS):
            @pl.when(n_blocks % MOE_MAX_BLOCKS == r)
            def _():
                body(base, r * MOE_ROWS)

    def one_hot(r0, rows):
        return slot_ref[pl.ds(e, 1), :] == lax.broadcasted_iota(jnp.int32, (rows, tm), 0) + r0

    def swiglu(x):
        return _swiglu_partial(x, w1_ref[0], w3_ref[0], w2_ref[0])

    last_j = pl.num_programs(2) - 1

    @pl.when(j == 0)
    def _():
        def gather_first(r0, rows):
            sel = jnp.where(one_hot(r0, rows), 1.0, 0.0).astype(BF16)
            x = jnp.dot(sel, xn_ref[...], preferred_element_type=F32).astype(BF16)
            xe_ref[pl.ds(r0, rows), :] = x
            ye_ref[pl.ds(r0, rows), :] = swiglu(x)
        for_row_blocks(gather_first)

    @pl.when((j > 0) & (j < last_j))
    def _():
        def middle(r0, rows):
            ye_ref[pl.ds(r0, rows), :] += swiglu(xe_ref[pl.ds(r0, rows), :])
        for_row_blocks(middle)

    @pl.when((j > 0) & (j == last_j))
    def _():
        def last_scatter(r0, rows):
            y = ye_ref[pl.ds(r0, rows), :] + swiglu(xe_ref[pl.ds(r0, rows), :])
            hot = one_hot(r0, rows)
            weight = jnp.sum(jnp.where(hot, comb_ref[pl.ds(e, 1), :], 0.0), axis=1, keepdims=True)
            o_ref[...] += lax.dot_general(jnp.where(hot, 1.0, 0.0).astype(BF16),
                                          (y * weight).astype(BF16),
                                          (((0,), (0,)), ((), ())), preferred_element_type=F32)
        for_row_blocks(last_scatter)

    @pl.when((e == ne - 1) & (j == pl.num_programs(2) - 1))
    def _():
        o_ref[...] = _ple(o_ref[...], p_ref, wpg_ref, wpp_ref)


def _moe(h, g, router, w1, w3, w2, p, wpg, wpp):
    t, d = h.shape
    ne, _, f = w1.shape
    tm, tf = MOE_TOKENS, MOE_COLS
    assert f % tf == 0 and f // tf >= 2
    row = lambda n: pl.BlockSpec((tm, n), lambda i, e, j: (i, 0))
    return pl.pallas_call(
        _moe_kernel,
        grid=(t // tm, ne, f // tf),
        in_specs=[row(d), _const_spec((1, d)), _const_spec((ne, d)),
                  pl.BlockSpec((1, d, tf), lambda i, e, j: (e, 0, j)),
                  pl.BlockSpec((1, d, tf), lambda i, e, j: (e, 0, j)),
                  pl.BlockSpec((1, tf, d), lambda i, e, j: (e, j, 0)),
                  row(p.shape[1]), _const_spec(wpg.shape), _const_spec(wpp.shape)],
        out_specs=row(d),
        out_shape=jax.ShapeDtypeStruct((t, d), F32),
        scratch_shapes=[pltpu.VMEM((tm, d), BF16), pltpu.VMEM((ne, tm), F32),
                        pltpu.VMEM((ne, tm), jnp.int32), pltpu.SMEM((ne,), jnp.int32),
                        pltpu.VMEM((tm, d), BF16), pltpu.VMEM((tm, d), F32)],
        compiler_params=_cparams(("parallel", "arbitrary", "arbitrary")),
        name="moe",
    )(h, g, router.T, w1, w3, w2, p, wpg, wpp)


def _final_norm_kernel(h_ref, g_ref, o_ref):
    o_ref[...] = _rms(h_ref[...], g_ref[...])


def _final_norm(h, g):
    t, d = h.shape
    tm = NORM_ROWS
    row = pl.BlockSpec((tm, d), lambda i: (i, 0))
    return pl.pallas_call(
        _final_norm_kernel, grid=(t // tm,),
        in_specs=[row, _const_spec((1, d))], out_specs=row,
        out_shape=jax.ShapeDtypeStruct((t, d), F32),
        compiler_params=_cparams(("parallel",)), name="final_norm",
    )(h, g)


ATT_HALF = 64
ATT_QB = 128
ATT_KB = ATT_QB + 2 * ATT_HALF
NEG = -1e30
assert all(w // (2 * d) == ATT_HALF for w, d in ATT_PATTERNS)


def _t5_bucket(rel):
    half = REL_BUCKETS // 2
    max_exact = half // 2
    n = np.abs(rel)
    large = max_exact + (np.log(np.maximum(n, 1) / max_exact) / math.log(REL_MAX_DIST / max_exact)
                         * (half - max_exact)).astype(np.int32)
    large = np.minimum(large, half - 1)
    return np.where(rel > 0, half, 0) + np.where(n < max_exact, n, large)


def _att_bias_tables(rel_bias, g, dil):
    i = np.arange(ATT_QB)[:, None]
    j = np.arange(ATT_KB)[None, :]
    rel = j - ATT_HALF - i
    band = np.abs(rel) <= ATT_HALF
    pick = np.eye(REL_BUCKETS, dtype=np.float32)[_t5_bucket(dil * rel)]
    heads = rel_bias[:, g * ATT_HEADS:(g + 1) * ATT_HEADS].astype(F32)
    bias = jnp.einsum('qkb,bh->hqk', pick, heads, precision=lax.Precision.HIGHEST)
    tables = []
    for v in range(4):
        ok = band
        if v & 1:
            ok = ok & (j >= ATT_HALF)
        if v & 2:
            ok = ok & (j < ATT_QB + ATT_HALF)
        tables.append(jnp.where(ok[None], bias, NEG))
    return jnp.stack(tables)


def _attn_kernel(q_ref, k_ref, v_ref, kp_ref, kn_ref, vp_ref, vn_ref, bias_ref, o_ref, lse_ref,
                 qbuf, kbuf, vbuf, obuf, lbuf, *, tq, n_blocks):
    flat = lambda ref: ref[0].reshape(-1, ref.shape[-1])
    qbuf[...] = flat(q_ref)
    kbuf[0:ATT_HALF] = flat(kp_ref)
    kbuf[ATT_HALF:ATT_HALF + tq] = flat(k_ref)
    kbuf[ATT_HALF + tq:] = flat(kn_ref)
    vbuf[0:ATT_HALF] = flat(vp_ref)
    vbuf[ATT_HALF:ATT_HALF + tq] = flat(v_ref)
    vbuf[ATT_HALF + tq:] = flat(vn_ref)
    nsb = tq // ATT_QB
    first = pl.program_id(2) * nsb
    lane = lax.broadcasted_iota(jnp.int32, (ATT_QB, LANES), 1)
    low = lane < ATT_HEAD_DIM
    lane_row = lax.broadcasted_iota(jnp.int32, (1, LANES), 1)
    keep = [(lane_row < ATT_HEAD_DIM).astype(BF16), (lane_row >= ATT_HEAD_DIM).astype(BF16)]
    ones = jnp.ones((ATT_KB, LANES), BF16)

    def block(sb, carry):
        r0 = pl.multiple_of(sb * ATT_QB, ATT_QB)
        gsb = first + sb
        variant = (gsb == 0).astype(jnp.int32) + 2 * (gsb == n_blocks - 1).astype(jnp.int32)
        q = qbuf[pl.ds(r0, ATT_QB), :] * (ATT_HEAD_DIM ** -0.5)
        lse_all = jnp.zeros((ATT_QB, LANES), F32)
        outs = []
        for pair in range(ATT_HEADS // 2):
            cols = slice(pair * LANES, (pair + 1) * LANES)
            qp = q[:, cols]
            kp = kbuf[pl.ds(r0, ATT_KB), cols]
            vp = jnp.concatenate([vbuf[pl.ds(r0, ATT_KB), cols], ones], axis=1)
            res = []
            for half in range(2):
                h = 2 * pair + half
                s = lax.dot_general(qp * keep[half], kp, (((1,), (1,)), ((), ())),
                                    preferred_element_type=F32)
                s = s + bias_ref[variant, h]
                m = jnp.max(s, axis=-1, keepdims=True)
                e = jnp.exp(s - m)
                pv = jnp.dot(e.astype(BF16), vp, preferred_element_type=F32)
                den = pv[:, LANES:]
                res.append(pv[:, :LANES] / den)
                lse_all = jnp.where(lane == h, m + jnp.log(den), lse_all)
            outs.append(jnp.where(low, res[0], res[1]))
        obuf[pl.ds(r0, ATT_QB), :] = jnp.concatenate(outs, axis=1).astype(obuf.dtype)
        lbuf[pl.ds(r0, ATT_QB), :] = lse_all
        return carry

    lax.fori_loop(0, nsb, block, 0, unroll=4)
    o_ref[0] = obuf[...].reshape(o_ref.shape[1:])
    lse_ref[0] = lbuf[...].reshape(lse_ref.shape[1:])


def _attention_group(qkv, bias_tables, dil, bsz, seq, name):
    sub_len = seq // dil
    assert sub_len % ATT_QB == 0 and seq % ATT_TILE == 0 and ATT_TILE % dil == 0
    rows = ATT_TILE // dil
    tq = min(SEQ_ROWS, sub_len)
    w = ATT_WIDTH
    hrows = min(rows, ATT_HALF)
    n_hb = sub_len // ATT_HALF

    def view(a, chunk):
        return a.reshape(bsz, (seq // ATT_TILE) * (rows // chunk), dil, chunk, a.shape[-1])

    def main(part):
        return pl.BlockSpec((1, tq // rows, None, rows, w), lambda b, r, n: (b, n, r, 0, part))

    def halo(part, nxt):
        if nxt:
            blk = lambda n: jnp.minimum((n + 1) * (tq // ATT_HALF), n_hb - 1)
        else:
            blk = lambda n: jnp.maximum(n * (tq // ATT_HALF) - 1, 0)
        return pl.BlockSpec((1, ATT_HALF // hrows, None, hrows, w),
                            lambda b, r, n: (b, blk(n), r, 0, part))

    mv, hv = view(qkv, rows), view(qkv, hrows)
    o, lse = pl.pallas_call(
        functools.partial(_attn_kernel, tq=tq, n_blocks=sub_len // ATT_QB),
        grid=(bsz, dil, sub_len // tq),
        in_specs=[main(0), main(1), main(2), halo(1, False), halo(1, True), halo(2, False),
                  halo(2, True), _const_spec(bias_tables.shape)],
        out_specs=[pl.BlockSpec((1, tq // rows, None, rows, w), lambda b, r, n: (b, n, r, 0, 0)),
                   pl.BlockSpec((1, tq // rows, None, rows, LANES), lambda b, r, n: (b, n, r, 0, 0))],
        out_shape=[jax.ShapeDtypeStruct((bsz, seq // ATT_TILE, dil, rows, w), BF16),
                   jax.ShapeDtypeStruct((bsz, seq // ATT_TILE, dil, rows, LANES), F32)],
        scratch_shapes=[pltpu.VMEM((tq, w), BF16),
                        pltpu.VMEM((tq + 2 * ATT_HALF, w), BF16),
                        pltpu.VMEM((tq + 2 * ATT_HALF, w), BF16),
                        pltpu.VMEM((tq, w), BF16), pltpu.VMEM((tq, LANES), F32)],
        compiler_params=_cparams(("parallel", "parallel", "parallel")),
        name=name,
    )(mv, mv, mv, hv, hv, hv, hv, bias_tables)
    return o.reshape(bsz * seq, w), lse.reshape(bsz * seq, LANES)


SUBLANES = 8
CONV_HALO = 16
CONV_ROWS = 64


def _conformer_kernel(u_ref, up_ref, un_ref, dw_ref, dwb_ref, lng_ref, lnb_ref, o_ref,
                      hp_ref, sh_ref, *, ts):
    n = pl.program_id(1)

    def glu(u):
        return u[:, :CONV_CH] * jax.nn.sigmoid(u[:, CONV_CH:])

    hp_ref[0:CONV_HALO] = jnp.where(n > 0, glu(up_ref[0]), 0.0)
    hp_ref[CONV_HALO:CONV_HALO + ts] = glu(u_ref[0])
    hp_ref[CONV_HALO + ts:] = jnp.where(n < pl.num_programs(1) - 1, glu(un_ref[0]), 0.0)
    span = ts + 2 * CONV_HALO - SUBLANES
    for b in range(1, SUBLANES):
        sh_ref[b - 1] = hp_ref[pl.ds(b, span), :]
    first = CONV_HALO - CONV_K // 2

    def rows(c, carry):
        r0 = pl.multiple_of(c * CONV_ROWS, CONV_ROWS)
        acc = jnp.broadcast_to(dwb_ref[...], (CONV_ROWS, CONV_CH))
        for k in range(CONV_K):
            a, b = divmod(first + k, SUBLANES)
            at = pl.ds(r0 + a * SUBLANES, CONV_ROWS)
            tap = hp_ref[at, :] if b == 0 else sh_ref[b - 1, at, :]
            acc = acc + dw_ref[pl.ds(k, 1), :] * tap
        mu = jnp.mean(acc, axis=-1, keepdims=True)
        cen = acc - mu
        var = jnp.mean(cen * cen, axis=-1, keepdims=True)
        y = cen * lax.rsqrt(var + EPS) * lng_ref[...] + lnb_ref[...]
        o_ref[0, pl.ds(r0, CONV_ROWS), :] = (y * jax.nn.sigmoid(y)).astype(o_ref.dtype)
        return carry

    lax.fori_loop(0, ts // CONV_ROWS, rows, 0, unroll=4)


def _conformer(u, dw, dw_b, ln_g, ln_b, bsz, seq):
    ts = min(SEQ_ROWS, seq)
    view = u.reshape(bsz, seq, 2 * CONV_CH)
    hb = ts // CONV_HALO
    n_hb = seq // CONV_HALO
    out = pl.pallas_call(
        functools.partial(_conformer_kernel, ts=ts),
        grid=(bsz, seq // ts),
        in_specs=[pl.BlockSpec((1, ts, 2 * CONV_CH), lambda b, n: (b, n, 0)),
                  pl.BlockSpec((1, CONV_HALO, 2 * CONV_CH),
                               lambda b, n: (b, jnp.maximum(n * hb - 1, 0), 0)),
                  pl.BlockSpec((1, CONV_HALO, 2 * CONV_CH),
                               lambda b, n: (b, jnp.minimum((n + 1) * hb, n_hb - 1), 0)),
                  _const_spec(dw.shape), _const_spec((1, CONV_CH)), _const_spec((1, CONV_CH)),
                  _const_spec((1, CONV_CH))],
        out_specs=pl.BlockSpec((1, ts, CONV_CH), lambda b, n: (b, n, 0)),
        out_shape=jax.ShapeDtypeStruct((bsz, seq, CONV_CH), BF16),
        scratch_shapes=[pltpu.VMEM((ts + 2 * CONV_HALO, CONV_CH), F32),
                        pltpu.VMEM((SUBLANES - 1, ts + 2 * CONV_HALO - SUBLANES, CONV_CH), F32)],
        compiler_params=_cparams(("parallel", "parallel")),
        name="conformer",
    )(view, view, view, dw, dw_b[None], ln_g[None], ln_b[None])
    return out.reshape(bsz * seq, CONV_CH)


SSD_HALO = SUBLANES
SSD_ROWS = 64
SSD_BC = SSM_GROUPS * SSM_STATE
HEADS_PER_GROUP = SSM_HEADS // SSM_GROUPS
GROUP_LANES = HEADS_PER_GROUP * SSM_HEAD_DIM
assert SSD_BC == LANES and SSM_CHUNK == LANES and 2 * SSM_HEADS <= LANES


def _ssd_pre_kernel(x_ref, xp_ref, xn_ref, dt_ref, cw_ref, cb_ref, dtb_ref, arow_ref, tri_ref,
                    ex_ref, xo_ref, dto_ref, acs_ref, nf_ref, nb_ref, cy_ref, hp_ref, sh_ref,
                    *, ts, phases):
    n = pl.program_id(1)
    hp_ref[0:SSD_HALO] = jnp.where(n > 0, xp_ref[0], 0.0)
    hp_ref[SSD_HALO:SSD_HALO + ts] = x_ref[0]
    hp_ref[SSD_HALO + ts:] = jnp.where(n < pl.num_programs(1) - 1, xn_ref[0], 0.0)
    for i, b in enumerate(phases):
        sh_ref[i] = hp_ref[pl.ds(b, ts + SSD_HALO), :]
    first = SSD_HALO - SSM_CONV // 2

    def rows(c, carry):
        r0 = pl.multiple_of(c * SSD_ROWS, SSD_ROWS)
        acc = jnp.broadcast_to(cb_ref[...], (SSD_ROWS, SSM_CONV_CH))
        for k in range(SSM_CONV):
            a, b = divmod(first + k, SUBLANES)
            acc = acc + cw_ref[pl.ds(k, 1), :] * sh_ref[phases.index(b),
                                                        pl.ds(r0 + a * SUBLANES, SSD_ROWS), :]
        xo_ref[0, pl.ds(r0, SSD_ROWS), :] = acc * jax.nn.sigmoid(acc)
        return carry

    lax.fori_loop(0, ts // SSD_ROWS, rows, 0, unroll=True)
    x = dt_ref[0] + dtb_ref[...]
    softplus = jnp.maximum(x, 0.0) + jnp.log1p(jnp.exp(-jnp.abs(x)))
    lane = lax.broadcasted_iota(jnp.int32, x.shape, 1)
    dtv = jnp.where(lane < 2 * SSM_HEADS, softplus, 0.0)
    dto_ref[0] = dtv

    nch = ts // SSM_CHUNK
    chunk = lambda k: slice(k * SSM_CHUNK, (k + 1) * SSM_CHUNK)
    dta = dtv * arow_ref[...]
    acs_all = _cumsum_both(jnp.concatenate([dta[chunk(k)] for k in range(nch)], axis=1), tri_ref)
    clane = lax.broadcasted_iota(jnp.int32, (SSM_CHUNK, DT_PAD), 1)
    ws, totals = [], [[], []]
    for k in range(nch):
        acs = acs_all[:, chunk(k)]
        acs_ref[0, chunk(k), :] = acs
        tot_f, tot_b = acs[SSM_CHUNK - 1:SSM_CHUNK, :], acs[0:1, :]
        to_end = jnp.where(clane < SSM_HEADS, tot_f - acs,
                           jnp.where(clane < 2 * SSM_HEADS, tot_b - acs, 0.0))
        ws.append(dtv[chunk(k)] * jnp.exp(to_end))
        totals[0].append(jnp.exp(tot_f))
        totals[1].append(jnp.exp(tot_b))
    w_all = jnp.concatenate(ws, axis=0)
    xs = xo_ref[0, :, :SSM_INNER]
    hl = lax.broadcasted_iota(jnp.int32, (SSM_STATE, SSM_INNER), 1)
    pad_rows = jnp.zeros((SUBLANES - nch % SUBLANES, DT_PAD), F32)
    for d, new_ref in enumerate((nf_ref, nb_ref)):
        xw = (xs * _expand(w_all, ex_ref, d)).astype(BF16)
        carry = _expand(jnp.concatenate(totals[d] + [pad_rows], axis=0), ex_ref, d)
        for k in range(nch):
            bmat = xo_ref[0, chunk(k), SSM_INNER:SSM_INNER + SSD_BC].astype(BF16)
            full = lax.dot_general(bmat, xw[chunk(k)], (((0,), (0,)), ((), ())),
                                   preferred_element_type=F32)
            new = full[:SSM_STATE]
            for g in range(1, SSM_GROUPS):
                new = jnp.where(hl >= g * GROUP_LANES, full[g * SSM_STATE:(g + 1) * SSM_STATE], new)
            new_ref[0, k] = new
            cy_ref[0, k, d:d + 1, :] = carry[k:k + 1]


def _ssd_pre(xbc, dt_raw, conv_w, conv_b, dt_bias, consts, bsz, seq):
    ts = min(SEQ_ROWS, seq)
    nc = seq // SSM_CHUNK
    nch = ts // SSM_CHUNK
    a_row, tri, expand = consts
    new = jax.ShapeDtypeStruct((bsz, nc, SSM_STATE, SSM_INNER), F32)
    new_spec = pl.BlockSpec((1, nch, SSM_STATE, SSM_INNER), lambda b, n: (b, n, 0, 0))
    xv = xbc.reshape(bsz, seq, SSM_CONV_CH)
    dv = dt_raw.reshape(bsz, seq, DT_PAD)
    hb = ts // SSD_HALO
    n_hb = seq // SSD_HALO
    first = SSD_HALO - SSM_CONV // 2
    phases = tuple(sorted({(first + k) % SUBLANES for k in range(SSM_CONV)}))
    dtb = jnp.zeros((1, DT_PAD), F32).at[0, :2 * SSM_HEADS].set(dt_bias.reshape(-1))
    return pl.pallas_call(
        functools.partial(_ssd_pre_kernel, ts=ts, phases=phases),
        grid=(bsz, seq // ts),
        in_specs=[pl.BlockSpec((1, ts, SSM_CONV_CH), lambda b, n: (b, n, 0)),
                  pl.BlockSpec((1, SSD_HALO, SSM_CONV_CH),
                               lambda b, n: (b, jnp.maximum(n * hb - 1, 0), 0)),
                  pl.BlockSpec((1, SSD_HALO, SSM_CONV_CH),
                               lambda b, n: (b, jnp.minimum((n + 1) * hb, n_hb - 1), 0)),
                  pl.BlockSpec((1, ts, DT_PAD), lambda b, n: (b, n, 0)),
                  _const_spec(conv_w.shape), _const_spec((1, SSM_CONV_CH)),
                  _const_spec((1, DT_PAD)), _const_spec(a_row.shape), _const_spec(tri.shape),
                  _const_spec(expand.shape)],
        out_specs=[pl.BlockSpec((1, ts, SSM_CONV_CH), lambda b, n: (b, n, 0)),
                   pl.BlockSpec((1, ts, DT_PAD), lambda b, n: (b, n, 0)),
                   pl.BlockSpec((1, ts, DT_PAD), lambda b, n: (b, n, 0)),
                   new_spec, new_spec,
                   pl.BlockSpec((1, nch, 2, SSM_INNER), lambda b, n: (b, n, 0, 0))],
        out_shape=[jax.ShapeDtypeStruct((bsz, seq, SSM_CONV_CH), F32),
                   jax.ShapeDtypeStruct((bsz, seq, DT_PAD), F32),
                   jax.ShapeDtypeStruct((bsz, seq, DT_PAD), F32),
                   new, new, jax.ShapeDtypeStruct((bsz, nc, 2, SSM_INNER), F32)],
        scratch_shapes=[pltpu.VMEM((ts + 2 * SSD_HALO, SSM_CONV_CH), F32),
                        pltpu.VMEM((len(phases), ts + SSD_HALO, SSM_CONV_CH), F32)],
        compiler_params=_cparams(("parallel", "parallel")),
        name="ssd_pre",
    )(xv, xv, xv, dv, conv_w, conv_b[None], dtb, a_row, tri, expand)


def _ssd_consts(a_log):
    a_row = jnp.zeros((1, DT_PAD), F32).at[0, :2 * SSM_HEADS].set(-jnp.exp(a_log.reshape(-1)))
    lower = np.tril(np.ones((SSM_CHUNK, SSM_CHUNK), np.float32))
    tri = jnp.asarray(np.concatenate([lower, lower.T], axis=0), BF16)
    expand = np.zeros((2, DT_PAD, SSM_INNER), np.float32)
    for d in range(2):
        for h in range(SSM_HEADS):
            expand[d, d * SSM_HEADS + h, h * SSM_HEAD_DIM:(h + 1) * SSM_HEAD_DIM] = 1.0
    return a_row, tri, jnp.asarray(expand, BF16)


def _cumsum_both(dta, tri_ref):
    hi = dta.astype(BF16)
    r1 = dta - hi.astype(F32)
    mid = r1.astype(BF16)
    lo = (r1 - mid.astype(F32)).astype(BF16)
    both = (jnp.dot(tri_ref[...], hi, preferred_element_type=F32)
            + jnp.dot(tri_ref[...], mid, preferred_element_type=F32)
            + jnp.dot(tri_ref[...], lo, preferred_element_type=F32))
    lane = lax.broadcasted_iota(jnp.int32, dta.shape, 1) % DT_PAD
    return jnp.where(lane < SSM_HEADS, both[:SSM_CHUNK], both[SSM_CHUNK:])


def _expand(v, ex_ref, d):
    return _split_dot(v, ex_ref[d])


def _ssd_scan_kernel(nf_ref, nb_ref, cf_ref, cb_ref, pf_ref, pb_ref, sf_ref, sb_ref, *, nch):
    @pl.when(pl.program_id(1) == 0)
    def _():
        sf_ref[...] = jnp.zeros_like(sf_ref)
        sb_ref[...] = jnp.zeros_like(sb_ref)

    def scan(d, order, new_ref, carry_ref, st_ref, out_ref):
        st = st_ref[...]
        for k in order:
            out_ref[0, k] = st.astype(out_ref.dtype)
            st = st * carry_ref[0, k, d:d + 1, :] + new_ref[0, k]
        st_ref[...] = st

    scan(0, range(nch), nf_ref, cf_ref, sf_ref, pf_ref)
    scan(1, range(nch - 1, -1, -1), nb_ref, cb_ref, sb_ref, pb_ref)


SSD_SCAN_CHUNKS = 16


def _ssd_states(new_f, new_b, carry, bsz, seq):
    nc = seq // SSM_CHUNK
    nch = min(SSD_SCAN_CHUNKS, nc)
    steps = nc // nch
    fwd = lambda b, c: (b, c, 0, 0)
    bwd = lambda b, c: (b, steps - 1 - c, 0, 0)
    st = jax.ShapeDtypeStruct((bsz, nc, SSM_STATE, SSM_INNER), BF16)
    blk = (1, nch, SSM_STATE, SSM_INNER)
    return pl.pallas_call(
        functools.partial(_ssd_scan_kernel, nch=nch),
        grid=(bsz, steps),
        in_specs=[pl.BlockSpec(blk, fwd), pl.BlockSpec(blk, bwd),
                  pl.BlockSpec((1, nch, 2, SSM_INNER), fwd),
                  pl.BlockSpec((1, nch, 2, SSM_INNER), bwd)],
        out_specs=[pl.BlockSpec(blk, fwd), pl.BlockSpec(blk, bwd)],
        out_shape=[st, st],
        scratch_shapes=[pltpu.VMEM((SSM_STATE, SSM_INNER), F32),
                        pltpu.VMEM((SSM_STATE, SSM_INNER), F32)],
        compiler_params=_cparams(("parallel", "arbitrary")),
        name="ssd_scan",
    )(new_f, new_b, carry, carry)


def _ssd_out_kernel(x_ref, dt_ref, acs_ref, z_ref, pf_ref, pb_ref, ex_ref, dskip_ref, ng_ref,
                    o_ref, *, nck):
    for k in range(nck):
        _ssd_out_chunk(k, x_ref, dt_ref, acs_ref, z_ref, pf_ref, pb_ref, ex_ref, dskip_ref,
                       ng_ref, o_ref)


def _ssd_out_chunk(k, x_ref, dt_ref, acs_ref, z_ref, pf_ref, pb_ref, ex_ref, dskip_ref, ng_ref,
                   o_ref):
    rows = slice(k * SSM_CHUNK, (k + 1) * SSM_CHUNK)
    xs = x_ref[0, rows, :SSM_INNER]
    bmat = x_ref[0, rows, SSM_INNER:SSM_INNER + SSD_BC].astype(BF16)
    cmat = x_ref[0, rows, SSM_INNER + SSD_BC:].astype(BF16)
    dtv = dt_ref[0, rows, :]
    acs = acs_ref[0, rows, :]
    acs_t = acs.T
    dt_t = dtv.T
    eacs = jnp.exp(acs)
    row = lax.broadcasted_iota(jnp.int32, (SSM_CHUNK, SSM_CHUNK), 0)
    col = lax.broadcasted_iota(jnp.int32, (SSM_CHUNK, SSM_CHUNK), 1)
    past, now = col < row, col == row
    low = lax.broadcasted_iota(jnp.int32, (SSM_CHUNK, LANES), 1) < SSM_HEAD_DIM
    glane = lax.broadcasted_iota(jnp.int32, (1, SSD_BC), 1) // SSM_STATE
    scores = [lax.dot_general(cmat * (glane == g).astype(BF16), bmat,
                              (((1,), (1,)), ((), ())), preferred_element_type=F32)
              for g in range(SSM_GROUPS)]
    xs_b = xs.astype(BF16)
    keep = [low.astype(BF16), (~low).astype(BF16)]
    diag = []
    for pair in range(SSM_HEADS // 2):
        xp = xs_b[:, pair * LANES:(pair + 1) * LANES]
        mats = []
        for half in range(2):
            f = 2 * pair + half
            b = SSM_HEADS + f
            seg = jnp.where(past | now, acs[:, f:f + 1] - acs_t[f:f + 1, :],
                            acs[:, b:b + 1] - acs_t[b:b + 1, :])
            dts = (jnp.where(past, dt_t[f:f + 1, :], dt_t[b:b + 1, :])
                   + jnp.where(now, dt_t[f:f + 1, :], 0.0))
            mats.append((scores[f // HEADS_PER_GROUP] * jnp.exp(seg) * dts).astype(BF16))
        diag.append(jnp.dot(jnp.concatenate(mats, axis=1),
                            jnp.concatenate([xp * keep[0], xp * keep[1]], axis=0),
                            preferred_element_type=F32))
    y = dskip_ref[...] * xs + jnp.concatenate(diag, axis=1)
    hgroup = lax.broadcasted_iota(jnp.int32, (1, SSM_INNER), 1) // GROUP_LANES
    for d, p_ref in enumerate((pf_ref, pb_ref)):
        prev = p_ref[0, k]
        stacked = jnp.concatenate([prev * (hgroup == g).astype(BF16) for g in range(SSM_GROUPS)],
                                  axis=0)
        off = jnp.dot(cmat, stacked, preferred_element_type=F32)
        y = y + off * _expand(eacs, ex_ref, d)
    z = z_ref[0, rows, :]
    y = y * (z * jax.nn.sigmoid(z))
    o_ref[0, rows, :] = _rms(y, ng_ref[...]).astype(o_ref.dtype)


SSD_OUT_CHUNKS = 4


def _ssd_out(xact, dtv, acs, z, prev_f, prev_b, expand, d_skip, norm_g, bsz, seq):
    nc = seq // SSM_CHUNK
    nck = min(SSD_OUT_CHUNKS, nc)
    dsk = jnp.repeat(d_skip, SSM_HEAD_DIM)[None]
    chunk = lambda n: pl.BlockSpec((1, nck * SSM_CHUNK, n), lambda b, c: (b, c, 0))
    state = pl.BlockSpec((1, nck, SSM_STATE, SSM_INNER), lambda b, c: (b, c, 0, 0))
    out = pl.pallas_call(
        functools.partial(_ssd_out_kernel, nck=nck),
        grid=(bsz, nc // nck),
        in_specs=[chunk(SSM_CONV_CH), chunk(DT_PAD), chunk(DT_PAD), chunk(SSM_INNER), state, state,
                  _const_spec(expand.shape), _const_spec(dsk.shape), _const_spec((1, SSM_INNER))],
        out_specs=chunk(SSM_INNER),
        out_shape=jax.ShapeDtypeStruct((bsz, seq, SSM_INNER), BF16),
        compiler_params=_cparams(("parallel", "parallel")),
        name="ssd_out",
    )(xact, dtv, acs, z.reshape(bsz, seq, SSM_INNER), prev_f, prev_b, expand, dsk, norm_g[None])
    return out.reshape(bsz * seq, SSM_INNER)


def _ssd(z, xbc, dt_raw, conv_w, conv_b, a_log, dt_bias, d_skip, norm_g, bsz, seq):
    consts = _ssd_consts(a_log)
    xact, dtv, acs, new_f, new_b, carry = _ssd_pre(xbc, dt_raw, conv_w, conv_b, dt_bias, consts,
                                                   bsz, seq)
    prev_f, prev_b = _ssd_states(new_f, new_b, carry, bsz, seq)
    return _ssd_out(xact, dtv, acs, z, prev_f, prev_b, consts[2], d_skip, norm_g, bsz, seq)


FNET_COLS = 4096


def _dft_cos_sin(n):
    ang = 2.0 * np.pi * np.outer(np.arange(n), np.arange(n)) / n
    return np.cos(ang), np.sin(ang)


def _fnet_consts(seq):
    c = FNET_GROUP_DIM
    n1 = seq // LANES
    c1, s1 = _dft_cos_sin(n1)
    stage1 = np.concatenate([c1, -s1], axis=0)
    ang = 2.0 * np.pi * np.outer(np.arange(n1), np.arange(LANES)) / seq
    twr = np.repeat(np.cos(ang), c, axis=1)
    twi = np.repeat(-np.sin(ang), c, axis=1)
    cc, sc = _dft_cos_sin(c)
    chan = np.block([[cc, -sc], [sc, cc]])
    c2, s2 = _dft_cos_sin(LANES)
    return (jnp.asarray(stage1, BF16), jnp.asarray(twr, F32), jnp.asarray(twi, F32),
            jnp.asarray(chan, BF16), jnp.asarray(c2, BF16), jnp.asarray(s2, BF16))


def _fnet1_kernel(x_ref, f_ref, twr_ref, twi_ref, o_ref, *, n1):
    c = FNET_GROUP_DIM
    a = jnp.dot(f_ref[...], x_ref[0].astype(BF16), preferred_element_type=F32)
    ar, ai = a[:n1], a[n1:]
    twr, twi = twr_ref[...], twi_ref[...]
    re = (ar * twr - ai * twi).astype(o_ref.dtype)
    im = (ar * twi + ai * twr).astype(o_ref.dtype)
    for j in range(re.shape[1] // c):
        o_ref[0, :, (2 * j) * c:(2 * j + 1) * c] = re[:, j * c:(j + 1) * c]
        o_ref[0, :, (2 * j + 1) * c:(2 * j + 2) * c] = im[:, j * c:(j + 1) * c]


def _fnet2_kernel(a_ref, chan_ref, c2_ref, s2_ref, o_ref, g_ref, scr_ref, *, n1, scale):
    c = FNET_GROUP_DIM
    pitch = scr_ref.shape[0] // LANES
    per = min(8, n1)
    for i in range(n1 // per):
        blk = a_ref[0, i * per:(i + 1) * per].reshape(per * LANES, 2 * c)
        g = jnp.dot(blk, chan_ref[...], preferred_element_type=F32).astype(BF16)
        g_ref[i * per:(i + 1) * per] = g.reshape(per, LANES, 2 * c)

    def body(k1, carry):
        g = g_ref[k1]
        y = (jnp.dot(c2_ref[...], g[:, :c], preferred_element_type=F32)
             + jnp.dot(s2_ref[...], g[:, c:], preferred_element_type=F32))
        scr_ref[pl.ds(k1, LANES, stride=pitch), :] = y * scale
        return carry

    lax.fori_loop(0, n1, body, 0, unroll=4)

    def compact(k2, carry):
        src = pl.multiple_of(k2 * pitch, SUBLANES)
        dst = pl.multiple_of(k2 * n1, n1)
        o_ref[0, pl.ds(dst, n1), :] = scr_ref[pl.ds(src, n1), :].astype(o_ref.dtype)
        return carry

    lax.fori_loop(0, LANES, compact, 0, unroll=8)


def _fourier(fn, bsz, seq):
    c = FNET_GROUP_DIM
    assert c == LANES and seq % LANES == 0
    n1 = seq // LANES
    stage1, twr, twi, chan, c2, s2 = _fnet_consts(seq)
    ncols = LANES * c
    nb = min(FNET_COLS, ncols)
    x2 = fn.reshape(bsz * FNET_GROUPS, n1, ncols)
    a = pl.pallas_call(
        functools.partial(_fnet1_kernel, n1=n1),
        grid=(ncols // nb, bsz * FNET_GROUPS),
        in_specs=[pl.BlockSpec((1, n1, nb), lambda j, i: (i, 0, j)),
                  _const_spec(stage1.shape),
                  pl.BlockSpec((n1, nb), lambda j, i: (0, j)),
                  pl.BlockSpec((n1, nb), lambda j, i: (0, j))],
        out_specs=pl.BlockSpec((1, n1, 2 * nb), lambda j, i: (i, 0, j)),
        out_shape=jax.ShapeDtypeStruct((bsz * FNET_GROUPS, n1, 2 * ncols), BF16),
        compiler_params=_cparams(("parallel", "parallel")),
        name="fnet1",
    )(x2, stage1, twr, twi)
    a4 = a.reshape(bsz * FNET_GROUPS, n1, LANES, 2 * c)
    out = pl.pallas_call(
        functools.partial(_fnet2_kernel, n1=n1, scale=1.0 / math.sqrt(seq * c)),
        grid=(bsz, FNET_GROUPS),
        in_specs=[pl.BlockSpec((1, n1, LANES, 2 * c), lambda b, g: (b * FNET_GROUPS + g, 0, 0, 0)),
                  _const_spec(chan.shape), _const_spec(c2.shape), _const_spec(s2.shape)],
        out_specs=pl.BlockSpec((1, seq, c), lambda b, g: (b, 0, g)),
        out_shape=jax.ShapeDtypeStruct((bsz, seq, FNET_WIDTH), BF16),
        scratch_shapes=[pltpu.VMEM((n1, LANES, 2 * c), BF16),
                        pltpu.VMEM((LANES * (n1 + SUBLANES), c), F32)],
        compiler_params=_cparams(("parallel", "parallel")),
        name="fnet2",
    )(a4, chan, c2, s2)
    return out.reshape(bsz * seq, FNET_WIDTH)


def kernel(x, p, rel_bias, norm_mix, w_in, conv_dw, conv_dw_b, conv_ln_g, conv_ln_b, conv_out,
           ssm_conv_w, ssm_conv_b, ssm_a_log, ssm_dt_bias, ssm_d, ssm_norm, ssm_out,
           attn_out, fnet_out, w_gate, b_gate, w_out, norm_ffn, ffn_w1, ffn_w3, ffn_w2,
           moe_router, moe_w1, moe_w3, moe_w2, ple_gate, ple_proj, final_norm):
    bsz, seq, d = x.shape
    depth = w_in.shape[0]
    t = bsz * seq
    h = x.reshape(t, d)
    bias_tables = [_att_bias_tables(rel_bias, g, dil) for g, (_, dil) in enumerate(ATT_PATTERNS)]
    for l in range(depth):
        *qkv, cnf, z, xbc, fn, dt = _inproj(
            h, norm_mix[l][None], _reorder_w_in(w_in[l]),
            (conv_dw[l], conv_dw_b[l], conv_ln_g[l], conv_ln_b[l]), bsz, seq)
        att = [_attention_group(qkv[g], bias_tables[g], dil, bsz, seq, f"attn{g}")
               for g, (_, dil) in enumerate(ATT_PATTERNS)]
        ssd = _ssd(z, xbc, dt, ssm_conv_w[l], ssm_conv_b[l], ssm_a_log[l], ssm_dt_bias[l],
                   ssm_d[l], ssm_norm[l], bsz, seq)
        fnt = _fourier(fn, bsz, seq)
        wbr = jnp.stack([attn_out[l], conv_out[l], ssm_out[l], fnet_out[l]]).astype(BF16)
        h = _mix(h, norm_mix[l][None], att, (cnf, ssd, fnt), wbr, w_gate[l].astype(BF16),
                 b_gate[l][:, None, :], w_out[l].astype(BF16))
        pl_in = p[l].reshape(t, -1)
        wpg, wpp = ple_gate[l].astype(BF16), ple_proj[l].astype(BF16)
        i = l // 2
        if l % 2 == 0:
            h = _ffn(h, norm_ffn[l][None], ffn_w1[i].astype(BF16), ffn_w3[i].astype(BF16),
                     ffn_w2[i].astype(BF16), pl_in, wpg, wpp)
        else:
            h = _moe(h, norm_ffn[l][None], moe_router[i], moe_w1[i].astype(BF16),
                     moe_w3[i].astype(BF16), moe_w2[i].astype(BF16), pl_in, wpg, wpp)
    return _final_norm(h, final_norm[None]).reshape(bsz, seq, d)
```

```python
import functools
import math

import numpy as np
import jax
import jax.numpy as jnp
from jax import lax
from jax.experimental import pallas as pl
from jax.experimental.pallas import tpu as pltpu

F32 = jnp.float32
BF16 = jnp.bfloat16
HI = lax.Precision.HIGHEST

EPS = 1e-6
N_BRANCHES = 4
CONV_CH = 512
CONV_K = 31
SSM_HEADS = 8
SSM_HEAD_DIM = 64
SSM_INNER = SSM_HEADS * SSM_HEAD_DIM
SSM_GROUPS = 2
SSM_STATE = 64
SSM_CONV = 5
SSM_CONV_CH = SSM_INNER + 2 * SSM_GROUPS * SSM_STATE
SSM_CHUNK = 128
ATT_PATTERNS = ((128, 1), (512, 4), (2048, 16))
ATT_GROUPS = len(ATT_PATTERNS)
ATT_HEADS = 8
ATT_HEAD_DIM = 64
ATT_WIDTH = ATT_HEADS * ATT_HEAD_DIM
REL_BUCKETS = 32
REL_MAX_DIST = 1024
FNET_GROUPS = 4
FNET_GROUP_DIM = 128
FNET_WIDTH = FNET_GROUPS * FNET_GROUP_DIM
N_EXPERTS = 8
TOP_K = 2

ATT_IN_COLS = 3 * ATT_GROUPS * ATT_WIDTH
CONV_IN_COLS = 2 * CONV_CH
SSM_IN_COLS = SSM_INNER + SSM_CONV_CH + 2 * SSM_HEADS
OFF_CONV = ATT_IN_COLS
OFF_SSM = OFF_CONV + CONV_IN_COLS
OFF_FNET = OFF_SSM + SSM_IN_COLS

LANES = 128
DT_PAD = LANES
V7X_VMEM_BYTES = 64 * 1024 * 1024
VMEM_LIMIT = V7X_VMEM_BYTES - 8 * 1024 * 1024

MIX_ROWS = 512
FFN_ROWS, FFN_COLS = 512, 1536
MOE_TOKENS, MOE_COLS = 1024, 1792
NORM_ROWS = 1024
SEQ_ROWS = 512


def _cparams(sem):
    return pltpu.CompilerParams(dimension_semantics=sem, vmem_limit_bytes=VMEM_LIMIT)


def _const_spec(shape):
    nd = len(shape)
    return pl.BlockSpec(shape, lambda *_: (0,) * nd, pipeline_mode=pl.Buffered(1))


def _rms(x, g):
    return x * lax.rsqrt(jnp.mean(x * x, axis=-1, keepdims=True) + EPS) * g


_SEC_QKV = (0, ATT_IN_COLS)
_SEC_CONV = (_SEC_QKV[0] + _SEC_QKV[1], CONV_IN_COLS)
_SEC_Z = (_SEC_CONV[0] + _SEC_CONV[1], SSM_INNER)
_SEC_XBC = (_SEC_Z[0] + _SEC_Z[1], SSM_CONV_CH)
_SEC_FNET = (_SEC_XBC[0] + _SEC_XBC[1], FNET_WIDTH)
_SEC_DT = (_SEC_FNET[0] + _SEC_FNET[1], DT_PAD)
_IN_COLS_PAD = _SEC_DT[0] + _SEC_DT[1]
_MM_CHUNK = 512


ATT_TILE = 256
QKV_COLS = 3 * ATT_WIDTH
INPROJ_ROWS = 2 * ATT_TILE


def _reorder_w_in(w):
    d = w.shape[0]
    qkv = w[:, :OFF_CONV].reshape(d, 3, ATT_GROUPS, ATT_WIDTH).transpose(0, 2, 1, 3)
    ssm = w[:, OFF_SSM:OFF_FNET]
    dt = ssm[:, SSM_INNER + SSM_CONV_CH:]
    parts = [qkv.reshape(d, OFF_CONV), w[:, OFF_CONV:OFF_SSM], ssm[:, :SSM_INNER],
             ssm[:, SSM_INNER:SSM_INNER + SSM_CONV_CH], w[:, OFF_FNET:],
             dt, jnp.zeros((d, DT_PAD - dt.shape[1]), w.dtype)]
    return jnp.concatenate(parts, axis=1).astype(BF16)


def _deinterleave_matrix(dil):
    s = np.arange(ATT_TILE)
    m = np.zeros((ATT_TILE, ATT_TILE), np.float32)
    m[(s % dil) * (ATT_TILE // dil) + s // dil, s] = 1.0
    return m


def _inproj_kernel(h_ref, hp_ref, hn_ref, g_ref, w_ref, perm_ref, dw_ref, dwb_ref, lng_ref,
                   lnb_ref, q0_ref, q1_ref, q2_ref, cnf_ref, z_ref, xbc_ref, fn_ref, dt_ref,
                   pad_ref, sh_ref, *, spt):
    tm = h_ref.shape[0]
    pos = pl.program_id(0) % spt
    norm = lambda ref: _rms(ref[...], g_ref[...]).astype(BF16)
    xn = norm(h_ref)

    xe = jnp.concatenate([norm(hp_ref), xn, norm(hn_ref)], axis=0)
    c0 = _SEC_CONV[0]
    lin = jnp.dot(xe, w_ref[:, c0:c0 + CONV_CH], preferred_element_type=F32)
    gate = jnp.dot(xe, w_ref[:, c0 + CONV_CH:c0 + 2 * CONV_CH], preferred_element_type=F32)
    hid = lin * jax.nn.sigmoid(gate)
    pad_ref[0:CONV_HALO] = jnp.where(pos > 0, hid[:CONV_HALO], 0.0)
    pad_ref[CONV_HALO:CONV_HALO + tm] = hid[CONV_HALO:CONV_HALO + tm]
    pad_ref[CONV_HALO + tm:] = jnp.where(pos < spt - 1, hid[CONV_HALO + tm:], 0.0)
    span = tm + 2 * CONV_HALO - SUBLANES
    for b in range(1, SUBLANES):
        sh_ref[b - 1] = pad_ref[pl.ds(b, span), :]
    first = CONV_HALO - CONV_K // 2

    def conv_rows(c):
        acc = jnp.broadcast_to(dwb_ref[...], (CONV_ROWS, CONV_CH))
        for k in range(CONV_K):
            a, b = divmod(first + k, SUBLANES)
            at = pl.ds(c * CONV_ROWS + a * SUBLANES, CONV_ROWS)
            tap = pad_ref[at, :] if b == 0 else sh_ref[b - 1, at, :]
            acc = acc + dw_ref[pl.ds(k, 1), :] * tap
        mu = jnp.mean(acc, axis=-1, keepdims=True)
        cen = acc - mu
        var = jnp.mean(cen * cen, axis=-1, keepdims=True)
        y = cen * lax.rsqrt(var + EPS) * lng_ref[...] + lnb_ref[...]
        cnf_ref[pl.ds(c * CONV_ROWS, CONV_ROWS), :] = (y * jax.nn.sigmoid(y)).astype(cnf_ref.dtype)

    for c in range(tm // CONV_ROWS):
        conv_rows(c)

    def section(x, sec, store):
        start, width = sec
        for c in range(0, width, _MM_CHUNK):
            cw = min(_MM_CHUNK, width - c)
            store(c, cw, jnp.dot(x, w_ref[:, start + c:start + c + cw],
                                 preferred_element_type=F32))

    def to(ref):
        def store(c, cw, val):
            ref[:, c:c + cw] = val.astype(ref.dtype)
        return store

    def to_fnet(c, cw, val):
        for g in range(cw // FNET_GROUP_DIM):
            fn_ref[0, c // FNET_GROUP_DIM + g] = val[:, g * FNET_GROUP_DIM:(g + 1) * FNET_GROUP_DIM]

    for g, q_ref in enumerate((q0_ref, q1_ref, q2_ref)):
        x = xn
        if ATT_PATTERNS[g][1] > 1:
            x = jnp.concatenate(
                [jnp.dot(perm_ref[g], xn[i * ATT_TILE:(i + 1) * ATT_TILE],
                         preferred_element_type=F32) for i in range(xn.shape[0] // ATT_TILE)],
                axis=0).astype(BF16)
        section(x, (g * QKV_COLS, QKV_COLS), to(q_ref))
    section(xn, _SEC_Z, to(z_ref))
    section(xn, _SEC_XBC, to(xbc_ref))
    section(xn, _SEC_FNET, to_fnet)
    section(xn, _SEC_DT, to(dt_ref))


def _inproj(h, g, w, conv, bsz, seq):
    t, d = h.shape
    tm = INPROJ_ROWS
    spt = seq // tm
    dw, dw_b, ln_g, ln_b = conv
    perm = jnp.asarray(np.stack([_deinterleave_matrix(dil) for _, dil in ATT_PATTERNS]), BF16)
    row = lambda n: pl.BlockSpec((tm, n), lambda i: (i, 0))
    hb = tm // CONV_HALO
    n_hb = t // CONV_HALO
    qkv = jax.ShapeDtypeStruct((t, QKV_COLS), BF16)
    vec = _const_spec((1, CONV_CH))
    return pl.pallas_call(
        functools.partial(_inproj_kernel, spt=spt),
        grid=(t // tm,),
        in_specs=[row(d),
                  pl.BlockSpec((CONV_HALO, d), lambda i: (jnp.maximum(i * hb - 1, 0), 0)),
                  pl.BlockSpec((CONV_HALO, d), lambda i: (jnp.minimum((i + 1) * hb, n_hb - 1), 0)),
                  _const_spec((1, d)), _const_spec(w.shape), _const_spec(perm.shape),
                  _const_spec(dw.shape), vec, vec, vec],
        out_specs=[row(QKV_COLS), row(QKV_COLS), row(QKV_COLS), row(CONV_CH),
                   row(SSM_INNER), row(SSM_CONV_CH),
                   pl.BlockSpec((1, FNET_GROUPS, tm, FNET_GROUP_DIM),
                                lambda i: (i // spt, 0, i % spt, 0)),
                   row(DT_PAD)],
        out_shape=[qkv, qkv, qkv,
                   jax.ShapeDtypeStruct((t, CONV_CH), BF16),
                   jax.ShapeDtypeStruct((t, SSM_INNER), F32),
                   jax.ShapeDtypeStruct((t, SSM_CONV_CH), F32),
                   jax.ShapeDtypeStruct((bsz, FNET_GROUPS, seq, FNET_GROUP_DIM), F32),
                   jax.ShapeDtypeStruct((t, DT_PAD), F32)],
        scratch_shapes=[pltpu.VMEM((tm + 2 * CONV_HALO, CONV_CH), F32),
                        pltpu.VMEM((SUBLANES - 1, tm + 2 * CONV_HALO - SUBLANES, CONV_CH), F32)],
        compiler_params=_cparams(("parallel",)),
        name="inproj",
    )(h, h, h, g, w, perm, dw, dw_b[None], ln_g[None], ln_b[None])


def _split_dot(v, m):
    hi = v.astype(BF16)
    lo = (v - hi.astype(F32)).astype(BF16)
    return (jnp.dot(hi, m, preferred_element_type=F32) + jnp.dot(lo, m, preferred_element_type=F32))


def _interleave(pt, v):
    n = v.shape[1]
    if v.dtype != BF16:
        hi = v.astype(BF16)
        v = jnp.concatenate([hi, (v - hi.astype(F32)).astype(BF16)], axis=1)
    tiles = []
    for i in range(v.shape[0] // ATT_TILE):
        r = jnp.dot(pt, v[i * ATT_TILE:(i + 1) * ATT_TILE], preferred_element_type=F32)
        tiles.append(r if r.shape[1] == n else r[:, :n] + r[:, n:])
    return jnp.concatenate(tiles, axis=0)


def _mix_kernel(h_ref, g_ref, o0_ref, o1_ref, o2_ref, l0_ref, l1_ref, l2_ref, b1_ref, b2_ref,
                b3_ref, pt_ref, hx_ref, wbr_ref, wg_ref, cg_ref, wo_ref, o_ref):
    h = h_ref[...]
    xn = _rms(h, g_ref[...]).astype(BF16)
    outs, lses = [], []
    for g, (og_ref, lg_ref) in enumerate(zip((o0_ref, o1_ref, o2_ref), (l0_ref, l1_ref, l2_ref))):
        if ATT_PATTERNS[g][1] > 1:
            outs.append(_interleave(pt_ref[g], og_ref[...]))
            lses.append(_interleave(pt_ref[g], lg_ref[...]))
        else:
            outs.append(og_ref[...].astype(F32))
            lses.append(lg_ref[...])
    top = jnp.maximum(jnp.maximum(lses[0], lses[1]), lses[2])
    es = [jnp.exp(l - top) for l in lses]
    inv = 1.0 / (es[0] + es[1] + es[2])
    att = None
    for e, og in zip(es, outs):
        term = og * jnp.dot((e * inv).astype(BF16), hx_ref[...], preferred_element_type=F32)
        att = term if att is None else att + term
    acc = None
    for b, hid in enumerate((att.astype(BF16), b1_ref[...], b2_ref[...], b3_ref[...])):
        gate = jax.nn.sigmoid(jnp.dot(xn, wg_ref[b], preferred_element_type=F32) + cg_ref[b])
        br = jnp.dot(hid, wbr_ref[b], preferred_element_type=F32)
        acc = gate * br if acc is None else acc + gate * br
    o_ref[...] = h + jnp.dot(acc.astype(BF16), wo_ref[...], preferred_element_type=F32)


def _mix(h, g, att, others, wbr, wg, cg, wo):
    t, d = h.shape
    tm = MIX_ROWS
    row = lambda n: pl.BlockSpec((tm, n), lambda i: (i, 0))
    head_expand = np.zeros((LANES, ATT_WIDTH), np.float32)
    for hd in range(ATT_HEADS):
        head_expand[hd, hd * ATT_HEAD_DIM:(hd + 1) * ATT_HEAD_DIM] = 1.0
    head_expand = jnp.asarray(head_expand, BF16)
    unperm = jnp.asarray(np.stack([_deinterleave_matrix(dil).T for _, dil in ATT_PATTERNS]), BF16)
    outs = [o for o, _ in att]
    lses = [l for _, l in att]
    return pl.pallas_call(
        _mix_kernel,
        grid=(t // tm,),
        in_specs=[row(d), _const_spec((1, d))] + [row(a.shape[1]) for a in outs + lses + list(others)]
                 + [_const_spec(unperm.shape), _const_spec(head_expand.shape), _const_spec(wbr.shape),
                    _const_spec(wg.shape), _const_spec(cg.shape), _const_spec(wo.shape)],
        out_specs=row(d),
        out_shape=jax.ShapeDtypeStruct((t, d), F32),
        compiler_params=_cparams(("parallel",)),
        name="mix",
    )(h, g, *outs, *lses, *others, unperm, head_expand, wbr, wg, cg, wo)


def _ple(h2, p_ref, wpg_ref, wpp_ref):
    gate = jax.nn.sigmoid(jnp.dot(h2.astype(BF16), wpg_ref[...], preferred_element_type=F32))
    pe = jnp.dot(p_ref[...].astype(BF16), wpp_ref[...], preferred_element_type=F32)
    return h2 + gate * pe


def _swiglu_partial(xn, w1, w3, w2):
    a = jnp.dot(xn, w1, preferred_element_type=F32)
    b = jnp.dot(xn, w3, preferred_element_type=F32)
    hid = a * jax.nn.sigmoid(a) * b
    return jnp.dot(hid.astype(BF16), w2, preferred_element_type=F32)


def _ffn_kernel(h_ref, g_ref, w1_ref, w3_ref, w2_ref, p_ref, wpg_ref, wpp_ref, o_ref):
    h = h_ref[...]
    xn = _rms(h, g_ref[...]).astype(BF16)
    acc = h
    f = w1_ref.shape[1]
    for c0 in range(0, f, FFN_COLS):
        c1 = min(c0 + FFN_COLS, f)
        acc = acc + _swiglu_partial(xn, w1_ref[:, c0:c1], w3_ref[:, c0:c1], w2_ref[c0:c1, :])
    o_ref[...] = _ple(acc, p_ref, wpg_ref, wpp_ref)


def _ffn(h, g, w1, w3, w2, p, wpg, wpp):
    t, d = h.shape
    tm = FFN_ROWS
    row = lambda n: pl.BlockSpec((tm, n), lambda i: (i, 0))
    return pl.pallas_call(
        _ffn_kernel,
        grid=(t // tm,),
        in_specs=[row(d), _const_spec((1, d)), _const_spec(w1.shape), _const_spec(w3.shape),
                  _const_spec(w2.shape), row(p.shape[1]), _const_spec(wpg.shape),
                  _const_spec(wpp.shape)],
        out_specs=row(d),
        out_shape=jax.ShapeDtypeStruct((t, d), F32),
        compiler_params=_cparams(("parallel",)),
        name="ffn",
    )(h, g, w1, w3, w2, p, wpg, wpp)


MOE_ROWS = 128
MOE_MAX_BLOCKS = 2


def _moe_route(logits):
    ne, tm = logits.shape
    eidx = lax.broadcasted_iota(jnp.int32, logits.shape, 0)
    m1 = jnp.max(logits, axis=0, keepdims=True)
    i1 = jnp.min(jnp.where(logits == m1, eidx, ne), axis=0, keepdims=True)
    rest = jnp.where(eidx == i1, -jnp.inf, logits)
    m2 = jnp.max(rest, axis=0, keepdims=True)
    i2 = jnp.min(jnp.where(rest == m2, eidx, ne), axis=0, keepdims=True)
    e2 = jnp.exp(m2 - m1)
    den = 1.0 + e2
    combine = jnp.where(eidx == i1, 1.0 / den, 0.0) + jnp.where(eidx == i2, e2 / den, 0.0)
    routed = jnp.where((eidx == i1) | (eidx == i2), 1.0, 0.0)
    r = lax.broadcasted_iota(jnp.int32, (LANES, LANES), 0)
    c = lax.broadcasted_iota(jnp.int32, (LANES, LANES), 1)
    before = jnp.where(r < c, 1.0, 0.0).astype(BF16)
    counts = jnp.zeros((ne, 1), F32)
    slots = []
    for k in range(tm // LANES):
        blk = routed[:, k * LANES:(k + 1) * LANES]
        slots.append(jnp.dot(blk.astype(BF16), before, preferred_element_type=F32) + counts)
        counts = counts + jnp.sum(blk, axis=1, keepdims=True)
    slot = jnp.where(routed > 0.0, jnp.concatenate(slots, axis=1), -1.0).astype(jnp.int32)
    return combine, slot, counts


def _moe_kernel(h_ref, g_ref, rt_ref, w1_ref, w3_ref, w2_ref, p_ref, wpg_ref, wpp_ref, o_ref,
                xn_ref, comb_ref, slot_ref, cnt_ref, xe_ref, ye_ref):
    e = pl.program_id(1)
    j = pl.program_id(2)
    ne = pl.num_programs(1)
    tm = xn_ref.shape[0]

    @pl.when((e == 0) & (j == 0))
    def _():
        h = h_ref[...]
        xn = _rms(h, g_ref[...])
        xn_ref[...] = xn.astype(BF16)
        logits = lax.dot_general(rt_ref[...], xn, (((1,), (1,)), ((), ())),
                                 preferred_element_type=F32, precision=HI)
        combine, slot, counts = _moe_route(logits)
        comb_ref[...] = combine
        slot_ref[...] = slot
        for k in range(comb_ref.shape[0]):
            cnt_ref[k] = jnp.sum(counts[k:k + 1, :]).astype(jnp.int32)
        o_ref[...] = h

    n_blocks = (cnt_ref[e] + MOE_ROWS - 1) // MOE_ROWS

    def for_row_blocks(body):
        full = MOE_MAX_BLOCKS * MOE_ROWS

        def whole(i, carry):
            body(pl.multiple_of(i * full, full), full)
            return carry
        lax.fori_loop(0, n_blocks // MOE_MAX_BLOCKS, whole, 0)
        base = pl.multiple_of((n_blocks // MOE_MAX_BLOCKS) * full, full)
        for r in range(1, MOE_MAX_BLOCKS):
            @pl.when(n_blocks % MOE_MAX_BLOCKS == r)
            def _():
                body(base, r * MOE_ROWS)

    def one_hot(r0, rows):
        return slot_ref[pl.ds(e, 1), :] == lax.broadcasted_iota(jnp.int32, (rows, tm), 0) + r0

    def swiglu(x):
        return _swiglu_partial(x, w1_ref[0], w3_ref[0], w2_ref[0])

    last_j = pl.num_programs(2) - 1

    @pl.when(j == 0)
    def _():
        def gather_first(r0, rows):
            sel = jnp.where(one_hot(r0, rows), 1.0, 0.0).astype(BF16)
            x = jnp.dot(sel, xn_ref[...], preferred_element_type=F32).astype(BF16)
            xe_ref[pl.ds(r0, rows), :] = x
            ye_ref[pl.ds(r0, rows), :] = swiglu(x)
        for_row_blocks(gather_first)

    @pl.when((j > 0) & (j < last_j))
    def _():
        def middle(r0, rows):
            ye_ref[pl.ds(r0, rows), :] += swiglu(xe_ref[pl.ds(r0, rows), :])
        for_row_blocks(middle)

    @pl.when((j > 0) & (j == last_j))
    def _():
        def last_scatter(r0, rows):
            y = ye_ref[pl.ds(r0, rows), :] + swiglu(xe_ref[pl.ds(r0, rows), :])
            hot = one_hot(r0, rows)
            weight = jnp.sum(jnp.where(hot, comb_ref[pl.ds(e, 1), :], 0.0), axis=1, keepdims=True)
            o_ref[...] += lax.dot_general(jnp.where(hot, 1.0, 0.0).astype(BF16),
                                          (y * weight).astype(BF16),
                                          (((0,), (0,)), ((), ())), preferred_element_type=F32)
        for_row_blocks(last_scatter)

    @pl.when((e == ne - 1) & (j == pl.num_programs(2) - 1))
    def _():
        o_ref[...] = _ple(o_ref[...], p_ref, wpg_ref, wpp_ref)


def _moe(h, g, router, w1, w3, w2, p, wpg, wpp):
    t, d = h.shape
    ne, _, f = w1.shape
    tm, tf = MOE_TOKENS, MOE_COLS
    assert f % tf == 0 and f // tf >= 2
    row = lambda n: pl.BlockSpec((tm, n), lambda i, e, j: (i, 0))
    return pl.pallas_call(
        _moe_kernel,
        grid=(t // tm, ne, f // tf),
        in_specs=[row(d), _const_spec((1, d)), _const_spec((ne, d)),
                  pl.BlockSpec((1, d, tf), lambda i, e, j: (e, 0, j)),
                  pl.BlockSpec((1, d, tf), lambda i, e, j: (e, 0, j)),
                  pl.BlockSpec((1, tf, d), lambda i, e, j: (e, j, 0)),
                  row(p.shape[1]), _const_spec(wpg.shape), _const_spec(wpp.shape)],
        out_specs=row(d),
        out_shape=jax.ShapeDtypeStruct((t, d), F32),
        scratch_shapes=[pltpu.VMEM((tm, d), BF16), pltpu.VMEM((ne, tm), F32),
                        pltpu.VMEM((ne, tm), jnp.int32), pltpu.SMEM((ne,), jnp.int32),
                        pltpu.VMEM((tm, d), BF16), pltpu.VMEM((tm, d), F32)],
        compiler_params=_cparams(("parallel", "arbitrary", "arbitrary")),
        name="moe",
    )(h, g, router.T, w1, w3, w2, p, wpg, wpp)


def _final_norm_kernel(h_ref, g_ref, o_ref):
    o_ref[...] = _rms(h_ref[...], g_ref[...])


def _final_norm(h, g):
    t, d = h.shape
    tm = NORM_ROWS
    row = pl.BlockSpec((tm, d), lambda i: (i, 0))
    return pl.pallas_call(
        _final_norm_kernel, grid=(t // tm,),
        in_specs=[row, _const_spec((1, d))], out_specs=row,
        out_shape=jax.ShapeDtypeStruct((t, d), F32),
        compiler_params=_cparams(("parallel",)), name="final_norm",
    )(h, g)


ATT_HALF = 64
ATT_QB = 128
ATT_KB = ATT_QB + 2 * ATT_HALF
NEG = -1e30
assert all(w // (2 * d) == ATT_HALF for w, d in ATT_PATTERNS)


def _t5_bucket(rel):
    half = REL_BUCKETS // 2
    max_exact = half // 2
    n = np.abs(rel)
    large = max_exact + (np.log(np.maximum(n, 1) / max_exact) / math.log(REL_MAX_DIST / max_exact)
                         * (half - max_exact)).astype(np.int32)
    large = np.minimum(large, half - 1)
    return np.where(rel > 0, half, 0) + np.where(n < max_exact, n, large)


def _att_bias_tables(rel_bias, g, dil):
    i = np.arange(ATT_QB)[:, None]
    j = np.arange(ATT_KB)[None, :]
    rel = j - ATT_HALF - i
    band = np.abs(rel) <= ATT_HALF
    pick = np.eye(REL_BUCKETS, dtype=np.float32)[_t5_bucket(dil * rel)]
    heads = rel_bias[:, g * ATT_HEADS:(g + 1) * ATT_HEADS].astype(F32)
    bias = jnp.einsum('qkb,bh->hqk', pick, heads, precision=lax.Precision.HIGHEST)
    tables = []
    for v in range(4):
        ok = band
        if v & 1:
            ok = ok & (j >= ATT_HALF)
        if v & 2:
            ok = ok & (j < ATT_QB + ATT_HALF)
        tables.append(jnp.where(ok[None], bias, NEG))
    return jnp.stack(tables)


def _attn_kernel(q_ref, k_ref, v_ref, kp_ref, kn_ref, vp_ref, vn_ref, bias_ref, o_ref, lse_ref,
                 qbuf, kbuf, vbuf, obuf, lbuf, *, tq, n_blocks):
    flat = lambda ref: ref[0].reshape(-1, ref.shape[-1])
    qbuf[...] = flat(q_ref)
    kbuf[0:ATT_HALF] = flat(kp_ref)
    kbuf[ATT_HALF:ATT_HALF + tq] = flat(k_ref)
    kbuf[ATT_HALF + tq:] = flat(kn_ref)
    vbuf[0:ATT_HALF] = flat(vp_ref)
    vbuf[ATT_HALF:ATT_HALF + tq] = flat(v_ref)
    vbuf[ATT_HALF + tq:] = flat(vn_ref)
    nsb = tq // ATT_QB
    first = pl.program_id(2) * nsb
    lane = lax.broadcasted_iota(jnp.int32, (ATT_QB, LANES), 1)
    low = lane < ATT_HEAD_DIM
    lane_row = lax.broadcasted_iota(jnp.int32, (1, LANES), 1)
    keep = [(lane_row < ATT_HEAD_DIM).astype(BF16), (lane_row >= ATT_HEAD_DIM).astype(BF16)]
    ones = jnp.ones((ATT_KB, LANES), BF16)

    def block(sb, carry):
        r0 = pl.multiple_of(sb * ATT_QB, ATT_QB)
        gsb = first + sb
        variant = (gsb == 0).astype(jnp.int32) + 2 * (gsb == n_blocks - 1).astype(jnp.int32)
        q = qbuf[pl.ds(r0, ATT_QB), :] * (ATT_HEAD_DIM ** -0.5)
        lse_all = jnp.zeros((ATT_QB, LANES), F32)
        outs = []
        for pair in range(ATT_HEADS // 2):
            cols = slice(pair * LANES, (pair + 1) * LANES)
            qp = q[:, cols]
            kp = kbuf[pl.ds(r0, ATT_KB), cols]
            vp = jnp.concatenate([vbuf[pl.ds(r0, ATT_KB), cols], ones], axis=1)
            res = []
            for half in range(2):
                h = 2 * pair + half
                s = lax.dot_general(qp * keep[half], kp, (((1,), (1,)), ((), ())),
                                    preferred_element_type=F32)
                s = s + bias_ref[variant, h]
                m = jnp.max(s, axis=-1, keepdims=True)
                e = jnp.exp(s - m)
                pv = jnp.dot(e.astype(BF16), vp, preferred_element_type=F32)
                den = pv[:, LANES:]
                res.append(pv[:, :LANES] / den)
                lse_all = jnp.where(lane == h, m + jnp.log(den), lse_all)
            outs.append(jnp.where(low, res[0], res[1]))
        obuf[pl.ds(r0, ATT_QB), :] = jnp.concatenate(outs, axis=1).astype(obuf.dtype)
        lbuf[pl.ds(r0, ATT_QB), :] = lse_all
        return carry

    lax.fori_loop(0, nsb, block, 0, unroll=4)
    o_ref[0] = obuf[...].reshape(o_ref.shape[1:])
    lse_ref[0] = lbuf[...].reshape(lse_ref.shape[1:])


def _attention_group(qkv, bias_tables, dil, bsz, seq, name):
    sub_len = seq // dil
    assert sub_len % ATT_QB == 0 and seq % ATT_TILE == 0 and ATT_TILE % dil == 0
    rows = ATT_TILE // dil
    tq = min(SEQ_ROWS, sub_len)
    w = ATT_WIDTH
    hrows = min(rows, ATT_HALF)
    n_hb = sub_len // ATT_HALF

    def view(a, chunk):
        return a.reshape(bsz, (seq // ATT_TILE) * (rows // chunk), dil, chunk, a.shape[-1])

    def main(part):
        return pl.BlockSpec((1, tq // rows, None, rows, w), lambda b, r, n: (b, n, r, 0, part))

    def halo(part, nxt):
        if nxt:
            blk = lambda n: jnp.minimum((n + 1) * (tq // ATT_HALF), n_hb - 1)
        else:
            blk = lambda n: jnp.maximum(n * (tq // ATT_HALF) - 1, 0)
        return pl.BlockSpec((1, ATT_HALF // hrows, None, hrows, w),
                            lambda b, r, n: (b, blk(n), r, 0, part))

    mv, hv = view(qkv, rows), view(qkv, hrows)
    o, lse = pl.pallas_call(
        functools.partial(_attn_kernel, tq=tq, n_blocks=sub_len // ATT_QB),
        grid=(bsz, dil, sub_len // tq),
        in_specs=[main(0), main(1), main(2), halo(1, False), halo(1, True), halo(2, False),
                  halo(2, True), _const_spec(bias_tables.shape)],
        out_specs=[pl.BlockSpec((1, tq // rows, None, rows, w), lambda b, r, n: (b, n, r, 0, 0)),
                   pl.BlockSpec((1, tq // rows, None, rows, LANES), lambda b, r, n: (b, n, r, 0, 0))],
        out_shape=[jax.ShapeDtypeStruct((bsz, seq // ATT_TILE, dil, rows, w), BF16),
                   jax.ShapeDtypeStruct((bsz, seq // ATT_TILE, dil, rows, LANES), F32)],
        scratch_shapes=[pltpu.VMEM((tq, w), BF16),
                        pltpu.VMEM((tq + 2 * ATT_HALF, w), BF16),
                        pltpu.VMEM((tq + 2 * ATT_HALF, w), BF16),
                        pltpu.VMEM((tq, w), BF16), pltpu.VMEM((tq, LANES), F32)],
        compiler_params=_cparams(("parallel", "parallel", "parallel")),
        name=name,
    )(mv, mv, mv, hv, hv, hv, hv, bias_tables)
    return o.reshape(bsz * seq, w), lse.reshape(bsz * seq, LANES)


SUBLANES = 8
CONV_HALO = 16
CONV_ROWS = 64


SSD_HALO = SUBLANES
SSD_ROWS = 64
SSD_BC = SSM_GROUPS * SSM_STATE
HEADS_PER_GROUP = SSM_HEADS // SSM_GROUPS
GROUP_LANES = HEADS_PER_GROUP * SSM_HEAD_DIM
assert SSD_BC == LANES and SSM_CHUNK == LANES and 2 * SSM_HEADS <= LANES


def _ssd_pre_kernel(x_ref, xp_ref, xn_ref, dt_ref, cw_ref, cb_ref, dtb_ref, arow_ref, tri_ref,
                    ex_ref, xo_ref, dto_ref, acs_ref, nf_ref, nb_ref, cy_ref, hp_ref, sh_ref,
                    *, ts, phases):
    n = pl.program_id(1)
    hp_ref[0:SSD_HALO] = jnp.where(n > 0, xp_ref[0], 0.0)
    hp_ref[SSD_HALO:SSD_HALO + ts] = x_ref[0]
    hp_ref[SSD_HALO + ts:] = jnp.where(n < pl.num_programs(1) - 1, xn_ref[0], 0.0)
    for i, b in enumerate(phases):
        sh_ref[i] = hp_ref[pl.ds(b, ts + SSD_HALO), :]
    first = SSD_HALO - SSM_CONV // 2

    def rows(c, carry):
        r0 = pl.multiple_of(c * SSD_ROWS, SSD_ROWS)
        acc = jnp.broadcast_to(cb_ref[...], (SSD_ROWS, SSM_CONV_CH))
        for k in range(SSM_CONV):
            a, b = divmod(first + k, SUBLANES)
            acc = acc + cw_ref[pl.ds(k, 1), :] * sh_ref[phases.index(b),
                                                        pl.ds(r0 + a * SUBLANES, SSD_ROWS), :]
        xo_ref[0, pl.ds(r0, SSD_ROWS), :] = acc * jax.nn.sigmoid(acc)
        return carry

    lax.fori_loop(0, ts // SSD_ROWS, rows, 0, unroll=True)
    x = dt_ref[0] + dtb_ref[...]
    softplus = jnp.maximum(x, 0.0) + jnp.log1p(jnp.exp(-jnp.abs(x)))
    lane = lax.broadcasted_iota(jnp.int32, x.shape, 1)
    dtv = jnp.where(lane < 2 * SSM_HEADS, softplus, 0.0)
    dto_ref[0] = dtv

    nch = ts // SSM_CHUNK
    chunk = lambda k: slice(k * SSM_CHUNK, (k + 1) * SSM_CHUNK)
    dta = dtv * arow_ref[...]
    acs_all = _cumsum_both(jnp.concatenate([dta[chunk(k)] for k in range(nch)], axis=1), tri_ref)
    clane = lax.broadcasted_iota(jnp.int32, (SSM_CHUNK, DT_PAD), 1)
    ws, totals = [], [[], []]
    for k in range(nch):
        acs = acs_all[:, chunk(k)]
        acs_ref[0, chunk(k), :] = acs
        tot_f, tot_b = acs[SSM_CHUNK - 1:SSM_CHUNK, :], acs[0:1, :]
        to_end = jnp.where(clane < SSM_HEADS, tot_f - acs,
                           jnp.where(clane < 2 * SSM_HEADS, tot_b - acs, 0.0))
        ws.append(dtv[chunk(k)] * jnp.exp(to_end))
        totals[0].append(jnp.exp(tot_f))
        totals[1].append(jnp.exp(tot_b))
    w_all = jnp.concatenate(ws, axis=0)
    xs = xo_ref[0, :, :SSM_INNER]
    hl = lax.broadcasted_iota(jnp.int32, (SSM_STATE, SSM_INNER), 1)
    pad_rows = jnp.zeros((SUBLANES - nch % SUBLANES, DT_PAD), F32)
    for d, new_ref in enumerate((nf_ref, nb_ref)):
        xw = (xs * _expand(w_all, ex_ref, d)).astype(BF16)
        carry = _expand(jnp.concatenate(totals[d] + [pad_rows], axis=0), ex_ref, d)
        for k in range(nch):
            bmat = xo_ref[0, chunk(k), SSM_INNER:SSM_INNER + SSD_BC].astype(BF16)
            full = lax.dot_general(bmat, xw[chunk(k)], (((0,), (0,)), ((), ())),
                                   preferred_element_type=F32)
            new = full[:SSM_STATE]
            for g in range(1, SSM_GROUPS):
                new = jnp.where(hl >= g * GROUP_LANES, full[g * SSM_STATE:(g + 1) * SSM_STATE], new)
            new_ref[0, k] = new
            cy_ref[0, k, d:d + 1, :] = carry[k:k + 1]


def _ssd_pre(xbc, dt_raw, conv_w, conv_b, dt_bias, consts, bsz, seq):
    ts = min(SEQ_ROWS, seq)
    nc = seq // SSM_CHUNK
    nch = ts // SSM_CHUNK
    a_row, tri, expand = consts
    new = jax.ShapeDtypeStruct((bsz, nc, SSM_STATE, SSM_INNER), F32)
    new_spec = pl.BlockSpec((1, nch, SSM_STATE, SSM_INNER), lambda b, n: (b, n, 0, 0))
    xv = xbc.reshape(bsz, seq, SSM_CONV_CH)
    dv = dt_raw.reshape(bsz, seq, DT_PAD)
    hb = ts // SSD_HALO
    n_hb = seq // SSD_HALO
    first = SSD_HALO - SSM_CONV // 2
    phases = tuple(sorted({(first + k) % SUBLANES for k in range(SSM_CONV)}))
    dtb = jnp.zeros((1, DT_PAD), F32).at[0, :2 * SSM_HEADS].set(dt_bias.reshape(-1))
    return pl.pallas_call(
        functools.partial(_ssd_pre_kernel, ts=ts, phases=phases),
        grid=(bsz, seq // ts),
        in_specs=[pl.BlockSpec((1, ts, SSM_CONV_CH), lambda b, n: (b, n, 0)),
                  pl.BlockSpec((1, SSD_HALO, SSM_CONV_CH),
                               lambda b, n: (b, jnp.maximum(n * hb - 1, 0), 0)),
                  pl.BlockSpec((1, SSD_HALO, SSM_CONV_CH),
                               lambda b, n: (b, jnp.minimum((n + 1) * hb, n_hb - 1), 0)),
                  pl.BlockSpec((1, ts, DT_PAD), lambda b, n: (b, n, 0)),
                  _const_spec(conv_w.shape), _const_spec((1, SSM_CONV_CH)),
                  _const_spec((1, DT_PAD)), _const_spec(a_row.shape), _const_spec(tri.shape),
                  _const_spec(expand.shape)],
        out_specs=[pl.BlockSpec((1, ts, SSM_CONV_CH), lambda b, n: (b, n, 0)),
                   pl.BlockSpec((1, ts, DT_PAD), lambda b, n: (b, n, 0)),
                   pl.BlockSpec((1, ts, DT_PAD), lambda b, n: (b, n, 0)),
                   new_spec, new_spec,
                   pl.BlockSpec((1, nch, 2, SSM_INNER), lambda b, n: (b, n, 0, 0))],
        out_shape=[jax.ShapeDtypeStruct((bsz, seq, SSM_CONV_CH), F32),
                   jax.ShapeDtypeStruct((bsz, seq, DT_PAD), F32),
                   jax.ShapeDtypeStruct((bsz, seq, DT_PAD), F32),
                   new, new, jax.ShapeDtypeStruct((bsz, nc, 2, SSM_INNER), F32)],
        scratch_shapes=[pltpu.VMEM((ts + 2 * SSD_HALO, SSM_CONV_CH), F32),
                        pltpu.VMEM((len(phases), ts + SSD_HALO, SSM_CONV_CH), F32)],
        compiler_params=_cparams(("parallel", "parallel")),
        name="ssd_pre",
    )(xv, xv, xv, dv, conv_w, conv_b[None], dtb, a_row, tri, expand)


def _ssd_consts(a_log):
    a_row = jnp.zeros((1, DT_PAD), F32).at[0, :2 * SSM_HEADS].set(-jnp.exp(a_log.reshape(-1)))
    lower = np.tril(np.ones((SSM_CHUNK, SSM_CHUNK), np.float32))
    tri = jnp.asarray(np.concatenate([lower, lower.T], axis=0), BF16)
    expand = np.zeros((2, DT_PAD, SSM_INNER), np.float32)
    for d in range(2):
        for h in range(SSM_HEADS):
            expand[d, d * SSM_HEADS + h, h * SSM_HEAD_DIM:(h + 1) * SSM_HEAD_DIM] = 1.0
    return a_row, tri, jnp.asarray(expand, BF16)


def _cumsum_both(dta, tri_ref):
    hi = dta.astype(BF16)
    r1 = dta - hi.astype(F32)
    mid = r1.astype(BF16)
    lo = (r1 - mid.astype(F32)).astype(BF16)
    both = (jnp.dot(tri_ref[...], hi, preferred_element_type=F32)
            + jnp.dot(tri_ref[...], mid, preferred_element_type=F32)
            + jnp.dot(tri_ref[...], lo, preferred_element_type=F32))
    lane = lax.broadcasted_iota(jnp.int32, dta.shape, 1) % DT_PAD
    return jnp.where(lane < SSM_HEADS, both[:SSM_CHUNK], both[SSM_CHUNK:])


def _expand(v, ex_ref, d):
    return _split_dot(v, ex_ref[d])


def _ssd_scan_kernel(nf_ref, nb_ref, cf_ref, cb_ref, pf_ref, pb_ref, sf_ref, sb_ref, *, nch):
    @pl.when(pl.program_id(1) == 0)
    def _():
        sf_ref[...] = jnp.zeros_like(sf_ref)
        sb_ref[...] = jnp.zeros_like(sb_ref)

    def scan(d, order, new_ref, carry_ref, st_ref, out_ref):
        st = st_ref[...]
        for k in order:
            out_ref[0, k] = st.astype(out_ref.dtype)
            st = st * carry_ref[0, k, d:d + 1, :] + new_ref[0, k]
        st_ref[...] = st

    scan(0, range(nch), nf_ref, cf_ref, sf_ref, pf_ref)
    scan(1, range(nch - 1, -1, -1), nb_ref, cb_ref, sb_ref, pb_ref)


SSD_SCAN_CHUNKS = 16


def _ssd_states(new_f, new_b, carry, bsz, seq):
    nc = seq // SSM_CHUNK
    nch = min(SSD_SCAN_CHUNKS, nc)
    steps = nc // nch
    fwd = lambda b, c: (b, c, 0, 0)
    bwd = lambda b, c: (b, steps - 1 - c, 0, 0)
    st = jax.ShapeDtypeStruct((bsz, nc, SSM_STATE, SSM_INNER), BF16)
    blk = (1, nch, SSM_STATE, SSM_INNER)
    return pl.pallas_call(
        functools.partial(_ssd_scan_kernel, nch=nch),
        grid=(bsz, steps),
        in_specs=[pl.BlockSpec(blk, fwd), pl.BlockSpec(blk, bwd),
                  pl.BlockSpec((1, nch, 2, SSM_INNER), fwd),
                  pl.BlockSpec((1, nch, 2, SSM_INNER), bwd)],
        out_specs=[pl.BlockSpec(blk, fwd), pl.BlockSpec(blk, bwd)],
        out_shape=[st, st],
        scratch_shapes=[pltpu.VMEM((SSM_STATE, SSM_INNER), F32),
                        pltpu.VMEM((SSM_STATE, SSM_INNER), F32)],
        compiler_params=_cparams(("parallel", "arbitrary")),
        name="ssd_scan",
    )(new_f, new_b, carry, carry)


def _ssd_out_kernel(x_ref, dt_ref, acs_ref, z_ref, pf_ref, pb_ref, ex_ref, dskip_ref, ng_ref,
                    o_ref, *, nck):
    for k in range(nck):
        _ssd_out_chunk(k, x_ref, dt_ref, acs_ref, z_ref, pf_ref, pb_ref, ex_ref, dskip_ref,
                       ng_ref, o_ref)


def _ssd_out_chunk(k, x_ref, dt_ref, acs_ref, z_ref, pf_ref, pb_ref, ex_ref, dskip_ref, ng_ref,
                   o_ref):
    rows = slice(k * SSM_CHUNK, (k + 1) * SSM_CHUNK)
    xs = x_ref[0, rows, :SSM_INNER]
    bmat = x_ref[0, rows, SSM_INNER:SSM_INNER + SSD_BC].astype(BF16)
    cmat = x_ref[0, rows, SSM_INNER + SSD_BC:].astype(BF16)
    dtv = dt_ref[0, rows, :]
    acs = acs_ref[0, rows, :]
    acs_t = acs.T
    dt_t = dtv.T
    eacs = jnp.exp(acs)
    row = lax.broadcasted_iota(jnp.int32, (SSM_CHUNK, SSM_CHUNK), 0)
    col = lax.broadcasted_iota(jnp.int32, (SSM_CHUNK, SSM_CHUNK), 1)
    past, now = col < row, col == row
    low = lax.broadcasted_iota(jnp.int32, (SSM_CHUNK, LANES), 1) < SSM_HEAD_DIM
    glane = lax.broadcasted_iota(jnp.int32, (1, SSD_BC), 1) // SSM_STATE
    scores = [lax.dot_general(cmat * (glane == g).astype(BF16), bmat,
                              (((1,), (1,)), ((), ())), preferred_element_type=F32)
              for g in range(SSM_GROUPS)]
    xs_b = xs.astype(BF16)
    keep = [low.astype(BF16), (~low).astype(BF16)]
    diag = []
    for pair in range(SSM_HEADS // 2):
        xp = xs_b[:, pair * LANES:(pair + 1) * LANES]
        mats = []
        for half in range(2):
            f = 2 * pair + half
            b = SSM_HEADS + f
            seg = jnp.where(past | now, acs[:, f:f + 1] - acs_t[f:f + 1, :],
                            acs[:, b:b + 1] - acs_t[b:b + 1, :])
            dts = (jnp.where(past, dt_t[f:f + 1, :], dt_t[b:b + 1, :])
                   + jnp.where(now, dt_t[f:f + 1, :], 0.0))
            mats.append((scores[f // HEADS_PER_GROUP] * jnp.exp(seg) * dts).astype(BF16))
        diag.append(jnp.dot(jnp.concatenate(mats, axis=1),
                            jnp.concatenate([xp * keep[0], xp * keep[1]], axis=0),
                            preferred_element_type=F32))
    y = dskip_ref[...] * xs + jnp.concatenate(diag, axis=1)
    hgroup = lax.broadcasted_iota(jnp.int32, (1, SSM_INNER), 1) // GROUP_LANES
    for d, p_ref in enumerate((pf_ref, pb_ref)):
        prev = p_ref[0, k]
        stacked = jnp.concatenate([prev * (hgroup == g).astype(BF16) for g in range(SSM_GROUPS)],
                                  axis=0)
        off = jnp.dot(cmat, stacked, preferred_element_type=F32)
        y = y + off * _expand(eacs, ex_ref, d)
    z = z_ref[0, rows, :]
    y = y * (z * jax.nn.sigmoid(z))
    o_ref[0, rows, :] = _rms(y, ng_ref[...]).astype(o_ref.dtype)


SSD_OUT_CHUNKS = 4


def _ssd_out(xact, dtv, acs, z, prev_f, prev_b, expand, d_skip, norm_g, bsz, seq):
    nc = seq // SSM_CHUNK
    nck = min(SSD_OUT_CHUNKS, nc)
    dsk = jnp.repeat(d_skip, SSM_HEAD_DIM)[None]
    chunk = lambda n: pl.BlockSpec((1, nck * SSM_CHUNK, n), lambda b, c: (b, c, 0))
    state = pl.BlockSpec((1, nck, SSM_STATE, SSM_INNER), lambda b, c: (b, c, 0, 0))
    out = pl.pallas_call(
        functools.partial(_ssd_out_kernel, nck=nck),
        grid=(bsz, nc // nck),
        in_specs=[chunk(SSM_CONV_CH), chunk(DT_PAD), chunk(DT_PAD), chunk(SSM_INNER), state, state,
                  _const_spec(expand.shape), _const_spec(dsk.shape), _const_spec((1, SSM_INNER))],
        out_specs=chunk(SSM_INNER),
        out_shape=jax.ShapeDtypeStruct((bsz, seq, SSM_INNER), BF16),
        compiler_params=_cparams(("parallel", "parallel")),
        name="ssd_out",
    )(xact, dtv, acs, z.reshape(bsz, seq, SSM_INNER), prev_f, prev_b, expand, dsk, norm_g[None])
    return out.reshape(bsz * seq, SSM_INNER)


def _ssd(z, xbc, dt_raw, conv_w, conv_b, a_log, dt_bias, d_skip, norm_g, bsz, seq):
    consts = _ssd_consts(a_log)
    xact, dtv, acs, new_f, new_b, carry = _ssd_pre(xbc, dt_raw, conv_w, conv_b, dt_bias, consts,
                                                   bsz, seq)
    prev_f, prev_b = _ssd_states(new_f, new_b, carry, bsz, seq)
    return _ssd_out(xact, dtv, acs, z, prev_f, prev_b, consts[2], d_skip, norm_g, bsz, seq)


FNET_COLS = 4096


def _dft_cos_sin(n):
    ang = 2.0 * np.pi * np.outer(np.arange(n), np.arange(n)) / n
    return np.cos(ang), np.sin(ang)


def _fnet_consts(seq):
    c = FNET_GROUP_DIM
    n1 = seq // LANES
    c1, s1 = _dft_cos_sin(n1)
    stage1 = np.concatenate([c1, -s1], axis=0)
    ang = 2.0 * np.pi * np.outer(np.arange(n1), np.arange(LANES)) / seq
    twr = np.repeat(np.cos(ang), c, axis=1)
    twi = np.repeat(-np.sin(ang), c, axis=1)
    cc, sc = _dft_cos_sin(c)
    chan = np.block([[cc, -sc], [sc, cc]])
    c2, s2 = _dft_cos_sin(LANES)
    return (jnp.asarray(stage1, BF16), jnp.asarray(twr, F32), jnp.asarray(twi, F32),
            jnp.asarray(chan, BF16), jnp.asarray(c2, BF16), jnp.asarray(s2, BF16))


def _fnet1_kernel(x_ref, f_ref, twr_ref, twi_ref, o_ref, *, n1):
    c = FNET_GROUP_DIM
    a = jnp.dot(f_ref[...], x_ref[0].astype(BF16), preferred_element_type=F32)
    ar, ai = a[:n1], a[n1:]
    twr, twi = twr_ref[...], twi_ref[...]
    re = (ar * twr - ai * twi).astype(o_ref.dtype)
    im = (ar * twi + ai * twr).astype(o_ref.dtype)
    for j in range(re.shape[1] // c):
        o_ref[0, :, (2 * j) * c:(2 * j + 1) * c] = re[:, j * c:(j + 1) * c]
        o_ref[0, :, (2 * j + 1) * c:(2 * j + 2) * c] = im[:, j * c:(j + 1) * c]


def _fnet2_kernel(a_ref, chan_ref, c2_ref, s2_ref, o_ref, g_ref, scr_ref, *, n1, scale):
    c = FNET_GROUP_DIM
    pitch = scr_ref.shape[0] // LANES
    per = min(8, n1)
    for i in range(n1 // per):
        blk = a_ref[0, i * per:(i + 1) * per].reshape(per * LANES, 2 * c)
        g = jnp.dot(blk, chan_ref[...], preferred_element_type=F32).astype(BF16)
        g_ref[i * per:(i + 1) * per] = g.reshape(per, LANES, 2 * c)

    def body(k1, carry):
        g = g_ref[k1]
        y = (jnp.dot(c2_ref[...], g[:, :c], preferred_element_type=F32)
             + jnp.dot(s2_ref[...], g[:, c:], preferred_element_type=F32))
        scr_ref[pl.ds(k1, LANES, stride=pitch), :] = y * scale
        return carry

    lax.fori_loop(0, n1, body, 0, unroll=4)

    def compact(k2, carry):
        src = pl.multiple_of(k2 * pitch, SUBLANES)
        dst = pl.multiple_of(k2 * n1, n1)
        o_ref[0, pl.ds(dst, n1), :] = scr_ref[pl.ds(src, n1), :].astype(o_ref.dtype)
        return carry

    lax.fori_loop(0, LANES, compact, 0, unroll=8)


def _fourier(fn, bsz, seq):
    c = FNET_GROUP_DIM
    assert c == LANES and seq % LANES == 0
    n1 = seq // LANES
    stage1, twr, twi, chan, c2, s2 = _fnet_consts(seq)
    ncols = LANES * c
    nb = min(FNET_COLS, ncols)
    x2 = fn.reshape(bsz * FNET_GROUPS, n1, ncols)
    a = pl.pallas_call(
        functools.partial(_fnet1_kernel, n1=n1),
        grid=(ncols // nb, bsz * FNET_GROUPS),
        in_specs=[pl.BlockSpec((1, n1, nb), lambda j, i: (i, 0, j)),
                  _const_spec(stage1.shape),
                  pl.BlockSpec((n1, nb), lambda j, i: (0, j)),
                  pl.BlockSpec((n1, nb), lambda j, i: (0, j))],
        out_specs=pl.BlockSpec((1, n1, 2 * nb), lambda j, i: (i, 0, j)),
        out_shape=jax.ShapeDtypeStruct((bsz * FNET_GROUPS, n1, 2 * ncols), BF16),
        compiler_params=_cparams(("parallel", "parallel")),
        name="fnet1",
    )(x2, stage1, twr, twi)
    a4 = a.reshape(bsz * FNET_GROUPS, n1, LANES, 2 * c)
    out = pl.pallas_call(
        functools.partial(_fnet2_kernel, n1=n1, scale=1.0 / math.sqrt(seq * c)),
        grid=(bsz, FNET_GROUPS),
        in_specs=[pl.BlockSpec((1, n1, LANES, 2 * c), lambda b, g: (b * FNET_GROUPS + g, 0, 0, 0)),
                  _const_spec(chan.shape), _const_spec(c2.shape), _const_spec(s2.shape)],
        out_specs=pl.BlockSpec((1, seq, c), lambda b, g: (b, 0, g)),
        out_shape=jax.ShapeDtypeStruct((bsz, seq, FNET_WIDTH), BF16),
        scratch_shapes=[pltpu.VMEM((n1, LANES, 2 * c), BF16),
                        pltpu.VMEM((LANES * (n1 + SUBLANES), c), F32)],
        compiler_params=_cparams(("parallel", "parallel")),
        name="fnet2",
    )(a4, chan, c2, s2)
    return out.reshape(bsz * seq, FNET_WIDTH)


def kernel(x, p, rel_bias, norm_mix, w_in, conv_dw, conv_dw_b, conv_ln_g, conv_ln_b, conv_out,
           ssm_conv_w, ssm_conv_b, ssm_a_log, ssm_dt_bias, ssm_d, ssm_norm, ssm_out,
           attn_out, fnet_out, w_gate, b_gate, w_out, norm_ffn, ffn_w1, ffn_w3, ffn_w2,
           moe_router, moe_w1, moe_w3, moe_w2, ple_gate, ple_proj, final_norm):
    bsz, seq, d = x.shape
    depth = w_in.shape[0]
    t = bsz * seq
    h = x.reshape(t, d)
    bias_tables = [_att_bias_tables(rel_bias, g, dil) for g, (_, dil) in enumerate(ATT_PATTERNS)]
    for l in range(depth):
        *qkv, cnf, z, xbc, fn, dt = _inproj(
            h, norm_mix[l][None], _reorder_w_in(w_in[l]),
            (conv_dw[l], conv_dw_b[l], conv_ln_g[l], conv_ln_b[l]), bsz, seq)
        att = [_attention_group(qkv[g], bias_tables[g], dil, bsz, seq, f"attn{g}")
               for g, (_, dil) in enumerate(ATT_PATTERNS)]
        ssd = _ssd(z, xbc, dt, ssm_conv_w[l], ssm_conv_b[l], ssm_a_log[l], ssm_dt_bias[l],
                   ssm_d[l], ssm_norm[l], bsz, seq)
        fnt = _fourier(fn, bsz, seq)
        wbr = jnp.stack([attn_out[l], conv_out[l], ssm_out[l], fnet_out[l]]).astype(BF16)
        h = _mix(h, norm_mix[l][None], att, (cnf, ssd, fnt), wbr, w_gate[l].astype(BF16),
                 b_gate[l][:, None, :], w_out[l].astype(BF16))
        pl_in = p[l].reshape(t, -1)
        wpg, wpp = ple_gate[l].astype(BF16), ple_proj[l].astype(BF16)
        i = l // 2
        if l % 2 == 0:
            h = _ffn(h, norm_ffn[l][None], ffn_w1[i].astype(BF16), ffn_w3[i].astype(BF16),
                     ffn_w2[i].astype(BF16), pl_in, wpg, wpp)
        else:
            h = _moe(h, norm_ffn[l][None], moe_router[i], moe_w1[i].astype(BF16),
                     moe_w3[i].astype(BF16), moe_w2[i].astype(BF16), pl_in, wpg, wpp)
    return _final_norm(h, final_norm[None]).reshape(bsz, seq, d)
```

```python
import functools
import math

import numpy as np
import jax
import jax.numpy as jnp
from jax import lax
from jax.experimental import pallas as pl
from jax.experimental.pallas import tpu as pltpu

F32 = jnp.float32
BF16 = jnp.bfloat16
HI = lax.Precision.HIGHEST

EPS = 1e-6
N_BRANCHES = 4
CONV_CH = 512
CONV_K = 31
SSM_HEADS = 8
SSM_HEAD_DIM = 64
SSM_INNER = SSM_HEADS * SSM_HEAD_DIM
SSM_GROUPS = 2
SSM_STATE = 64
SSM_CONV = 5
SSM_CONV_CH = SSM_INNER + 2 * SSM_GROUPS * SSM_STATE
SSM_CHUNK = 128
ATT_PATTERNS = ((128, 1), (512, 4), (2048, 16))
ATT_GROUPS = len(ATT_PATTERNS)
ATT_HEADS = 8
ATT_HEAD_DIM = 64
ATT_WIDTH = ATT_HEADS * ATT_HEAD_DIM
REL_BUCKETS = 32
REL_MAX_DIST = 1024
FNET_GROUPS = 4
FNET_GROUP_DIM = 128
FNET_WIDTH = FNET_GROUPS * FNET_GROUP_DIM
N_EXPERTS = 8
TOP_K = 2

ATT_IN_COLS = 3 * ATT_GROUPS * ATT_WIDTH
CONV_IN_COLS = 2 * CONV_CH
SSM_IN_COLS = SSM_INNER + SSM_CONV_CH + 2 * SSM_HEADS
OFF_CONV = ATT_IN_COLS
OFF_SSM = OFF_CONV + CONV_IN_COLS
OFF_FNET = OFF_SSM + SSM_IN_COLS

LANES = 128
DT_PAD = LANES
V7X_VMEM_BYTES = 64 * 1024 * 1024
VMEM_LIMIT = V7X_VMEM_BYTES - 8 * 1024 * 1024

MIX_ROWS = 512
FFN_ROWS, FFN_COLS = 512, 1536
MOE_TOKENS, MOE_COLS = 1024, 1792
NORM_ROWS = 1024
SEQ_ROWS = 512


def _cparams(sem):
    return pltpu.CompilerParams(dimension_semantics=sem, vmem_limit_bytes=VMEM_LIMIT)


def _const_spec(shape):
    nd = len(shape)
    return pl.BlockSpec(shape, lambda *_: (0,) * nd, pipeline_mode=pl.Buffered(1))


def _rms(x, g):
    return x * lax.rsqrt(jnp.mean(x * x, axis=-1, keepdims=True) + EPS) * g


_SEC_QKV = (0, ATT_IN_COLS)
_SEC_CONV = (_SEC_QKV[0] + _SEC_QKV[1], CONV_IN_COLS)
_SEC_Z = (_SEC_CONV[0] + _SEC_CONV[1], SSM_INNER)
_SEC_XBC = (_SEC_Z[0] + _SEC_Z[1], SSM_CONV_CH)
_SEC_FNET = (_SEC_XBC[0] + _SEC_XBC[1], FNET_WIDTH)
_SEC_DT = (_SEC_FNET[0] + _SEC_FNET[1], DT_PAD)
_IN_COLS_PAD = _SEC_DT[0] + _SEC_DT[1]
_MM_CHUNK = 512


ATT_TILE = 256
QKV_COLS = 3 * ATT_WIDTH
INPROJ_ROWS = 2 * ATT_TILE


def _reorder_w_in(w):
    d = w.shape[0]
    qkv = w[:, :OFF_CONV].reshape(d, 3, ATT_GROUPS, ATT_WIDTH).transpose(0, 2, 1, 3)
    ssm = w[:, OFF_SSM:OFF_FNET]
    dt = ssm[:, SSM_INNER + SSM_CONV_CH:]
    parts = [qkv.reshape(d, OFF_CONV), w[:, OFF_CONV:OFF_SSM], ssm[:, :SSM_INNER],
             ssm[:, SSM_INNER:SSM_INNER + SSM_CONV_CH], w[:, OFF_FNET:],
             dt, jnp.zeros((d, DT_PAD - dt.shape[1]), w.dtype)]
    return jnp.concatenate(parts, axis=1).astype(BF16)


def _deinterleave_matrix(dil):
    s = np.arange(ATT_TILE)
    m = np.zeros((ATT_TILE, ATT_TILE), np.float32)
    m[(s % dil) * (ATT_TILE // dil) + s // dil, s] = 1.0
    return m


def _inproj_kernel(h_ref, hp_ref, hn_ref, g_ref, w_ref, perm_ref, dw_ref, dwb_ref, lng_ref,
                   lnb_ref, q0_ref, q1_ref, q2_ref, cnf_ref, z_ref, xbc_ref, fn_ref, dt_ref,
                   pad_ref, sh_ref, *, spt):
    tm = h_ref.shape[0]
    pos = pl.program_id(0) % spt
    norm = lambda ref: _rms(ref[...], g_ref[...]).astype(BF16)
    xn = norm(h_ref)

    xe = jnp.concatenate([norm(hp_ref), xn, norm(hn_ref)], axis=0)
    c0 = _SEC_CONV[0]
    lin = jnp.dot(xe, w_ref[:, c0:c0 + CONV_CH], preferred_element_type=F32)
    gate = jnp.dot(xe, w_ref[:, c0 + CONV_CH:c0 + 2 * CONV_CH], preferred_element_type=F32)
    hid = lin * jax.nn.sigmoid(gate)
    pad_ref[0:CONV_HALO] = jnp.where(pos > 0, hid[:CONV_HALO], 0.0)
    pad_ref[CONV_HALO:CONV_HALO + tm] = hid[CONV_HALO:CONV_HALO + tm]
    pad_ref[CONV_HALO + tm:] = jnp.where(pos < spt - 1, hid[CONV_HALO + tm:], 0.0)
    span = tm + 2 * CONV_HALO - SUBLANES
    for b in range(1, SUBLANES):
        sh_ref[b - 1] = pad_ref[pl.ds(b, span), :]
    first = CONV_HALO - CONV_K // 2

    def conv_rows(c):
        acc = jnp.broadcast_to(dwb_ref[...], (CONV_ROWS, CONV_CH))
        for k in range(CONV_K):
            a, b = divmod(first + k, SUBLANES)
            at = pl.ds(c * CONV_ROWS + a * SUBLANES, CONV_ROWS)
            tap = pad_ref[at, :] if b == 0 else sh_ref[b - 1, at, :]
            acc = acc + dw_ref[pl.ds(k, 1), :] * tap
        mu = jnp.mean(acc, axis=-1, keepdims=True)
        cen = acc - mu
        var = jnp.mean(cen * cen, axis=-1, keepdims=True)
        y = cen * lax.rsqrt(var + EPS) * lng_ref[...] + lnb_ref[...]
        cnf_ref[pl.ds(c * CONV_ROWS, CONV_ROWS), :] = (y * jax.nn.sigmoid(y)).astype(cnf_ref.dtype)

    for c in range(tm // CONV_ROWS):
        conv_rows(c)

    def section(x, sec, store):
        start, width = sec
        for c in range(0, width, _MM_CHUNK):
            cw = min(_MM_CHUNK, width - c)
            store(c, cw, jnp.dot(x, w_ref[:, start + c:start + c + cw],
                                 preferred_element_type=F32))

    def to(ref):
        def store(c, cw, val):
            ref[:, c:c + cw] = val.astype(ref.dtype)
        return store

    def to_fnet(c, cw, val):
        for g in range(cw // FNET_GROUP_DIM):
            fn_ref[0, c // FNET_GROUP_DIM + g] = val[:, g * FNET_GROUP_DIM:(g + 1) * FNET_GROUP_DIM]

    for g, q_ref in enumerate((q0_ref, q1_ref, q2_ref)):
        x = xn
        if ATT_PATTERNS[g][1] > 1:
            x = jnp.concatenate(
                [jnp.dot(perm_ref[g], xn[i * ATT_TILE:(i + 1) * ATT_TILE],
                         preferred_element_type=F32) for i in range(xn.shape[0] // ATT_TILE)],
                axis=0).astype(BF16)
        section(x, (g * QKV_COLS, QKV_COLS), to(q_ref))
    section(xn, _SEC_Z, to(z_ref))
    section(xn, _SEC_XBC, to(xbc_ref))
    section(xn, _SEC_FNET, to_fnet)
    section(xn, _SEC_DT, to(dt_ref))


def _inproj(h, g, w, conv, bsz, seq):
    t, d = h.shape
    tm = INPROJ_ROWS
    spt = seq // tm
    dw, dw_b, ln_g, ln_b = conv
    perm = jnp.asarray(np.stack([_deinterleave_matrix(dil) for _, dil in ATT_PATTERNS]), BF16)
    row = lambda n: pl.BlockSpec((tm, n), lambda i: (i, 0))
    hb = tm // CONV_HALO
    n_hb = t // CONV_HALO
    qkv = jax.ShapeDtypeStruct((t, QKV_COLS), BF16)
    vec = _const_spec((1, CONV_CH))
    return pl.pallas_call(
        functools.partial(_inproj_kernel, spt=spt),
        grid=(t // tm,),
        in_specs=[row(d),
                  pl.BlockSpec((CONV_HALO, d), lambda i: (jnp.maximum(i * hb - 1, 0), 0)),
                  pl.BlockSpec((CONV_HALO, d), lambda i: (jnp.minimum((i + 1) * hb, n_hb - 1), 0)),
                  _const_spec((1, d)), _const_spec(w.shape), _const_spec(perm.shape),
                  _const_spec(dw.shape), vec, vec, vec],
        out_specs=[row(QKV_COLS), row(QKV_COLS), row(QKV_COLS), row(CONV_CH),
                   row(SSM_INNER), row(SSM_CONV_CH),
                   pl.BlockSpec((1, FNET_GROUPS, tm, FNET_GROUP_DIM),
                                lambda i: (i // spt, 0, i % spt, 0)),
                   row(DT_PAD)],
        out_shape=[qkv, qkv, qkv,
                   jax.ShapeDtypeStruct((t, CONV_CH), BF16),
                   jax.ShapeDtypeStruct((t, SSM_INNER), F32),
                   jax.ShapeDtypeStruct((t, SSM_CONV_CH), F32),
                   jax.ShapeDtypeStruct((bsz, FNET_GROUPS, seq, FNET_GROUP_DIM), F32),
                   jax.ShapeDtypeStruct((t, DT_PAD), F32)],
        scratch_shapes=[pltpu.VMEM((tm + 2 * CONV_HALO, CONV_CH), F32),
                        pltpu.VMEM((SUBLANES - 1, tm + 2 * CONV_HALO - SUBLANES, CONV_CH), F32)],
        compiler_params=_cparams(("parallel",)),
        name="inproj",
    )(h, h, h, g, w, perm, dw, dw_b[None], ln_g[None], ln_b[None])


def _split_dot(v, m):
    hi = v.astype(BF16)
    lo = (v - hi.astype(F32)).astype(BF16)
    return (jnp.dot(hi, m, preferred_element_type=F32) + jnp.dot(lo, m, preferred_element_type=F32))


def _interleave(pt, v):
    n = v.shape[1]
    if v.dtype != BF16:
        hi = v.astype(BF16)
        v = jnp.concatenate([hi, (v - hi.astype(F32)).astype(BF16)], axis=1)
    tiles = []
    for i in range(v.shape[0] // ATT_TILE):
        r = jnp.dot(pt, v[i * ATT_TILE:(i + 1) * ATT_TILE], preferred_element_type=F32)
        tiles.append(r if r.shape[1] == n else r[:, :n] + r[:, n:])
    return jnp.concatenate(tiles, axis=0)


def _mix_kernel(h_ref, g_ref, o0_ref, o1_ref, o2_ref, l0_ref, l1_ref, l2_ref, b1_ref, b2_ref,
                b3_ref, pt_ref, hx_ref, wbr_ref, wg_ref, cg_ref, wo_ref, o_ref):
    h = h_ref[...]
    xn = _rms(h, g_ref[...]).astype(BF16)
    outs, lses = [], []
    for g, (og_ref, lg_ref) in enumerate(zip((o0_ref, o1_ref, o2_ref), (l0_ref, l1_ref, l2_ref))):
        if ATT_PATTERNS[g][1] > 1:
            outs.append(_interleave(pt_ref[g], og_ref[...]))
            lses.append(_interleave(pt_ref[g], lg_ref[...]))
        else:
            outs.append(og_ref[...].astype(F32))
            lses.append(lg_ref[...])
    top = jnp.maximum(jnp.maximum(lses[0], lses[1]), lses[2])
    es = [jnp.exp(l - top) for l in lses]
    inv = 1.0 / (es[0] + es[1] + es[2])
    att = None
    for e, og in zip(es, outs):
        term = og * jnp.dot((e * inv).astype(BF16), hx_ref[...], preferred_element_type=F32)
        att = term if att is None else att + term
    acc = None
    for b, hid in enumerate((att.astype(BF16), b1_ref[...], b2_ref[...], b3_ref[...])):
        gate = jax.nn.sigmoid(jnp.dot(xn, wg_ref[b], preferred_element_type=F32) + cg_ref[b])
        br = jnp.dot(hid, wbr_ref[b], preferred_element_type=F32)
        acc = gate * br if acc is None else acc + gate * br
    o_ref[...] = h + jnp.dot(acc.astype(BF16), wo_ref[...], preferred_element_type=F32)


def _mix(h, g, att, others, wbr, wg, cg, wo):
    t, d = h.shape
    tm = MIX_ROWS
    row = lambda n: pl.BlockSpec((tm, n), lambda i: (i, 0))
    head_expand = np.zeros((LANES, ATT_WIDTH), np.float32)
    for hd in range(ATT_HEADS):
        head_expand[hd, hd * ATT_HEAD_DIM:(hd + 1) * ATT_HEAD_DIM] = 1.0
    head_expand = jnp.asarray(head_expand, BF16)
    unperm = jnp.asarray(np.stack([_deinterleave_matrix(dil).T for _, dil in ATT_PATTERNS]), BF16)
    outs = [o for o, _ in att]
    lses = [l for _, l in att]
    return pl.pallas_call(
        _mix_kernel,
        grid=(t // tm,),
        in_specs=[row(d), _const_spec((1, d))] + [row(a.shape[1]) for a in outs + lses + list(others)]
                 + [_const_spec(unperm.shape), _const_spec(head_expand.shape), _const_spec(wbr.shape),
                    _const_spec(wg.shape), _const_spec(cg.shape), _const_spec(wo.shape)],
        out_specs=row(d),
        out_shape=jax.ShapeDtypeStruct((t, d), F32),
        compiler_params=_cparams(("parallel",)),
        name="mix",
    )(h, g, *outs, *lses, *others, unperm, head_expand, wbr, wg, cg, wo)


def _ple(h2, p_ref, wpg_ref, wpp_ref):
    gate = jax.nn.sigmoid(jnp.dot(h2.astype(BF16), wpg_ref[...], preferred_element_type=F32))
    pe = jnp.dot(p_ref[...].astype(BF16), wpp_ref[...], preferred_element_type=F32)
    return h2 + gate * pe


def _swiglu_partial(xn, w1, w3, w2):
    a = jnp.dot(xn, w1, preferred_element_type=F32)
    b = jnp.dot(xn, w3, preferred_element_type=F32)
    hid = a * jax.nn.sigmoid(a) * b
    return jnp.dot(hid.astype(BF16), w2, preferred_element_type=F32)


def _ffn_kernel(h_ref, g_ref, w1_ref, w3_ref, w2_ref, p_ref, wpg_ref, wpp_ref, o_ref):
    h = h_ref[...]
    xn = _rms(h, g_ref[...]).astype(BF16)
    acc = h
    f = w1_ref.shape[1]
    for c0 in range(0, f, FFN_COLS):
        c1 = min(c0 + FFN_COLS, f)
        acc = acc + _swiglu_partial(xn, w1_ref[:, c0:c1], w3_ref[:, c0:c1], w2_ref[c0:c1, :])
    o_ref[...] = _ple(acc, p_ref, wpg_ref, wpp_ref)


def _ffn(h, g, w1, w3, w2, p, wpg, wpp):
    t, d = h.shape
    tm = FFN_ROWS
    row = lambda n: pl.BlockSpec((tm, n), lambda i: (i, 0))
    return pl.pallas_call(
        _ffn_kernel,
        grid=(t // tm,),
        in_specs=[row(d), _const_spec((1, d)), _const_spec(w1.shape), _const_spec(w3.shape),
                  _const_spec(w2.shape), row(p.shape[1]), _const_spec(wpg.shape),
                  _const_spec(wpp.shape)],
        out_specs=row(d),
        out_shape=jax.ShapeDtypeStruct((t, d), F32),
        compiler_params=_cparams(("parallel",)),
        name="ffn",
    )(h, g, w1, w3, w2, p, wpg, wpp)


MOE_ROWS = 128
MOE_MAX_BLOCKS = 2


def _moe_route(logits):
    ne, tm = logits.shape
    eidx = lax.broadcasted_iota(jnp.int32, logits.shape, 0)
    m1 = jnp.max(logits, axis=0, keepdims=True)
    i1 = jnp.min(jnp.where(logits == m1, eidx, ne), axis=0, keepdims=True)
    rest = jnp.where(eidx == i1, -jnp.inf, logits)
    m2 = jnp.max(rest, axis=0, keepdims=True)
    i2 = jnp.min(jnp.where(rest == m2, eidx, ne), axis=0, keepdims=True)
    e2 = jnp.exp(m2 - m1)
    den = 1.0 + e2
    combine = jnp.where(eidx == i1, 1.0 / den, 0.0) + jnp.where(eidx == i2, e2 / den, 0.0)
    routed = jnp.where((eidx == i1) | (eidx == i2), 1.0, 0.0)
    r = lax.broadcasted_iota(jnp.int32, (LANES, LANES), 0)
    c = lax.broadcasted_iota(jnp.int32, (LANES, LANES), 1)
    before = jnp.where(r < c, 1.0, 0.0).astype(BF16)
    counts = jnp.zeros((ne, 1), F32)
    slots = []
    for k in range(tm // LANES):
        blk = routed[:, k * LANES:(k + 1) * LANES]
        slots.append(jnp.dot(blk.astype(BF16), before, preferred_element_type=F32) + counts)
        counts = counts + jnp.sum(blk, axis=1, keepdims=True)
    slot = jnp.where(routed > 0.0, jnp.concatenate(slots, axis=1), -1.0).astype(jnp.int32)
    return combine, slot, counts


def _moe_kernel(h_ref, g_ref, rt_ref, w1_ref, w3_ref, w2_ref, p_ref, wpg_ref, wpp_ref, o_ref,
                xn_ref, comb_ref, slot_ref, cnt_ref, xe_ref, ye_ref):
    e = pl.program_id(1)
    j = pl.program_id(2)
    ne = pl.num_programs(1)
    tm = xn_ref.shape[0]

    @pl.when((e == 0) & (j == 0))
    def _():
        h = h_ref[...]
        xn = _rms(h, g_ref[...])
        xn_ref[...] = xn.astype(BF16)
        logits = lax.dot_general(rt_ref[...], xn, (((1,), (1,)), ((), ())),
                                 preferred_element_type=F32, precision=HI)
        combine, slot, counts = _moe_route(logits)
        comb_ref[...] = combine
        slot_ref[...] = slot
        for k in range(comb_ref.shape[0]):
            cnt_ref[k] = jnp.sum(counts[k:k + 1, :]).astype(jnp.int32)
        o_ref[...] = h

    n_blocks = (cnt_ref[e] + MOE_ROWS - 1) // MOE_ROWS

    def for_row_blocks(body):
        full = MOE_MAX_BLOCKS * MOE_ROWS

        def whole(i, carry):
            body(pl.multiple_of(i * full, full), full)
            return carry
        lax.fori_loop(0, n_blocks // MOE_MAX_BLOCKS, whole, 0)
        base = pl.multiple_of((n_blocks // MOE_MAX_BLOCKS) * full, full)
        for r in range(1, MOE_MAX_BLOCKS):
            @pl.when(n_blocks % MOE_MAX_BLOCKS == r)
            def _():
                body(base, r * MOE_ROWS)

    def one_hot(r0, rows):
        return slot_ref[pl.ds(e, 1), :] == lax.broadcasted_iota(jnp.int32, (rows, tm), 0) + r0

    def swiglu(x):
        return _swiglu_partial(x, w1_ref[0], w3_ref[0], w2_ref[0])

    last_j = pl.num_programs(2) - 1

    @pl.when(j == 0)
    def _():
        def gather_first(r0, rows):
            sel = jnp.where(one_hot(r0, rows), 1.0, 0.0).astype(BF16)
            x = jnp.dot(sel, xn_ref[...], preferred_element_type=F32).astype(BF16)
            xe_ref[pl.ds(r0, rows), :] = x
            ye_ref[pl.ds(r0, rows), :] = swiglu(x)
        for_row_blocks(gather_first)

    @pl.when((j > 0) & (j < last_j))
    def _():
        def middle(r0, rows):
            ye_ref[pl.ds(r0, rows), :] += swiglu(xe_ref[pl.ds(r0, rows), :])
        for_row_blocks(middle)

    @pl.when((j > 0) & (j == last_j))
    def _():
        def last_scatter(r0, rows):
            y = ye_ref[pl.ds(r0, rows), :] + swiglu(xe_ref[pl.ds(r0, rows), :])
            hot = one_hot(r0, rows)
            weight = jnp.sum(jnp.where(hot, comb_ref[pl.ds(e, 1), :], 0.0), axis=1, keepdims=True)
            o_ref[...] += lax.dot_general(jnp.where(hot, 1.0, 0.0).astype(BF16),
                                          (y * weight).astype(BF16),
                                          (((0,), (0,)), ((), ())), preferred_element_type=F32)
        for_row_blocks(last_scatter)

    @pl.when((e == ne - 1) & (j == pl.num_programs(2) - 1))
    def _():
        o_ref[...] = _ple(o_ref[...], p_ref, wpg_ref, wpp_ref)


def _moe(h, g, router, w1, w3, w2, p, wpg, wpp):
    t, d = h.shape
    ne, _, f = w1.shape
    tm, tf = MOE_TOKENS, MOE_COLS
    assert f % tf == 0 and f // tf >= 2
    row = lambda n: pl.BlockSpec((tm, n), lambda i, e, j: (i, 0))
    return pl.pallas_call(
        _moe_kernel,
        grid=(t // tm, ne, f // tf),
        in_specs=[row(d), _const_spec((1, d)), _const_spec((ne, d)),
                  pl.BlockSpec((1, d, tf), lambda i, e, j: (e, 0, j)),
                  pl.BlockSpec((1, d, tf), lambda i, e, j: (e, 0, j)),
                  pl.BlockSpec((1, tf, d), lambda i, e, j: (e, j, 0)),
                  row(p.shape[1]), _const_spec(wpg.shape), _const_spec(wpp.shape)],
        out_specs=row(d),
        out_shape=jax.ShapeDtypeStruct((t, d), F32),
        scratch_shapes=[pltpu.VMEM((tm, d), BF16), pltpu.VMEM((ne, tm), F32),
                        pltpu.VMEM((ne, tm), jnp.int32), pltpu.SMEM((ne,), jnp.int32),
                        pltpu.VMEM((tm, d), BF16), pltpu.VMEM((tm, d), F32)],
        compiler_params=_cparams(("parallel", "arbitrary", "arbitrary")),
        name="moe",
    )(h, g, router.T, w1, w3, w2, p, wpg, wpp)


def _final_norm_kernel(h_ref, g_ref, o_ref):
    o_ref[...] = _rms(h_ref[...], g_ref[...])


def _final_norm(h, g):
    t, d = h.shape
    tm = NORM_ROWS
    row = pl.BlockSpec((tm, d), lambda i: (i, 0))
    return pl.pallas_call(
        _final_norm_kernel, grid=(t // tm,),
        in_specs=[row, _const_spec((1, d))], out_specs=row,
        out_shape=jax.ShapeDtypeStruct((t, d), F32),
        compiler_params=_cparams(("parallel",)), name="final_norm",
    )(h, g)


ATT_HALF = 64
ATT_QB = 128
ATT_KB = ATT_QB + 2 * ATT_HALF
NEG = -1e30
assert all(w // (2 * d) == ATT_HALF for w, d in ATT_PATTERNS)


def _t5_bucket(rel):
    half = REL_BUCKETS // 2
    max_exact = half // 2
    n = np.abs(rel)
    large = max_exact + (np.log(np.maximum(n, 1) / max_exact) / math.log(REL_MAX_DIST / max_exact)
                         * (half - max_exact)).astype(np.int32)
    large = np.minimum(large, half - 1)
    return np.where(rel > 0, half, 0) + np.where(n < max_exact, n, large)


def _att_bias_tables(rel_bias, g, dil):
    i = np.arange(ATT_QB)[:, None]
    j = np.arange(ATT_KB)[None, :]
    rel = j - ATT_HALF - i
    band = np.abs(rel) <= ATT_HALF
    pick = np.eye(REL_BUCKETS, dtype=np.float32)[_t5_bucket(dil * rel)]
    heads = rel_bias[:, g * ATT_HEADS:(g + 1) * ATT_HEADS].astype(F32)
    bias = jnp.einsum('qkb,bh->hqk', pick, heads, precision=lax.Precision.HIGHEST)
    tables = []
    for v in range(4):
        ok = band
        if v & 1:
            ok = ok & (j >= ATT_HALF)
        if v & 2:
            ok = ok & (j < ATT_QB + ATT_HALF)
        tables.append(jnp.where(ok[None], bias, NEG))
    return jnp.stack(tables)


def _attn_kernel(q_ref, k_ref, v_ref, kp_ref, kn_ref, vp_ref, vn_ref, bias_ref, o_ref, lse_ref,
                 qbuf, kbuf, vbuf, obuf, lbuf, *, tq, n_blocks):
    flat = lambda ref: ref[0].reshape(-1, ref.shape[-1])
    qbuf[...] = flat(q_ref)
    kbuf[0:ATT_HALF] = flat(kp_ref)
    kbuf[ATT_HALF:ATT_HALF + tq] = flat(k_ref)
    kbuf[ATT_HALF + tq:] = flat(kn_ref)
    vbuf[0:ATT_HALF] = flat(vp_ref)
    vbuf[ATT_HALF:ATT_HALF + tq] = flat(v_ref)
    vbuf[ATT_HALF + tq:] = flat(vn_ref)
    nsb = tq // ATT_QB
    first = pl.program_id(2) * nsb
    lane = lax.broadcasted_iota(jnp.int32, (ATT_QB, LANES), 1)
    low = lane < ATT_HEAD_DIM
    lane_row = lax.broadcasted_iota(jnp.int32, (1, LANES), 1)
    keep = [(lane_row < ATT_HEAD_DIM).astype(BF16), (lane_row >= ATT_HEAD_DIM).astype(BF16)]
    ones = jnp.ones((ATT_KB, LANES), BF16)

    def block(sb, carry):
        r0 = pl.multiple_of(sb * ATT_QB, ATT_QB)
        gsb = first + sb
        variant = (gsb == 0).astype(jnp.int32) + 2 * (gsb == n_blocks - 1).astype(jnp.int32)
        q = qbuf[pl.ds(r0, ATT_QB), :] * (ATT_HEAD_DIM ** -0.5)
        lse_all = jnp.zeros((ATT_QB, LANES), F32)
        outs = []
        for pair in range(ATT_HEADS // 2):
            cols = slice(pair * LANES, (pair + 1) * LANES)
            qp = q[:, cols]
            kp = kbuf[pl.ds(r0, ATT_KB), cols]
            vp = jnp.concatenate([vbuf[pl.ds(r0, ATT_KB), cols], ones], axis=1)
            res = []
            for half in range(2):
                h = 2 * pair + half
                s = lax.dot_general(qp * keep[half], kp, (((1,), (1,)), ((), ())),
                                    preferred_element_type=F32)
                s = s + bias_ref[variant, h]
                m = jnp.max(s, axis=-1, keepdims=True)
                e = jnp.exp(s - m)
                pv = jnp.dot(e.astype(BF16), vp, preferred_element_type=F32)
                den = pv[:, LANES:]
                res.append(pv[:, :LANES] / den)
                lse_all = jnp.where(lane == h, m + jnp.log(den), lse_all)
            outs.append(jnp.where(low, res[0], res[1]))
        obuf[pl.ds(r0, ATT_QB), :] = jnp.concatenate(outs, axis=1).astype(obuf.dtype)
        lbuf[pl.ds(r0, ATT_QB), :] = lse_all
        return carry

    lax.fori_loop(0, nsb, block, 0, unroll=4)
    o_ref[0] = obuf[...].reshape(o_ref.shape[1:])
    lse_ref[0] = lbuf[...].reshape(lse_ref.shape[1:])


def _attention_group(qkv, bias_tables, dil, bsz, seq, name):
    sub_len = seq // dil
    assert sub_len % ATT_QB == 0 and seq % ATT_TILE == 0 and ATT_TILE % dil == 0
    rows = ATT_TILE // dil
    tq = min(SEQ_ROWS, sub_len)
    w = ATT_WIDTH
    hrows = min(rows, ATT_HALF)
    n_hb = sub_len // ATT_HALF

    def view(a, chunk):
        return a.reshape(bsz, (seq // ATT_TILE) * (rows // chunk), dil, chunk, a.shape[-1])

    def main(part):
        return pl.BlockSpec((1, tq // rows, None, rows, w), lambda b, r, n: (b, n, r, 0, part))

    def halo(part, nxt):
        if nxt:
            blk = lambda n: jnp.minimum((n + 1) * (tq // ATT_HALF), n_hb - 1)
        else:
            blk = lambda n: jnp.maximum(n * (tq // ATT_HALF) - 1, 0)
        return pl.BlockSpec((1, ATT_HALF // hrows, None, hrows, w),
                            lambda b, r, n: (b, blk(n), r, 0, part))

    mv, hv = view(qkv, rows), view(qkv, hrows)
    o, lse = pl.pallas_call(
        functools.partial(_attn_kernel, tq=tq, n_blocks=sub_len // ATT_QB),
        grid=(bsz, dil, sub_len // tq),
        in_specs=[main(0), main(1), main(2), halo(1, False), halo(1, True), halo(2, False),
                  halo(2, True), _const_spec(bias_tables.shape)],
        out_specs=[pl.BlockSpec((1, tq // rows, None, rows, w), lambda b, r, n: (b, n, r, 0, 0)),
                   pl.BlockSpec((1, tq // rows, None, rows, LANES), lambda b, r, n: (b, n, r, 0, 0))],
        out_shape=[jax.ShapeDtypeStruct((bsz, seq // ATT_TILE, dil, rows, w), BF16),
                   jax.ShapeDtypeStruct((bsz, seq // ATT_TILE, dil, rows, LANES), F32)],
        scratch_shapes=[pltpu.VMEM((tq, w), BF16),
                        pltpu.VMEM((tq + 2 * ATT_HALF, w), BF16),
                        pltpu.VMEM((tq + 2 * ATT_HALF, w), BF16),
                        pltpu.VMEM((tq, w), BF16), pltpu.VMEM((tq, LANES), F32)],
        compiler_params=_cparams(("parallel", "parallel", "parallel")),
        name=name,
    )(mv, mv, mv, hv, hv, hv, hv, bias_tables)
    return o.reshape(bsz * seq, w), lse.reshape(bsz * seq, LANES)


SUBLANES = 8
CONV_HALO = 16
CONV_ROWS = 64


SSD_HALO = SUBLANES
SSD_ROWS = 64
SSD_BC = SSM_GROUPS * SSM_STATE
HEADS_PER_GROUP = SSM_HEADS // SSM_GROUPS
GROUP_LANES = HEADS_PER_GROUP * SSM_HEAD_DIM
assert SSD_BC == LANES and SSM_CHUNK == LANES and 2 * SSM_HEADS <= LANES


def _ssd_pre_kernel(x_ref, xp_ref, xn_ref, dt_ref, cw_ref, cb_ref, dtb_ref, arow_ref, tri_ref,
                    ex_ref, xo_ref, dto_ref, acs_ref, nf_ref, nb_ref, cy_ref, hp_ref, sh_ref,
                    *, ts, phases):
    n = pl.program_id(1)
    hp_ref[0:SSD_HALO] = jnp.where(n > 0, xp_ref[0], 0.0)
    hp_ref[SSD_HALO:SSD_HALO + ts] = x_ref[0]
    hp_ref[SSD_HALO + ts:] = jnp.where(n < pl.num_programs(1) - 1, xn_ref[0], 0.0)
    for i, b in enumerate(phases):
        sh_ref[i] = hp_ref[pl.ds(b, ts + SSD_HALO), :]
    first = SSD_HALO - SSM_CONV // 2

    def rows(c, carry):
        r0 = pl.multiple_of(c * SSD_ROWS, SSD_ROWS)
        acc = jnp.broadcast_to(cb_ref[...], (SSD_ROWS, SSM_CONV_CH))
        for k in range(SSM_CONV):
            a, b = divmod(first + k, SUBLANES)
            acc = acc + cw_ref[pl.ds(k, 1), :] * sh_ref[phases.index(b),
                                                        pl.ds(r0 + a * SUBLANES, SSD_ROWS), :]
        xo_ref[0, pl.ds(r0, SSD_ROWS), :] = acc * jax.nn.sigmoid(acc)
        return carry

    lax.fori_loop(0, ts // SSD_ROWS, rows, 0, unroll=True)
    x = dt_ref[0] + dtb_ref[...]
    softplus = jnp.maximum(x, 0.0) + jnp.log1p(jnp.exp(-jnp.abs(x)))
    lane = lax.broadcasted_iota(jnp.int32, x.shape, 1)
    dtv = jnp.where(lane < 2 * SSM_HEADS, softplus, 0.0)
    dto_ref[0] = dtv

    nch = ts // SSM_CHUNK
    chunk = lambda k: slice(k * SSM_CHUNK, (k + 1) * SSM_CHUNK)
    dta = dtv * arow_ref[...]
    acs_all = _cumsum_both(jnp.concatenate([dta[chunk(k)] for k in range(nch)], axis=1), tri_ref)
    clane = lax.broadcasted_iota(jnp.int32, (SSM_CHUNK, DT_PAD), 1)
    ws, totals = [], [[], []]
    for k in range(nch):
        acs = acs_all[:, chunk(k)]
        acs_ref[0, chunk(k), :] = acs
        tot_f, tot_b = acs[SSM_CHUNK - 1:SSM_CHUNK, :], acs[0:1, :]
        to_end = jnp.where(clane < SSM_HEADS, tot_f - acs,
                           jnp.where(clane < 2 * SSM_HEADS, tot_b - acs, 0.0))
        ws.append(dtv[chunk(k)] * jnp.exp(to_end))
        totals[0].append(jnp.exp(tot_f))
        totals[1].append(jnp.exp(tot_b))
    w_all = jnp.concatenate(ws, axis=0)
    xs = xo_ref[0, :, :SSM_INNER]
    hl = lax.broadcasted_iota(jnp.int32, (SSM_STATE, SSM_INNER), 1)
    pad_rows = jnp.zeros((SUBLANES - nch % SUBLANES, DT_PAD), F32)
    for d, new_ref in enumerate((nf_ref, nb_ref)):
        xw = (xs * _expand(w_all, ex_ref, d)).astype(BF16)
        carry = _expand(jnp.concatenate(totals[d] + [pad_rows], axis=0), ex_ref, d)
        for k in range(nch):
            bmat = xo_ref[0, chunk(k), SSM_INNER:SSM_INNER + SSD_BC].astype(BF16)
            full = lax.dot_general(bmat, xw[chunk(k)], (((0,), (0,)), ((), ())),
                                   preferred_element_type=F32)
            new = full[:SSM_STATE]
            for g in range(1, SSM_GROUPS):
                new = jnp.where(hl >= g * GROUP_LANES, full[g * SSM_STATE:(g + 1) * SSM_STATE], new)
            new_ref[0, k] = new
            cy_ref[0, k, d:d + 1, :] = carry[k:k + 1]


def _ssd_pre(xbc, dt_raw, conv_w, conv_b, dt_bias, consts, bsz, seq):
    ts = min(SEQ_ROWS, seq)
    nc = seq // SSM_CHUNK
    nch = ts // SSM_CHUNK
    a_row, tri, expand = consts
    new = jax.ShapeDtypeStruct((bsz, nc, SSM_STATE, SSM_INNER), F32)
    new_spec = pl.BlockSpec((1, nch, SSM_STATE, SSM_INNER), lambda b, n: (b, n, 0, 0))
    xv = xbc.reshape(bsz, seq, SSM_CONV_CH)
    dv = dt_raw.reshape(bsz, seq, DT_PAD)
    hb = ts // SSD_HALO
    n_hb = seq // SSD_HALO
    first = SSD_HALO - SSM_CONV // 2
    phases = tuple(sorted({(first + k) % SUBLANES for k in range(SSM_CONV)}))
    dtb = jnp.zeros((1, DT_PAD), F32).at[0, :2 * SSM_HEADS].set(dt_bias.reshape(-1))
    return pl.pallas_call(
        functools.partial(_ssd_pre_kernel, ts=ts, phases=phases),
        grid=(bsz, seq // ts),
        in_specs=[pl.BlockSpec((1, ts, SSM_CONV_CH), lambda b, n: (b, n, 0)),
                  pl.BlockSpec((1, SSD_HALO, SSM_CONV_CH),
                               lambda b, n: (b, jnp.maximum(n * hb - 1, 0), 0)),
                  pl.BlockSpec((1, SSD_HALO, SSM_CONV_CH),
                               lambda b, n: (b, jnp.minimum((n + 1) * hb, n_hb - 1), 0)),
                  pl.BlockSpec((1, ts, DT_PAD), lambda b, n: (b, n, 0)),
                  _const_spec(conv_w.shape), _const_spec((1, SSM_CONV_CH)),
                  _const_spec((1, DT_PAD)), _const_spec(a_row.shape), _const_spec(tri.shape),
                  _const_spec(expand.shape)],
        out_specs=[pl.BlockSpec((1, ts, SSM_CONV_CH), lambda b, n: (b, n, 0)),
                   pl.BlockSpec((1, ts, DT_PAD), lambda b, n: (b, n, 0)),
                   pl.BlockSpec((1, ts, DT_PAD), lambda b, n: (b, n, 0)),
                   new_spec, new_spec,
                   pl.BlockSpec((1, nch, 2, SSM_INNER), lambda b, n: (b, n, 0, 0))],
        out_shape=[jax.ShapeDtypeStruct((bsz, seq, SSM_CONV_CH), F32),
                   jax.ShapeDtypeStruct((bsz, seq, DT_PAD), F32),
                   jax.ShapeDtypeStruct((bsz, seq, DT_PAD), F32),
                   new, new, jax.ShapeDtypeStruct((bsz, nc, 2, SSM_INNER), F32)],
        scratch_shapes=[pltpu.VMEM((ts + 2 * SSD_HALO, SSM_CONV_CH), F32),
                        pltpu.VMEM((len(phases), ts + SSD_HALO, SSM_CONV_CH), F32)],
        compiler_params=_cparams(("parallel", "parallel")),
        name="ssd_pre",
    )(xv, xv, xv, dv, conv_w, conv_b[None], dtb, a_row, tri, expand)


def _ssd_consts(a_log):
    a_row = jnp.zeros((1, DT_PAD), F32).at[0, :2 * SSM_HEADS].set(-jnp.exp(a_log.reshape(-1)))
    lower = np.tril(np.ones((SSM_CHUNK, SSM_CHUNK), np.float32))
    tri = jnp.asarray(np.concatenate([lower, lower.T], axis=0), BF16)
    expand = np.zeros((2, DT_PAD, SSM_INNER), np.float32)
    for d in range(2):
        for h in range(SSM_HEADS):
            expand[d, d * SSM_HEADS + h, h * SSM_HEAD_DIM:(h + 1) * SSM_HEAD_DIM] = 1.0
    return a_row, tri, jnp.asarray(expand, BF16)


def _cumsum_both(dta, tri_ref):
    hi = dta.astype(BF16)
    r1 = dta - hi.astype(F32)
    mid = r1.astype(BF16)
    lo = (r1 - mid.astype(F32)).astype(BF16)
    both = (jnp.dot(tri_ref[...], hi, preferred_element_type=F32)
            + jnp.dot(tri_ref[...], mid, preferred_element_type=F32)
            + jnp.dot(tri_ref[...], lo, preferred_element_type=F32))
    lane = lax.broadcasted_iota(jnp.int32, dta.shape, 1) % DT_PAD
    return jnp.where(lane < SSM_HEADS, both[:SSM_CHUNK], both[SSM_CHUNK:])


def _expand(v, ex_ref, d):
    return _split_dot(v, ex_ref[d])


def _ssd_scan_kernel(nf_ref, nb_ref, cf_ref, cb_ref, pf_ref, pb_ref, sf_ref, sb_ref, *, nch):
    @pl.when(pl.program_id(1) == 0)
    def _():
        sf_ref[...] = jnp.zeros_like(sf_ref)
        sb_ref[...] = jnp.zeros_like(sb_ref)

    def scan(d, order, new_ref, carry_ref, st_ref, out_ref):
        st = st_ref[...]
        for k in order:
            out_ref[0, k] = st.astype(out_ref.dtype)
            st = st * carry_ref[0, k, d:d + 1, :] + new_ref[0, k]
        st_ref[...] = st

    scan(0, range(nch), nf_ref, cf_ref, sf_ref, pf_ref)
    scan(1, range(nch - 1, -1, -1), nb_ref, cb_ref, sb_ref, pb_ref)


SSD_SCAN_CHUNKS = 16


def _ssd_states(new_f, new_b, carry, bsz, seq):
    nc = seq // SSM_CHUNK
    nch = min(SSD_SCAN_CHUNKS, nc)
    steps = nc // nch
    fwd = lambda b, c: (b, c, 0, 0)
    bwd = lambda b, c: (b, steps - 1 - c, 0, 0)
    st = jax.ShapeDtypeStruct((bsz, nc, SSM_STATE, SSM_INNER), BF16)
    blk = (1, nch, SSM_STATE, SSM_INNER)
    return pl.pallas_call(
        functools.partial(_ssd_scan_kernel, nch=nch),
        grid=(bsz, steps),
        in_specs=[pl.BlockSpec(blk, fwd), pl.BlockSpec(blk, bwd),
                  pl.BlockSpec((1, nch, 2, SSM_INNER), fwd),
                  pl.BlockSpec((1, nch, 2, SSM_INNER), bwd)],
        out_specs=[pl.BlockSpec(blk, fwd), pl.BlockSpec(blk, bwd)],
        out_shape=[st, st],
        scratch_shapes=[pltpu.VMEM((SSM_STATE, SSM_INNER), F32),
                        pltpu.VMEM((SSM_STATE, SSM_INNER), F32)],
        compiler_params=_cparams(("parallel", "arbitrary")),
        name="ssd_scan",
    )(new_f, new_b, carry, carry)


def _ssd_out_kernel(x_ref, dt_ref, acs_ref, z_ref, pf_ref, pb_ref, ex_ref, dskip_ref, ng_ref,
                    o_ref, *, nck):
    eacs = jnp.exp(acs_ref[0])
    decay = [_expand(eacs, ex_ref, d) for d in range(2)]
    for k in range(nck):
        _ssd_out_chunk(k, x_ref, dt_ref, acs_ref, z_ref, pf_ref, pb_ref, decay, dskip_ref,
                       ng_ref, o_ref)


def _ssd_out_chunk(k, x_ref, dt_ref, acs_ref, z_ref, pf_ref, pb_ref, decay, dskip_ref, ng_ref,
                   o_ref):
    rows = slice(k * SSM_CHUNK, (k + 1) * SSM_CHUNK)
    xs = x_ref[0, rows, :SSM_INNER]
    bmat = x_ref[0, rows, SSM_INNER:SSM_INNER + SSD_BC].astype(BF16)
    cmat = x_ref[0, rows, SSM_INNER + SSD_BC:].astype(BF16)
    dtv = dt_ref[0, rows, :]
    acs = acs_ref[0, rows, :]
    acs_t = acs.T
    dt_t = dtv.T
    row = lax.broadcasted_iota(jnp.int32, (SSM_CHUNK, SSM_CHUNK), 0)
    col = lax.broadcasted_iota(jnp.int32, (SSM_CHUNK, SSM_CHUNK), 1)
    past, now = col < row, col == row
    low = lax.broadcasted_iota(jnp.int32, (SSM_CHUNK, LANES), 1) < SSM_HEAD_DIM
    glane = lax.broadcasted_iota(jnp.int32, (1, SSD_BC), 1) // SSM_STATE
    both = lax.dot_general(
        jnp.concatenate([cmat * (glane == g).astype(BF16) for g in range(SSM_GROUPS)], axis=0),
        bmat, (((1,), (1,)), ((), ())), preferred_element_type=F32)
    scores = [both[g * SSM_CHUNK:(g + 1) * SSM_CHUNK] for g in range(SSM_GROUPS)]
    xs_b = xs.astype(BF16)
    keep = [low.astype(BF16), (~low).astype(BF16)]
    diag = []
    for pair in range(SSM_HEADS // 2):
        xp = xs_b[:, pair * LANES:(pair + 1) * LANES]
        mats = []
        for half in range(2):
            f = 2 * pair + half
            b = SSM_HEADS + f
            seg = jnp.where(past | now, acs[:, f:f + 1] - acs_t[f:f + 1, :],
                            acs[:, b:b + 1] - acs_t[b:b + 1, :])
            dts = (jnp.where(past, dt_t[f:f + 1, :], dt_t[b:b + 1, :])
                   + jnp.where(now, dt_t[f:f + 1, :], 0.0))
            mats.append((scores[f // HEADS_PER_GROUP] * jnp.exp(seg) * dts).astype(BF16))
        diag.append(jnp.dot(jnp.concatenate(mats, axis=1),
                            jnp.concatenate([xp * keep[0], xp * keep[1]], axis=0),
                            preferred_element_type=F32))
    y = dskip_ref[...] * xs + jnp.concatenate(diag, axis=1)
    hgroup = lax.broadcasted_iota(jnp.int32, (1, SSM_INNER), 1) // GROUP_LANES
    stacked = jnp.concatenate(
        [jnp.concatenate([p_ref[0, k] * (hgroup == g).astype(BF16) for g in range(SSM_GROUPS)],
                         axis=0) for p_ref in (pf_ref, pb_ref)], axis=1)
    off = jnp.dot(cmat, stacked, preferred_element_type=F32)
    for d in range(2):
        y = y + off[:, d * SSM_INNER:(d + 1) * SSM_INNER] * decay[d][rows]
    z = z_ref[0, rows, :]
    y = y * (z * jax.nn.sigmoid(z))
    o_ref[0, rows, :] = _rms(y, ng_ref[...]).astype(o_ref.dtype)


SSD_OUT_CHUNKS = 4


def _ssd_out(xact, dtv, acs, z, prev_f, prev_b, expand, d_skip, norm_g, bsz, seq):
    nc = seq // SSM_CHUNK
    nck = min(SSD_OUT_CHUNKS, nc)
    dsk = jnp.repeat(d_skip, SSM_HEAD_DIM)[None]
    chunk = lambda n: pl.BlockSpec((1, nck * SSM_CHUNK, n), lambda b, c: (b, c, 0))
    state = pl.BlockSpec((1, nck, SSM_STATE, SSM_INNER), lambda b, c: (b, c, 0, 0))
    out = pl.pallas_call(
        functools.partial(_ssd_out_kernel, nck=nck),
        grid=(bsz, nc // nck),
        in_specs=[chunk(SSM_CONV_CH), chunk(DT_PAD), chunk(DT_PAD), chunk(SSM_INNER), state, state,
                  _const_spec(expand.shape), _const_spec(dsk.shape), _const_spec((1, SSM_INNER))],
        out_specs=chunk(SSM_INNER),
        out_shape=jax.ShapeDtypeStruct((bsz, seq, SSM_INNER), BF16),
        compiler_params=_cparams(("parallel", "parallel")),
        name="ssd_out",
    )(xact, dtv, acs, z.reshape(bsz, seq, SSM_INNER), prev_f, prev_b, expand, dsk, norm_g[None])
    return out.reshape(bsz * seq, SSM_INNER)


def _ssd(z, xbc, dt_raw, conv_w, conv_b, a_log, dt_bias, d_skip, norm_g, bsz, seq):
    consts = _ssd_consts(a_log)
    xact, dtv, acs, new_f, new_b, carry = _ssd_pre(xbc, dt_raw, conv_w, conv_b, dt_bias, consts,
                                                   bsz, seq)
    prev_f, prev_b = _ssd_states(new_f, new_b, carry, bsz, seq)
    return _ssd_out(xact, dtv, acs, z, prev_f, prev_b, consts[2], d_skip, norm_g, bsz, seq)


FNET_COLS = 4096


def _dft_cos_sin(n):
    ang = 2.0 * np.pi * np.outer(np.arange(n), np.arange(n)) / n
    return np.cos(ang), np.sin(ang)


def _fnet_consts(seq):
    c = FNET_GROUP_DIM
    n1 = seq // LANES
    c1, s1 = _dft_cos_sin(n1)
    stage1 = np.concatenate([c1, -s1], axis=0)
    ang = 2.0 * np.pi * np.outer(np.arange(n1), np.arange(LANES)) / seq
    twr = np.repeat(np.cos(ang), c, axis=1)
    twi = np.repeat(-np.sin(ang), c, axis=1)
    cc, sc = _dft_cos_sin(c)
    chan = np.block([[cc, -sc], [sc, cc]])
    c2, s2 = _dft_cos_sin(LANES)
    return (jnp.asarray(stage1, BF16), jnp.asarray(twr, F32), jnp.asarray(twi, F32),
            jnp.asarray(chan, BF16), jnp.asarray(c2, BF16), jnp.asarray(s2, BF16))


def _fnet1_kernel(x_ref, f_ref, twr_ref, twi_ref, o_ref, *, n1):
    c = FNET_GROUP_DIM
    a = jnp.dot(f_ref[...], x_ref[0].astype(BF16), preferred_element_type=F32)
    ar, ai = a[:n1], a[n1:]
    twr, twi = twr_ref[...], twi_ref[...]
    re = (ar * twr - ai * twi).astype(o_ref.dtype)
    im = (ar * twi + ai * twr).astype(o_ref.dtype)
    for j in range(re.shape[1] // c):
        o_ref[0, :, (2 * j) * c:(2 * j + 1) * c] = re[:, j * c:(j + 1) * c]
        o_ref[0, :, (2 * j + 1) * c:(2 * j + 2) * c] = im[:, j * c:(j + 1) * c]


def _fnet2_kernel(a_ref, chan_ref, c2_ref, s2_ref, o_ref, g_ref, scr_ref, *, n1, scale):
    c = FNET_GROUP_DIM
    pitch = scr_ref.shape[0] // LANES
    per = min(8, n1)
    for i in range(n1 // per):
        blk = a_ref[0, i * per:(i + 1) * per].reshape(per * LANES, 2 * c)
        g = jnp.dot(blk, chan_ref[...], preferred_element_type=F32).astype(BF16)
        g_ref[i * per:(i + 1) * per] = g.reshape(per, LANES, 2 * c)

    def body(k1, carry):
        g = g_ref[k1]
        y = (jnp.dot(c2_ref[...], g[:, :c], preferred_element_type=F32)
             + jnp.dot(s2_ref[...], g[:, c:], preferred_element_type=F32))
        scr_ref[pl.ds(k1, LANES, stride=pitch), :] = y * scale
        return carry

    lax.fori_loop(0, n1, body, 0, unroll=4)

    def compact(k2, carry):
        src = pl.multiple_of(k2 * pitch, SUBLANES)
        dst = pl.multiple_of(k2 * n1, n1)
        o_ref[0, pl.ds(dst, n1), :] = scr_ref[pl.ds(src, n1), :].astype(o_ref.dtype)
        return carry

    lax.fori_loop(0, LANES, compact, 0, unroll=8)


def _fourier(fn, bsz, seq):
    c = FNET_GROUP_DIM
    assert c == LANES and seq % LANES == 0
    n1 = seq // LANES
    stage1, twr, twi, chan, c2, s2 = _fnet_consts(seq)
    ncols = LANES * c
    nb = min(FNET_COLS, ncols)
    x2 = fn.reshape(bsz * FNET_GROUPS, n1, ncols)
    a = pl.pallas_call(
        functools.partial(_fnet1_kernel, n1=n1),
        grid=(ncols // nb, bsz * FNET_GROUPS),
        in_specs=[pl.BlockSpec((1, n1, nb), lambda j, i: (i, 0, j)),
                  _const_spec(stage1.shape),
                  pl.BlockSpec((n1, nb), lambda j, i: (0, j)),
                  pl.BlockSpec((n1, nb), lambda j, i: (0, j))],
        out_specs=pl.BlockSpec((1, n1, 2 * nb), lambda j, i: (i, 0, j)),
        out_shape=jax.ShapeDtypeStruct((bsz * FNET_GROUPS, n1, 2 * ncols), BF16),
        compiler_params=_cparams(("parallel", "parallel")),
        name="fnet1",
    )(x2, stage1, twr, twi)
    a4 = a.reshape(bsz * FNET_GROUPS, n1, LANES, 2 * c)
    out = pl.pallas_call(
        functools.partial(_fnet2_kernel, n1=n1, scale=1.0 / math.sqrt(seq * c)),
        grid=(bsz, FNET_GROUPS),
        in_specs=[pl.BlockSpec((1, n1, LANES, 2 * c), lambda b, g: (b * FNET_GROUPS + g, 0, 0, 0)),
                  _const_spec(chan.shape), _const_spec(c2.shape), _const_spec(s2.shape)],
        out_specs=pl.BlockSpec((1, seq, c), lambda b, g: (b, 0, g)),
        out_shape=jax.ShapeDtypeStruct((bsz, seq, FNET_WIDTH), BF16),
        scratch_shapes=[pltpu.VMEM((n1, LANES, 2 * c), BF16),
                        pltpu.VMEM((LANES * (n1 + SUBLANES), c), F32)],
        compiler_params=_cparams(("parallel", "parallel")),
        name="fnet2",
    )(a4, chan, c2, s2)
    return out.reshape(bsz * seq, FNET_WIDTH)


def kernel(x, p, rel_bias, norm_mix, w_in, conv_dw, conv_dw_b, conv_ln_g, conv_ln_b, conv_out,
           ssm_conv_w, ssm_conv_b, ssm_a_log, ssm_dt_bias, ssm_d, ssm_norm, ssm_out,
           attn_out, fnet_out, w_gate, b_gate, w_out, norm_ffn, ffn_w1, ffn_w3, ffn_w2,
           moe_router, moe_w1, moe_w3, moe_w2, ple_gate, ple_proj, final_norm):
    bsz, seq, d = x.shape
    depth = w_in.shape[0]
    t = bsz * seq
    h = x.reshape(t, d)
    bias_tables = [_att_bias_tables(rel_bias, g, dil) for g, (_, dil) in enumerate(ATT_PATTERNS)]
    for l in range(depth):
        *qkv, cnf, z, xbc, fn, dt = _inproj(
            h, norm_mix[l][None], _reorder_w_in(w_in[l]),
            (conv_dw[l], conv_dw_b[l], conv_ln_g[l], conv_ln_b[l]), bsz, seq)
        att = [_attention_group(qkv[g], bias_tables[g], dil, bsz, seq, f"attn{g}")
               for g, (_, dil) in enumerate(ATT_PATTERNS)]
        ssd = _ssd(z, xbc, dt, ssm_conv_w[l], ssm_conv_b[l], ssm_a_log[l], ssm_dt_bias[l],
                   ssm_d[l], ssm_norm[l], bsz, seq)
        fnt = _fourier(fn, bsz, seq)
        wbr = jnp.stack([attn_out[l], conv_out[l], ssm_out[l], fnet_out[l]]).astype(BF16)
        h = _mix(h, norm_mix[l][None], att, (cnf, ssd, fnt), wbr, w_gate[l].astype(BF16),
                 b_gate[l][:, None, :], w_out[l].astype(BF16))
        pl_in = p[l].reshape(t, -1)
        wpg, wpp = ple_gate[l].astype(BF16), ple_proj[l].astype(BF16)
        i = l // 2
        if l % 2 == 0:
            h = _ffn(h, norm_ffn[l][None], ffn_w1[i].astype(BF16), ffn_w3[i].astype(BF16),
                     ffn_w2[i].astype(BF16), pl_in, wpg, wpp)
        else:
            h = _moe(h, norm_ffn[l][None], moe_router[i], moe_w1[i].astype(BF16),
                     moe_w3[i].astype(BF16), moe_w2[i].astype(BF16), pl_in, wpg, wpp)
    return _final_norm(h, final_norm[None]).reshape(bsz, seq, d)
```

```python
import functools
import math

import numpy as np
import jax
import jax.numpy as jnp
from jax import lax
from jax.experimental import pallas as pl
from jax.experimental.pallas import tpu as pltpu

F32 = jnp.float32
BF16 = jnp.bfloat16
HI = lax.Precision.HIGHEST

EPS = 1e-6
N_BRANCHES = 4
CONV_CH = 512
CONV_K = 31
SSM_HEADS = 8
SSM_HEAD_DIM = 64
SSM_INNER = SSM_HEADS * SSM_HEAD_DIM
SSM_GROUPS = 2
SSM_STATE = 64
SSM_CONV = 5
SSM_CONV_CH = SSM_INNER + 2 * SSM_GROUPS * SSM_STATE
SSM_CHUNK = 128
ATT_PATTERNS = ((128, 1), (512, 4), (2048, 16))
ATT_GROUPS = len(ATT_PATTERNS)
ATT_HEADS = 8
ATT_HEAD_DIM = 64
ATT_WIDTH = ATT_HEADS * ATT_HEAD_DIM
REL_BUCKETS = 32
REL_MAX_DIST = 1024
FNET_GROUPS = 4
FNET_GROUP_DIM = 128
FNET_WIDTH = FNET_GROUPS * FNET_GROUP_DIM
N_EXPERTS = 8
TOP_K = 2

ATT_IN_COLS = 3 * ATT_GROUPS * ATT_WIDTH
CONV_IN_COLS = 2 * CONV_CH
SSM_IN_COLS = SSM_INNER + SSM_CONV_CH + 2 * SSM_HEADS
OFF_CONV = ATT_IN_COLS
OFF_SSM = OFF_CONV + CONV_IN_COLS
OFF_FNET = OFF_SSM + SSM_IN_COLS

LANES = 128
DT_PAD = LANES
V7X_VMEM_BYTES = 64 * 1024 * 1024
VMEM_LIMIT = V7X_VMEM_BYTES - 8 * 1024 * 1024

MIX_ROWS = 512
FFN_ROWS, FFN_COLS = 512, 1536
MOE_TOKENS, MOE_COLS = 1024, 1792
NORM_ROWS = 1024
SEQ_ROWS = 512


def _cparams(sem):
    return pltpu.CompilerParams(dimension_semantics=sem, vmem_limit_bytes=VMEM_LIMIT)


def _const_spec(shape):
    nd = len(shape)
    return pl.BlockSpec(shape, lambda *_: (0,) * nd, pipeline_mode=pl.Buffered(1))


def _rms(x, g):
    return x * lax.rsqrt(jnp.mean(x * x, axis=-1, keepdims=True) + EPS) * g


_SEC_QKV = (0, ATT_IN_COLS)
_SEC_CONV = (_SEC_QKV[0] + _SEC_QKV[1], CONV_IN_COLS)
_SEC_Z = (_SEC_CONV[0] + _SEC_CONV[1], SSM_INNER)
_SEC_XBC = (_SEC_Z[0] + _SEC_Z[1], SSM_CONV_CH)
_SEC_FNET = (_SEC_XBC[0] + _SEC_XBC[1], FNET_WIDTH)
_SEC_DT = (_SEC_FNET[0] + _SEC_FNET[1], DT_PAD)
_IN_COLS_PAD = _SEC_DT[0] + _SEC_DT[1]
_MM_CHUNK = 512


ATT_TILE = 256
QKV_COLS = 3 * ATT_WIDTH
INPROJ_ROWS = 2 * ATT_TILE


def _reorder_w_in(w):
    d = w.shape[0]
    qkv = w[:, :OFF_CONV].reshape(d, 3, ATT_GROUPS, ATT_WIDTH).transpose(0, 2, 1, 3)
    ssm = w[:, OFF_SSM:OFF_FNET]
    dt = ssm[:, SSM_INNER + SSM_CONV_CH:]
    parts = [qkv.reshape(d, OFF_CONV), w[:, OFF_CONV:OFF_SSM], ssm[:, :SSM_INNER],
             ssm[:, SSM_INNER:SSM_INNER + SSM_CONV_CH], w[:, OFF_FNET:],
             dt, jnp.zeros((d, DT_PAD - dt.shape[1]), w.dtype)]
    return jnp.concatenate(parts, axis=1).astype(BF16)


def _deinterleave_matrix(dil):
    s = np.arange(ATT_TILE)
    m = np.zeros((ATT_TILE, ATT_TILE), np.float32)
    m[(s % dil) * (ATT_TILE // dil) + s // dil, s] = 1.0
    return m


def _inproj_kernel(h_ref, hp_ref, hn_ref, g_ref, w_ref, perm_ref, dw_ref, dwb_ref, lng_ref,
                   lnb_ref, q0_ref, q1_ref, q2_ref, cnf_ref, z_ref, xbc_ref, fn_ref, dt_ref,
                   pad_ref, sh_ref, *, spt):
    tm = h_ref.shape[0]
    pos = pl.program_id(0) % spt
    norm = lambda ref: _rms(ref[...], g_ref[...]).astype(BF16)
    xn = norm(h_ref)

    xe = jnp.concatenate([norm(hp_ref), xn, norm(hn_ref)], axis=0)
    c0 = _SEC_CONV[0]
    lin = jnp.dot(xe, w_ref[:, c0:c0 + CONV_CH], preferred_element_type=F32)
    gate = jnp.dot(xe, w_ref[:, c0 + CONV_CH:c0 + 2 * CONV_CH], preferred_element_type=F32)
    hid = lin * jax.nn.sigmoid(gate)
    pad_ref[0:CONV_HALO] = jnp.where(pos > 0, hid[:CONV_HALO], 0.0)
    pad_ref[CONV_HALO:CONV_HALO + tm] = hid[CONV_HALO:CONV_HALO + tm]
    pad_ref[CONV_HALO + tm:] = jnp.where(pos < spt - 1, hid[CONV_HALO + tm:], 0.0)
    span = tm + 2 * CONV_HALO - SUBLANES
    for b in range(1, SUBLANES):
        sh_ref[b - 1] = pad_ref[pl.ds(b, span), :]
    first = CONV_HALO - CONV_K // 2

    def conv_rows(c):
        acc = jnp.broadcast_to(dwb_ref[...], (CONV_ROWS, CONV_CH))
        for k in range(CONV_K):
            a, b = divmod(first + k, SUBLANES)
            at = pl.ds(c * CONV_ROWS + a * SUBLANES, CONV_ROWS)
            tap = pad_ref[at, :] if b == 0 else sh_ref[b - 1, at, :]
            acc = acc + dw_ref[pl.ds(k, 1), :] * tap
        mu = jnp.mean(acc, axis=-1, keepdims=True)
        cen = acc - mu
        var = jnp.mean(cen * cen, axis=-1, keepdims=True)
        y = cen * lax.rsqrt(var + EPS) * lng_ref[...] + lnb_ref[...]
        cnf_ref[pl.ds(c * CONV_ROWS, CONV_ROWS), :] = (y * jax.nn.sigmoid(y)).astype(cnf_ref.dtype)

    for c in range(tm // CONV_ROWS):
        conv_rows(c)

    def section(x, sec, store):
        start, width = sec
        for c in range(0, width, _MM_CHUNK):
            cw = min(_MM_CHUNK, width - c)
            store(c, cw, jnp.dot(x, w_ref[:, start + c:start + c + cw],
                                 preferred_element_type=F32))

    def to(ref):
        def store(c, cw, val):
            ref[:, c:c + cw] = val.astype(ref.dtype)
        return store

    def to_fnet(c, cw, val):
        for g in range(cw // FNET_GROUP_DIM):
            fn_ref[0, c // FNET_GROUP_DIM + g] = val[:, g * FNET_GROUP_DIM:(g + 1) * FNET_GROUP_DIM]

    for g, q_ref in enumerate((q0_ref, q1_ref, q2_ref)):
        x = xn
        if ATT_PATTERNS[g][1] > 1:
            x = jnp.concatenate(
                [jnp.dot(perm_ref[g], xn[i * ATT_TILE:(i + 1) * ATT_TILE],
                         preferred_element_type=F32) for i in range(xn.shape[0] // ATT_TILE)],
                axis=0).astype(BF16)
        section(x, (g * QKV_COLS, QKV_COLS), to(q_ref))
    section(xn, _SEC_Z, to(z_ref))
    section(xn, _SEC_XBC, to(xbc_ref))
    section(xn, _SEC_FNET, to_fnet)
    section(xn, _SEC_DT, to(dt_ref))


def _inproj(h, g, w, conv, bsz, seq):
    t, d = h.shape
    tm = INPROJ_ROWS
    spt = seq // tm
    dw, dw_b, ln_g, ln_b = conv
    perm = jnp.asarray(np.stack([_deinterleave_matrix(dil) for _, dil in ATT_PATTERNS]), BF16)
    row = lambda n: pl.BlockSpec((tm, n), lambda i: (i, 0))
    hb = tm // CONV_HALO
    n_hb = t // CONV_HALO
    qkv = jax.ShapeDtypeStruct((t, QKV_COLS), BF16)
    vec = _const_spec((1, CONV_CH))
    return pl.pallas_call(
        functools.partial(_inproj_kernel, spt=spt),
        grid=(t // tm,),
        in_specs=[row(d),
                  pl.BlockSpec((CONV_HALO, d), lambda i: (jnp.maximum(i * hb - 1, 0), 0)),
                  pl.BlockSpec((CONV_HALO, d), lambda i: (jnp.minimum((i + 1) * hb, n_hb - 1), 0)),
                  _const_spec((1, d)), _const_spec(w.shape), _const_spec(perm.shape),
                  _const_spec(dw.shape), vec, vec, vec],
        out_specs=[row(QKV_COLS), row(QKV_COLS), row(QKV_COLS), row(CONV_CH),
                   row(SSM_INNER), row(SSM_CONV_CH),
                   pl.BlockSpec((1, FNET_GROUPS, tm, FNET_GROUP_DIM),
                                lambda i: (i // spt, 0, i % spt, 0)),
                   row(DT_PAD)],
        out_shape=[qkv, qkv, qkv,
                   jax.ShapeDtypeStruct((t, CONV_CH), BF16),
                   jax.ShapeDtypeStruct((t, SSM_INNER), F32),
                   jax.ShapeDtypeStruct((t, SSM_CONV_CH), F32),
                   jax.ShapeDtypeStruct((bsz, FNET_GROUPS, seq, FNET_GROUP_DIM), F32),
                   jax.ShapeDtypeStruct((t, DT_PAD), F32)],
        scratch_shapes=[pltpu.VMEM((tm + 2 * CONV_HALO, CONV_CH), F32),
                        pltpu.VMEM((SUBLANES - 1, tm + 2 * CONV_HALO - SUBLANES, CONV_CH), F32)],
        compiler_params=_cparams(("parallel",)),
        name="inproj",
    )(h, h, h, g, w, perm, dw, dw_b[None], ln_g[None], ln_b[None])


def _split_dot(v, m):
    hi = v.astype(BF16)
    lo = (v - hi.astype(F32)).astype(BF16)
    return (jnp.dot(hi, m, preferred_element_type=F32) + jnp.dot(lo, m, preferred_element_type=F32))


def _interleave(pt, v):
    n = v.shape[1]
    if v.dtype != BF16:
        hi = v.astype(BF16)
        v = jnp.concatenate([hi, (v - hi.astype(F32)).astype(BF16)], axis=1)
    tiles = []
    for i in range(v.shape[0] // ATT_TILE):
        r = jnp.dot(pt, v[i * ATT_TILE:(i + 1) * ATT_TILE], preferred_element_type=F32)
        tiles.append(r if r.shape[1] == n else r[:, :n] + r[:, n:])
    return jnp.concatenate(tiles, axis=0)


def _mix_kernel(h_ref, g_ref, o0_ref, o1_ref, o2_ref, l0_ref, l1_ref, l2_ref, b1_ref, b2_ref,
                b3_ref, pt_ref, hx_ref, wbr_ref, wg_ref, cg_ref, wo_ref, o_ref):
    h = h_ref[...]
    xn = _rms(h, g_ref[...]).astype(BF16)
    outs, lses = [], []
    for g, (og_ref, lg_ref) in enumerate(zip((o0_ref, o1_ref, o2_ref), (l0_ref, l1_ref, l2_ref))):
        if ATT_PATTERNS[g][1] > 1:
            outs.append(_interleave(pt_ref[g], og_ref[...]))
            lses.append(_interleave(pt_ref[g], lg_ref[...]))
        else:
            outs.append(og_ref[...].astype(F32))
            lses.append(lg_ref[...])
    top = jnp.maximum(jnp.maximum(lses[0], lses[1]), lses[2])
    es = [jnp.exp(l - top) for l in lses]
    inv = 1.0 / (es[0] + es[1] + es[2])
    att = None
    for e, og in zip(es, outs):
        term = og * jnp.dot((e * inv).astype(BF16), hx_ref[...], preferred_element_type=F32)
        att = term if att is None else att + term
    acc = None
    for b, hid in enumerate((att.astype(BF16), b1_ref[...], b2_ref[...], b3_ref[...])):
        gate = jax.nn.sigmoid(jnp.dot(xn, wg_ref[b], preferred_element_type=F32) + cg_ref[b])
        br = jnp.dot(hid, wbr_ref[b], preferred_element_type=F32)
        acc = gate * br if acc is None else acc + gate * br
    o_ref[...] = h + jnp.dot(acc.astype(BF16), wo_ref[...], preferred_element_type=F32)


def _mix(h, g, att, others, wbr, wg, cg, wo):
    t, d = h.shape
    tm = MIX_ROWS
    row = lambda n: pl.BlockSpec((tm, n), lambda i: (i, 0))
    head_expand = np.zeros((LANES, ATT_WIDTH), np.float32)
    for hd in range(ATT_HEADS):
        head_expand[hd, hd * ATT_HEAD_DIM:(hd + 1) * ATT_HEAD_DIM] = 1.0
    head_expand = jnp.asarray(head_expand, BF16)
    unperm = jnp.asarray(np.stack([_deinterleave_matrix(dil).T for _, dil in ATT_PATTERNS]), BF16)
    outs = [o for o, _ in att]
    lses = [l for _, l in att]
    return pl.pallas_call(
        _mix_kernel,
        grid=(t // tm,),
        in_specs=[row(d), _const_spec((1, d))] + [row(a.shape[1]) for a in outs + lses + list(others)]
                 + [_const_spec(unperm.shape), _const_spec(head_expand.shape), _const_spec(wbr.shape),
                    _const_spec(wg.shape), _const_spec(cg.shape), _const_spec(wo.shape)],
        out_specs=row(d),
        out_shape=jax.ShapeDtypeStruct((t, d), F32),
        compiler_params=_cparams(("parallel",)),
        name="mix",
    )(h, g, *outs, *lses, *others, unperm, head_expand, wbr, wg, cg, wo)


def _ple(h2, p_ref, wpg_ref, wpp_ref):
    gate = jax.nn.sigmoid(jnp.dot(h2.astype(BF16), wpg_ref[...], preferred_element_type=F32))
    pe = jnp.dot(p_ref[...].astype(BF16), wpp_ref[...], preferred_element_type=F32)
    return h2 + gate * pe


def _swiglu_partial(xn, w1, w3, w2):
    a = jnp.dot(xn, w1, preferred_element_type=F32)
    b = jnp.dot(xn, w3, preferred_element_type=F32)
    hid = a * jax.nn.sigmoid(a) * b
    return jnp.dot(hid.astype(BF16), w2, preferred_element_type=F32)


def _ffn_kernel(h_ref, g_ref, w1_ref, w3_ref, w2_ref, p_ref, wpg_ref, wpp_ref, o_ref):
    h = h_ref[...]
    xn = _rms(h, g_ref[...]).astype(BF16)
    acc = h
    f = w1_ref.shape[1]
    for c0 in range(0, f, FFN_COLS):
        c1 = min(c0 + FFN_COLS, f)
        acc = acc + _swiglu_partial(xn, w1_ref[:, c0:c1], w3_ref[:, c0:c1], w2_ref[c0:c1, :])
    o_ref[...] = _ple(acc, p_ref, wpg_ref, wpp_ref)


def _ffn(h, g, w1, w3, w2, p, wpg, wpp):
    t, d = h.shape
    tm = FFN_ROWS
    row = lambda n: pl.BlockSpec((tm, n), lambda i: (i, 0))
    return pl.pallas_call(
        _ffn_kernel,
        grid=(t // tm,),
        in_specs=[row(d), _const_spec((1, d)), _const_spec(w1.shape), _const_spec(w3.shape),
                  _const_spec(w2.shape), row(p.shape[1]), _const_spec(wpg.shape),
                  _const_spec(wpp.shape)],
        out_specs=row(d),
        out_shape=jax.ShapeDtypeStruct((t, d), F32),
        compiler_params=_cparams(("parallel",)),
        name="ffn",
    )(h, g, w1, w3, w2, p, wpg, wpp)


MOE_ROWS = 128
MOE_MAX_BLOCKS = 2


def _moe_route(logits):
    ne, tm = logits.shape
    eidx = lax.broadcasted_iota(jnp.int32, logits.shape, 0)
    m1 = jnp.max(logits, axis=0, keepdims=True)
    i1 = jnp.min(jnp.where(logits == m1, eidx, ne), axis=0, keepdims=True)
    rest = jnp.where(eidx == i1, -jnp.inf, logits)
    m2 = jnp.max(rest, axis=0, keepdims=True)
    i2 = jnp.min(jnp.where(rest == m2, eidx, ne), axis=0, keepdims=True)
    e2 = jnp.exp(m2 - m1)
    den = 1.0 + e2
    combine = jnp.where(eidx == i1, 1.0 / den, 0.0) + jnp.where(eidx == i2, e2 / den, 0.0)
    routed = jnp.where((eidx == i1) | (eidx == i2), 1.0, 0.0)
    r = lax.broadcasted_iota(jnp.int32, (LANES, LANES), 0)
    c = lax.broadcasted_iota(jnp.int32, (LANES, LANES), 1)
    before = jnp.where(r < c, 1.0, 0.0).astype(BF16)
    counts = jnp.zeros((ne, 1), F32)
    slots = []
    for k in range(tm // LANES):
        blk = routed[:, k * LANES:(k + 1) * LANES]
        slots.append(jnp.dot(blk.astype(BF16), before, preferred_element_type=F32) + counts)
        counts = counts + jnp.sum(blk, axis=1, keepdims=True)
    slot = jnp.where(routed > 0.0, jnp.concatenate(slots, axis=1), -1.0).astype(jnp.int32)
    return combine, slot, counts


def _moe_kernel(h_ref, g_ref, rt_ref, w1_ref, w3_ref, w2_ref, p_ref, wpg_ref, wpp_ref, o_ref,
                xn_ref, comb_ref, slot_ref, cnt_ref, xe_ref, ye_ref):
    e = pl.program_id(1)
    j = pl.program_id(2)
    ne = pl.num_programs(1)
    tm = xn_ref.shape[0]

    @pl.when((e == 0) & (j == 0))
    def _():
        h = h_ref[...]
        xn = _rms(h, g_ref[...])
        xn_ref[...] = xn.astype(BF16)
        logits = lax.dot_general(rt_ref[...], xn, (((1,), (1,)), ((), ())),
                                 preferred_element_type=F32, precision=HI)
        combine, slot, counts = _moe_route(logits)
        comb_ref[...] = combine
        slot_ref[...] = slot
        for k in range(comb_ref.shape[0]):
            cnt_ref[k] = jnp.sum(counts[k:k + 1, :]).astype(jnp.int32)
        o_ref[...] = h

    n_blocks = (cnt_ref[e] + MOE_ROWS - 1) // MOE_ROWS

    def for_row_blocks(body):
        full = MOE_MAX_BLOCKS * MOE_ROWS

        def whole(i, carry):
            body(pl.multiple_of(i * full, full), full)
            return carry
        lax.fori_loop(0, n_blocks // MOE_MAX_BLOCKS, whole, 0)
        base = pl.multiple_of((n_blocks // MOE_MAX_BLOCKS) * full, full)
        for r in range(1, MOE_MAX_BLOCKS):
            @pl.when(n_blocks % MOE_MAX_BLOCKS == r)
            def _():
                body(base, r * MOE_ROWS)

    def one_hot(r0, rows):
        return slot_ref[pl.ds(e, 1), :] == lax.broadcasted_iota(jnp.int32, (rows, tm), 0) + r0

    def swiglu(x):
        return _swiglu_partial(x, w1_ref[0], w3_ref[0], w2_ref[0])

    last_j = pl.num_programs(2) - 1

    @pl.when(j == 0)
    def _():
        def gather_first(r0, rows):
            sel = jnp.where(one_hot(r0, rows), 1.0, 0.0).astype(BF16)
            x = jnp.dot(sel, xn_ref[...], preferred_element_type=F32).astype(BF16)
            xe_ref[pl.ds(r0, rows), :] = x
            ye_ref[pl.ds(r0, rows), :] = swiglu(x)
        for_row_blocks(gather_first)

    @pl.when((j > 0) & (j < last_j))
    def _():
        def middle(r0, rows):
            ye_ref[pl.ds(r0, rows), :] += swiglu(xe_ref[pl.ds(r0, rows), :])
        for_row_blocks(middle)

    @pl.when((j > 0) & (j == last_j))
    def _():
        def last_scatter(r0, rows):
            y = ye_ref[pl.ds(r0, rows), :] + swiglu(xe_ref[pl.ds(r0, rows), :])
            hot = one_hot(r0, rows)
            weight = jnp.sum(jnp.where(hot, comb_ref[pl.ds(e, 1), :], 0.0), axis=1, keepdims=True)
            o_ref[...] += lax.dot_general(jnp.where(hot, 1.0, 0.0).astype(BF16),
                                          (y * weight).astype(BF16),
                                          (((0,), (0,)), ((), ())), preferred_element_type=F32)
        for_row_blocks(last_scatter)

    @pl.when((e == ne - 1) & (j == pl.num_programs(2) - 1))
    def _():
        o_ref[...] = _ple(o_ref[...], p_ref, wpg_ref, wpp_ref)


def _moe(h, g, router, w1, w3, w2, p, wpg, wpp):
    t, d = h.shape
    ne, _, f = w1.shape
    tm, tf = MOE_TOKENS, MOE_COLS
    assert f % tf == 0 and f // tf >= 2
    row = lambda n: pl.BlockSpec((tm, n), lambda i, e, j: (i, 0))
    return pl.pallas_call(
        _moe_kernel,
        grid=(t // tm, ne, f // tf),
        in_specs=[row(d), _const_spec((1, d)), _const_spec((ne, d)),
                  pl.BlockSpec((1, d, tf), lambda i, e, j: (e, 0, j)),
                  pl.BlockSpec((1, d, tf), lambda i, e, j: (e, 0, j)),
                  pl.BlockSpec((1, tf, d), lambda i, e, j: (e, j, 0)),
                  row(p.shape[1]), _const_spec(wpg.shape), _const_spec(wpp.shape)],
        out_specs=row(d),
        out_shape=jax.ShapeDtypeStruct((t, d), F32),
        scratch_shapes=[pltpu.VMEM((tm, d), BF16), pltpu.VMEM((ne, tm), F32),
                        pltpu.VMEM((ne, tm), jnp.int32), pltpu.SMEM((ne,), jnp.int32),
                        pltpu.VMEM((tm, d), BF16), pltpu.VMEM((tm, d), F32)],
        compiler_params=_cparams(("parallel", "arbitrary", "arbitrary")),
        name="moe",
    )(h, g, router.T, w1, w3, w2, p, wpg, wpp)


def _final_norm_kernel(h_ref, g_ref, o_ref):
    o_ref[...] = _rms(h_ref[...], g_ref[...])


def _final_norm(h, g):
    t, d = h.shape
    tm = NORM_ROWS
    row = pl.BlockSpec((tm, d), lambda i: (i, 0))
    return pl.pallas_call(
        _final_norm_kernel, grid=(t // tm,),
        in_specs=[row, _const_spec((1, d))], out_specs=row,
        out_shape=jax.ShapeDtypeStruct((t, d), F32),
        compiler_params=_cparams(("parallel",)), name="final_norm",
    )(h, g)


ATT_HALF = 64
ATT_QB = 128
ATT_KB = ATT_QB + 2 * ATT_HALF
NEG = -1e30
assert all(w // (2 * d) == ATT_HALF for w, d in ATT_PATTERNS)


def _t5_bucket(rel):
    half = REL_BUCKETS // 2
    max_exact = half // 2
    n = np.abs(rel)
    large = max_exact + (np.log(np.maximum(n, 1) / max_exact) / math.log(REL_MAX_DIST / max_exact)
                         * (half - max_exact)).astype(np.int32)
    large = np.minimum(large, half - 1)
    return np.where(rel > 0, half, 0) + np.where(n < max_exact, n, large)


def _att_bias_tables(rel_bias, g, dil):
    i = np.arange(ATT_QB)[:, None]
    j = np.arange(ATT_KB)[None, :]
    rel = j - ATT_HALF - i
    band = np.abs(rel) <= ATT_HALF
    pick = np.eye(REL_BUCKETS, dtype=np.float32)[_t5_bucket(dil * rel)]
    heads = rel_bias[:, g * ATT_HEADS:(g + 1) * ATT_HEADS].astype(F32)
    bias = jnp.einsum('qkb,bh->hqk', pick, heads, precision=lax.Precision.HIGHEST)
    tables = []
    for v in range(4):
        ok = band
        if v & 1:
            ok = ok & (j >= ATT_HALF)
        if v & 2:
            ok = ok & (j < ATT_QB + ATT_HALF)
        tables.append(jnp.where(ok[None], bias, NEG))
    return jnp.stack(tables)


def _attn_kernel(q_ref, k_ref, v_ref, kp_ref, kn_ref, vp_ref, vn_ref, bias_ref, o_ref, lse_ref,
                 qbuf, kbuf, vbuf, obuf, lbuf, *, tq, n_blocks):
    flat = lambda ref: ref[0].reshape(-1, ref.shape[-1])
    qbuf[...] = flat(q_ref)
    kbuf[0:ATT_HALF] = flat(kp_ref)
    kbuf[ATT_HALF:ATT_HALF + tq] = flat(k_ref)
    kbuf[ATT_HALF + tq:] = flat(kn_ref)
    vbuf[0:ATT_HALF] = flat(vp_ref)
    vbuf[ATT_HALF:ATT_HALF + tq] = flat(v_ref)
    vbuf[ATT_HALF + tq:] = flat(vn_ref)
    nsb = tq // ATT_QB
    first = pl.program_id(2) * nsb
    lane = lax.broadcasted_iota(jnp.int32, (ATT_QB, LANES), 1)
    low = lane < ATT_HEAD_DIM
    lane_row = lax.broadcasted_iota(jnp.int32, (1, LANES), 1)
    keep = [(lane_row < ATT_HEAD_DIM).astype(BF16), (lane_row >= ATT_HEAD_DIM).astype(BF16)]
    ones = jnp.ones((ATT_KB, LANES), BF16)

    def block(sb, carry):
        r0 = pl.multiple_of(sb * ATT_QB, ATT_QB)
        gsb = first + sb
        variant = (gsb == 0).astype(jnp.int32) + 2 * (gsb == n_blocks - 1).astype(jnp.int32)
        q = qbuf[pl.ds(r0, ATT_QB), :] * (ATT_HEAD_DIM ** -0.5)
        lse_all = jnp.zeros((ATT_QB, LANES), F32)
        outs = []
        for pair in range(ATT_HEADS // 2):
            cols = slice(pair * LANES, (pair + 1) * LANES)
            qp = q[:, cols]
            kp = kbuf[pl.ds(r0, ATT_KB), cols]
            vp = jnp.concatenate([vbuf[pl.ds(r0, ATT_KB), cols], ones], axis=1)
            q2 = jnp.concatenate([qp * keep[0], qp * keep[1]], axis=0)
            s = lax.dot_general(q2, kp, (((1,), (1,)), ((), ())), preferred_element_type=F32)
            s = s + bias_ref[variant, 2 * pair:2 * pair + 2].reshape(2 * ATT_QB, ATT_KB)
            m = jnp.max(s, axis=-1, keepdims=True)
            e = jnp.exp(s - m)
            pv = jnp.dot(e.astype(BF16), vp, preferred_element_type=F32)
            den = pv[:, LANES:]
            o2 = pv[:, :LANES] / den
            lse2 = m + jnp.log(den)
            outs.append(jnp.where(low, o2[:ATT_QB], o2[ATT_QB:]))
            lse_all = jnp.where(lane == 2 * pair, lse2[:ATT_QB],
                                jnp.where(lane == 2 * pair + 1, lse2[ATT_QB:], lse_all))
        obuf[pl.ds(r0, ATT_QB), :] = jnp.concatenate(outs, axis=1).astype(obuf.dtype)
        lbuf[pl.ds(r0, ATT_QB), :] = lse_all
        return carry

    lax.fori_loop(0, nsb, block, 0, unroll=4)
    o_ref[0] = obuf[...].reshape(o_ref.shape[1:])
    lse_ref[0] = lbuf[...].reshape(lse_ref.shape[1:])


def _attention_group(qkv, bias_tables, dil, bsz, seq, name):
    sub_len = seq // dil
    assert sub_len % ATT_QB == 0 and seq % ATT_TILE == 0 and ATT_TILE % dil == 0
    rows = ATT_TILE // dil
    tq = min(SEQ_ROWS, sub_len)
    w = ATT_WIDTH
    hrows = min(rows, ATT_HALF)
    n_hb = sub_len // ATT_HALF

    def view(a, chunk):
        return a.reshape(bsz, (seq // ATT_TILE) * (rows // chunk), dil, chunk, a.shape[-1])

    def main(part):
        return pl.BlockSpec((1, tq // rows, None, rows, w), lambda b, r, n: (b, n, r, 0, part))

    def halo(part, nxt):
        if nxt:
            blk = lambda n: jnp.minimum((n + 1) * (tq // ATT_HALF), n_hb - 1)
        else:
            blk = lambda n: jnp.maximum(n * (tq // ATT_HALF) - 1, 0)
        return pl.BlockSpec((1, ATT_HALF // hrows, None, hrows, w),
                            lambda b, r, n: (b, blk(n), r, 0, part))

    mv, hv = view(qkv, rows), view(qkv, hrows)
    o, lse = pl.pallas_call(
        functools.partial(_attn_kernel, tq=tq, n_blocks=sub_len // ATT_QB),
        grid=(bsz, dil, sub_len // tq),
        in_specs=[main(0), main(1), main(2), halo(1, False), halo(1, True), halo(2, False),
                  halo(2, True), _const_spec(bias_tables.shape)],
        out_specs=[pl.BlockSpec((1, tq // rows, None, rows, w), lambda b, r, n: (b, n, r, 0, 0)),
                   pl.BlockSpec((1, tq // rows, None, rows, LANES), lambda b, r, n: (b, n, r, 0, 0))],
        out_shape=[jax.ShapeDtypeStruct((bsz, seq // ATT_TILE, dil, rows, w), BF16),
                   jax.ShapeDtypeStruct((bsz, seq // ATT_TILE, dil, rows, LANES), F32)],
        scratch_shapes=[pltpu.VMEM((tq, w), BF16),
                        pltpu.VMEM((tq + 2 * ATT_HALF, w), BF16),
                        pltpu.VMEM((tq + 2 * ATT_HALF, w), BF16),
                        pltpu.VMEM((tq, w), BF16), pltpu.VMEM((tq, LANES), F32)],
        compiler_params=_cparams(("parallel", "parallel", "parallel")),
        name=name,
    )(mv, mv, mv, hv, hv, hv, hv, bias_tables)
    return o.reshape(bsz * seq, w), lse.reshape(bsz * seq, LANES)


SUBLANES = 8
CONV_HALO = 16
CONV_ROWS = 64


SSD_HALO = SUBLANES
SSD_ROWS = 64
SSD_BC = SSM_GROUPS * SSM_STATE
HEADS_PER_GROUP = SSM_HEADS // SSM_GROUPS
GROUP_LANES = HEADS_PER_GROUP * SSM_HEAD_DIM
assert SSD_BC == LANES and SSM_CHUNK == LANES and 2 * SSM_HEADS <= LANES


def _ssd_pre_kernel(x_ref, xp_ref, xn_ref, dt_ref, cw_ref, cb_ref, dtb_ref, arow_ref, tri_ref,
                    ex_ref, xo_ref, dto_ref, acs_ref, nf_ref, nb_ref, cy_ref, hp_ref, sh_ref,
                    *, ts, phases):
    n = pl.program_id(1)
    hp_ref[0:SSD_HALO] = jnp.where(n > 0, xp_ref[0], 0.0)
    hp_ref[SSD_HALO:SSD_HALO + ts] = x_ref[0]
    hp_ref[SSD_HALO + ts:] = jnp.where(n < pl.num_programs(1) - 1, xn_ref[0], 0.0)
    for i, b in enumerate(phases):
        sh_ref[i] = hp_ref[pl.ds(b, ts + SSD_HALO), :]
    first = SSD_HALO - SSM_CONV // 2

    def rows(c, carry):
        r0 = pl.multiple_of(c * SSD_ROWS, SSD_ROWS)
        acc = jnp.broadcast_to(cb_ref[...], (SSD_ROWS, SSM_CONV_CH))
        for k in range(SSM_CONV):
            a, b = divmod(first + k, SUBLANES)
            acc = acc + cw_ref[pl.ds(k, 1), :] * sh_ref[phases.index(b),
                                                        pl.ds(r0 + a * SUBLANES, SSD_ROWS), :]
        xo_ref[0, pl.ds(r0, SSD_ROWS), :] = acc * jax.nn.sigmoid(acc)
        return carry

    lax.fori_loop(0, ts // SSD_ROWS, rows, 0, unroll=True)
    x = dt_ref[0] + dtb_ref[...]
    softplus = jnp.maximum(x, 0.0) + jnp.log1p(jnp.exp(-jnp.abs(x)))
    lane = lax.broadcasted_iota(jnp.int32, x.shape, 1)
    dtv = jnp.where(lane < 2 * SSM_HEADS, softplus, 0.0)
    dto_ref[0] = dtv

    nch = ts // SSM_CHUNK
    chunk = lambda k: slice(k * SSM_CHUNK, (k + 1) * SSM_CHUNK)
    dta = dtv * arow_ref[...]
    acs_all = _cumsum_both(jnp.concatenate([dta[chunk(k)] for k in range(nch)], axis=1), tri_ref)
    clane = lax.broadcasted_iota(jnp.int32, (SSM_CHUNK, DT_PAD), 1)
    ws, totals = [], [[], []]
    for k in range(nch):
        acs = acs_all[:, chunk(k)]
        acs_ref[0, chunk(k), :] = acs
        tot_f, tot_b = acs[SSM_CHUNK - 1:SSM_CHUNK, :], acs[0:1, :]
        to_end = jnp.where(clane < SSM_HEADS, tot_f - acs,
                           jnp.where(clane < 2 * SSM_HEADS, tot_b - acs, 0.0))
        ws.append(dtv[chunk(k)] * jnp.exp(to_end))
        totals[0].append(jnp.exp(tot_f))
        totals[1].append(jnp.exp(tot_b))
    w_all = jnp.concatenate(ws, axis=0)
    xs = xo_ref[0, :, :SSM_INNER]
    hl = lax.broadcasted_iota(jnp.int32, (SSM_STATE, SSM_INNER), 1)
    pad_rows = jnp.zeros((SUBLANES - nch % SUBLANES, DT_PAD), F32)
    for d, new_ref in enumerate((nf_ref, nb_ref)):
        xw = (xs * _expand(w_all, ex_ref, d)).astype(BF16)
        carry = _expand(jnp.concatenate(totals[d] + [pad_rows], axis=0), ex_ref, d)
        for k in range(nch):
            bmat = xo_ref[0, chunk(k), SSM_INNER:SSM_INNER + SSD_BC].astype(BF16)
            full = lax.dot_general(bmat, xw[chunk(k)], (((0,), (0,)), ((), ())),
                                   preferred_element_type=F32)
            new = full[:SSM_STATE]
            for g in range(1, SSM_GROUPS):
                new = jnp.where(hl >= g * GROUP_LANES, full[g * SSM_STATE:(g + 1) * SSM_STATE], new)
            new_ref[0, k] = new
            cy_ref[0, k, d:d + 1, :] = carry[k:k + 1]


def _ssd_pre(xbc, dt_raw, conv_w, conv_b, dt_bias, consts, bsz, seq):
    ts = min(SEQ_ROWS, seq)
    nc = seq // SSM_CHUNK
    nch = ts // SSM_CHUNK
    a_row, tri, expand = consts
    new = jax.ShapeDtypeStruct((bsz, nc, SSM_STATE, SSM_INNER), F32)
    new_spec = pl.BlockSpec((1, nch, SSM_STATE, SSM_INNER), lambda b, n: (b, n, 0, 0))
    xv = xbc.reshape(bsz, seq, SSM_CONV_CH)
    dv = dt_raw.reshape(bsz, seq, DT_PAD)
    hb = ts // SSD_HALO
    n_hb = seq // SSD_HALO
    first = SSD_HALO - SSM_CONV // 2
    phases = tuple(sorted({(first + k) % SUBLANES for k in range(SSM_CONV)}))
    dtb = jnp.zeros((1, DT_PAD), F32).at[0, :2 * SSM_HEADS].set(dt_bias.reshape(-1))
    return pl.pallas_call(
        functools.partial(_ssd_pre_kernel, ts=ts, phases=phases),
        grid=(bsz, seq // ts),
        in_specs=[pl.BlockSpec((1, ts, SSM_CONV_CH), lambda b, n: (b, n, 0)),
                  pl.BlockSpec((1, SSD_HALO, SSM_CONV_CH),
                               lambda b, n: (b, jnp.maximum(n * hb - 1, 0), 0)),
                  pl.BlockSpec((1, SSD_HALO, SSM_CONV_CH),
                               lambda b, n: (b, jnp.minimum((n + 1) * hb, n_hb - 1), 0)),
                  pl.BlockSpec((1, ts, DT_PAD), lambda b, n: (b, n, 0)),
                  _const_spec(conv_w.shape), _const_spec((1, SSM_CONV_CH)),
                  _const_spec((1, DT_PAD)), _const_spec(a_row.shape), _const_spec(tri.shape),
                  _const_spec(expand.shape)],
        out_specs=[pl.BlockSpec((1, ts, SSM_CONV_CH), lambda b, n: (b, n, 0)),
                   pl.BlockSpec((1, ts, DT_PAD), lambda b, n: (b, n, 0)),
                   pl.BlockSpec((1, ts, DT_PAD), lambda b, n: (b, n, 0)),
                   new_spec, new_spec,
                   pl.BlockSpec((1, nch, 2, SSM_INNER), lambda b, n: (b, n, 0, 0))],
        out_shape=[jax.ShapeDtypeStruct((bsz, seq, SSM_CONV_CH), F32),
                   jax.ShapeDtypeStruct((bsz, seq, DT_PAD), F32),
                   jax.ShapeDtypeStruct((bsz, seq, DT_PAD), F32),
                   new, new, jax.ShapeDtypeStruct((bsz, nc, 2, SSM_INNER), F32)],
        scratch_shapes=[pltpu.VMEM((ts + 2 * SSD_HALO, SSM_CONV_CH), F32),
                        pltpu.VMEM((len(phases), ts + SSD_HALO, SSM_CONV_CH), F32)],
        compiler_params=_cparams(("parallel", "parallel")),
        name="ssd_pre",
    )(xv, xv, xv, dv, conv_w, conv_b[None], dtb, a_row, tri, expand)


def _ssd_consts(a_log):
    a_row = jnp.zeros((1, DT_PAD), F32).at[0, :2 * SSM_HEADS].set(-jnp.exp(a_log.reshape(-1)))
    lower = np.tril(np.ones((SSM_CHUNK, SSM_CHUNK), np.float32))
    tri = jnp.asarray(np.concatenate([lower, lower.T], axis=0), BF16)
    expand = np.zeros((2, DT_PAD, SSM_INNER), np.float32)
    for d in range(2):
        for h in range(SSM_HEADS):
            expand[d, d * SSM_HEADS + h, h * SSM_HEAD_DIM:(h + 1) * SSM_HEAD_DIM] = 1.0
    return a_row, tri, jnp.asarray(expand, BF16)


def _cumsum_both(dta, tri_ref):
    hi = dta.astype(BF16)
    r1 = dta - hi.astype(F32)
    mid = r1.astype(BF16)
    lo = (r1 - mid.astype(F32)).astype(BF16)
    both = (jnp.dot(tri_ref[...], hi, preferred_element_type=F32)
            + jnp.dot(tri_ref[...], mid, preferred_element_type=F32)
            + jnp.dot(tri_ref[...], lo, preferred_element_type=F32))
    lane = lax.broadcasted_iota(jnp.int32, dta.shape, 1) % DT_PAD
    return jnp.where(lane < SSM_HEADS, both[:SSM_CHUNK], both[SSM_CHUNK:])


def _expand(v, ex_ref, d):
    return _split_dot(v, ex_ref[d])


def _ssd_scan_kernel(nf_ref, nb_ref, cf_ref, cb_ref, pf_ref, pb_ref, sf_ref, sb_ref, *, nch):
    @pl.when(pl.program_id(1) == 0)
    def _():
        sf_ref[...] = jnp.zeros_like(sf_ref)
        sb_ref[...] = jnp.zeros_like(sb_ref)

    def scan(d, order, new_ref, carry_ref, st_ref, out_ref):
        st = st_ref[...]
        for k in order:
            out_ref[0, k] = st.astype(out_ref.dtype)
            st = st * carry_ref[0, k, d:d + 1, :] + new_ref[0, k]
        st_ref[...] = st

    scan(0, range(nch), nf_ref, cf_ref, sf_ref, pf_ref)
    scan(1, range(nch - 1, -1, -1), nb_ref, cb_ref, sb_ref, pb_ref)


SSD_SCAN_CHUNKS = 16


def _ssd_states(new_f, new_b, carry, bsz, seq):
    nc = seq // SSM_CHUNK
    nch = min(SSD_SCAN_CHUNKS, nc)
    steps = nc // nch
    fwd = lambda b, c: (b, c, 0, 0)
    bwd = lambda b, c: (b, steps - 1 - c, 0, 0)
    st = jax.ShapeDtypeStruct((bsz, nc, SSM_STATE, SSM_INNER), BF16)
    blk = (1, nch, SSM_STATE, SSM_INNER)
    return pl.pallas_call(
        functools.partial(_ssd_scan_kernel, nch=nch),
        grid=(bsz, steps),
        in_specs=[pl.BlockSpec(blk, fwd), pl.BlockSpec(blk, bwd),
                  pl.BlockSpec((1, nch, 2, SSM_INNER), fwd),
                  pl.BlockSpec((1, nch, 2, SSM_INNER), bwd)],
        out_specs=[pl.BlockSpec(blk, fwd), pl.BlockSpec(blk, bwd)],
        out_shape=[st, st],
        scratch_shapes=[pltpu.VMEM((SSM_STATE, SSM_INNER), F32),
                        pltpu.VMEM((SSM_STATE, SSM_INNER), F32)],
        compiler_params=_cparams(("parallel", "arbitrary")),
        name="ssd_scan",
    )(new_f, new_b, carry, carry)


def _ssd_out_kernel(x_ref, dt_ref, acs_ref, z_ref, pf_ref, pb_ref, ex_ref, dskip_ref, ng_ref,
                    o_ref, *, nck):
    eacs = jnp.exp(acs_ref[0])
    decay = [_expand(eacs, ex_ref, d) for d in range(2)]
    for k in range(nck):
        _ssd_out_chunk(k, x_ref, dt_ref, acs_ref, z_ref, pf_ref, pb_ref, decay, dskip_ref,
                       ng_ref, o_ref)


def _ssd_out_chunk(k, x_ref, dt_ref, acs_ref, z_ref, pf_ref, pb_ref, decay, dskip_ref, ng_ref,
                   o_ref):
    rows = slice(k * SSM_CHUNK, (k + 1) * SSM_CHUNK)
    xs = x_ref[0, rows, :SSM_INNER]
    bmat = x_ref[0, rows, SSM_INNER:SSM_INNER + SSD_BC].astype(BF16)
    cmat = x_ref[0, rows, SSM_INNER + SSD_BC:].astype(BF16)
    dtv = dt_ref[0, rows, :]
    acs = acs_ref[0, rows, :]
    acs_t = acs.T
    dt_t = dtv.T
    row = lax.broadcasted_iota(jnp.int32, (SSM_CHUNK, SSM_CHUNK), 0)
    col = lax.broadcasted_iota(jnp.int32, (SSM_CHUNK, SSM_CHUNK), 1)
    past, now = col < row, col == row
    low = lax.broadcasted_iota(jnp.int32, (SSM_CHUNK, LANES), 1) < SSM_HEAD_DIM
    glane = lax.broadcasted_iota(jnp.int32, (1, SSD_BC), 1) // SSM_STATE
    both = lax.dot_general(
        jnp.concatenate([cmat * (glane == g).astype(BF16) for g in range(SSM_GROUPS)], axis=0),
        bmat, (((1,), (1,)), ((), ())), preferred_element_type=F32)
    scores = [both[g * SSM_CHUNK:(g + 1) * SSM_CHUNK] for g in range(SSM_GROUPS)]
    xs_b = xs.astype(BF16)
    keep = [low.astype(BF16), (~low).astype(BF16)]
    diag = []
    for pair in range(SSM_HEADS // 2):
        xp = xs_b[:, pair * LANES:(pair + 1) * LANES]
        mats = []
        for half in range(2):
            f = 2 * pair + half
            b = SSM_HEADS + f
            seg = jnp.where(past | now, acs[:, f:f + 1] - acs_t[f:f + 1, :],
                            acs[:, b:b + 1] - acs_t[b:b + 1, :])
            dts = (jnp.where(past, dt_t[f:f + 1, :], dt_t[b:b + 1, :])
                   + jnp.where(now, dt_t[f:f + 1, :], 0.0))
            mats.append((scores[f // HEADS_PER_GROUP] * jnp.exp(seg) * dts).astype(BF16))
        diag.append(jnp.dot(jnp.concatenate(mats, axis=1),
                            jnp.concatenate([xp * keep[0], xp * keep[1]], axis=0),
                            preferred_element_type=F32))
    y = dskip_ref[...] * xs + jnp.concatenate(diag, axis=1)
    hgroup = lax.broadcasted_iota(jnp.int32, (1, SSM_INNER), 1) // GROUP_LANES
    stacked = jnp.concatenate(
        [jnp.concatenate([p_ref[0, k] * (hgroup == g).astype(BF16) for g in range(SSM_GROUPS)],
                         axis=0) for p_ref in (pf_ref, pb_ref)], axis=1)
    off = jnp.dot(cmat, stacked, preferred_element_type=F32)
    for d in range(2):
        y = y + off[:, d * SSM_INNER:(d + 1) * SSM_INNER] * decay[d][rows]
    z = z_ref[0, rows, :]
    y = y * (z * jax.nn.sigmoid(z))
    o_ref[0, rows, :] = _rms(y, ng_ref[...]).astype(o_ref.dtype)


SSD_OUT_CHUNKS = 4


def _ssd_out(xact, dtv, acs, z, prev_f, prev_b, expand, d_skip, norm_g, bsz, seq):
    nc = seq // SSM_CHUNK
    nck = min(SSD_OUT_CHUNKS, nc)
    dsk = jnp.repeat(d_skip, SSM_HEAD_DIM)[None]
    chunk = lambda n: pl.BlockSpec((1, nck * SSM_CHUNK, n), lambda b, c: (b, c, 0))
    state = pl.BlockSpec((1, nck, SSM_STATE, SSM_INNER), lambda b, c: (b, c, 0, 0))
    out = pl.pallas_call(
        functools.partial(_ssd_out_kernel, nck=nck),
        grid=(bsz, nc // nck),
        in_specs=[chunk(SSM_CONV_CH), chunk(DT_PAD), chunk(DT_PAD), chunk(SSM_INNER), state, state,
                  _const_spec(expand.shape), _const_spec(dsk.shape), _const_spec((1, SSM_INNER))],
        out_specs=chunk(SSM_INNER),
        out_shape=jax.ShapeDtypeStruct((bsz, seq, SSM_INNER), BF16),
        compiler_params=_cparams(("parallel", "parallel")),
        name="ssd_out",
    )(xact, dtv, acs, z.reshape(bsz, seq, SSM_INNER), prev_f, prev_b, expand, dsk, norm_g[None])
    return out.reshape(bsz * seq, SSM_INNER)


def _ssd(z, xbc, dt_raw, conv_w, conv_b, a_log, dt_bias, d_skip, norm_g, bsz, seq):
    consts = _ssd_consts(a_log)
    xact, dtv, acs, new_f, new_b, carry = _ssd_pre(xbc, dt_raw, conv_w, conv_b, dt_bias, consts,
                                                   bsz, seq)
    prev_f, prev_b = _ssd_states(new_f, new_b, carry, bsz, seq)
    return _ssd_out(xact, dtv, acs, z, prev_f, prev_b, consts[2], d_skip, norm_g, bsz, seq)


FNET_COLS = 4096


def _dft_cos_sin(n):
    ang = 2.0 * np.pi * np.outer(np.arange(n), np.arange(n)) / n
    return np.cos(ang), np.sin(ang)


def _fnet_consts(seq):
    c = FNET_GROUP_DIM
    n1 = seq // LANES
    c1, s1 = _dft_cos_sin(n1)
    stage1 = np.concatenate([c1, -s1], axis=0)
    ang = 2.0 * np.pi * np.outer(np.arange(n1), np.arange(LANES)) / seq
    twr = np.repeat(np.cos(ang), c, axis=1)
    twi = np.repeat(-np.sin(ang), c, axis=1)
    cc, sc = _dft_cos_sin(c)
    chan = np.block([[cc, -sc], [sc, cc]])
    c2, s2 = _dft_cos_sin(LANES)
    return (jnp.asarray(stage1, BF16), jnp.asarray(twr, F32), jnp.asarray(twi, F32),
            jnp.asarray(chan, BF16), jnp.asarray(c2, BF16), jnp.asarray(s2, BF16))


def _fnet1_kernel(x_ref, f_ref, twr_ref, twi_ref, o_ref, *, n1):
    c = FNET_GROUP_DIM
    a = jnp.dot(f_ref[...], x_ref[0].astype(BF16), preferred_element_type=F32)
    ar, ai = a[:n1], a[n1:]
    twr, twi = twr_ref[...], twi_ref[...]
    re = (ar * twr - ai * twi).astype(o_ref.dtype)
    im = (ar * twi + ai * twr).astype(o_ref.dtype)
    for j in range(re.shape[1] // c):
        o_ref[0, :, (2 * j) * c:(2 * j + 1) * c] = re[:, j * c:(j + 1) * c]
        o_ref[0, :, (2 * j + 1) * c:(2 * j + 2) * c] = im[:, j * c:(j + 1) * c]


def _fnet2_kernel(a_ref, chan_ref, c2_ref, s2_ref, o_ref, g_ref, scr_ref, *, n1, scale):
    c = FNET_GROUP_DIM
    pitch = scr_ref.shape[0] // LANES
    per = min(8, n1)
    for i in range(n1 // per):
        blk = a_ref[0, i * per:(i + 1) * per].reshape(per * LANES, 2 * c)
        g = jnp.dot(blk, chan_ref[...], preferred_element_type=F32).astype(BF16)
        g_ref[i * per:(i + 1) * per] = g.reshape(per, LANES, 2 * c)

    def body(i, carry):
        ga, gb = g_ref[2 * i], g_ref[2 * i + 1]
        y = (jnp.dot(c2_ref[...], jnp.concatenate([ga[:, :c], gb[:, :c]], axis=1),
                     preferred_element_type=F32)
             + jnp.dot(s2_ref[...], jnp.concatenate([ga[:, c:], gb[:, c:]], axis=1),
                       preferred_element_type=F32)) * scale
        scr_ref[pl.ds(2 * i, LANES, stride=pitch), :] = y[:, :c]
        scr_ref[pl.ds(2 * i + 1, LANES, stride=pitch), :] = y[:, c:]
        return carry

    lax.fori_loop(0, n1 // 2, body, 0, unroll=4)

    def compact(k2, carry):
        src = pl.multiple_of(k2 * pitch, SUBLANES)
        dst = pl.multiple_of(k2 * n1, n1)
        o_ref[0, pl.ds(dst, n1), :] = scr_ref[pl.ds(src, n1), :].astype(o_ref.dtype)
        return carry

    lax.fori_loop(0, LANES, compact, 0, unroll=8)


def _fourier(fn, bsz, seq):
    c = FNET_GROUP_DIM
    assert c == LANES and seq % LANES == 0
    n1 = seq // LANES
    stage1, twr, twi, chan, c2, s2 = _fnet_consts(seq)
    ncols = LANES * c
    nb = min(FNET_COLS, ncols)
    x2 = fn.reshape(bsz * FNET_GROUPS, n1, ncols)
    a = pl.pallas_call(
        functools.partial(_fnet1_kernel, n1=n1),
        grid=(ncols // nb, bsz * FNET_GROUPS),
        in_specs=[pl.BlockSpec((1, n1, nb), lambda j, i: (i, 0, j)),
                  _const_spec(stage1.shape),
                  pl.BlockSpec((n1, nb), lambda j, i: (0, j)),
                  pl.BlockSpec((n1, nb), lambda j, i: (0, j))],
        out_specs=pl.BlockSpec((1, n1, 2 * nb), lambda j, i: (i, 0, j)),
        out_shape=jax.ShapeDtypeStruct((bsz * FNET_GROUPS, n1, 2 * ncols), BF16),
        compiler_params=_cparams(("parallel", "parallel")),
        name="fnet1",
    )(x2, stage1, twr, twi)
    a4 = a.reshape(bsz * FNET_GROUPS, n1, LANES, 2 * c)
    out = pl.pallas_call(
        functools.partial(_fnet2_kernel, n1=n1, scale=1.0 / math.sqrt(seq * c)),
        grid=(bsz, FNET_GROUPS),
        in_specs=[pl.BlockSpec((1, n1, LANES, 2 * c), lambda b, g: (b * FNET_GROUPS + g, 0, 0, 0)),
                  _const_spec(chan.shape), _const_spec(c2.shape), _const_spec(s2.shape)],
        out_specs=pl.BlockSpec((1, seq, c), lambda b, g: (b, 0, g)),
        out_shape=jax.ShapeDtypeStruct((bsz, seq, FNET_WIDTH), BF16),
        scratch_shapes=[pltpu.VMEM((n1, LANES, 2 * c), BF16),
                        pltpu.VMEM((LANES * (n1 + SUBLANES), c), F32)],
        compiler_params=_cparams(("parallel", "parallel")),
        name="fnet2",
    )(a4, chan, c2, s2)
    return out.reshape(bsz * seq, FNET_WIDTH)


def kernel(x, p, rel_bias, norm_mix, w_in, conv_dw, conv_dw_b, conv_ln_g, conv_ln_b, conv_out,
           ssm_conv_w, ssm_conv_b, ssm_a_log, ssm_dt_bias, ssm_d, ssm_norm, ssm_out,
           attn_out, fnet_out, w_gate, b_gate, w_out, norm_ffn, ffn_w1, ffn_w3, ffn_w2,
           moe_router, moe_w1, moe_w3, moe_w2, ple_gate, ple_proj, final_norm):
    bsz, seq, d = x.shape
    depth = w_in.shape[0]
    t = bsz * seq
    h = x.reshape(t, d)
    bias_tables = [_att_bias_tables(rel_bias, g, dil) for g, (_, dil) in enumerate(ATT_PATTERNS)]
    for l in range(depth):
        *qkv, cnf, z, xbc, fn, dt = _inproj(
            h, norm_mix[l][None], _reorder_w_in(w_in[l]),
            (conv_dw[l], conv_dw_b[l], conv_ln_g[l], conv_ln_b[l]), bsz, seq)
        att = [_attention_group(qkv[g], bias_tables[g], dil, bsz, seq, f"attn{g}")
               for g, (_, dil) in enumerate(ATT_PATTERNS)]
        ssd = _ssd(z, xbc, dt, ssm_conv_w[l], ssm_conv_b[l], ssm_a_log[l], ssm_dt_bias[l],
                   ssm_d[l], ssm_norm[l], bsz, seq)
        fnt = _fourier(fn, bsz, seq)
        wbr = jnp.stack([attn_out[l], conv_out[l], ssm_out[l], fnet_out[l]]).astype(BF16)
        h = _mix(h, norm_mix[l][None], att, (cnf, ssd, fnt), wbr, w_gate[l].astype(BF16),
                 b_gate[l][:, None, :], w_out[l].astype(BF16))
        pl_in = p[l].reshape(t, -1)
        wpg, wpp = ple_gate[l].astype(BF16), ple_proj[l].astype(BF16)
        i = l // 2
        if l % 2 == 0:
            h = _ffn(h, norm_ffn[l][None], ffn_w1[i].astype(BF16), ffn_w3[i].astype(BF16),
                     ffn_w2[i].astype(BF16), pl_in, wpg, wpp)
        else:
            h = _moe(h, norm_ffn[l][None], moe_router[i], moe_w1[i].astype(BF16),
                     moe_w3[i].astype(BF16), moe_w2[i].astype(BF16), pl_in, wpg, wpp)
    return _final_norm(h, final_norm[None]).reshape(bsz, seq, d)
```
